```python
import math
import jax, jax.numpy as jnp
from jax import lax
import numpy as np

D_MODEL = 1024
BATCH = 8
SEQ = 4096
DEPTH = 4

GRID_W = 64
CTX_LEN = 256
N_MOD = 6
EPS = 1e-6
D_MIX = D_MODEL
HY_CH = D_MIX // 4
S5_CH = D_MIX // 4
ATT_W = D_MIX // 2
HY_CONV = 3
HY_FILTER_DIM = 64
HY_BANDS = 16
HY_EMB = 1 + 2 * HY_BANDS
HY_DECAY_MIN = -math.log(1e-2) / 1.5
HY_DECAY_MAX = -math.log(1e-2) / 0.3
S5_GROUP = 16
S5_GROUPS = S5_CH // S5_GROUP
S5_STATE = 64
S5_DT_MIN = 1e-3
S5_DT_MAX = 1e-1
ATT_HEAD_DIM = 64
ATT_HEADS = ATT_W // (2 * ATT_HEAD_DIM)
ATT_V_DIM = 2 * ATT_HEAD_DIM
ROPE_HALF = ATT_HEAD_DIM // 2
ROPE_PAIRS_AXIS = ROPE_HALF // 2
ROPE_BASE = 10000.0
Q_BLOCK = 128
MOE_GROUPS = 4
MOE_EPG = 8
N_EXPERTS = MOE_GROUPS * MOE_EPG
MOE_TOP_K = 2
EXPERT_HIDDEN = D_MODEL // 2
MOE_BLOCK = 256
IN_COLS = 3 * HY_CH + S5_CH + 3 * ATT_W
IN_SPLITS = (3 * HY_CH, 3 * HY_CH + S5_CH, 3 * HY_CH + S5_CH + ATT_W, 3 * HY_CH + S5_CH + 2 * ATT_W)

kernel_name = 'hybrid_hyena_s5_diffattn_hmoe_dit'


def rms_norm(x, g):
    xf = x.astype(jnp.float32)
    y = xf * lax.rsqrt(jnp.mean(xf * xf, axis=-1, keepdims=True) + EPS)
    return (y * g.astype(jnp.float32)).astype(x.dtype)


def modulate(h, shift, scale):
    return h * (1.0 + scale) + shift


def short_conv(u, w, b):
    up = jnp.pad(u, ((0, 0), (1, 1), (0, 0)))
    return up[:, :-2] * w[0] + up[:, 1:-1] * w[1] + up[:, 2:] * w[2] + b


def hyena_filter(L, w1, b1, w2, b2, w3, freq):
    f32 = jnp.float32
    t = jnp.arange(L, dtype=f32) / L
    ang = (2.0 * math.pi) * t[:, None] * jnp.arange(1, HY_BANDS + 1, dtype=f32)
    feat = jnp.concatenate([t[:, None], jnp.cos(ang), jnp.sin(ang)], axis=-1)
    fr = freq.astype(f32)
    h = jnp.sin(fr * (feat @ w1.astype(f32) + b1.astype(f32)))
    h = jnp.sin(fr * (h @ w2.astype(f32) + b2.astype(f32)))
    h = (h @ w3.astype(f32)).reshape(L, 2, HY_CH)
    window = jnp.exp(-t[:, None] * jnp.linspace(HY_DECAY_MIN, HY_DECAY_MAX, HY_CH, dtype=f32))
    h = h * window[:, None, :]
    filt = jnp.concatenate([h[:, 0], jnp.zeros((1, HY_CH), f32), h[:0:-1, 1]], axis=0)
    return filt / (jnp.sum(jnp.abs(filt), axis=0, keepdims=True) + EPS)


def hyena_mixer(p, conv_w, conv_b, filt_params, skip, norm_g):
    L = p.shape[1]
    u = short_conv(p, conv_w, conv_b).astype(jnp.float32)
    x0, x1, v = jnp.split(u, 3, axis=-1)
    z = x1 * v
    filt = hyena_filter(L, *filt_params)
    zf = jnp.fft.rfft(z, n=2 * L, axis=1)
    ff = jnp.fft.rfft(filt, n=2 * L, axis=0)
    y = jnp.fft.irfft(zf * ff[None], n=2 * L, axis=1)[:, :L]
    return rms_norm(x0 * (y + skip.astype(jnp.float32) * z), norm_g)


def _scan_combine(left, right):
    a_l, b_l = left
    a_r, b_r = right
    return a_l * a_r, a_r * b_l + b_r


def linear_scan(a_bar, bu, reverse):
    a = jnp.broadcast_to(a_bar, (1, bu.shape[1]) + a_bar.shape)
    _, h = lax.associative_scan(_scan_combine, (a, bu), reverse=reverse, axis=1)
    return h


def s5_mixer(u_ctx, u_lat, a_re, a_im, log_dt, b_re, b_im, c_re, c_im, d_skip, glu_w, norm_g, need_ctx):
    f32 = jnp.float32
    B, Lc, _ = u_ctx.shape
    L = u_lat.shape[1]
    uc = u_ctx.astype(f32).reshape(B, Lc, S5_GROUPS, S5_GROUP)
    ul = u_lat.astype(f32).reshape(B, L, S5_GROUPS, S5_GROUP)
    dsk = d_skip.astype(f32).reshape(S5_GROUPS, S5_GROUP)
    y_lat = dsk * ul
    ctx_terms = [dsk * uc] if need_ctx else []
    for direction in range(2):
        rev = direction == 1
        A = lax.complex(a_re[direction].astype(f32), a_im[direction].astype(f32))
        dtA = jnp.exp(log_dt[direction].astype(f32))[:, None] * A
        a_bar = jnp.exp(dtA)
        b_bar = ((a_bar - 1.0) / A)[:, :, None] * lax.complex(b_re[direction].astype(f32), b_im[direction].astype(f32))
        c_r = c_re[direction].astype(f32)
        c_i = c_im[direction].astype(f32)

        def drive(u):
            return lax.complex(jnp.einsum('blgh,gph->blgp', u, b_bar.real), jnp.einsum('blgh,gph->blgp', u, b_bar.imag))

        def readout(h):
            return jnp.einsum('blgp,ghp->blgh', h.real, c_r) - jnp.einsum('blgp,ghp->blgh', h.imag, c_i)

        h_ctx = linear_scan(a_bar, drive(uc), rev)
        h0 = h_ctx[:, 0] if rev else h_ctx[:, -1]
        steps = (L - jnp.arange(L)) if rev else (jnp.arange(L) + 1)
        carry = jnp.exp(steps.astype(f32)[:, None, None] * dtA)
        h_lat = linear_scan(a_bar, drive(ul), rev) + carry[None] * h0[:, None]
        y_lat = y_lat + readout(h_lat)
        if need_ctx:
            ctx_terms.append(readout(h_ctx))

    def glu(y, n):
        g = jax.nn.gelu(y.reshape(B, n, S5_CH))
        return rms_norm(g * jax.nn.sigmoid(g @ glu_w.astype(f32)), norm_g)

    out_ctx = glu(sum(ctx_terms), Lc) if need_ctx else None
    return glu(y_lat, L), out_ctx


def axial_rope(L):
    rows = L // GRID_W
    row = jnp.repeat(jnp.arange(rows, dtype=jnp.float32), GRID_W)
    col = jnp.tile(jnp.arange(GRID_W, dtype=jnp.float32), rows)
    inv = ROPE_BASE ** (-jnp.arange(ROPE_PAIRS_AXIS, dtype=jnp.float32) / ROPE_PAIRS_AXIS)
    ang = jnp.concatenate([row[:, None] * inv, col[:, None] * inv], axis=-1)
    return jnp.cos(ang), jnp.sin(ang)


def apply_rope(t, cos, sin):
    t1, t2 = t[..., :ROPE_HALF], t[..., ROPE_HALF:]
    return jnp.concatenate([t1 * cos - t2 * sin, t1 * sin + t2 * cos], axis=-1)


def diff_attention(ql, kl, vl, qc, kc, vc, cos, sin, lam, lam_init, subln_g, need_ctx):
    f32 = jnp.float32
    B, L, _ = ql.shape
    Lc = qc.shape[1]
    scale = ATT_HEAD_DIM ** -0.5

    def heads_qk(t, n):
        return t.reshape(B, n, ATT_HEADS, 2, ATT_HEAD_DIM).transpose(0, 2, 3, 1, 4)

    def heads_v(t, n):
        return t.reshape(B, n, ATT_HEADS, ATT_V_DIM).transpose(0, 2, 1, 3)

    def diff_map(logits):
        p = jax.nn.softmax(logits.astype(f32) * scale, axis=-1)
        return p[:, :, 0] - lam * p[:, :, 1]

    def finish(o, n):
        o = rms_norm(o, subln_g) * (1.0 - lam_init)
        return o.transpose(0, 2, 1, 3).reshape(B, n, ATT_HEADS * ATT_V_DIM)

    q_lat = heads_qk(ql, L)
    q_lat_rot = apply_rope(q_lat, cos, sin)
    k_lat_rot = apply_rope(heads_qk(kl, L), cos, sin)
    k_ctx = heads_qk(kc, Lc)
    v_ctx = heads_v(vc, Lc)
    v_all = jnp.concatenate([v_ctx, heads_v(vl, L)], axis=2)

    def latent_block(i):
        s = i * Q_BLOCK
        qb = lax.dynamic_slice_in_dim(q_lat, s, Q_BLOCK, axis=3)
        qrb = lax.dynamic_slice_in_dim(q_lat_rot, s, Q_BLOCK, axis=3)
        logits = jnp.concatenate([
            jnp.einsum('bhmqd,bhmkd->bhmqk', qb, k_ctx, preferred_element_type=f32),
            jnp.einsum('bhmqd,bhmkd->bhmqk', qrb, k_lat_rot, preferred_element_type=f32)], axis=-1)
        return jnp.einsum('bhqk,bhkv->bhqv', diff_map(logits), v_all)

    o = lax.map(latent_block, jnp.arange(L // Q_BLOCK))
    o = o.transpose(1, 2, 0, 3, 4).reshape(B, ATT_HEADS, L, ATT_V_DIM)
    out_ctx = None
    if need_ctx:
        a_c = diff_map(jnp.einsum('bhmqd,bhmkd->bhmqk', heads_qk(qc, Lc), k_ctx, preferred_element_type=f32))
        out_ctx = finish(jnp.einsum('bhqk,bhkv->bhqv', a_c, v_ctx), Lc)
    return finish(o, L), out_ctx


def hier_moe(h, w_g, b_g, w_e, b_e, w1, w3, w2):
    f32 = jnp.float32
    T, D = h.shape
    hf = h.astype(f32)
    g_logits = hf @ w_g.astype(f32) + b_g.astype(f32)
    g_idx = jnp.argmax(g_logits, axis=-1)
    p_group = jnp.take_along_axis(jax.nn.softmax(g_logits, axis=-1), g_idx[:, None], axis=1)
    e_logits = (hf @ w_e.astype(f32) + b_e.astype(f32)).reshape(T, MOE_GROUPS, MOE_EPG)
    e_logits = jnp.take_along_axis(e_logits, g_idx[:, None, None], axis=1)[:, 0]
    top_p, top_i = lax.top_k(jax.nn.softmax(e_logits, axis=-1), MOE_TOP_K)
    gate = (p_group * top_p / jnp.sum(top_p, axis=-1, keepdims=True)).reshape(-1)
    expert = (g_idx[:, None] * MOE_EPG + top_i).reshape(-1).astype(jnp.int32)
    tok = jnp.repeat(jnp.arange(T, dtype=jnp.int32), MOE_TOP_K)
    n_assign = T * MOE_TOP_K
    n_blocks = -(-n_assign // MOE_BLOCK) + N_EXPERTS
    n_pad = n_blocks * MOE_BLOCK
    order = jnp.argsort(expert)
    se = expert[order]
    counts = jnp.bincount(expert, length=N_EXPERTS)
    start = jnp.cumsum(counts) - counts
    padded = (counts + MOE_BLOCK - 1) // MOE_BLOCK * MOE_BLOCK
    pad_end = jnp.cumsum(padded)
    pad_start = pad_end - padded
    dest = pad_start[se] + jnp.arange(n_assign, dtype=jnp.int32) - start[se]
    slot_tok = jnp.full((n_pad,), T, jnp.int32).at[dest].set(tok[order])
    slot_gate = jnp.zeros((n_pad,), f32).at[dest].set(gate[order])
    block_e = jnp.minimum(jnp.searchsorted(pad_end, jnp.arange(n_blocks) * MOE_BLOCK, side='right'), N_EXPERTS - 1)
    h_pad = jnp.concatenate([h, jnp.zeros((1, D), h.dtype)], axis=0)
    xb = h_pad[slot_tok].reshape(n_blocks, MOE_BLOCK, D)

    def run_block(args):
        xi, e = args
        return (jax.nn.silu(xi @ w1[e]) * (xi @ w3[e])) @ w2[e]

    yb = lax.map(run_block, (xb, block_e)).reshape(n_pad, D)
    y = jax.ops.segment_sum(yb * slot_gate[:, None].astype(yb.dtype), slot_tok, num_segments=T + 1)
    return y[:T]


def setup_inputs(seed: int = 0) -> dict:
    key = jax.random.key(seed)
    keys = iter(jax.random.split(key, 64))
    f32 = jnp.float32

    def normal(shape, scale):
        return scale * jax.random.normal(next(keys), shape, f32)

    def gain(shape):
        return 1.0 + normal(shape, 0.01)

    G, P, H = S5_GROUPS, S5_STATE, S5_GROUP
    E, F = N_EXPERTS, EXPERT_HIDDEN
    return {
        'x': normal((BATCH, SEQ, D_MODEL), 1.0),
        'c': normal((BATCH, D_MODEL), 1.0),
        'ctx': normal((BATCH, CTX_LEN, D_MODEL), 1.0),
        'c_ctx': normal((D_MODEL,), 1.0),
        'w_mod': normal((DEPTH, D_MODEL, N_MOD * D_MODEL), 0.5 * D_MODEL ** -0.5),
        'b_mod': normal((DEPTH, N_MOD * D_MODEL), 0.02),
        'norm1_g': gain((DEPTH, D_MODEL)),
        'norm2_g': gain((DEPTH, D_MODEL)),
        'final_g': gain((D_MODEL,)),
        'w_in': normal((DEPTH, D_MODEL, IN_COLS), D_MODEL ** -0.5),
        'w_out': normal((DEPTH, D_MIX, D_MODEL), D_MIX ** -0.5),
        'hy_conv_w': normal((DEPTH, HY_CONV, 3 * HY_CH), HY_CONV ** -0.5),
        'hy_conv_b': normal((DEPTH, 3 * HY_CH), 0.01),
        'hy_ffn_w1': normal((DEPTH, HY_EMB, HY_FILTER_DIM), HY_EMB ** -0.5),
        'hy_ffn_b1': normal((DEPTH, HY_FILTER_DIM), 0.1),
        'hy_ffn_w2': normal((DEPTH, HY_FILTER_DIM, HY_FILTER_DIM), HY_FILTER_DIM ** -0.5),
        'hy_ffn_b2': normal((DEPTH, HY_FILTER_DIM), 0.1),
        'hy_ffn_w3': normal((DEPTH, HY_FILTER_DIM, 2 * HY_CH), HY_FILTER_DIM ** -0.5),
        'hy_freq': gain((DEPTH, HY_FILTER_DIM)),
        'hy_skip': normal((DEPTH, HY_CH), 0.5),
        'hy_norm_g': gain((DEPTH, HY_CH)),
        's5_a_re': -0.5 + normal((DEPTH, 2, G, P), 0.01),
        's5_a_im': math.pi * jnp.arange(P, dtype=f32) + normal((DEPTH, 2, G, P), 0.01),
        's5_log_dt': jax.random.uniform(next(keys), (DEPTH, 2, G), f32, math.log(S5_DT_MIN), math.log(S5_DT_MAX)),
        's5_b_re': normal((DEPTH, 2, G, P, H), (2 * H) ** -0.5),
        's5_b_im': normal((DEPTH, 2, G, P, H), (2 * H) ** -0.5),
        's5_c_re': normal((DEPTH, 2, G, H, P), (2 * P) ** -0.5),
        's5_c_im': normal((DEPTH, 2, G, H, P), (2 * P) ** -0.5),
        's5_d': normal((DEPTH, S5_CH), 0.5),
        's5_glu_w': normal((DEPTH, S5_CH, S5_CH), S5_CH ** -0.5),
        's5_norm_g': gain((DEPTH, S5_CH)),
        'att_lq1': normal((DEPTH, ATT_HEAD_DIM), 0.1),
        'att_lk1': normal((DEPTH, ATT_HEAD_DIM), 0.1),
        'att_lq2': normal((DEPTH, ATT_HEAD_DIM), 0.1),
        'att_lk2': normal((DEPTH, ATT_HEAD_DIM), 0.1),
        'att_subln_g': gain((DEPTH, ATT_V_DIM)),
        'moe_wg': normal((DEPTH, D_MODEL, MOE_GROUPS), D_MODEL ** -0.5),
        'moe_bg': normal((DEPTH, MOE_GROUPS), 0.01),
        'moe_we': normal((DEPTH, D_MODEL, E), D_MODEL ** -0.5),
        'moe_be': normal((DEPTH, E), 0.01),
        'moe_w1': normal((DEPTH, E, D_MODEL, F), D_MODEL ** -0.5),
        'moe_w3': normal((DEPTH, E, D_MODEL, F), D_MODEL ** -0.5),
        'moe_w2': normal((DEPTH, E, F, D_MODEL), F ** -0.5),
    }


def reference(x, c, ctx, c_ctx, w_mod, b_mod, norm1_g, norm2_g, final_g, w_in, w_out,
              hy_conv_w, hy_conv_b, hy_ffn_w1, hy_ffn_b1, hy_ffn_w2, hy_ffn_b2, hy_ffn_w3, hy_freq,
              hy_skip, hy_norm_g, s5_a_re, s5_a_im, s5_log_dt, s5_b_re, s5_b_im, s5_c_re, s5_c_im,
              s5_d, s5_glu_w, s5_norm_g, att_lq1, att_lk1, att_lq2, att_lk2, att_subln_g,
              moe_wg, moe_bg, moe_we, moe_be, moe_w1, moe_w3, moe_w2):
    f32 = jnp.float32
    B, L, D = x.shape
    Lc = ctx.shape[1]
    cos, sin = axial_rope(L)
    silu_c = jax.nn.silu(c)
    silu_cc = jax.nn.silu(c_ctx)
    for l in range(DEPTH):
        need_ctx = l < DEPTH - 1
        lam_init = 0.8 - 0.6 * math.exp(-0.3 * l)
        sh1, sc1, g1, sh2, sc2, g2 = jnp.split((silu_c @ w_mod[l] + b_mod[l])[:, None, :], N_MOD, axis=-1)
        csh1, csc1, cg1, csh2, csc2, cg2 = jnp.split(silu_cc @ w_mod[l] + b_mod[l], N_MOD, axis=-1)

        p_lat = modulate(rms_norm(x, norm1_g[l]), sh1, sc1) @ w_in[l]
        p_ctx = modulate(rms_norm(ctx, norm1_g[l]), csh1, csc1) @ w_in[l]
        hy_l, s5_l, q_l, k_l, v_l = jnp.split(p_lat, IN_SPLITS, axis=-1)
        hy_c, s5_c, q_c, k_c, v_c = jnp.split(p_ctx, IN_SPLITS, axis=-1)

        filt_params = (hy_ffn_w1[l], hy_ffn_b1[l], hy_ffn_w2[l], hy_ffn_b2[l], hy_ffn_w3[l], hy_freq[l])
        hy_lat = hyena_mixer(hy_l, hy_conv_w[l], hy_conv_b[l], filt_params, hy_skip[l], hy_norm_g[l])

        s5_lat, s5_ctx = s5_mixer(s5_c, s5_l, s5_a_re[l], s5_a_im[l], s5_log_dt[l], s5_b_re[l], s5_b_im[l],
                                  s5_c_re[l], s5_c_im[l], s5_d[l], s5_glu_w[l], s5_norm_g[l], need_ctx)

        lam = (jnp.exp(jnp.sum(att_lq1[l].astype(f32) * att_lk1[l].astype(f32)))
               - jnp.exp(jnp.sum(att_lq2[l].astype(f32) * att_lk2[l].astype(f32))) + lam_init)
        att_lat, att_ctx = diff_attention(q_l, k_l, v_l, q_c, k_c, v_c, cos, sin, lam, lam_init,
                                          att_subln_g[l], need_ctx)

        mix_lat = jnp.concatenate([hy_lat, s5_lat, att_lat], axis=-1).astype(x.dtype)
        x = x + g1 * (mix_lat @ w_out[l])
        if need_ctx:
            hy_ctx = hyena_mixer(hy_c, hy_conv_w[l], hy_conv_b[l], filt_params, hy_skip[l], hy_norm_g[l])
            mix_ctx = jnp.concatenate([hy_ctx, s5_ctx, att_ctx], axis=-1).astype(ctx.dtype)
            ctx = ctx + cg1 * (mix_ctx @ w_out[l])

        moe_params = (moe_wg[l], moe_bg[l], moe_we[l], moe_be[l], moe_w1[l], moe_w3[l], moe_w2[l])
        h_lat = modulate(rms_norm(x, norm2_g[l]), sh2, sc2).reshape(B * L, D)
        if need_ctx:
            h_ctx = modulate(rms_norm(ctx, norm2_g[l]), csh2, csc2).reshape(B * Lc, D)
            y = hier_moe(jnp.concatenate([h_lat, h_ctx], axis=0), *moe_params)
            x = x + g2 * y[:B * L].reshape(B, L, D)
            ctx = ctx + cg2 * y[B * L:].reshape(B, Lc, D)
        else:
            x = x + g2 * hier_moe(h_lat, *moe_params).reshape(B, L, D)
    return rms_norm(x, final_g)
```

```python
import functools
import math

import jax
import jax.numpy as jnp
from jax import lax
from jax.experimental import pallas as pl
from jax.experimental.pallas import tpu as pltpu

D_MODEL = 1024
DEPTH = 4
GRID_W = 64
N_MOD = 6
EPS = 1e-6
HY_CH = D_MODEL // 4
S5_CH = D_MODEL // 4
ATT_W = D_MODEL // 2
HY_BANDS = 16
HY_DECAY_MIN = -math.log(1e-2) / 1.5
HY_DECAY_MAX = -math.log(1e-2) / 0.3
S5_GROUP = 16
S5_GROUPS = S5_CH // S5_GROUP
ATT_HEAD_DIM = 64
ATT_HEADS = ATT_W // (2 * ATT_HEAD_DIM)
ATT_V_DIM = 2 * ATT_HEAD_DIM
ROPE_HALF = ATT_HEAD_DIM // 2
ROPE_PAIRS_AXIS = ROPE_HALF // 2
ROPE_BASE = 10000.0
MOE_GROUPS = 4
MOE_EPG = 8
N_EXPERTS = MOE_GROUPS * MOE_EPG
MOE_TOP_K = 2
EXPERT_HIDDEN = D_MODEL // 2
MOE_BLOCK = 256
IN_SPLITS = (3 * HY_CH, 3 * HY_CH + S5_CH, 3 * HY_CH + S5_CH + ATT_W, 3 * HY_CH + S5_CH + 2 * ATT_W)

LANES = 128
VMEM_LIMIT = 48 * 1024 * 1024

F32 = jnp.float32
BF16 = jnp.bfloat16


def _params(n_axes):
    return pltpu.CompilerParams(dimension_semantics=("arbitrary",) * n_axes, vmem_limit_bytes=VMEM_LIMIT)


def _mm_kernel(x_ref, w_ref, o_ref):
    o_ref[...] = jnp.dot(x_ref[...].astype(BF16), w_ref[...], preferred_element_type=F32)


def pl_matmul(x, w, tm=512):
    M, K = x.shape
    N = w.shape[1]
    tm = min(tm, M)
    return pl.pallas_call(
        _mm_kernel,
        grid=(M // tm,),
        in_specs=[pl.BlockSpec((tm, K), lambda i: (i, 0)), pl.BlockSpec((K, N), lambda i: (0, 0))],
        out_specs=pl.BlockSpec((tm, N), lambda i: (i, 0)),
        out_shape=jax.ShapeDtypeStruct((M, N), F32),
        compiler_params=_params(1),
        name="dense_matmul",
    )(x, w.astype(BF16))


def _mm3_kernel(x_ref, wh_ref, wl_ref, o_ref):
    x = x_ref[...]
    xh = x.astype(BF16)
    xl = (x - xh.astype(F32)).astype(BF16)
    acc = jnp.dot(xh, wh_ref[...], preferred_element_type=F32)
    acc += jnp.dot(xl, wh_ref[...], preferred_element_type=F32)
    acc += jnp.dot(xh, wl_ref[...], preferred_element_type=F32)
    o_ref[...] = acc


def pl_matmul_hi(x, w, tm=512):
    M, K = x.shape
    N = w.shape[1]
    wh = w.astype(BF16)
    wl = (w - wh.astype(F32)).astype(BF16)
    return pl.pallas_call(
        _mm3_kernel,
        grid=(M // tm,),
        in_specs=[pl.BlockSpec((tm, K), lambda i: (i, 0)), pl.BlockSpec((K, N), lambda i: (0, 0)),
                  pl.BlockSpec((K, N), lambda i: (0, 0))],
        out_specs=pl.BlockSpec((tm, N), lambda i: (i, 0)),
        out_shape=jax.ShapeDtypeStruct((M, N), F32),
        compiler_params=_params(1),
        name="router_matmul",
    )(x, wh, wl)


def _dot_nt(a, b):
    return lax.dot_general(a, b, (((1,), (1,)), ((), ())), preferred_element_type=F32)


def _attn_kernel(qp_ref, qr_ref, k_ref, v_ref, g_ref, lam_ref, o_ref, *, n_ctx, n_keys):
    lane = lax.broadcasted_iota(jnp.int32, (1, LANES), 1)
    qp = qp_ref[0]
    qr = qr_ref[0]
    zero = jnp.zeros((), BF16)
    probs = []
    for m in range(2):
        in_map = (lane >= m * ATT_HEAD_DIM) & (lane < (m + 1) * ATT_HEAD_DIM)
        s_c = _dot_nt(jnp.where(in_map, qp, zero), k_ref[0, :n_ctx, :])
        mx = jnp.max(s_c, axis=-1, keepdims=True)
        if n_keys > n_ctx:
            s_l = _dot_nt(jnp.where(in_map, qr, zero), k_ref[0, n_ctx:, :])
            mx = jnp.maximum(mx, jnp.max(s_l, axis=-1, keepdims=True))
            p_l = jnp.exp(s_l - mx)
        p_c = jnp.exp(s_c - mx)
        den = jnp.sum(p_c, axis=-1, keepdims=True)
        if n_keys > n_ctx:
            den = den + jnp.sum(p_l, axis=-1, keepdims=True)
            probs.append((p_c, p_l, 1.0 / den))
        else:
            probs.append((p_c, None, 1.0 / den))
    r0 = probs[0][2]
    r1 = lam_ref[0:1, 0:1] * probs[1][2]
    a_c = (probs[0][0] * r0 - probs[1][0] * r1).astype(BF16)
    o = jnp.dot(a_c, v_ref[0, :n_ctx, :], preferred_element_type=F32)
    if n_keys > n_ctx:
        a_l = (probs[0][1] * r0 - probs[1][1] * r1).astype(BF16)
        o = o + jnp.dot(a_l, v_ref[0, n_ctx:, :], preferred_element_type=F32)
    o = o * lax.rsqrt(jnp.mean(o * o, axis=-1, keepdims=True) + EPS)
    o_ref[0] = o * g_ref[...]


def pl_diff_attention(qp, qr, k, v, g_scaled, lam, n_ctx, tq=128):
    B, Lq, _ = qp.shape
    n_keys = k.shape[1]
    lam_arr = jnp.full((8, LANES), lam, F32)
    qspec = pl.BlockSpec((1, tq, LANES), lambda b, h, i: (b, i, h))
    kspec = pl.BlockSpec((1, n_keys, LANES), lambda b, h, i: (b, 0, h))
    return pl.pallas_call(
        functools.partial(_attn_kernel, n_ctx=n_ctx, n_keys=n_keys),
        grid=(B, ATT_HEADS, Lq // tq),
        in_specs=[qspec, qspec, kspec, kspec,
                  pl.BlockSpec((1, LANES), lambda b, h, i: (0, 0)),
                  pl.BlockSpec((8, LANES), lambda b, h, i: (0, 0))],
        out_specs=pl.BlockSpec((1, tq, LANES), lambda b, h, i: (b, i, h)),
        out_shape=jax.ShapeDtypeStruct((B, Lq, ATT_W), F32),
        compiler_params=_params(3),
        name="diff_attention",
    )(qp, qr, k, v, g_scaled, lam_arr)


def _moe_kernel(be_ref, nb_ref, x_ref, w1_ref, w3_ref, w2_ref, o_ref):
    del be_ref
    i = pl.program_id(0)

    @pl.when(i < nb_ref[0])
    def _():
        x = x_ref[...]
        a = jnp.dot(x, w1_ref[0], preferred_element_type=F32)
        b = jnp.dot(x, w3_ref[0], preferred_element_type=F32)
        h = (a * jax.nn.sigmoid(a)) * b
        o_ref[...] = jnp.dot(h.astype(BF16), w2_ref[0], preferred_element_type=F32)

    @pl.when(i >= nb_ref[0])
    def _():
        o_ref[...] = jnp.zeros_like(o_ref)


def pl_moe_ffn(xb, block_e, n_used, w1, w3, w2):
    n_pad, D = xb.shape
    n_blocks = n_pad // MOE_BLOCK
    F = w1.shape[-1]
    grid_spec = pltpu.PrefetchScalarGridSpec(
        num_scalar_prefetch=2,
        grid=(n_blocks,),
        in_specs=[pl.BlockSpec((MOE_BLOCK, D), lambda i, be, nb: (i, 0)),
                  pl.BlockSpec((1, D, F), lambda i, be, nb: (be[i], 0, 0)),
                  pl.BlockSpec((1, D, F), lambda i, be, nb: (be[i], 0, 0)),
                  pl.BlockSpec((1, F, D), lambda i, be, nb: (be[i], 0, 0))],
        out_specs=pl.BlockSpec((MOE_BLOCK, D), lambda i, be, nb: (i, 0)),
    )
    return pl.pallas_call(
        _moe_kernel,
        grid_spec=grid_spec,
        out_shape=jax.ShapeDtypeStruct((n_pad, D), F32),
        compiler_params=_params(1),
        name="moe_ffn",
    )(block_e, n_used, xb, w1, w3, w2)


def rms_norm(x, g):
    return x * lax.rsqrt(jnp.mean(x * x, axis=-1, keepdims=True) + EPS) * g


def modulate(h, shift, scale):
    return h * (1.0 + scale) + shift


def short_conv(u, w, b):
    up = jnp.pad(u, ((0, 0), (1, 1), (0, 0)))
    return up[:, :-2] * w[0] + up[:, 1:-1] * w[1] + up[:, 2:] * w[2] + b


def hyena_filter(L, w1, b1, w2, b2, w3, freq):
    t = jnp.arange(L, dtype=F32) / L
    ang = (2.0 * math.pi) * t[:, None] * jnp.arange(1, HY_BANDS + 1, dtype=F32)
    feat = jnp.concatenate([t[:, None], jnp.cos(ang), jnp.sin(ang)], axis=-1)
    hp = lax.Precision.HIGHEST
    h = jnp.sin(freq * (jnp.dot(feat, w1, precision=hp) + b1))
    h = jnp.sin(freq * (jnp.dot(h, w2, precision=hp) + b2))
    h = jnp.dot(h, w3, precision=hp).reshape(L, 2, HY_CH)
    window = jnp.exp(-t[:, None] * jnp.linspace(HY_DECAY_MIN, HY_DECAY_MAX, HY_CH, dtype=F32))
    h = h * window[:, None, :]
    filt = jnp.concatenate([h[:, 0], jnp.zeros((1, HY_CH), F32), h[:0:-1, 1]], axis=0)
    return filt / (jnp.sum(jnp.abs(filt), axis=0, keepdims=True) + EPS)


def hyena_mixer(p, conv_w, conv_b, filt_params, skip, norm_g):
    L = p.shape[1]
    u = short_conv(p, conv_w, conv_b)
    x0, x1, v = jnp.split(u, 3, axis=-1)
    z = x1 * v
    filt = hyena_filter(L, *filt_params)
    zf = jnp.fft.rfft(z, n=2 * L, axis=1)
    ff = jnp.fft.rfft(filt, n=2 * L, axis=0)
    y = jnp.fft.irfft(zf * ff[None], n=2 * L, axis=1)[:, :L]
    return rms_norm(x0 * (y + skip * z), norm_g)


def _scan_combine(left, right):
    a_l, b_l = left
    a_r, b_r = right
    return a_l * a_r, a_r * b_l + b_r


def linear_scan(a_bar, bu, reverse):
    a = jnp.broadcast_to(a_bar, (1, bu.shape[1]) + a_bar.shape)
    _, h = lax.associative_scan(_scan_combine, (a, bu), reverse=reverse, axis=1)
    return h


def s5_mixer(u_ctx, u_lat, a_re, a_im, log_dt, b_re, b_im, c_re, c_im, d_skip, glu_w, norm_g):
    B, Lc, _ = u_ctx.shape
    L = u_lat.shape[1]
    uc = u_ctx.reshape(B, Lc, S5_GROUPS, S5_GROUP)
    ul = u_lat.reshape(B, L, S5_GROUPS, S5_GROUP)
    dsk = d_skip.reshape(S5_GROUPS, S5_GROUP)
    y_lat = dsk * ul
    y_ctx = dsk * uc
    for direction in range(2):
        rev = direction == 1
        A = lax.complex(a_re[direction], a_im[direction])
        dtA = jnp.exp(log_dt[direction])[:, None] * A
        a_bar = jnp.exp(dtA)
        b_bar = ((a_bar - 1.0) / A)[:, :, None] * lax.complex(b_re[direction], b_im[direction])
        c_r = c_re[direction]
        c_i = c_im[direction]

        def drive(u):
            return lax.complex(jnp.einsum('blgh,gph->blgp', u, b_bar.real),
                               jnp.einsum('blgh,gph->blgp', u, b_bar.imag))

        def readout(h):
            return jnp.einsum('blgp,ghp->blgh', h.real, c_r) - jnp.einsum('blgp,ghp->blgh', h.imag, c_i)

        h_ctx = linear_scan(a_bar, drive(uc), rev)
        h0 = h_ctx[:, 0] if rev else h_ctx[:, -1]
        steps = (L - jnp.arange(L)) if rev else (jnp.arange(L) + 1)
        carry = jnp.exp(steps.astype(F32)[:, None, None] * dtA)
        h_lat = linear_scan(a_bar, drive(ul), rev) + carry[None] * h0[:, None]
        y_lat = y_lat + readout(h_lat)
        y_ctx = y_ctx + readout(h_ctx)

    def glu(y, n):
        g = jax.nn.gelu(y.reshape(B * n, S5_CH))
        return rms_norm(g * jax.nn.sigmoid(pl_matmul(g, glu_w)), norm_g).reshape(B, n, S5_CH)

    return glu(y_lat, L), glu(y_ctx, Lc)


def axial_rope(L):
    rows = L // GRID_W
    row = jnp.repeat(jnp.arange(rows, dtype=F32), GRID_W)
    col = jnp.tile(jnp.arange(GRID_W, dtype=F32), rows)
    inv = ROPE_BASE ** (-jnp.arange(ROPE_PAIRS_AXIS, dtype=F32) / ROPE_PAIRS_AXIS)
    ang = jnp.concatenate([row[:, None] * inv, col[:, None] * inv], axis=-1)
    return jnp.cos(ang), jnp.sin(ang)


def apply_rope(t, cos, sin):
    B, L, W = t.shape
    t4 = t.reshape(B, L, W // ATT_HEAD_DIM, ATT_HEAD_DIM)
    t1, t2 = t4[..., :ROPE_HALF], t4[..., ROPE_HALF:]
    c = cos[None, :, None, :]
    s = sin[None, :, None, :]
    return jnp.concatenate([t1 * c - t2 * s, t1 * s + t2 * c], axis=-1).reshape(B, L, W)


def diff_attention(ql, kl, vl, qc, kc, vc, cos, sin, lam, lam_init, subln_g):
    scale = ATT_HEAD_DIM ** -0.5
    Lc = qc.shape[1]
    g_scaled = (subln_g * (1.0 - lam_init)).reshape(1, ATT_V_DIM)
    ql_p = (ql * scale).astype(BF16)
    ql_r = (apply_rope(ql, cos, sin) * scale).astype(BF16)
    k_all = jnp.concatenate([kc, apply_rope(kl, cos, sin)], axis=1).astype(BF16)
    v_all = jnp.concatenate([vc, vl], axis=1).astype(BF16)
    o_lat = pl_diff_attention(ql_p, ql_r, k_all, v_all, g_scaled, lam, Lc)
    qc_p = (qc * scale).astype(BF16)
    o_ctx = pl_diff_attention(qc_p, qc_p, kc.astype(BF16), vc.astype(BF16), g_scaled, lam, Lc)
    return o_lat, o_ctx


def hier_moe(h, w_g, b_g, w_e, b_e, w1, w3, w2):
    T, D = h.shape
    w_r = jnp.zeros((D, LANES), F32).at[:, :MOE_GROUPS].set(w_g).at[:, MOE_GROUPS:MOE_GROUPS + N_EXPERTS].set(w_e)
    logits = pl_matmul_hi(h, w_r)
    g_logits = logits[:, :MOE_GROUPS] + b_g
    g_idx = jnp.argmax(g_logits, axis=-1)
    p_group = jnp.take_along_axis(jax.nn.softmax(g_logits, axis=-1), g_idx[:, None], axis=1)
    e_logits = (logits[:, MOE_GROUPS:MOE_GROUPS + N_EXPERTS] + b_e).reshape(T, MOE_GROUPS, MOE_EPG)
    e_logits = jnp.take_along_axis(e_logits, g_idx[:, None, None], axis=1)[:, 0]
    top_p, top_i = lax.top_k(jax.nn.softmax(e_logits, axis=-1), MOE_TOP_K)
    gate = p_group * top_p / jnp.sum(top_p, axis=-1, keepdims=True)
    expert = (g_idx[:, None] * MOE_EPG + top_i).reshape(-1).astype(jnp.int32)
    tok = jnp.repeat(jnp.arange(T, dtype=jnp.int32), MOE_TOP_K)
    n_assign = T * MOE_TOP_K
    n_blocks = -(-n_assign // MOE_BLOCK) + N_EXPERTS
    n_pad = n_blocks * MOE_BLOCK
    order = jnp.argsort(expert)
    se = expert[order]
    counts = jnp.bincount(expert, length=N_EXPERTS)
    start = jnp.cumsum(counts) - counts
    padded = (counts + MOE_BLOCK - 1) // MOE_BLOCK * MOE_BLOCK
    pad_end = jnp.cumsum(padded)
    pad_start = pad_end - padded
    dest = (pad_start[se] + jnp.arange(n_assign, dtype=jnp.int32) - start[se]).astype(jnp.int32)
    slot_tok = jnp.full((n_pad,), T, jnp.int32).at[dest].set(tok[order])
    slot_of_assign = jnp.zeros((n_assign,), jnp.int32).at[order].set(dest)
    block_e = jnp.minimum(jnp.searchsorted(pad_end, jnp.arange(n_blocks) * MOE_BLOCK, side='right'),
                          N_EXPERTS - 1).astype(jnp.int32)
    n_used = (pad_end[-1:] // MOE_BLOCK).astype(jnp.int32)
    h_pad = jnp.concatenate([h.astype(BF16), jnp.zeros((1, D), BF16)], axis=0)
    xb = h_pad[slot_tok]
    yb = pl_moe_ffn(xb, block_e, n_used, w1.astype(BF16), w3.astype(BF16), w2.astype(BF16))
    ys = yb[slot_of_assign].reshape(T, MOE_TOP_K, D)
    return jnp.sum(ys * gate[:, :, None], axis=1)


def kernel(x, c, ctx, c_ctx, w_mod, b_mod, norm1_g, norm2_g, final_g, w_in, w_out, hy_conv_w, hy_conv_b, hy_ffn_w1, hy_ffn_b1, hy_ffn_w2, hy_ffn_b2, hy_ffn_w3, hy_freq, hy_skip, hy_norm_g, s5_a_re, s5_a_im, s5_log_dt, s5_b_re, s5_b_im, s5_c_re, s5_c_im, s5_d, s5_glu_w, s5_norm_g, att_lq1, att_lk1, att_lq2, att_lk2, att_subln_g, moe_wg, moe_bg, moe_we, moe_be, moe_w1, moe_w3, moe_w2):
    B, L, D = x.shape
    Lc = ctx.shape[1]
    cos, sin = axial_rope(L)
    silu_c = jax.nn.silu(c)
    silu_cc = jax.nn.silu(c_ctx)
    hp = lax.Precision.HIGHEST
    for l in range(DEPTH):
        lam_init = 0.8 - 0.6 * math.exp(-0.3 * l)
        mod = jnp.dot(silu_c, w_mod[l], precision=hp) + b_mod[l]
        sh1, sc1, g1, sh2, sc2, g2 = jnp.split(mod[:, None, :], N_MOD, axis=-1)
        cmod = jnp.dot(silu_cc, w_mod[l], precision=hp) + b_mod[l]
        csh1, csc1, cg1, csh2, csc2, cg2 = jnp.split(cmod, N_MOD, axis=-1)

        h_lat = modulate(rms_norm(x, norm1_g[l]), sh1, sc1).reshape(B * L, D)
        h_ctx = modulate(rms_norm(ctx, norm1_g[l]), csh1, csc1).reshape(B * Lc, D)
        p_lat = pl_matmul(h_lat, w_in[l]).reshape(B, L, -1)
        p_ctx = pl_matmul(h_ctx, w_in[l]).reshape(B, Lc, -1)
        hy_l, s5_l, q_l, k_l, v_l = jnp.split(p_lat, IN_SPLITS, axis=-1)
        hy_c, s5_c, q_c, k_c, v_c = jnp.split(p_ctx, IN_SPLITS, axis=-1)

        filt_params = (hy_ffn_w1[l], hy_ffn_b1[l], hy_ffn_w2[l], hy_ffn_b2[l], hy_ffn_w3[l], hy_freq[l])
        hy_lat = hyena_mixer(hy_l, hy_conv_w[l], hy_conv_b[l], filt_params, hy_skip[l], hy_norm_g[l])
        hy_ctx = hyena_mixer(hy_c, hy_conv_w[l], hy_conv_b[l], filt_params, hy_skip[l], hy_norm_g[l])

        s5_lat, s5_ctx = s5_mixer(s5_c, s5_l, s5_a_re[l], s5_a_im[l], s5_log_dt[l], s5_b_re[l], s5_b_im[l],
                                  s5_c_re[l], s5_c_im[l], s5_d[l], s5_glu_w[l], s5_norm_g[l])

        lam = (jnp.exp(jnp.sum(att_lq1[l] * att_lk1[l])) - jnp.exp(jnp.sum(att_lq2[l] * att_lk2[l])) + lam_init)
        att_lat, att_ctx = diff_attention(q_l, k_l, v_l, q_c, k_c, v_c, cos, sin, lam, lam_init, att_subln_g[l])

        mix_lat = jnp.concatenate([hy_lat, s5_lat, att_lat], axis=-1).reshape(B * L, D)
        mix_ctx = jnp.concatenate([hy_ctx, s5_ctx, att_ctx], axis=-1).reshape(B * Lc, D)
        x = x + g1 * pl_matmul(mix_lat, w_out[l]).reshape(B, L, D)
        ctx = ctx + cg1 * pl_matmul(mix_ctx, w_out[l]).reshape(B, Lc, D)

        moe_params = (moe_wg[l], moe_bg[l], moe_we[l], moe_be[l], moe_w1[l], moe_w3[l], moe_w2[l])
        h_lat = modulate(rms_norm(x, norm2_g[l]), sh2, sc2).reshape(B * L, D)
        h_ctx = modulate(rms_norm(ctx, norm2_g[l]), csh2, csc2).reshape(B * Lc, D)
        y = hier_moe(jnp.concatenate([h_lat, h_ctx], axis=0), *moe_params)
        x = x + g2 * y[:B * L].reshape(B, L, D)
        ctx = ctx + cg2 * y[B * L:].reshape(B, Lc, D)
    return rms_norm(x, final_g)
```

```python
import functools
import math

import jax
import jax.numpy as jnp
from jax import lax
from jax.experimental import pallas as pl
from jax.experimental.pallas import tpu as pltpu

D_MODEL = 1024
DEPTH = 4
GRID_W = 64
N_MOD = 6
EPS = 1e-6
HY_CH = D_MODEL // 4
S5_CH = D_MODEL // 4
ATT_W = D_MODEL // 2
HY_BANDS = 16
HY_DECAY_MIN = -math.log(1e-2) / 1.5
HY_DECAY_MAX = -math.log(1e-2) / 0.3
S5_GROUP = 16
S5_GROUPS = S5_CH // S5_GROUP
ATT_HEAD_DIM = 64
ATT_HEADS = ATT_W // (2 * ATT_HEAD_DIM)
ATT_V_DIM = 2 * ATT_HEAD_DIM
ROPE_HALF = ATT_HEAD_DIM // 2
ROPE_PAIRS_AXIS = ROPE_HALF // 2
ROPE_BASE = 10000.0
MOE_GROUPS = 4
MOE_EPG = 8
N_EXPERTS = MOE_GROUPS * MOE_EPG
MOE_TOP_K = 2
EXPERT_HIDDEN = D_MODEL // 2
MOE_BLOCK = 256
IN_SPLITS = (3 * HY_CH, 3 * HY_CH + S5_CH, 3 * HY_CH + S5_CH + ATT_W, 3 * HY_CH + S5_CH + 2 * ATT_W)

LANES = 128
VMEM_LIMIT = 48 * 1024 * 1024

F32 = jnp.float32
BF16 = jnp.bfloat16


def _params(n_axes):
    return pltpu.CompilerParams(dimension_semantics=("arbitrary",) * n_axes, vmem_limit_bytes=VMEM_LIMIT)


def _mm_kernel(x_ref, w_ref, o_ref):
    o_ref[...] = jnp.dot(x_ref[...].astype(BF16), w_ref[...], preferred_element_type=F32)


def pl_matmul(x, w, tm=512):
    M, K = x.shape
    N = w.shape[1]
    tm = min(tm, M)
    return pl.pallas_call(
        _mm_kernel,
        grid=(M // tm,),
        in_specs=[pl.BlockSpec((tm, K), lambda i: (i, 0)), pl.BlockSpec((K, N), lambda i: (0, 0))],
        out_specs=pl.BlockSpec((tm, N), lambda i: (i, 0)),
        out_shape=jax.ShapeDtypeStruct((M, N), F32),
        compiler_params=_params(1),
        name="dense_matmul",
    )(x, w.astype(BF16))


def _mm3_kernel(x_ref, wh_ref, wl_ref, o_ref):
    x = x_ref[...]
    xh = x.astype(BF16)
    xl = (x - xh.astype(F32)).astype(BF16)
    acc = jnp.dot(xh, wh_ref[...], preferred_element_type=F32)
    acc += jnp.dot(xl, wh_ref[...], preferred_element_type=F32)
    acc += jnp.dot(xh, wl_ref[...], preferred_element_type=F32)
    o_ref[...] = acc


def pl_matmul_hi(x, w, tm=512):
    M, K = x.shape
    N = w.shape[1]
    wh = w.astype(BF16)
    wl = (w - wh.astype(F32)).astype(BF16)
    return pl.pallas_call(
        _mm3_kernel,
        grid=(M // tm,),
        in_specs=[pl.BlockSpec((tm, K), lambda i: (i, 0)), pl.BlockSpec((K, N), lambda i: (0, 0)),
                  pl.BlockSpec((K, N), lambda i: (0, 0))],
        out_specs=pl.BlockSpec((tm, N), lambda i: (i, 0)),
        out_shape=jax.ShapeDtypeStruct((M, N), F32),
        compiler_params=_params(1),
        name="router_matmul",
    )(x, wh, wl)


def _dot_nt(a, b):
    return lax.dot_general(a, b, (((1,), (1,)), ((), ())), preferred_element_type=F32)


def _attn_kernel(qp_ref, qr_ref, k_ref, v_ref, g_ref, lam_ref, o_ref, *, n_ctx, n_keys):
    lane = lax.broadcasted_iota(jnp.int32, (1, LANES), 1)
    qp = qp_ref[0]
    qr = qr_ref[0]
    zero = jnp.zeros((), BF16)
    probs = []
    for m in range(2):
        in_map = (lane >= m * ATT_HEAD_DIM) & (lane < (m + 1) * ATT_HEAD_DIM)
        s_c = _dot_nt(jnp.where(in_map, qp, zero), k_ref[0, :n_ctx, :])
        mx = jnp.max(s_c, axis=-1, keepdims=True)
        if n_keys > n_ctx:
            s_l = _dot_nt(jnp.where(in_map, qr, zero), k_ref[0, n_ctx:, :])
            mx = jnp.maximum(mx, jnp.max(s_l, axis=-1, keepdims=True))
            p_l = jnp.exp(s_l - mx)
        p_c = jnp.exp(s_c - mx)
        den = jnp.sum(p_c, axis=-1, keepdims=True)
        if n_keys > n_ctx:
            den = den + jnp.sum(p_l, axis=-1, keepdims=True)
            probs.append((p_c, p_l, 1.0 / den))
        else:
            probs.append((p_c, None, 1.0 / den))
    r0 = probs[0][2]
    r1 = lam_ref[0:1, 0:1] * probs[1][2]
    a_c = (probs[0][0] * r0 - probs[1][0] * r1).astype(BF16)
    o = jnp.dot(a_c, v_ref[0, :n_ctx, :], preferred_element_type=F32)
    if n_keys > n_ctx:
        a_l = (probs[0][1] * r0 - probs[1][1] * r1).astype(BF16)
        o = o + jnp.dot(a_l, v_ref[0, n_ctx:, :], preferred_element_type=F32)
    o = o * lax.rsqrt(jnp.mean(o * o, axis=-1, keepdims=True) + EPS)
    o_ref[0] = o * g_ref[...]


def pl_diff_attention(qp, qr, k, v, g_scaled, lam, n_ctx, tq=128):
    B, Lq, _ = qp.shape
    n_keys = k.shape[1]
    lam_arr = jnp.full((8, LANES), lam, F32)
    qspec = pl.BlockSpec((1, tq, LANES), lambda b, h, i: (b, i, h))
    kspec = pl.BlockSpec((1, n_keys, LANES), lambda b, h, i: (b, 0, h))
    return pl.pallas_call(
        functools.partial(_attn_kernel, n_ctx=n_ctx, n_keys=n_keys),
        grid=(B, ATT_HEADS, Lq // tq),
        in_specs=[qspec, qspec, kspec, kspec,
                  pl.BlockSpec((1, LANES), lambda b, h, i: (0, 0)),
                  pl.BlockSpec((8, LANES), lambda b, h, i: (0, 0))],
        out_specs=pl.BlockSpec((1, tq, LANES), lambda b, h, i: (b, i, h)),
        out_shape=jax.ShapeDtypeStruct((B, Lq, ATT_W), F32),
        compiler_params=_params(3),
        name="diff_attention",
    )(qp, qr, k, v, g_scaled, lam_arr)


def _moe_kernel(be_ref, nb_ref, x_ref, w1_ref, w3_ref, w2_ref, o_ref):
    del be_ref
    i = pl.program_id(0)

    @pl.when(i < nb_ref[0])
    def _():
        x = x_ref[...]
        a = jnp.dot(x, w1_ref[0], preferred_element_type=F32)
        b = jnp.dot(x, w3_ref[0], preferred_element_type=F32)
        h = (a * jax.nn.sigmoid(a)) * b
        o_ref[...] = jnp.dot(h.astype(BF16), w2_ref[0], preferred_element_type=F32)

    @pl.when(i >= nb_ref[0])
    def _():
        o_ref[...] = jnp.zeros_like(o_ref)


def pl_moe_ffn(xb, block_e, n_used, w1, w3, w2):
    n_pad, D = xb.shape
    n_blocks = n_pad // MOE_BLOCK
    F = w1.shape[-1]
    grid_spec = pltpu.PrefetchScalarGridSpec(
        num_scalar_prefetch=2,
        grid=(n_blocks,),
        in_specs=[pl.BlockSpec((MOE_BLOCK, D), lambda i, be, nb: (i, 0)),
                  pl.BlockSpec((1, D, F), lambda i, be, nb: (be[i], 0, 0)),
                  pl.BlockSpec((1, D, F), lambda i, be, nb: (be[i], 0, 0)),
                  pl.BlockSpec((1, F, D), lambda i, be, nb: (be[i], 0, 0))],
        out_specs=pl.BlockSpec((MOE_BLOCK, D), lambda i, be, nb: (i, 0)),
    )
    return pl.pallas_call(
        _moe_kernel,
        grid_spec=grid_spec,
        out_shape=jax.ShapeDtypeStruct((n_pad, D), F32),
        compiler_params=_params(1),
        name="moe_ffn",
    )(block_e, n_used, xb, w1, w3, w2)


S5_STATES = S5_GROUPS * 64
S5_CHUNK = 64
SUBLANES = 8


def _s5_kernel(u_ref, wd_ref, wr_ref, ar_ref, ai_ref, d_ref, y_ref, x_scr, h_scr, hr_scr, hi_scr, *, reverse):
    ns = S5_STATES

    @pl.when(pl.program_id(0) == 0)
    def _():
        hr_scr[...] = jnp.zeros_like(hr_scr)
        hi_scr[...] = jnp.zeros_like(hi_scr)

    u = u_ref[...]
    x_scr[...] = jnp.dot(u.astype(BF16), wd_ref[...], preferred_element_type=F32)
    ar = ar_ref[...]
    ai = ai_ref[...]

    def step(hr, hi, t):
        r = pl.multiple_of(t * SUBLANES, SUBLANES)
        xr = x_scr[pl.ds(r, SUBLANES), :ns]
        xi = x_scr[pl.ds(r, SUBLANES), ns:]
        return ar * hr - ai * hi + xr, ar * hi + ai * hr + xi

    def body(j, carry):
        hr, hi = carry
        t0 = (S5_CHUNK - 1 - 2 * j) if reverse else 2 * j
        t1 = t0 - 1 if reverse else t0 + 1
        hr0, hi0 = step(hr, hi, t0)
        hr1, hi1 = step(hr0, hi0, t1)
        lo, hi_t = (t1, t0) if reverse else (t0, t1)
        first_r, second_r = (hr1, hr0) if reverse else (hr0, hr1)
        first_i, second_i = (hi1, hi0) if reverse else (hi0, hi1)
        del hi_t
        r = pl.multiple_of(lo * SUBLANES, 2 * SUBLANES)
        h_scr[pl.ds(r, 2 * SUBLANES), :ns] = jnp.concatenate([first_r, second_r], axis=0).astype(BF16)
        h_scr[pl.ds(r, 2 * SUBLANES), ns:] = jnp.concatenate([first_i, second_i], axis=0).astype(BF16)
        return hr1, hi1

    hr, hi = lax.fori_loop(0, S5_CHUNK // 2, body, (hr_scr[...], hi_scr[...]))
    hr_scr[...] = hr
    hi_scr[...] = hi
    y = jnp.dot(h_scr[...], wr_ref[...], preferred_element_type=F32)
    if not reverse:
        y = y + u * d_ref[...]
    y_ref[...] = y


def pl_s5_scan(u_tb, w_drive, w_read, a_re, a_im, d_skip, *, n_ctx_steps, batch, reverse):
    assert batch == SUBLANES
    rows, ch = u_tb.shape
    rc = S5_CHUNK * SUBLANES
    n_chunks = rows // rc
    n_ctx = n_ctx_steps // S5_CHUNK
    assert rows % rc == 0 and n_ctx_steps % S5_CHUNK == 0
    if reverse:
        def idx(i):
            return (jnp.where(i < n_ctx, n_ctx - 1 - i, n_chunks - 1 + n_ctx - i), 0)
    else:
        def idx(i):
            return (i, 0)
    const = lambda i: (0, 0)
    ns2 = 2 * S5_STATES
    return pl.pallas_call(
        functools.partial(_s5_kernel, reverse=reverse),
        grid=(n_chunks,),
        in_specs=[pl.BlockSpec((rc, ch), idx),
                  pl.BlockSpec((ch, ns2), const),
                  pl.BlockSpec((ns2, ch), const),
                  pl.BlockSpec((SUBLANES, S5_STATES), const),
                  pl.BlockSpec((SUBLANES, S5_STATES), const),
                  pl.BlockSpec((1, ch), const)],
        out_specs=pl.BlockSpec((rc, ch), idx),
        out_shape=jax.ShapeDtypeStruct((rows, ch), F32),
        scratch_shapes=[pltpu.VMEM((rc, ns2), F32), pltpu.VMEM((rc, ns2), BF16),
                        pltpu.VMEM((SUBLANES, S5_STATES), F32), pltpu.VMEM((SUBLANES, S5_STATES), F32)],
        compiler_params=_params(1),
        name="s5_scan_rev" if reverse else "s5_scan_fwd",
    )(u_tb, w_drive.astype(BF16), w_read.astype(BF16),
      jnp.broadcast_to(a_re[None, :], (SUBLANES, S5_STATES)),
      jnp.broadcast_to(a_im[None, :], (SUBLANES, S5_STATES)),
      d_skip.reshape(1, ch))


def rms_norm(x, g):
    return x * lax.rsqrt(jnp.mean(x * x, axis=-1, keepdims=True) + EPS) * g


def modulate(h, shift, scale):
    return h * (1.0 + scale) + shift


def short_conv(u, w, b):
    up = jnp.pad(u, ((0, 0), (1, 1), (0, 0)))
    return up[:, :-2] * w[0] + up[:, 1:-1] * w[1] + up[:, 2:] * w[2] + b


def hyena_filter(L, w1, b1, w2, b2, w3, freq):
    t = jnp.arange(L, dtype=F32) / L
    ang = (2.0 * math.pi) * t[:, None] * jnp.arange(1, HY_BANDS + 1, dtype=F32)
    feat = jnp.concatenate([t[:, None], jnp.cos(ang), jnp.sin(ang)], axis=-1)
    hp = lax.Precision.HIGHEST
    h = jnp.sin(freq * (jnp.dot(feat, w1, precision=hp) + b1))
    h = jnp.sin(freq * (jnp.dot(h, w2, precision=hp) + b2))
    h = jnp.dot(h, w3, precision=hp).reshape(L, 2, HY_CH)
    window = jnp.exp(-t[:, None] * jnp.linspace(HY_DECAY_MIN, HY_DECAY_MAX, HY_CH, dtype=F32))
    h = h * window[:, None, :]
    filt = jnp.concatenate([h[:, 0], jnp.zeros((1, HY_CH), F32), h[:0:-1, 1]], axis=0)
    return filt / (jnp.sum(jnp.abs(filt), axis=0, keepdims=True) + EPS)


def hyena_mixer(p, conv_w, conv_b, filt_params, skip, norm_g):
    L = p.shape[1]
    u = short_conv(p, conv_w, conv_b)
    x0, x1, v = jnp.split(u, 3, axis=-1)
    z = x1 * v
    filt = hyena_filter(L, *filt_params)
    zf = jnp.fft.rfft(z, n=2 * L, axis=1)
    ff = jnp.fft.rfft(filt, n=2 * L, axis=0)
    y = jnp.fft.irfft(zf * ff[None], n=2 * L, axis=1)[:, :L]
    return rms_norm(x0 * (y + skip * z), norm_g)


def _block_diag(blocks):
    G, r, c = blocks.shape
    eye = jnp.eye(G, dtype=blocks.dtype)
    return (eye[:, None, :, None] * blocks[:, :, None, :]).reshape(G * r, G * c)


def s5_mixer(u_ctx, u_lat, a_re, a_im, log_dt, b_re, b_im, c_re, c_im, d_skip, glu_w, norm_g):
    B, Lc, _ = u_ctx.shape
    L = u_lat.shape[1]
    Lt = Lc + L
    u_tb = jnp.concatenate([u_ctx, u_lat], axis=1).transpose(1, 0, 2).reshape(Lt * B, S5_CH)
    y_tb = None
    for direction in range(2):
        A = lax.complex(a_re[direction], a_im[direction])
        dtA = jnp.exp(log_dt[direction])[:, None] * A
        a_bar = jnp.exp(dtA)
        b_bar = ((a_bar - 1.0) / A)[:, :, None] * lax.complex(b_re[direction], b_im[direction])
        bt_re = jnp.transpose(b_bar.real, (0, 2, 1))
        bt_im = jnp.transpose(b_bar.imag, (0, 2, 1))
        w_drive = jnp.concatenate([_block_diag(bt_re), _block_diag(bt_im)], axis=1)
        ct_re = jnp.transpose(c_re[direction], (0, 2, 1))
        ct_im = jnp.transpose(c_im[direction], (0, 2, 1))
        w_read = jnp.concatenate([_block_diag(ct_re), -_block_diag(ct_im)], axis=0)
        y_dir = pl_s5_scan(u_tb, w_drive, w_read, a_bar.real.reshape(-1), a_bar.imag.reshape(-1), d_skip,
                           n_ctx_steps=Lc, batch=B, reverse=direction == 1)
        y_tb = y_dir if y_tb is None else y_tb + y_dir
    y = y_tb.reshape(Lt, B, S5_CH).transpose(1, 0, 2)

    def glu(yy, n):
        g = jax.nn.gelu(yy.reshape(B * n, S5_CH))
        return rms_norm(g * jax.nn.sigmoid(pl_matmul(g, glu_w)), norm_g).reshape(B, n, S5_CH)

    return glu(y[:, Lc:], L), glu(y[:, :Lc], Lc)


def axial_rope(L):
    rows = L // GRID_W
    row = jnp.repeat(jnp.arange(rows, dtype=F32), GRID_W)
    col = jnp.tile(jnp.arange(GRID_W, dtype=F32), rows)
    inv = ROPE_BASE ** (-jnp.arange(ROPE_PAIRS_AXIS, dtype=F32) / ROPE_PAIRS_AXIS)
    ang = jnp.concatenate([row[:, None] * inv, col[:, None] * inv], axis=-1)
    return jnp.cos(ang), jnp.sin(ang)


def apply_rope(t, cos, sin):
    B, L, W = t.shape
    t4 = t.reshape(B, L, W // ATT_HEAD_DIM, ATT_HEAD_DIM)
    t1, t2 = t4[..., :ROPE_HALF], t4[..., ROPE_HALF:]
    c = cos[None, :, None, :]
    s = sin[None, :, None, :]
    return jnp.concatenate([t1 * c - t2 * s, t1 * s + t2 * c], axis=-1).reshape(B, L, W)


def diff_attention(ql, kl, vl, qc, kc, vc, cos, sin, lam, lam_init, subln_g):
    scale = ATT_HEAD_DIM ** -0.5
    Lc = qc.shape[1]
    g_scaled = (subln_g * (1.0 - lam_init)).reshape(1, ATT_V_DIM)
    ql_p = (ql * scale).astype(BF16)
    ql_r = (apply_rope(ql, cos, sin) * scale).astype(BF16)
    k_all = jnp.concatenate([kc, apply_rope(kl, cos, sin)], axis=1).astype(BF16)
    v_all = jnp.concatenate([vc, vl], axis=1).astype(BF16)
    o_lat = pl_diff_attention(ql_p, ql_r, k_all, v_all, g_scaled, lam, Lc)
    qc_p = (qc * scale).astype(BF16)
    o_ctx = pl_diff_attention(qc_p, qc_p, kc.astype(BF16), vc.astype(BF16), g_scaled, lam, Lc)
    return o_lat, o_ctx


def hier_moe(h, w_g, b_g, w_e, b_e, w1, w3, w2):
    T, D = h.shape
    w_r = jnp.zeros((D, LANES), F32).at[:, :MOE_GROUPS].set(w_g).at[:, MOE_GROUPS:MOE_GROUPS + N_EXPERTS].set(w_e)
    logits = pl_matmul_hi(h, w_r)
    g_logits = logits[:, :MOE_GROUPS] + b_g
    g_idx = jnp.argmax(g_logits, axis=-1)
    p_group = jnp.take_along_axis(jax.nn.softmax(g_logits, axis=-1), g_idx[:, None], axis=1)
    e_logits = (logits[:, MOE_GROUPS:MOE_GROUPS + N_EXPERTS] + b_e).reshape(T, MOE_GROUPS, MOE_EPG)
    e_logits = jnp.take_along_axis(e_logits, g_idx[:, None, None], axis=1)[:, 0]
    top_p, top_i = lax.top_k(jax.nn.softmax(e_logits, axis=-1), MOE_TOP_K)
    gate = p_group * top_p / jnp.sum(top_p, axis=-1, keepdims=True)
    expert = (g_idx[:, None] * MOE_EPG + top_i).reshape(-1).astype(jnp.int32)
    tok = jnp.repeat(jnp.arange(T, dtype=jnp.int32), MOE_TOP_K)
    n_assign = T * MOE_TOP_K
    n_blocks = -(-n_assign // MOE_BLOCK) + N_EXPERTS
    n_pad = n_blocks * MOE_BLOCK
    order = jnp.argsort(expert)
    se = expert[order]
    counts = jnp.bincount(expert, length=N_EXPERTS)
    start = jnp.cumsum(counts) - counts
    padded = (counts + MOE_BLOCK - 1) // MOE_BLOCK * MOE_BLOCK
    pad_end = jnp.cumsum(padded)
    pad_start = pad_end - padded
    dest = (pad_start[se] + jnp.arange(n_assign, dtype=jnp.int32) - start[se]).astype(jnp.int32)
    slot_tok = jnp.full((n_pad,), T, jnp.int32).at[dest].set(tok[order])
    slot_of_assign = jnp.zeros((n_assign,), jnp.int32).at[order].set(dest)
    block_e = jnp.minimum(jnp.searchsorted(pad_end, jnp.arange(n_blocks) * MOE_BLOCK, side='right'),
                          N_EXPERTS - 1).astype(jnp.int32)
    n_used = (pad_end[-1:] // MOE_BLOCK).astype(jnp.int32)
    h_pad = jnp.concatenate([h.astype(BF16), jnp.zeros((1, D), BF16)], axis=0)
    xb = h_pad[slot_tok]
    yb = pl_moe_ffn(xb, block_e, n_used, w1.astype(BF16), w3.astype(BF16), w2.astype(BF16))
    ys = yb[slot_of_assign].reshape(T, MOE_TOP_K, D)
    return jnp.sum(ys * gate[:, :, None], axis=1)


def kernel(x, c, ctx, c_ctx, w_mod, b_mod, norm1_g, norm2_g, final_g, w_in, w_out, hy_conv_w, hy_conv_b, hy_ffn_w1, hy_ffn_b1, hy_ffn_w2, hy_ffn_b2, hy_ffn_w3, hy_freq, hy_skip, hy_norm_g, s5_a_re, s5_a_im, s5_log_dt, s5_b_re, s5_b_im, s5_c_re, s5_c_im, s5_d, s5_glu_w, s5_norm_g, att_lq1, att_lk1, att_lq2, att_lk2, att_subln_g, moe_wg, moe_bg, moe_we, moe_be, moe_w1, moe_w3, moe_w2):
    B, L, D = x.shape
    Lc = ctx.shape[1]
    cos, sin = axial_rope(L)
    silu_c = jax.nn.silu(c)
    silu_cc = jax.nn.silu(c_ctx)
    hp = lax.Precision.HIGHEST
    for l in range(DEPTH):
        lam_init = 0.8 - 0.6 * math.exp(-0.3 * l)
        mod = jnp.dot(silu_c, w_mod[l], precision=hp) + b_mod[l]
        sh1, sc1, g1, sh2, sc2, g2 = jnp.split(mod[:, None, :], N_MOD, axis=-1)
        cmod = jnp.dot(silu_cc, w_mod[l], precision=hp) + b_mod[l]
        csh1, csc1, cg1, csh2, csc2, cg2 = jnp.split(cmod, N_MOD, axis=-1)

        h_lat = modulate(rms_norm(x, norm1_g[l]), sh1, sc1).reshape(B * L, D)
        h_ctx = modulate(rms_norm(ctx, norm1_g[l]), csh1, csc1).reshape(B * Lc, D)
        p_lat = pl_matmul(h_lat, w_in[l]).reshape(B, L, -1)
        p_ctx = pl_matmul(h_ctx, w_in[l]).reshape(B, Lc, -1)
        hy_l, s5_l, q_l, k_l, v_l = jnp.split(p_lat, IN_SPLITS, axis=-1)
        hy_c, s5_c, q_c, k_c, v_c = jnp.split(p_ctx, IN_SPLITS, axis=-1)

        filt_params = (hy_ffn_w1[l], hy_ffn_b1[l], hy_ffn_w2[l], hy_ffn_b2[l], hy_ffn_w3[l], hy_freq[l])
        hy_lat = hyena_mixer(hy_l, hy_conv_w[l], hy_conv_b[l], filt_params, hy_skip[l], hy_norm_g[l])
        hy_ctx = hyena_mixer(hy_c, hy_conv_w[l], hy_conv_b[l], filt_params, hy_skip[l], hy_norm_g[l])

        s5_lat, s5_ctx = s5_mixer(s5_c, s5_l, s5_a_re[l], s5_a_im[l], s5_log_dt[l], s5_b_re[l], s5_b_im[l],
                                  s5_c_re[l], s5_c_im[l], s5_d[l], s5_glu_w[l], s5_norm_g[l])

        lam = (jnp.exp(jnp.sum(att_lq1[l] * att_lk1[l])) - jnp.exp(jnp.sum(att_lq2[l] * att_lk2[l])) + lam_init)
        att_lat, att_ctx = diff_attention(q_l, k_l, v_l, q_c, k_c, v_c, cos, sin, lam, lam_init, att_subln_g[l])

        mix_lat = jnp.concatenate([hy_lat, s5_lat, att_lat], axis=-1).reshape(B * L, D)
        mix_ctx = jnp.concatenate([hy_ctx, s5_ctx, att_ctx], axis=-1).reshape(B * Lc, D)
        x = x + g1 * pl_matmul(mix_lat, w_out[l]).reshape(B, L, D)
        ctx = ctx + cg1 * pl_matmul(mix_ctx, w_out[l]).reshape(B, Lc, D)

        moe_params = (moe_wg[l], moe_bg[l], moe_we[l], moe_be[l], moe_w1[l], moe_w3[l], moe_w2[l])
        h_lat = modulate(rms_norm(x, norm2_g[l]), sh2, sc2).reshape(B * L, D)
        h_ctx = modulate(rms_norm(ctx, norm2_g[l]), csh2, csc2).reshape(B * Lc, D)
        y = hier_moe(jnp.concatenate([h_lat, h_ctx], axis=0), *moe_params)
        x = x + g2 * y[:B * L].reshape(B, L, D)
        ctx = ctx + cg2 * y[B * L:].reshape(B, Lc, D)
    return rms_norm(x, final_g)
```

```python
import functools
import math

import jax
import jax.numpy as jnp
import numpy as np
from jax import lax
from jax.experimental import pallas as pl
from jax.experimental.pallas import tpu as pltpu

D_MODEL = 1024
DEPTH = 4
GRID_W = 64
N_MOD = 6
EPS = 1e-6
HY_CH = D_MODEL // 4
S5_CH = D_MODEL // 4
ATT_W = D_MODEL // 2
HY_BANDS = 16
HY_DECAY_MIN = -math.log(1e-2) / 1.5
HY_DECAY_MAX = -math.log(1e-2) / 0.3
S5_GROUP = 16
S5_GROUPS = S5_CH // S5_GROUP
ATT_HEAD_DIM = 64
ATT_HEADS = ATT_W // (2 * ATT_HEAD_DIM)
ATT_V_DIM = 2 * ATT_HEAD_DIM
ROPE_HALF = ATT_HEAD_DIM // 2
ROPE_PAIRS_AXIS = ROPE_HALF // 2
ROPE_BASE = 10000.0
MOE_GROUPS = 4
MOE_EPG = 8
N_EXPERTS = MOE_GROUPS * MOE_EPG
MOE_TOP_K = 2
EXPERT_HIDDEN = D_MODEL // 2
MOE_BLOCK = 256
IN_SPLITS = (3 * HY_CH, 3 * HY_CH + S5_CH, 3 * HY_CH + S5_CH + ATT_W, 3 * HY_CH + S5_CH + 2 * ATT_W)

LANES = 128
VMEM_LIMIT = 48 * 1024 * 1024

F32 = jnp.float32
BF16 = jnp.bfloat16


def _params(n_axes):
    return pltpu.CompilerParams(dimension_semantics=("arbitrary",) * n_axes, vmem_limit_bytes=VMEM_LIMIT)


def _mm_kernel(x_ref, w_ref, o_ref):
    o_ref[...] = jnp.dot(x_ref[...].astype(BF16), w_ref[...], preferred_element_type=F32)


def pl_matmul(x, w, tm=512):
    M, K = x.shape
    N = w.shape[1]
    tm = min(tm, M)
    return pl.pallas_call(
        _mm_kernel,
        grid=(M // tm,),
        in_specs=[pl.BlockSpec((tm, K), lambda i: (i, 0)), pl.BlockSpec((K, N), lambda i: (0, 0))],
        out_specs=pl.BlockSpec((tm, N), lambda i: (i, 0)),
        out_shape=jax.ShapeDtypeStruct((M, N), F32),
        compiler_params=_params(1),
        name="dense_matmul",
    )(x, w.astype(BF16))


def _mm3_kernel(x_ref, wh_ref, wl_ref, o_ref):
    x = x_ref[...]
    xh = x.astype(BF16)
    xl = (x - xh.astype(F32)).astype(BF16)
    acc = jnp.dot(xh, wh_ref[...], preferred_element_type=F32)
    acc += jnp.dot(xl, wh_ref[...], preferred_element_type=F32)
    acc += jnp.dot(xh, wl_ref[...], preferred_element_type=F32)
    o_ref[...] = acc


def pl_matmul_hi(x, w, tm=512):
    M, K = x.shape
    N = w.shape[1]
    wh = w.astype(BF16)
    wl = (w - wh.astype(F32)).astype(BF16)
    return pl.pallas_call(
        _mm3_kernel,
        grid=(M // tm,),
        in_specs=[pl.BlockSpec((tm, K), lambda i: (i, 0)), pl.BlockSpec((K, N), lambda i: (0, 0)),
                  pl.BlockSpec((K, N), lambda i: (0, 0))],
        out_specs=pl.BlockSpec((tm, N), lambda i: (i, 0)),
        out_shape=jax.ShapeDtypeStruct((M, N), F32),
        compiler_params=_params(1),
        name="router_matmul",
    )(x, wh, wl)


def _dot_nt(a, b):
    return lax.dot_general(a, b, (((1,), (1,)), ((), ())), preferred_element_type=F32)


def _attn_kernel(qp_ref, qr_ref, k_ref, v_ref, g_ref, lam_ref, o_ref, *, n_ctx, n_keys):
    lane = lax.broadcasted_iota(jnp.int32, (1, LANES), 1)
    qp = qp_ref[0]
    qr = qr_ref[0]
    zero = jnp.zeros((), BF16)
    probs = []
    for m in range(2):
        in_map = (lane >= m * ATT_HEAD_DIM) & (lane < (m + 1) * ATT_HEAD_DIM)
        s_c = _dot_nt(jnp.where(in_map, qp, zero), k_ref[0, :n_ctx, :])
        mx = jnp.max(s_c, axis=-1, keepdims=True)
        if n_keys > n_ctx:
            s_l = _dot_nt(jnp.where(in_map, qr, zero), k_ref[0, n_ctx:, :])
            mx = jnp.maximum(mx, jnp.max(s_l, axis=-1, keepdims=True))
            p_l = jnp.exp(s_l - mx)
        p_c = jnp.exp(s_c - mx)
        den = jnp.sum(p_c, axis=-1, keepdims=True)
        if n_keys > n_ctx:
            den = den + jnp.sum(p_l, axis=-1, keepdims=True)
            probs.append((p_c, p_l, 1.0 / den))
        else:
            probs.append((p_c, None, 1.0 / den))
    r0 = probs[0][2]
    r1 = lam_ref[0:1, 0:1] * probs[1][2]
    a_c = (probs[0][0] * r0 - probs[1][0] * r1).astype(BF16)
    o = jnp.dot(a_c, v_ref[0, :n_ctx, :], preferred_element_type=F32)
    if n_keys > n_ctx:
        a_l = (probs[0][1] * r0 - probs[1][1] * r1).astype(BF16)
        o = o + jnp.dot(a_l, v_ref[0, n_ctx:, :], preferred_element_type=F32)
    o = o * lax.rsqrt(jnp.mean(o * o, axis=-1, keepdims=True) + EPS)
    o_ref[0] = o * g_ref[...]


def pl_diff_attention(qp, qr, k, v, g_scaled, lam, n_ctx, tq=128):
    B, Lq, _ = qp.shape
    n_keys = k.shape[1]
    lam_arr = jnp.full((8, LANES), lam, F32)
    qspec = pl.BlockSpec((1, tq, LANES), lambda b, h, i: (b, i, h))
    kspec = pl.BlockSpec((1, n_keys, LANES), lambda b, h, i: (b, 0, h))
    return pl.pallas_call(
        functools.partial(_attn_kernel, n_ctx=n_ctx, n_keys=n_keys),
        grid=(B, ATT_HEADS, Lq // tq),
        in_specs=[qspec, qspec, kspec, kspec,
                  pl.BlockSpec((1, LANES), lambda b, h, i: (0, 0)),
                  pl.BlockSpec((8, LANES), lambda b, h, i: (0, 0))],
        out_specs=pl.BlockSpec((1, tq, LANES), lambda b, h, i: (b, i, h)),
        out_shape=jax.ShapeDtypeStruct((B, Lq, ATT_W), F32),
        compiler_params=_params(3),
        name="diff_attention",
    )(qp, qr, k, v, g_scaled, lam_arr)


def _moe_kernel(be_ref, nb_ref, x_ref, w1_ref, w3_ref, w2_ref, o_ref):
    del be_ref
    i = pl.program_id(0)

    @pl.when(i < nb_ref[0])
    def _():
        x = x_ref[...]
        a = jnp.dot(x, w1_ref[0], preferred_element_type=F32)
        b = jnp.dot(x, w3_ref[0], preferred_element_type=F32)
        h = (a * jax.nn.sigmoid(a)) * b
        o_ref[...] = jnp.dot(h.astype(BF16), w2_ref[0], preferred_element_type=F32)

    @pl.when(i >= nb_ref[0])
    def _():
        o_ref[...] = jnp.zeros_like(o_ref)


def pl_moe_ffn(xb, block_e, n_used, w1, w3, w2):
    n_pad, D = xb.shape
    n_blocks = n_pad // MOE_BLOCK
    F = w1.shape[-1]
    grid_spec = pltpu.PrefetchScalarGridSpec(
        num_scalar_prefetch=2,
        grid=(n_blocks,),
        in_specs=[pl.BlockSpec((MOE_BLOCK, D), lambda i, be, nb: (i, 0)),
                  pl.BlockSpec((1, D, F), lambda i, be, nb: (be[i], 0, 0)),
                  pl.BlockSpec((1, D, F), lambda i, be, nb: (be[i], 0, 0)),
                  pl.BlockSpec((1, F, D), lambda i, be, nb: (be[i], 0, 0))],
        out_specs=pl.BlockSpec((MOE_BLOCK, D), lambda i, be, nb: (i, 0)),
    )
    return pl.pallas_call(
        _moe_kernel,
        grid_spec=grid_spec,
        out_shape=jax.ShapeDtypeStruct((n_pad, D), F32),
        compiler_params=_params(1),
        name="moe_ffn",
    )(block_e, n_used, xb, w1, w3, w2)


S5_STATES = S5_GROUPS * 64
S5_CHUNK = 64
SUBLANES = 8


def _s5_kernel(u_ref, wd_ref, wr_ref, ar_ref, ai_ref, d_ref, y_ref, x_scr, h_scr, hr_scr, hi_scr, *, reverse):
    ns = S5_STATES

    @pl.when(pl.program_id(0) == 0)
    def _():
        hr_scr[...] = jnp.zeros_like(hr_scr)
        hi_scr[...] = jnp.zeros_like(hi_scr)

    u = u_ref[...]
    x_scr[...] = jnp.dot(u.astype(BF16), wd_ref[...], preferred_element_type=F32)
    ar = ar_ref[...]
    ai = ai_ref[...]

    def step(hr, hi, t):
        r = pl.multiple_of(t * SUBLANES, SUBLANES)
        xr = x_scr[pl.ds(r, SUBLANES), :ns]
        xi = x_scr[pl.ds(r, SUBLANES), ns:]
        return ar * hr - ai * hi + xr, ar * hi + ai * hr + xi

    def body(j, carry):
        hr, hi = carry
        t0 = (S5_CHUNK - 1 - 2 * j) if reverse else 2 * j
        t1 = t0 - 1 if reverse else t0 + 1
        hr0, hi0 = step(hr, hi, t0)
        hr1, hi1 = step(hr0, hi0, t1)
        lo, hi_t = (t1, t0) if reverse else (t0, t1)
        first_r, second_r = (hr1, hr0) if reverse else (hr0, hr1)
        first_i, second_i = (hi1, hi0) if reverse else (hi0, hi1)
        del hi_t
        r = pl.multiple_of(lo * SUBLANES, 2 * SUBLANES)
        h_scr[pl.ds(r, 2 * SUBLANES), :ns] = jnp.concatenate([first_r, second_r], axis=0).astype(BF16)
        h_scr[pl.ds(r, 2 * SUBLANES), ns:] = jnp.concatenate([first_i, second_i], axis=0).astype(BF16)
        return hr1, hi1

    hr, hi = lax.fori_loop(0, S5_CHUNK // 2, body, (hr_scr[...], hi_scr[...]))
    hr_scr[...] = hr
    hi_scr[...] = hi
    y = jnp.dot(h_scr[...], wr_ref[...], preferred_element_type=F32)
    if not reverse:
        y = y + u * d_ref[...]
    y_ref[...] = y


def pl_s5_scan(u_tb, w_drive, w_read, a_re, a_im, d_skip, *, n_ctx_steps, batch, reverse):
    assert batch == SUBLANES
    rows, ch = u_tb.shape
    rc = S5_CHUNK * SUBLANES
    n_chunks = rows // rc
    n_ctx = n_ctx_steps // S5_CHUNK
    assert rows % rc == 0 and n_ctx_steps % S5_CHUNK == 0
    if reverse:
        def idx(i):
            return (jnp.where(i < n_ctx, n_ctx - 1 - i, n_chunks - 1 + n_ctx - i), 0)
    else:
        def idx(i):
            return (i, 0)
    const = lambda i: (0, 0)
    ns2 = 2 * S5_STATES
    return pl.pallas_call(
        functools.partial(_s5_kernel, reverse=reverse),
        grid=(n_chunks,),
        in_specs=[pl.BlockSpec((rc, ch), idx),
                  pl.BlockSpec((ch, ns2), const),
                  pl.BlockSpec((ns2, ch), const),
                  pl.BlockSpec((SUBLANES, S5_STATES), const),
                  pl.BlockSpec((SUBLANES, S5_STATES), const),
                  pl.BlockSpec((1, ch), const)],
        out_specs=pl.BlockSpec((rc, ch), idx),
        out_shape=jax.ShapeDtypeStruct((rows, ch), F32),
        scratch_shapes=[pltpu.VMEM((rc, ns2), F32), pltpu.VMEM((rc, ns2), BF16),
                        pltpu.VMEM((SUBLANES, S5_STATES), F32), pltpu.VMEM((SUBLANES, S5_STATES), F32)],
        compiler_params=_params(1),
        name="s5_scan_rev" if reverse else "s5_scan_fwd",
    )(u_tb, w_drive.astype(BF16), w_read.astype(BF16),
      jnp.broadcast_to(a_re[None, :], (SUBLANES, S5_STATES)),
      jnp.broadcast_to(a_im[None, :], (SUBLANES, S5_STATES)),
      d_skip.reshape(1, ch))


FFT_N2 = 128
HYENA_VMEM_LIMIT = 56 * 1024 * 1024


def _fft_tables(L):
    N = 2 * L
    N1 = N // FFT_N2
    k1 = np.arange(N1)[:, None]
    n1 = np.arange(N1 // 2)[None, :]
    n2 = np.arange(FFT_N2)[:, None, None]
    ang = -2.0 * np.pi * (k1[None] * (n2 + FFT_N2 * n1[None])) / N
    mr, mi = np.cos(ang), np.sin(ang)
    ma = np.concatenate([np.concatenate([mr, -mi], axis=2), np.concatenate([mi, mr], axis=2)], axis=1)
    gr, gi = np.transpose(mr, (0, 2, 1)), -np.transpose(mi, (0, 2, 1))
    mainv = np.concatenate([np.concatenate([gr, -gi], axis=2), np.concatenate([gi, gr], axis=2)], axis=1)
    kk = np.arange(FFT_N2)
    a2 = -2.0 * np.pi * np.outer(kk, kk) / FFT_N2
    fr, fi = np.cos(a2), np.sin(a2)
    f_fwd = np.block([[fr, -fi], [fi, fr]])
    f_inv = np.block([[fr, fi], [-fi, fr]])
    return (jnp.asarray(ma, BF16), jnp.asarray(mainv, BF16), jnp.asarray(f_fwd, BF16), jnp.asarray(f_inv, BF16))


def _hyena_fft_kernel(z_ref, h_ref, ma_ref, mainv_ref, ff_ref, fi_ref, o_ref, a_scr, b_scr, *, n1_count):
    half = n1_count // 2
    n2c = FFT_N2

    def stage_a(n2, c):
        xr = z_ref[0, pl.ds(n2, half, stride=n2c), :]
        xi = z_ref[1, pl.ds(n2, half, stride=n2c), :]
        x = jnp.concatenate([xr, xi], axis=0).astype(BF16)
        r = jnp.dot(ma_ref[n2], x, preferred_element_type=F32)
        a_scr[pl.ds(pl.multiple_of(n2 * 2 * n1_count, 2 * n1_count), 2 * n1_count), :] = r
        return c

    lax.fori_loop(0, n2c, stage_a, 0)

    def stage_c(k1, c):
        ar = a_scr[pl.ds(k1, n2c, stride=2 * n1_count), :]
        ai = a_scr[pl.ds(n1_count + k1, n2c, stride=2 * n1_count), :]
        x = jnp.concatenate([ar, ai], axis=0).astype(BF16)
        y = jnp.dot(ff_ref[...], x, preferred_element_type=F32)
        yr, yi = y[:n2c], y[n2c:]
        hr = h_ref[0, k1].astype(F32)
        hi = h_ref[1, k1].astype(F32)
        x2 = jnp.concatenate([yr * hr - yi * hi, yr * hi + yi * hr], axis=0).astype(BF16)
        b = jnp.dot(fi_ref[...], x2, preferred_element_type=F32)
        b_scr[pl.ds(pl.multiple_of(k1 * 2 * n2c, 2 * n2c), 2 * n2c), :] = b
        return c

    lax.fori_loop(0, n1_count, stage_c, 0)

    def stage_a_inv(n2, c):
        br = b_scr[pl.ds(n2, n1_count, stride=2 * n2c), :]
        bi = b_scr[pl.ds(n2c + n2, n1_count, stride=2 * n2c), :]
        x = jnp.concatenate([br, bi], axis=0).astype(BF16)
        r = jnp.dot(mainv_ref[n2], x, preferred_element_type=F32)
        o_ref[0, pl.ds(n2, half, stride=n2c), :] = r[:half]
        o_ref[1, pl.ds(n2, half, stride=n2c), :] = r[half:]
        return c

    lax.fori_loop(0, n2c, stage_a_inv, 0)


def pl_hyena_conv(z, filt):
    B, L, C = z.shape
    N = 2 * L
    N1 = N // FFT_N2
    hf = jnp.fft.fft(filt, axis=0) / N
    h2 = hf.reshape(FFT_N2, N1, C).transpose(1, 0, 2)
    h = jnp.stack([h2.real, h2.imag]).astype(BF16)
    ma, mainv, f_fwd, f_inv = _fft_tables(L)
    cw = LANES
    const3 = lambda ct, bp: (0, 0, 0)
    const2 = lambda ct, bp: (0, 0)
    return pl.pallas_call(
        functools.partial(_hyena_fft_kernel, n1_count=N1),
        grid=(C // cw, B // 2),
        in_specs=[pl.BlockSpec((2, L, cw), lambda ct, bp: (bp, 0, ct)),
                  pl.BlockSpec((2, N1, FFT_N2, cw), lambda ct, bp: (0, 0, 0, ct)),
                  pl.BlockSpec(ma.shape, const3), pl.BlockSpec(mainv.shape, const3),
                  pl.BlockSpec(f_fwd.shape, const2), pl.BlockSpec(f_inv.shape, const2)],
        out_specs=pl.BlockSpec((2, L, cw), lambda ct, bp: (bp, 0, ct)),
        out_shape=jax.ShapeDtypeStruct((B, L, C), F32),
        scratch_shapes=[pltpu.VMEM((FFT_N2 * 2 * N1, cw), F32), pltpu.VMEM((N1 * 2 * FFT_N2, cw), F32)],
        compiler_params=pltpu.CompilerParams(dimension_semantics=("arbitrary", "arbitrary"),
                                             vmem_limit_bytes=HYENA_VMEM_LIMIT),
        name="hyena_fft_conv",
    )(z, h, ma, mainv, f_fwd, f_inv)


def rms_norm(x, g):
    return x * lax.rsqrt(jnp.mean(x * x, axis=-1, keepdims=True) + EPS) * g


def modulate(h, shift, scale):
    return h * (1.0 + scale) + shift


def short_conv(u, w, b):
    up = jnp.pad(u, ((0, 0), (1, 1), (0, 0)))
    return up[:, :-2] * w[0] + up[:, 1:-1] * w[1] + up[:, 2:] * w[2] + b


def hyena_filter(L, w1, b1, w2, b2, w3, freq):
    t = jnp.arange(L, dtype=F32) / L
    ang = (2.0 * math.pi) * t[:, None] * jnp.arange(1, HY_BANDS + 1, dtype=F32)
    feat = jnp.concatenate([t[:, None], jnp.cos(ang), jnp.sin(ang)], axis=-1)
    hp = lax.Precision.HIGHEST
    h = jnp.sin(freq * (jnp.dot(feat, w1, precision=hp) + b1))
    h = jnp.sin(freq * (jnp.dot(h, w2, precision=hp) + b2))
    h = jnp.dot(h, w3, precision=hp).reshape(L, 2, HY_CH)
    window = jnp.exp(-t[:, None] * jnp.linspace(HY_DECAY_MIN, HY_DECAY_MAX, HY_CH, dtype=F32))
    h = h * window[:, None, :]
    filt = jnp.concatenate([h[:, 0], jnp.zeros((1, HY_CH), F32), h[:0:-1, 1]], axis=0)
    return filt / (jnp.sum(jnp.abs(filt), axis=0, keepdims=True) + EPS)


def hyena_mixer(p, conv_w, conv_b, filt_params, skip, norm_g):
    L = p.shape[1]
    u = short_conv(p, conv_w, conv_b)
    x0, x1, v = jnp.split(u, 3, axis=-1)
    z = x1 * v
    filt = hyena_filter(L, *filt_params)
    if (2 * L) // FFT_N2 >= 16:
        y = pl_hyena_conv(z, filt)
    else:
        zf = jnp.fft.rfft(z, n=2 * L, axis=1)
        ff = jnp.fft.rfft(filt, n=2 * L, axis=0)
        y = jnp.fft.irfft(zf * ff[None], n=2 * L, axis=1)[:, :L]
    return rms_norm(x0 * (y + skip * z), norm_g)


def _block_diag(blocks):
    G, r, c = blocks.shape
    eye = jnp.eye(G, dtype=blocks.dtype)
    return (eye[:, None, :, None] * blocks[:, :, None, :]).reshape(G * r, G * c)


def s5_mixer(u_ctx, u_lat, a_re, a_im, log_dt, b_re, b_im, c_re, c_im, d_skip, glu_w, norm_g):
    B, Lc, _ = u_ctx.shape
    L = u_lat.shape[1]
    Lt = Lc + L
    u_tb = jnp.concatenate([u_ctx, u_lat], axis=1).transpose(1, 0, 2).reshape(Lt * B, S5_CH)
    y_tb = None
    for direction in range(2):
        A = lax.complex(a_re[direction], a_im[direction])
        dtA = jnp.exp(log_dt[direction])[:, None] * A
        a_bar = jnp.exp(dtA)
        b_bar = ((a_bar - 1.0) / A)[:, :, None] * lax.complex(b_re[direction], b_im[direction])
        bt_re = jnp.transpose(b_bar.real, (0, 2, 1))
        bt_im = jnp.transpose(b_bar.imag, (0, 2, 1))
        w_drive = jnp.concatenate([_block_diag(bt_re), _block_diag(bt_im)], axis=1)
        ct_re = jnp.transpose(c_re[direction], (0, 2, 1))
        ct_im = jnp.transpose(c_im[direction], (0, 2, 1))
        w_read = jnp.concatenate([_block_diag(ct_re), -_block_diag(ct_im)], axis=0)
        y_dir = pl_s5_scan(u_tb, w_drive, w_read, a_bar.real.reshape(-1), a_bar.imag.reshape(-1), d_skip,
                           n_ctx_steps=Lc, batch=B, reverse=direction == 1)
        y_tb = y_dir if y_tb is None else y_tb + y_dir
    y = y_tb.reshape(Lt, B, S5_CH).transpose(1, 0, 2)

    def glu(yy, n):
        g = jax.nn.gelu(yy.reshape(B * n, S5_CH))
        return rms_norm(g * jax.nn.sigmoid(pl_matmul(g, glu_w)), norm_g).reshape(B, n, S5_CH)

    return glu(y[:, Lc:], L), glu(y[:, :Lc], Lc)


def axial_rope(L):
    rows = L // GRID_W
    row = jnp.repeat(jnp.arange(rows, dtype=F32), GRID_W)
    col = jnp.tile(jnp.arange(GRID_W, dtype=F32), rows)
    inv = ROPE_BASE ** (-jnp.arange(ROPE_PAIRS_AXIS, dtype=F32) / ROPE_PAIRS_AXIS)
    ang = jnp.concatenate([row[:, None] * inv, col[:, None] * inv], axis=-1)
    return jnp.cos(ang), jnp.sin(ang)


def apply_rope(t, cos, sin):
    B, L, W = t.shape
    t4 = t.reshape(B, L, W // ATT_HEAD_DIM, ATT_HEAD_DIM)
    t1, t2 = t4[..., :ROPE_HALF], t4[..., ROPE_HALF:]
    c = cos[None, :, None, :]
    s = sin[None, :, None, :]
    return jnp.concatenate([t1 * c - t2 * s, t1 * s + t2 * c], axis=-1).reshape(B, L, W)


def diff_attention(ql, kl, vl, qc, kc, vc, cos, sin, lam, lam_init, subln_g):
    scale = ATT_HEAD_DIM ** -0.5
    Lc = qc.shape[1]
    g_scaled = (subln_g * (1.0 - lam_init)).reshape(1, ATT_V_DIM)
    ql_p = (ql * scale).astype(BF16)
    ql_r = (apply_rope(ql, cos, sin) * scale).astype(BF16)
    k_all = jnp.concatenate([kc, apply_rope(kl, cos, sin)], axis=1).astype(BF16)
    v_all = jnp.concatenate([vc, vl], axis=1).astype(BF16)
    o_lat = pl_diff_attention(ql_p, ql_r, k_all, v_all, g_scaled, lam, Lc)
    qc_p = (qc * scale).astype(BF16)
    o_ctx = pl_diff_attention(qc_p, qc_p, kc.astype(BF16), vc.astype(BF16), g_scaled, lam, Lc)
    return o_lat, o_ctx


def hier_moe(h, w_g, b_g, w_e, b_e, w1, w3, w2):
    T, D = h.shape
    w_r = jnp.zeros((D, LANES), F32).at[:, :MOE_GROUPS].set(w_g).at[:, MOE_GROUPS:MOE_GROUPS + N_EXPERTS].set(w_e)
    logits = pl_matmul_hi(h, w_r)
    g_logits = logits[:, :MOE_GROUPS] + b_g
    g_idx = jnp.argmax(g_logits, axis=-1)
    p_group = jnp.take_along_axis(jax.nn.softmax(g_logits, axis=-1), g_idx[:, None], axis=1)
    e_logits = (logits[:, MOE_GROUPS:MOE_GROUPS + N_EXPERTS] + b_e).reshape(T, MOE_GROUPS, MOE_EPG)
    e_logits = jnp.take_along_axis(e_logits, g_idx[:, None, None], axis=1)[:, 0]
    top_p, top_i = lax.top_k(jax.nn.softmax(e_logits, axis=-1), MOE_TOP_K)
    gate = p_group * top_p / jnp.sum(top_p, axis=-1, keepdims=True)
    expert = (g_idx[:, None] * MOE_EPG + top_i).reshape(-1).astype(jnp.int32)
    tok = jnp.repeat(jnp.arange(T, dtype=jnp.int32), MOE_TOP_K)
    n_assign = T * MOE_TOP_K
    n_blocks = -(-n_assign // MOE_BLOCK) + N_EXPERTS
    n_pad = n_blocks * MOE_BLOCK
    order = jnp.argsort(expert)
    onehot = (expert[:, None] == jnp.arange(N_EXPERTS, dtype=jnp.int32)[None, :]).astype(jnp.int32)
    csum = jnp.cumsum(onehot, axis=0)
    counts = csum[-1]
    rank = jnp.take_along_axis(csum, expert[:, None], axis=1)[:, 0] - 1
    start = jnp.cumsum(counts) - counts
    padded = (counts + MOE_BLOCK - 1) // MOE_BLOCK * MOE_BLOCK
    pad_end = jnp.cumsum(padded)
    pad_start = pad_end - padded
    slot_of_assign = (pad_start[expert] + rank).astype(jnp.int32)
    block_e = jnp.minimum(jnp.searchsorted(pad_end, jnp.arange(n_blocks) * MOE_BLOCK, side='right'),
                          N_EXPERTS - 1).astype(jnp.int32)
    slot_e = jnp.repeat(block_e, MOE_BLOCK)
    slot_r = jnp.arange(n_pad, dtype=jnp.int32) - pad_start[slot_e]
    slot_valid = (slot_r < counts[slot_e]) & (jnp.arange(n_pad) < pad_end[-1])
    sorted_pos = jnp.clip(start[slot_e] + slot_r, 0, n_assign - 1)
    slot_tok = jnp.where(slot_valid, tok[order[sorted_pos]], T).astype(jnp.int32)
    n_used = (pad_end[-1:] // MOE_BLOCK).astype(jnp.int32)
    h_pad = jnp.concatenate([h.astype(BF16), jnp.zeros((1, D), BF16)], axis=0)
    xb = h_pad[slot_tok]
    yb = pl_moe_ffn(xb, block_e, n_used, w1.astype(BF16), w3.astype(BF16), w2.astype(BF16))
    ys = yb[slot_of_assign].reshape(T, MOE_TOP_K, D)
    return jnp.sum(ys * gate[:, :, None], axis=1)


def kernel(x, c, ctx, c_ctx, w_mod, b_mod, norm1_g, norm2_g, final_g, w_in, w_out, hy_conv_w, hy_conv_b, hy_ffn_w1, hy_ffn_b1, hy_ffn_w2, hy_ffn_b2, hy_ffn_w3, hy_freq, hy_skip, hy_norm_g, s5_a_re, s5_a_im, s5_log_dt, s5_b_re, s5_b_im, s5_c_re, s5_c_im, s5_d, s5_glu_w, s5_norm_g, att_lq1, att_lk1, att_lq2, att_lk2, att_subln_g, moe_wg, moe_bg, moe_we, moe_be, moe_w1, moe_w3, moe_w2):
    B, L, D = x.shape
    Lc = ctx.shape[1]
    cos, sin = axial_rope(L)
    silu_c = jax.nn.silu(c)
    silu_cc = jax.nn.silu(c_ctx)
    hp = lax.Precision.HIGHEST
    for l in range(DEPTH):
        lam_init = 0.8 - 0.6 * math.exp(-0.3 * l)
        mod = jnp.dot(silu_c, w_mod[l], precision=hp) + b_mod[l]
        sh1, sc1, g1, sh2, sc2, g2 = jnp.split(mod[:, None, :], N_MOD, axis=-1)
        cmod = jnp.dot(silu_cc, w_mod[l], precision=hp) + b_mod[l]
        csh1, csc1, cg1, csh2, csc2, cg2 = jnp.split(cmod, N_MOD, axis=-1)

        h_lat = modulate(rms_norm(x, norm1_g[l]), sh1, sc1).reshape(B * L, D)
        h_ctx = modulate(rms_norm(ctx, norm1_g[l]), csh1, csc1).reshape(B * Lc, D)
        p_lat = pl_matmul(h_lat, w_in[l]).reshape(B, L, -1)
        p_ctx = pl_matmul(h_ctx, w_in[l]).reshape(B, Lc, -1)
        hy_l, s5_l, q_l, k_l, v_l = jnp.split(p_lat, IN_SPLITS, axis=-1)
        hy_c, s5_c, q_c, k_c, v_c = jnp.split(p_ctx, IN_SPLITS, axis=-1)

        filt_params = (hy_ffn_w1[l], hy_ffn_b1[l], hy_ffn_w2[l], hy_ffn_b2[l], hy_ffn_w3[l], hy_freq[l])
        hy_lat = hyena_mixer(hy_l, hy_conv_w[l], hy_conv_b[l], filt_params, hy_skip[l], hy_norm_g[l])
        hy_ctx = hyena_mixer(hy_c, hy_conv_w[l], hy_conv_b[l], filt_params, hy_skip[l], hy_norm_g[l])

        s5_lat, s5_ctx = s5_mixer(s5_c, s5_l, s5_a_re[l], s5_a_im[l], s5_log_dt[l], s5_b_re[l], s5_b_im[l],
                                  s5_c_re[l], s5_c_im[l], s5_d[l], s5_glu_w[l], s5_norm_g[l])

        lam = (jnp.exp(jnp.sum(att_lq1[l] * att_lk1[l])) - jnp.exp(jnp.sum(att_lq2[l] * att_lk2[l])) + lam_init)
        att_lat, att_ctx = diff_attention(q_l, k_l, v_l, q_c, k_c, v_c, cos, sin, lam, lam_init, att_subln_g[l])

        mix_lat = jnp.concatenate([hy_lat, s5_lat, att_lat], axis=-1).reshape(B * L, D)
        mix_ctx = jnp.concatenate([hy_ctx, s5_ctx, att_ctx], axis=-1).reshape(B * Lc, D)
        x = x + g1 * pl_matmul(mix_lat, w_out[l]).reshape(B, L, D)
        ctx = ctx + cg1 * pl_matmul(mix_ctx, w_out[l]).reshape(B, Lc, D)

        moe_params = (moe_wg[l], moe_bg[l], moe_we[l], moe_be[l], moe_w1[l], moe_w3[l], moe_w2[l])
        h_lat = modulate(rms_norm(x, norm2_g[l]), sh2, sc2).reshape(B * L, D)
        h_ctx = modulate(rms_norm(ctx, norm2_g[l]), csh2, csc2).reshape(B * Lc, D)
        y = hier_moe(jnp.concatenate([h_lat, h_ctx], axis=0), *moe_params)
        x = x + g2 * y[:B * L].reshape(B, L, D)
        ctx = ctx + cg2 * y[B * L:].reshape(B, Lc, D)
    return rms_norm(x, final_g)
```

```python
import functools
import math

import jax
import jax.numpy as jnp
import numpy as np
from jax import lax
from jax.experimental import pallas as pl
from jax.experimental.pallas import tpu as pltpu

D_MODEL = 1024
DEPTH = 4
GRID_W = 64
N_MOD = 6
EPS = 1e-6
HY_CH = D_MODEL // 4
S5_CH = D_MODEL // 4
ATT_W = D_MODEL // 2
HY_BANDS = 16
HY_DECAY_MIN = -math.log(1e-2) / 1.5
HY_DECAY_MAX = -math.log(1e-2) / 0.3
S5_GROUP = 16
S5_GROUPS = S5_CH // S5_GROUP
ATT_HEAD_DIM = 64
ATT_HEADS = ATT_W // (2 * ATT_HEAD_DIM)
ATT_V_DIM = 2 * ATT_HEAD_DIM
ROPE_HALF = ATT_HEAD_DIM // 2
ROPE_PAIRS_AXIS = ROPE_HALF // 2
ROPE_BASE = 10000.0
MOE_GROUPS = 4
MOE_EPG = 8
N_EXPERTS = MOE_GROUPS * MOE_EPG
MOE_TOP_K = 2
EXPERT_HIDDEN = D_MODEL // 2
MOE_BLOCK = 256
IN_SPLITS = (3 * HY_CH, 3 * HY_CH + S5_CH, 3 * HY_CH + S5_CH + ATT_W, 3 * HY_CH + S5_CH + 2 * ATT_W)

LANES = 128
VMEM_LIMIT = 48 * 1024 * 1024

F32 = jnp.float32
BF16 = jnp.bfloat16


def _params(n_axes):
    return pltpu.CompilerParams(dimension_semantics=("arbitrary",) * n_axes, vmem_limit_bytes=VMEM_LIMIT)


def _mm_kernel(x_ref, w_ref, o_ref):
    o_ref[...] = jnp.dot(x_ref[...].astype(BF16), w_ref[...], preferred_element_type=F32)


def pl_matmul(x, w, tm=512):
    M, K = x.shape
    N = w.shape[1]
    tm = min(tm, M)
    return pl.pallas_call(
        _mm_kernel,
        grid=(M // tm,),
        in_specs=[pl.BlockSpec((tm, K), lambda i: (i, 0)), pl.BlockSpec((K, N), lambda i: (0, 0))],
        out_specs=pl.BlockSpec((tm, N), lambda i: (i, 0)),
        out_shape=jax.ShapeDtypeStruct((M, N), F32),
        compiler_params=_params(1),
        name="dense_matmul",
    )(x, w.astype(BF16))


def _mm3_kernel(x_ref, wh_ref, wl_ref, o_ref):
    x = x_ref[...]
    xh = x.astype(BF16)
    xl = (x - xh.astype(F32)).astype(BF16)
    acc = jnp.dot(xh, wh_ref[...], preferred_element_type=F32)
    acc += jnp.dot(xl, wh_ref[...], preferred_element_type=F32)
    acc += jnp.dot(xh, wl_ref[...], preferred_element_type=F32)
    o_ref[...] = acc


def pl_matmul_hi(x, w, tm=512):
    M, K = x.shape
    N = w.shape[1]
    wh = w.astype(BF16)
    wl = (w - wh.astype(F32)).astype(BF16)
    return pl.pallas_call(
        _mm3_kernel,
        grid=(M // tm,),
        in_specs=[pl.BlockSpec((tm, K), lambda i: (i, 0)), pl.BlockSpec((K, N), lambda i: (0, 0)),
                  pl.BlockSpec((K, N), lambda i: (0, 0))],
        out_specs=pl.BlockSpec((tm, N), lambda i: (i, 0)),
        out_shape=jax.ShapeDtypeStruct((M, N), F32),
        compiler_params=_params(1),
        name="router_matmul",
    )(x, wh, wl)


def _dot_nt(a, b):
    return lax.dot_general(a, b, (((1,), (1,)), ((), ())), preferred_element_type=F32)


ATT_TQ = 256


def _attn_kernel(qp_ref, qr_ref, k_ref, v_ref, g_ref, lam_ref, o_ref, *, n_ctx, n_keys):
    tq = qp_ref.shape[1]
    first_map = lax.broadcasted_iota(jnp.int32, (1, LANES), 1) < ATT_HEAD_DIM
    zero = jnp.zeros((), BF16)

    def both_maps(q):
        return jnp.concatenate([jnp.where(first_map, q, zero), jnp.where(first_map, zero, q)], axis=0)

    has_lat = n_keys > n_ctx
    s_c = _dot_nt(both_maps(qp_ref[0]), k_ref[0, :n_ctx, :])
    mx = jnp.max(s_c, axis=-1, keepdims=True)
    if has_lat:
        s_l = _dot_nt(both_maps(qr_ref[0]), k_ref[0, n_ctx:, :])
        mx = jnp.maximum(mx, jnp.max(s_l, axis=-1, keepdims=True))
    o = jnp.dot(jnp.exp2(s_c - mx).astype(BF16), v_ref[0, :n_ctx, :], preferred_element_type=F32)
    if has_lat:
        o = o + jnp.dot(jnp.exp2(s_l - mx).astype(BF16), v_ref[0, n_ctx:, :], preferred_element_type=F32)
    o = o[:, :ATT_V_DIM] / o[:, ATT_V_DIM:]
    o = o[:tq] - lam_ref[0:1, :] * o[tq:]
    o = o * lax.rsqrt(jnp.mean(o * o, axis=-1, keepdims=True) + EPS)
    o_ref[0] = o * g_ref[...]


def pl_diff_attention(qp, qr, k, v_ext, g_scaled, lam, n_ctx):
    B, Lq, _ = qp.shape
    n_keys = k.shape[1]
    tq = min(ATT_TQ, Lq)
    lam_arr = jnp.full((8, LANES), lam, F32)
    qspec = pl.BlockSpec((1, tq, LANES), lambda b, h, i: (b, i, h))
    return pl.pallas_call(
        functools.partial(_attn_kernel, n_ctx=n_ctx, n_keys=n_keys),
        grid=(B, ATT_HEADS, Lq // tq),
        in_specs=[qspec, qspec,
                  pl.BlockSpec((1, n_keys, LANES), lambda b, h, i: (b, 0, h)),
                  pl.BlockSpec((1, n_keys, 2 * LANES), lambda b, h, i: (b, 0, h)),
                  pl.BlockSpec((1, LANES), lambda b, h, i: (0, 0)),
                  pl.BlockSpec((8, LANES), lambda b, h, i: (0, 0))],
        out_specs=pl.BlockSpec((1, tq, LANES), lambda b, h, i: (b, i, h)),
        out_shape=jax.ShapeDtypeStruct((B, Lq, ATT_W), F32),
        compiler_params=_params(3),
        name="diff_attention",
    )(qp, qr, k, v_ext, g_scaled, lam_arr)


def _moe_kernel(be_ref, nb_ref, x_ref, w1_ref, w3_ref, w2_ref, o_ref):
    del be_ref
    i = pl.program_id(0)

    @pl.when(i < nb_ref[0])
    def _():
        x = x_ref[...]
        a = jnp.dot(x, w1_ref[0], preferred_element_type=F32)
        b = jnp.dot(x, w3_ref[0], preferred_element_type=F32)
        h = (a * jax.nn.sigmoid(a)) * b
        o_ref[...] = jnp.dot(h.astype(BF16), w2_ref[0], preferred_element_type=F32)

    @pl.when(i >= nb_ref[0])
    def _():
        o_ref[...] = jnp.zeros_like(o_ref)


def pl_moe_ffn(xb, block_e, n_used, w1, w3, w2):
    n_pad, D = xb.shape
    n_blocks = n_pad // MOE_BLOCK
    F = w1.shape[-1]
    grid_spec = pltpu.PrefetchScalarGridSpec(
        num_scalar_prefetch=2,
        grid=(n_blocks,),
        in_specs=[pl.BlockSpec((MOE_BLOCK, D), lambda i, be, nb: (i, 0)),
                  pl.BlockSpec((1, D, F), lambda i, be, nb: (be[i], 0, 0)),
                  pl.BlockSpec((1, D, F), lambda i, be, nb: (be[i], 0, 0)),
                  pl.BlockSpec((1, F, D), lambda i, be, nb: (be[i], 0, 0))],
        out_specs=pl.BlockSpec((MOE_BLOCK, D), lambda i, be, nb: (i, 0)),
    )
    return pl.pallas_call(
        _moe_kernel,
        grid_spec=grid_spec,
        out_shape=jax.ShapeDtypeStruct((n_pad, D), F32),
        compiler_params=_params(1),
        name="moe_ffn",
    )(block_e, n_used, xb, w1, w3, w2)


S5_STATES = S5_GROUPS * 64
S5_CHUNK = 64
SUBLANES = 8


def _s5_kernel(u_ref, wd_ref, wr_ref, ar_ref, ai_ref, d_ref, y_ref, x_scr, h_scr, hr_scr, hi_scr, *, reverse):
    ns = S5_STATES

    @pl.when(pl.program_id(0) == 0)
    def _():
        hr_scr[...] = jnp.zeros_like(hr_scr)
        hi_scr[...] = jnp.zeros_like(hi_scr)

    u = u_ref[...]
    x_scr[...] = jnp.dot(u.astype(BF16), wd_ref[...], preferred_element_type=F32)
    ar = ar_ref[...]
    ai = ai_ref[...]

    def step(hr, hi, t):
        r = pl.multiple_of(t * SUBLANES, SUBLANES)
        xr = x_scr[pl.ds(r, SUBLANES), :ns]
        xi = x_scr[pl.ds(r, SUBLANES), ns:]
        return ar * hr - ai * hi + xr, ar * hi + ai * hr + xi

    def body(j, carry):
        hr, hi = carry
        t0 = (S5_CHUNK - 1 - 2 * j) if reverse else 2 * j
        t1 = t0 - 1 if reverse else t0 + 1
        hr0, hi0 = step(hr, hi, t0)
        hr1, hi1 = step(hr0, hi0, t1)
        lo, hi_t = (t1, t0) if reverse else (t0, t1)
        first_r, second_r = (hr1, hr0) if reverse else (hr0, hr1)
        first_i, second_i = (hi1, hi0) if reverse else (hi0, hi1)
        del hi_t
        r = pl.multiple_of(lo * SUBLANES, 2 * SUBLANES)
        h_scr[pl.ds(r, 2 * SUBLANES), :ns] = jnp.concatenate([first_r, second_r], axis=0).astype(BF16)
        h_scr[pl.ds(r, 2 * SUBLANES), ns:] = jnp.concatenate([first_i, second_i], axis=0).astype(BF16)
        return hr1, hi1

    hr, hi = lax.fori_loop(0, S5_CHUNK // 2, body, (hr_scr[...], hi_scr[...]))
    hr_scr[...] = hr
    hi_scr[...] = hi
    y = jnp.dot(h_scr[...], wr_ref[...], preferred_element_type=F32)
    if not reverse:
        y = y + u * d_ref[...]
    y_ref[...] = y


def pl_s5_scan(u_tb, w_drive, w_read, a_re, a_im, d_skip, *, n_ctx_steps, batch, reverse):
    assert batch == SUBLANES
    rows, ch = u_tb.shape
    rc = S5_CHUNK * SUBLANES
    n_chunks = rows // rc
    n_ctx = n_ctx_steps // S5_CHUNK
    assert rows % rc == 0 and n_ctx_steps % S5_CHUNK == 0
    if reverse:
        def idx(i):
            return (jnp.where(i < n_ctx, n_ctx - 1 - i, n_chunks - 1 + n_ctx - i), 0)
    else:
        def idx(i):
            return (i, 0)
    const = lambda i: (0, 0)
    ns2 = 2 * S5_STATES
    return pl.pallas_call(
        functools.partial(_s5_kernel, reverse=reverse),
        grid=(n_chunks,),
        in_specs=[pl.BlockSpec((rc, ch), idx),
                  pl.BlockSpec((ch, ns2), const),
                  pl.BlockSpec((ns2, ch), const),
                  pl.BlockSpec((SUBLANES, S5_STATES), const),
                  pl.BlockSpec((SUBLANES, S5_STATES), const),
                  pl.BlockSpec((1, ch), const)],
        out_specs=pl.BlockSpec((rc, ch), idx),
        out_shape=jax.ShapeDtypeStruct((rows, ch), F32),
        scratch_shapes=[pltpu.VMEM((rc, ns2), F32), pltpu.VMEM((rc, ns2), BF16),
                        pltpu.VMEM((SUBLANES, S5_STATES), F32), pltpu.VMEM((SUBLANES, S5_STATES), F32)],
        compiler_params=_params(1),
        name="s5_scan_rev" if reverse else "s5_scan_fwd",
    )(u_tb, w_drive.astype(BF16), w_read.astype(BF16),
      jnp.broadcast_to(a_re[None, :], (SUBLANES, S5_STATES)),
      jnp.broadcast_to(a_im[None, :], (SUBLANES, S5_STATES)),
      d_skip.reshape(1, ch))


FFT_N2 = 128
HYENA_VMEM_LIMIT = 56 * 1024 * 1024


def _fft_tables(L):
    N = 2 * L
    N1 = N // FFT_N2
    k1 = np.arange(N1)[:, None]
    n1 = np.arange(N1 // 2)[None, :]
    n2 = np.arange(FFT_N2)[:, None, None]
    ang = -2.0 * np.pi * (k1[None] * (n2 + FFT_N2 * n1[None])) / N
    mr, mi = np.cos(ang), np.sin(ang)
    ma = np.concatenate([np.concatenate([mr, -mi], axis=2), np.concatenate([mi, mr], axis=2)], axis=1)
    gr, gi = np.transpose(mr, (0, 2, 1)), -np.transpose(mi, (0, 2, 1))
    mainv = np.concatenate([np.concatenate([gr, -gi], axis=2), np.concatenate([gi, gr], axis=2)], axis=1)
    kk = np.arange(FFT_N2)
    a2 = -2.0 * np.pi * np.outer(kk, kk) / FFT_N2
    fr, fi = np.cos(a2), np.sin(a2)
    f_fwd = np.block([[fr, -fi], [fi, fr]])
    f_inv = np.block([[fr, fi], [-fi, fr]])
    return (jnp.asarray(ma, BF16), jnp.asarray(mainv, BF16), jnp.asarray(f_fwd, BF16), jnp.asarray(f_inv, BF16))


def _hyena_fft_kernel(z_ref, h_ref, ma_ref, mainv_ref, ff_ref, fi_ref, o_ref, a_scr, b_scr, *, n1_count):
    half = n1_count // 2
    n2c = FFT_N2

    def stage_a(n2, c):
        xr = z_ref[0, pl.ds(n2, half, stride=n2c), :]
        xi = z_ref[1, pl.ds(n2, half, stride=n2c), :]
        x = jnp.concatenate([xr, xi], axis=0).astype(BF16)
        r = jnp.dot(ma_ref[n2], x, preferred_element_type=F32)
        a_scr[pl.ds(pl.multiple_of(n2 * 2 * n1_count, 2 * n1_count), 2 * n1_count), :] = r
        return c

    lax.fori_loop(0, n2c, stage_a, 0)

    def stage_c(k1, c):
        ar = a_scr[pl.ds(k1, n2c, stride=2 * n1_count), :]
        ai = a_scr[pl.ds(n1_count + k1, n2c, stride=2 * n1_count), :]
        x = jnp.concatenate([ar, ai], axis=0).astype(BF16)
        y = jnp.dot(ff_ref[...], x, preferred_element_type=F32)
        yr, yi = y[:n2c], y[n2c:]
        hr = h_ref[0, k1].astype(F32)
        hi = h_ref[1, k1].astype(F32)
        x2 = jnp.concatenate([yr * hr - yi * hi, yr * hi + yi * hr], axis=0).astype(BF16)
        b = jnp.dot(fi_ref[...], x2, preferred_element_type=F32)
        b_scr[pl.ds(pl.multiple_of(k1 * 2 * n2c, 2 * n2c), 2 * n2c), :] = b
        return c

    lax.fori_loop(0, n1_count, stage_c, 0)

    def stage_a_inv(n2, c):
        br = b_scr[pl.ds(n2, n1_count, stride=2 * n2c), :]
        bi = b_scr[pl.ds(n2c + n2, n1_count, stride=2 * n2c), :]
        x = jnp.concatenate([br, bi], axis=0).astype(BF16)
        r = jnp.dot(mainv_ref[n2], x, preferred_element_type=F32)
        o_ref[0, pl.ds(n2, half, stride=n2c), :] = r[:half]
        o_ref[1, pl.ds(n2, half, stride=n2c), :] = r[half:]
        return c

    lax.fori_loop(0, n2c, stage_a_inv, 0)


def pl_hyena_conv(z, filt):
    B, L, C = z.shape
    N = 2 * L
    N1 = N // FFT_N2
    hf = jnp.fft.fft(filt, axis=0) / N
    h2 = hf.reshape(FFT_N2, N1, C).transpose(1, 0, 2)
    h = jnp.stack([h2.real, h2.imag]).astype(BF16)
    ma, mainv, f_fwd, f_inv = _fft_tables(L)
    cw = LANES
    const3 = lambda ct, bp: (0, 0, 0)
    const2 = lambda ct, bp: (0, 0)
    return pl.pallas_call(
        functools.partial(_hyena_fft_kernel, n1_count=N1),
        grid=(C // cw, B // 2),
        in_specs=[pl.BlockSpec((2, L, cw), lambda ct, bp: (bp, 0, ct)),
                  pl.BlockSpec((2, N1, FFT_N2, cw), lambda ct, bp: (0, 0, 0, ct)),
                  pl.BlockSpec(ma.shape, const3), pl.BlockSpec(mainv.shape, const3),
                  pl.BlockSpec(f_fwd.shape, const2), pl.BlockSpec(f_inv.shape, const2)],
        out_specs=pl.BlockSpec((2, L, cw), lambda ct, bp: (bp, 0, ct)),
        out_shape=jax.ShapeDtypeStruct((B, L, C), F32),
        scratch_shapes=[pltpu.VMEM((FFT_N2 * 2 * N1, cw), F32), pltpu.VMEM((N1 * 2 * FFT_N2, cw), F32)],
        compiler_params=pltpu.CompilerParams(dimension_semantics=("arbitrary", "arbitrary"),
                                             vmem_limit_bytes=HYENA_VMEM_LIMIT),
        name="hyena_fft_conv",
    )(z, h, ma, mainv, f_fwd, f_inv)


def rms_norm(x, g):
    return x * lax.rsqrt(jnp.mean(x * x, axis=-1, keepdims=True) + EPS) * g


def modulate(h, shift, scale):
    return h * (1.0 + scale) + shift


def short_conv(u, w, b):
    up = jnp.pad(u, ((0, 0), (1, 1), (0, 0)))
    return up[:, :-2] * w[0] + up[:, 1:-1] * w[1] + up[:, 2:] * w[2] + b


def hyena_filter(L, w1, b1, w2, b2, w3, freq):
    t = jnp.arange(L, dtype=F32) / L
    ang = (2.0 * math.pi) * t[:, None] * jnp.arange(1, HY_BANDS + 1, dtype=F32)
    feat = jnp.concatenate([t[:, None], jnp.cos(ang), jnp.sin(ang)], axis=-1)
    hp = lax.Precision.HIGHEST
    h = jnp.sin(freq * (jnp.dot(feat, w1, precision=hp) + b1))
    h = jnp.sin(freq * (jnp.dot(h, w2, precision=hp) + b2))
    h = jnp.dot(h, w3, precision=hp).reshape(L, 2, HY_CH)
    window = jnp.exp(-t[:, None] * jnp.linspace(HY_DECAY_MIN, HY_DECAY_MAX, HY_CH, dtype=F32))
    h = h * window[:, None, :]
    filt = jnp.concatenate([h[:, 0], jnp.zeros((1, HY_CH), F32), h[:0:-1, 1]], axis=0)
    return filt / (jnp.sum(jnp.abs(filt), axis=0, keepdims=True) + EPS)


def hyena_mixer(p, conv_w, conv_b, filt_params, skip, norm_g):
    L = p.shape[1]
    u = short_conv(p, conv_w, conv_b)
    x0, x1, v = jnp.split(u, 3, axis=-1)
    z = x1 * v
    filt = hyena_filter(L, *filt_params)
    if (2 * L) // FFT_N2 >= 16:
        y = pl_hyena_conv(z, filt)
    else:
        zf = jnp.fft.rfft(z, n=2 * L, axis=1)
        ff = jnp.fft.rfft(filt, n=2 * L, axis=0)
        y = jnp.fft.irfft(zf * ff[None], n=2 * L, axis=1)[:, :L]
    return rms_norm(x0 * (y + skip * z), norm_g)


def _block_diag(blocks):
    G, r, c = blocks.shape
    eye = jnp.eye(G, dtype=blocks.dtype)
    return (eye[:, None, :, None] * blocks[:, :, None, :]).reshape(G * r, G * c)


def s5_mixer(u_ctx, u_lat, a_re, a_im, log_dt, b_re, b_im, c_re, c_im, d_skip, glu_w, norm_g):
    B, Lc, _ = u_ctx.shape
    L = u_lat.shape[1]
    Lt = Lc + L
    u_tb = jnp.concatenate([u_ctx, u_lat], axis=1).transpose(1, 0, 2).reshape(Lt * B, S5_CH)
    y_tb = None
    for direction in range(2):
        A = lax.complex(a_re[direction], a_im[direction])
        dtA = jnp.exp(log_dt[direction])[:, None] * A
        a_bar = jnp.exp(dtA)
        b_bar = ((a_bar - 1.0) / A)[:, :, None] * lax.complex(b_re[direction], b_im[direction])
        bt_re = jnp.transpose(b_bar.real, (0, 2, 1))
        bt_im = jnp.transpose(b_bar.imag, (0, 2, 1))
        w_drive = jnp.concatenate([_block_diag(bt_re), _block_diag(bt_im)], axis=1)
        ct_re = jnp.transpose(c_re[direction], (0, 2, 1))
        ct_im = jnp.transpose(c_im[direction], (0, 2, 1))
        w_read = jnp.concatenate([_block_diag(ct_re), -_block_diag(ct_im)], axis=0)
        y_dir = pl_s5_scan(u_tb, w_drive, w_read, a_bar.real.reshape(-1), a_bar.imag.reshape(-1), d_skip,
                           n_ctx_steps=Lc, batch=B, reverse=direction == 1)
        y_tb = y_dir if y_tb is None else y_tb + y_dir
    y = y_tb.reshape(Lt, B, S5_CH).transpose(1, 0, 2)

    def glu(yy, n):
        g = jax.nn.gelu(yy.reshape(B * n, S5_CH))
        return rms_norm(g * jax.nn.sigmoid(pl_matmul(g, glu_w)), norm_g).reshape(B, n, S5_CH)

    return glu(y[:, Lc:], L), glu(y[:, :Lc], Lc)


def axial_rope(L):
    rows = L // GRID_W
    row = jnp.repeat(jnp.arange(rows, dtype=F32), GRID_W)
    col = jnp.tile(jnp.arange(GRID_W, dtype=F32), rows)
    inv = ROPE_BASE ** (-jnp.arange(ROPE_PAIRS_AXIS, dtype=F32) / ROPE_PAIRS_AXIS)
    ang = jnp.concatenate([row[:, None] * inv, col[:, None] * inv], axis=-1)
    return jnp.cos(ang), jnp.sin(ang)


def apply_rope(t, cos, sin):
    B, L, W = t.shape
    t4 = t.reshape(B, L, W // ATT_HEAD_DIM, ATT_HEAD_DIM)
    t1, t2 = t4[..., :ROPE_HALF], t4[..., ROPE_HALF:]
    c = cos[None, :, None, :]
    s = sin[None, :, None, :]
    return jnp.concatenate([t1 * c - t2 * s, t1 * s + t2 * c], axis=-1).reshape(B, L, W)


def diff_attention(ql, kl, vl, qc, kc, vc, cos, sin, lam, lam_init, subln_g):
    scale = ATT_HEAD_DIM ** -0.5 * math.log2(math.e)
    B, Lc, _ = qc.shape
    g_scaled = (subln_g * (1.0 - lam_init)).reshape(1, ATT_V_DIM)
    ql_p = (ql * scale).astype(BF16)
    ql_r = (apply_rope(ql, cos, sin) * scale).astype(BF16)
    k_all = jnp.concatenate([kc, apply_rope(kl, cos, sin)], axis=1).astype(BF16)
    v_all = jnp.concatenate([vc, vl], axis=1).astype(BF16).reshape(B, -1, ATT_HEADS, ATT_V_DIM)
    v_ext = jnp.concatenate([v_all, jnp.ones_like(v_all)], axis=-1).reshape(B, -1, 2 * ATT_W)
    o_lat = pl_diff_attention(ql_p, ql_r, k_all, v_ext, g_scaled, lam, Lc)
    qc_p = (qc * scale).astype(BF16)
    o_ctx = pl_diff_attention(qc_p, qc_p, k_all[:, :Lc], v_ext[:, :Lc], g_scaled, lam, Lc)
    return o_lat, o_ctx


def hier_moe(h, w_g, b_g, w_e, b_e, w1, w3, w2):
    T, D = h.shape
    w_r = jnp.zeros((D, LANES), F32).at[:, :MOE_GROUPS].set(w_g).at[:, MOE_GROUPS:MOE_GROUPS + N_EXPERTS].set(w_e)
    logits = pl_matmul_hi(h, w_r)
    g_logits = logits[:, :MOE_GROUPS] + b_g
    g_idx = jnp.argmax(g_logits, axis=-1)
    p_group = jnp.take_along_axis(jax.nn.softmax(g_logits, axis=-1), g_idx[:, None], axis=1)
    e_logits = (logits[:, MOE_GROUPS:MOE_GROUPS + N_EXPERTS] + b_e).reshape(T, MOE_GROUPS, MOE_EPG)
    e_logits = jnp.take_along_axis(e_logits, g_idx[:, None, None], axis=1)[:, 0]
    top_p, top_i = lax.top_k(jax.nn.softmax(e_logits, axis=-1), MOE_TOP_K)
    gate = p_group * top_p / jnp.sum(top_p, axis=-1, keepdims=True)
    expert = (g_idx[:, None] * MOE_EPG + top_i).reshape(-1).astype(jnp.int32)
    tok = jnp.repeat(jnp.arange(T, dtype=jnp.int32), MOE_TOP_K)
    n_assign = T * MOE_TOP_K
    n_blocks = -(-n_assign // MOE_BLOCK) + N_EXPERTS
    n_pad = n_blocks * MOE_BLOCK
    order = jnp.argsort(expert)
    onehot = (expert[:, None] == jnp.arange(N_EXPERTS, dtype=jnp.int32)[None, :]).astype(jnp.int32)
    csum = jnp.cumsum(onehot, axis=0)
    counts = csum[-1]
    rank = jnp.take_along_axis(csum, expert[:, None], axis=1)[:, 0] - 1
    start = jnp.cumsum(counts) - counts
    padded = (counts + MOE_BLOCK - 1) // MOE_BLOCK * MOE_BLOCK
    pad_end = jnp.cumsum(padded)
    pad_start = pad_end - padded
    slot_of_assign = (pad_start[expert] + rank).astype(jnp.int32)
    block_e = jnp.minimum(jnp.searchsorted(pad_end, jnp.arange(n_blocks) * MOE_BLOCK, side='right'),
                          N_EXPERTS - 1).astype(jnp.int32)
    slot_e = jnp.repeat(block_e, MOE_BLOCK)
    slot_r = jnp.arange(n_pad, dtype=jnp.int32) - pad_start[slot_e]
    slot_valid = (slot_r < counts[slot_e]) & (jnp.arange(n_pad) < pad_end[-1])
    sorted_pos = jnp.clip(start[slot_e] + slot_r, 0, n_assign - 1)
    slot_tok = jnp.where(slot_valid, tok[order[sorted_pos]], T).astype(jnp.int32)
    n_used = (pad_end[-1:] // MOE_BLOCK).astype(jnp.int32)
    h_pad = jnp.concatenate([h.astype(BF16), jnp.zeros((1, D), BF16)], axis=0)
    xb = h_pad[slot_tok]
    yb = pl_moe_ffn(xb, block_e, n_used, w1.astype(BF16), w3.astype(BF16), w2.astype(BF16))
    ys = yb[slot_of_assign].reshape(T, MOE_TOP_K, D)
    return jnp.sum(ys * gate[:, :, None], axis=1)


def kernel(x, c, ctx, c_ctx, w_mod, b_mod, norm1_g, norm2_g, final_g, w_in, w_out, hy_conv_w, hy_conv_b, hy_ffn_w1, hy_ffn_b1, hy_ffn_w2, hy_ffn_b2, hy_ffn_w3, hy_freq, hy_skip, hy_norm_g, s5_a_re, s5_a_im, s5_log_dt, s5_b_re, s5_b_im, s5_c_re, s5_c_im, s5_d, s5_glu_w, s5_norm_g, att_lq1, att_lk1, att_lq2, att_lk2, att_subln_g, moe_wg, moe_bg, moe_we, moe_be, moe_w1, moe_w3, moe_w2):
    B, L, D = x.shape
    Lc = ctx.shape[1]
    cos, sin = axial_rope(L)
    silu_c = jax.nn.silu(c)
    silu_cc = jax.nn.silu(c_ctx)
    hp = lax.Precision.HIGHEST
    for l in range(DEPTH):
        lam_init = 0.8 - 0.6 * math.exp(-0.3 * l)
        mod = jnp.dot(silu_c, w_mod[l], precision=hp) + b_mod[l]
        sh1, sc1, g1, sh2, sc2, g2 = jnp.split(mod[:, None, :], N_MOD, axis=-1)
        cmod = jnp.dot(silu_cc, w_mod[l], precision=hp) + b_mod[l]
        csh1, csc1, cg1, csh2, csc2, cg2 = jnp.split(cmod, N_MOD, axis=-1)

        h_lat = modulate(rms_norm(x, norm1_g[l]), sh1, sc1).reshape(B * L, D)
        h_ctx = modulate(rms_norm(ctx, norm1_g[l]), csh1, csc1).reshape(B * Lc, D)
        p_lat = pl_matmul(h_lat, w_in[l]).reshape(B, L, -1)
        p_ctx = pl_matmul(h_ctx, w_in[l]).reshape(B, Lc, -1)
        hy_l, s5_l, q_l, k_l, v_l = jnp.split(p_lat, IN_SPLITS, axis=-1)
        hy_c, s5_c, q_c, k_c, v_c = jnp.split(p_ctx, IN_SPLITS, axis=-1)

        filt_params = (hy_ffn_w1[l], hy_ffn_b1[l], hy_ffn_w2[l], hy_ffn_b2[l], hy_ffn_w3[l], hy_freq[l])
        hy_lat = hyena_mixer(hy_l, hy_conv_w[l], hy_conv_b[l], filt_params, hy_skip[l], hy_norm_g[l])
        hy_ctx = hyena_mixer(hy_c, hy_conv_w[l], hy_conv_b[l], filt_params, hy_skip[l], hy_norm_g[l])

        s5_lat, s5_ctx = s5_mixer(s5_c, s5_l, s5_a_re[l], s5_a_im[l], s5_log_dt[l], s5_b_re[l], s5_b_im[l],
                                  s5_c_re[l], s5_c_im[l], s5_d[l], s5_glu_w[l], s5_norm_g[l])

        lam = (jnp.exp(jnp.sum(att_lq1[l] * att_lk1[l])) - jnp.exp(jnp.sum(att_lq2[l] * att_lk2[l])) + lam_init)
        att_lat, att_ctx = diff_attention(q_l, k_l, v_l, q_c, k_c, v_c, cos, sin, lam, lam_init, att_subln_g[l])

        mix_lat = jnp.concatenate([hy_lat, s5_lat, att_lat], axis=-1).reshape(B * L, D)
        mix_ctx = jnp.concatenate([hy_ctx, s5_ctx, att_ctx], axis=-1).reshape(B * Lc, D)
        x = x + g1 * pl_matmul(mix_lat, w_out[l]).reshape(B, L, D)
        ctx = ctx + cg1 * pl_matmul(mix_ctx, w_out[l]).reshape(B, Lc, D)

        moe_params = (moe_wg[l], moe_bg[l], moe_we[l], moe_be[l], moe_w1[l], moe_w3[l], moe_w2[l])
        h_lat = modulate(rms_norm(x, norm2_g[l]), sh2, sc2).reshape(B * L, D)
        h_ctx = modulate(rms_norm(ctx, norm2_g[l]), csh2, csc2).reshape(B * Lc, D)
        y = hier_moe(jnp.concatenate([h_lat, h_ctx], axis=0), *moe_params)
        x = x + g2 * y[:B * L].reshape(B, L, D)
        ctx = ctx + cg2 * y[B * L:].reshape(B, Lc, D)
    return rms_norm(x, final_g)
```

```python
import functools
import math

import jax
import jax.numpy as jnp
import numpy as np
from jax import lax
from jax.experimental import pallas as pl
from jax.experimental.pallas import tpu as pltpu

D_MODEL = 1024
DEPTH = 4
GRID_W = 64
N_MOD = 6
EPS = 1e-6
HY_CH = D_MODEL // 4
S5_CH = D_MODEL // 4
ATT_W = D_MODEL // 2
HY_BANDS = 16
HY_DECAY_MIN = -math.log(1e-2) / 1.5
HY_DECAY_MAX = -math.log(1e-2) / 0.3
S5_GROUP = 16
S5_GROUPS = S5_CH // S5_GROUP
S5_STATE = 64
ATT_HEAD_DIM = 64
ATT_HEADS = ATT_W // (2 * ATT_HEAD_DIM)
ATT_V_DIM = 2 * ATT_HEAD_DIM
ROPE_HALF = ATT_HEAD_DIM // 2
ROPE_PAIRS_AXIS = ROPE_HALF // 2
ROPE_BASE = 10000.0
MOE_GROUPS = 4
MOE_EPG = 8
N_EXPERTS = MOE_GROUPS * MOE_EPG
MOE_TOP_K = 2
MOE_BLOCK = 256
IN_COLS = 3 * HY_CH + S5_CH + 3 * ATT_W
COL_S5 = 3 * HY_CH
COL_Q = COL_S5 + S5_CH
COL_K = COL_Q + ATT_W
COL_V = COL_K + ATT_W

LANES = 128
SUBLANES = 8
VMEM_LIMIT = 48 * 1024 * 1024
TOKEN_TILE = 256

F32 = jnp.float32
BF16 = jnp.bfloat16


def _params(n_axes, vmem=VMEM_LIMIT):
    return pltpu.CompilerParams(dimension_semantics=("arbitrary",) * n_axes, vmem_limit_bytes=vmem)


def _rms(x):
    return x * lax.rsqrt(jnp.mean(x * x, axis=-1, keepdims=True) + EPS)


def _in_kernel(x_ref, xp_ref, xn_ref, mod_ref, g_ref, w_ref, cw_ref, cb_ref, cos_ref, sin_ref,
               u0_ref, z_ref, s5_ref, qp_ref, qr_ref, k_ref, v_ref, *, n_lat_tiles, q_scale):
    i = pl.program_id(1)
    tm = x_ref.shape[1]
    d = x_ref.shape[2]
    g = g_ref[...]
    shift = mod_ref[0, :, 0:d]
    scale = mod_ref[0, :, d:2 * d]

    def norm_mod(xt):
        return (_rms(xt) * g) * (1.0 + scale) + shift

    h = jnp.concatenate([norm_mod(x_ref[0]), norm_mod(xp_ref[0]), norm_mod(xn_ref[0])], axis=0).astype(BF16)
    p = jnp.dot(h, w_ref[...], preferred_element_type=F32)

    hy = p[:tm, :COL_S5]
    is_ctx = i == n_lat_tiles
    has_prev = jnp.logical_and(i != 0, jnp.logical_not(is_ctx))
    has_next = jnp.logical_and(i != n_lat_tiles - 1, jnp.logical_not(is_ctx))
    prev_row = jnp.where(has_prev, p[tm + SUBLANES - 1:tm + SUBLANES, :COL_S5], 0.0)
    next_row = jnp.where(has_next, p[tm + SUBLANES:tm + SUBLANES + 1, :COL_S5], 0.0)
    rows = lax.broadcasted_iota(jnp.int32, (tm, 1), 0)
    up = jnp.where(rows == 0, prev_row, pltpu.roll(hy, 1, axis=0))
    dn = jnp.where(rows == tm - 1, next_row, pltpu.roll(hy, tm - 1, axis=0))
    u = up * cw_ref[0:1, :] + hy * cw_ref[1:2, :] + dn * cw_ref[2:3, :] + cb_ref[...]
    u0_ref[0] = u[:, :HY_CH]
    z_ref[0] = u[:, HY_CH:2 * HY_CH] * u[:, 2 * HY_CH:]

    s5_ref[...] = p[:tm, COL_S5:COL_Q]

    lane = lax.broadcasted_iota(jnp.int32, (1, ATT_W), 1)
    first_half = jnp.bitwise_and(lane, ATT_HEAD_DIM - 1) < ROPE_HALF
    cos = cos_ref[...]
    sin = sin_ref[...]

    def rope(t):
        partner = jnp.where(first_half, pltpu.roll(t, ATT_W - ROPE_HALF, axis=1), pltpu.roll(t, ROPE_HALF, axis=1))
        return t * cos + partner * sin

    q = p[:tm, COL_Q:COL_K] * q_scale
    qp_ref[0] = q.astype(BF16)
    qr_ref[0] = rope(q).astype(BF16)
    k_ref[0] = rope(p[:tm, COL_K:COL_V]).astype(BF16)
    v_ref[0] = p[:tm, COL_V:].astype(BF16)


def pl_in_proj(xs, modv, norm_g, w_in, conv_w, conv_b, cosf, sinf, n_lat_tiles, q_scale):
    B, Lt, D = xs.shape
    tm = TOKEN_TILE
    n_tiles = Lt // tm
    halo_per_tile = tm // SUBLANES
    n_halo_blocks = Lt // SUBLANES
    tok = lambda w: pl.BlockSpec((1, tm, w), lambda b, i: (b, i, 0))
    const = lambda shape: pl.BlockSpec(shape, lambda b, i: (0,) * len(shape))
    out_shapes = (jax.ShapeDtypeStruct((B, Lt, HY_CH), F32), jax.ShapeDtypeStruct((B, Lt, HY_CH), F32),
                  jax.ShapeDtypeStruct((Lt, B * S5_CH), F32),
                  jax.ShapeDtypeStruct((B, Lt, ATT_W), BF16), jax.ShapeDtypeStruct((B, Lt, ATT_W), BF16),
                  jax.ShapeDtypeStruct((B, Lt, ATT_W), BF16), jax.ShapeDtypeStruct((B, Lt, ATT_W), BF16))
    return pl.pallas_call(
        functools.partial(_in_kernel, n_lat_tiles=n_lat_tiles, q_scale=q_scale),
        grid=(B, n_tiles),
        in_specs=[tok(D),
                  pl.BlockSpec((1, SUBLANES, D), lambda b, i: (b, jnp.maximum(i * halo_per_tile - 1, 0), 0)),
                  pl.BlockSpec((1, SUBLANES, D),
                               lambda b, i: (b, jnp.minimum((i + 1) * halo_per_tile, n_halo_blocks - 1), 0)),
                  pl.BlockSpec((1, 1, N_MOD * D), lambda b, i: (2 * b + jnp.where(i >= n_lat_tiles, 1, 0), 0, 0)),
                  const((1, D)), const((D, IN_COLS)), const((3, COL_S5)), const((1, COL_S5)),
                  pl.BlockSpec((tm, ATT_W), lambda b, i: (i, 0)), pl.BlockSpec((tm, ATT_W), lambda b, i: (i, 0))],
        out_specs=(tok(HY_CH), tok(HY_CH), pl.BlockSpec((tm, S5_CH), lambda b, i: (i, b)),
                   tok(ATT_W), tok(ATT_W), tok(ATT_W), tok(ATT_W)),
        out_shape=out_shapes,
        compiler_params=_params(2),
        name="in_proj",
    )(xs, xs, xs, modv, norm_g.reshape(1, D), w_in.astype(BF16), conv_w, conv_b.reshape(1, COL_S5), cosf, sinf)


def _out_kernel(x_ref, u0_ref, z_ref, y_ref, sf_ref, sb_ref, att_ref, mod_ref, skip_ref, hg_ref, gw_ref, sg_ref,
                wo_ref, n2g_ref, rwh_ref, rwl_ref, xo_ref, h2_ref, lg_ref):
    d = x_ref.shape[2]
    hy = _rms(u0_ref[0] * (y_ref[0] + skip_ref[...] * z_ref[0])) * hg_ref[...]
    gl = jax.nn.gelu(sf_ref[...] + sb_ref[...])
    gate = jax.nn.sigmoid(jnp.dot(gl.astype(BF16), gw_ref[...], preferred_element_type=F32))
    s5 = _rms(gl * gate) * sg_ref[...]
    mix = jnp.concatenate([hy, s5, att_ref[0]], axis=1).astype(BF16)
    proj = jnp.dot(mix, wo_ref[...], preferred_element_type=F32)
    xn = x_ref[0] + mod_ref[0, :, 2 * d:3 * d] * proj
    xo_ref[0] = xn
    h2 = (_rms(xn) * n2g_ref[...]) * (1.0 + mod_ref[0, :, 4 * d:5 * d]) + mod_ref[0, :, 3 * d:4 * d]
    hh = h2.astype(BF16)
    hl = (h2 - hh.astype(F32)).astype(BF16)
    h2_ref[0] = hh
    lg = jnp.dot(hh, rwh_ref[...], preferred_element_type=F32)
    lg += jnp.dot(hl, rwh_ref[...], preferred_element_type=F32)
    lg += jnp.dot(hh, rwl_ref[...], preferred_element_type=F32)
    lg_ref[0] = lg


def pl_out_proj(xs, u0, z, y_hy, s5_f, s5_b, att, modv, hy_skip, hy_norm_g, glu_w, s5_norm_g, w_out, norm2_g,
                w_router, n_lat_tiles):
    B, Lt, D = xs.shape
    tm = TOKEN_TILE
    tok = lambda w: pl.BlockSpec((1, tm, w), lambda b, i: (b, i, 0))
    tb = pl.BlockSpec((tm, S5_CH), lambda b, i: (i, b))
    const = lambda shape: pl.BlockSpec(shape, lambda b, i: (0,) * len(shape))
    rwh = w_router.astype(BF16)
    rwl = (w_router - rwh.astype(F32)).astype(BF16)
    return pl.pallas_call(
        _out_kernel,
        grid=(B, Lt // tm),
        in_specs=[tok(D), tok(HY_CH), tok(HY_CH), tok(HY_CH), tb, tb, tok(ATT_W),
                  pl.BlockSpec((1, 1, N_MOD * D), lambda b, i: (2 * b + jnp.where(i >= n_lat_tiles, 1, 0), 0, 0)),
                  const((1, HY_CH)), const((1, HY_CH)), const((S5_CH, S5_CH)), const((1, S5_CH)),
                  const((D, D)), const((1, D)), const((D, LANES)), const((D, LANES))],
        out_specs=(tok(D), tok(D), tok(LANES)),
        out_shape=(jax.ShapeDtypeStruct((B, Lt, D), F32), jax.ShapeDtypeStruct((B, Lt, D), BF16),
                   jax.ShapeDtypeStruct((B, Lt, LANES), F32)),
        compiler_params=_params(2),
        name="out_proj",
    )(xs, u0, z, y_hy, s5_f, s5_b, att, modv, hy_skip.reshape(1, HY_CH), hy_norm_g.reshape(1, HY_CH),
      glu_w.astype(BF16), s5_norm_g.reshape(1, S5_CH), w_out.astype(BF16), norm2_g.reshape(1, D), rwh, rwl)


def _dot_nt(a, b):
    return lax.dot_general(a, b, (((1,), (1,)), ((), ())), preferred_element_type=F32)


ATT_TQ = 512
ATT_SUB = 256


def _attn_kernel(qp_ref, qr_ref, k_ref, v_ref, g_ref, lam_ref, *rest, n_lat):
    o_ref = rest[-1]
    tq = qp_ref.shape[1]
    first_map = lax.broadcasted_iota(jnp.int32, (1, LANES), 1) < ATT_HEAD_DIM
    zero = jnp.zeros((), BF16)
    sub = min(ATT_SUB, tq)
    for r0 in range(0, tq, sub):
        qp = qp_ref[0, r0:r0 + sub, :]
        qr = qr_ref[0, r0:r0 + sub, :]
        probs = []
        for m in range(2):
            in_map = first_map if m == 0 else jnp.logical_not(first_map)
            s_c = _dot_nt(jnp.where(in_map, qp, zero), k_ref[0, n_lat:, :])
            mx = jnp.max(s_c, axis=-1, keepdims=True)
            if n_lat:
                s_l = _dot_nt(jnp.where(in_map, qr, zero), k_ref[0, :n_lat, :])
                mx = jnp.maximum(mx, jnp.max(s_l, axis=-1, keepdims=True))
                p_l = jnp.exp2(s_l - mx)
            p_c = jnp.exp2(s_c - mx)
            den = jnp.sum(p_c, axis=-1, keepdims=True)
            if n_lat:
                den = den + jnp.sum(p_l, axis=-1, keepdims=True)
            probs.append((p_c, p_l if n_lat else None, 1.0 / den))
        w0 = probs[0][2]
        w1 = lam_ref[0:1, 0:1] * probs[1][2]
        a_c = (probs[0][0] * w0 - probs[1][0] * w1).astype(BF16)
        o = jnp.dot(a_c, v_ref[0, n_lat:, :], preferred_element_type=F32)
        if n_lat:
            a_l = (probs[0][1] * w0 - probs[1][1] * w1).astype(BF16)
            o = o + jnp.dot(a_l, v_ref[0, :n_lat, :], preferred_element_type=F32)
        o_ref[0, r0:r0 + sub, :] = _rms(o) * g_ref[...]


def pl_diff_attention(qp, qr, k, v, g_scaled, lam, n_lat):
    B, Lt, _ = qp.shape
    n_ctx = Lt - n_lat
    lam_arr = jnp.full((SUBLANES, LANES), lam, F32)
    small = [pl.BlockSpec((1, LANES), lambda b, h, i: (0, 0)), pl.BlockSpec((SUBLANES, LANES), lambda b, h, i: (0, 0))]
    tq = ATT_TQ
    qspec = pl.BlockSpec((1, tq, LANES), lambda b, h, i: (b, i, h))
    kspec = pl.BlockSpec((1, Lt, LANES), lambda b, h, i: (b, 0, h))
    out_lat = pl.pallas_call(
        functools.partial(_attn_kernel, n_lat=n_lat),
        grid=(B, ATT_HEADS, n_lat // tq),
        in_specs=[qspec, qspec, kspec, kspec] + small,
        out_specs=qspec,
        out_shape=jax.ShapeDtypeStruct((B, Lt, ATT_W), F32),
        compiler_params=_params(3),
        name="diff_attention",
    )(qp, qr, k, v, g_scaled, lam_arr)
    ctx_blk = n_lat // n_ctx
    cspec = pl.BlockSpec((1, n_ctx, LANES), lambda b, h, i: (b, ctx_blk, h))
    return pl.pallas_call(
        functools.partial(_attn_kernel, n_lat=0),
        grid=(B, ATT_HEADS, 1),
        in_specs=[cspec, cspec, cspec, cspec] + small + [pl.BlockSpec(memory_space=pl.ANY)],
        out_specs=cspec,
        out_shape=jax.ShapeDtypeStruct((B, Lt, ATT_W), F32),
        input_output_aliases={6: 0},
        compiler_params=_params(3),
        name="diff_attention_ctx",
    )(qp, qr, k, v, g_scaled, lam_arr, out_lat)


def _moe_kernel(be_ref, nb_ref, x_ref, w1_ref, w3_ref, w2_ref, o_ref):
    del be_ref
    i = pl.program_id(0)

    @pl.when(i < nb_ref[0])
    def _():
        x = x_ref[...]
        a = jnp.dot(x, w1_ref[0], preferred_element_type=F32)
        b = jnp.dot(x, w3_ref[0], preferred_element_type=F32)
        h = (a * jax.nn.sigmoid(a)) * b
        o_ref[...] = jnp.dot(h.astype(BF16), w2_ref[0], preferred_element_type=F32)

    @pl.when(i >= nb_ref[0])
    def _():
        o_ref[...] = jnp.zeros_like(o_ref)


def pl_moe_ffn(xb, block_e, n_used, w1, w3, w2):
    n_pad, D = xb.shape
    n_blocks = n_pad // MOE_BLOCK
    F = w1.shape[-1]
    grid_spec = pltpu.PrefetchScalarGridSpec(
        num_scalar_prefetch=2,
        grid=(n_blocks,),
        in_specs=[pl.BlockSpec((MOE_BLOCK, D), lambda i, be, nb: (i, 0)),
                  pl.BlockSpec((1, D, F), lambda i, be, nb: (be[i], 0, 0)),
                  pl.BlockSpec((1, D, F), lambda i, be, nb: (be[i], 0, 0)),
                  pl.BlockSpec((1, F, D), lambda i, be, nb: (be[i], 0, 0))],
        out_specs=pl.BlockSpec((MOE_BLOCK, D), lambda i, be, nb: (i, 0)),
    )
    return pl.pallas_call(
        _moe_kernel,
        grid_spec=grid_spec,
        out_shape=jax.ShapeDtypeStruct((n_pad, D), F32),
        compiler_params=_params(1),
        name="moe_ffn",
    )(block_e, n_used, xb, w1, w3, w2)


S5_STATES = S5_GROUPS * S5_STATE
S5_CHUNK = 64


def _s5_kernel(u_ref, wd_ref, wr_ref, ar_ref, ai_ref, d_ref, y_ref, x_scr, h_scr, hr_scr, hi_scr, *, reverse):
    ns = S5_STATES

    @pl.when(pl.program_id(0) == 0)
    def _():
        hr_scr[...] = jnp.zeros_like(hr_scr)
        hi_scr[...] = jnp.zeros_like(hi_scr)

    u = u_ref[...]
    x_scr[...] = jnp.dot(u.astype(BF16), wd_ref[...], preferred_element_type=F32)
    ar = ar_ref[...]
    ai = ai_ref[...]

    def step(hr, hi, t):
        r = pl.multiple_of(t * SUBLANES, SUBLANES)
        xr = x_scr[pl.ds(r, SUBLANES), :ns]
        xi = x_scr[pl.ds(r, SUBLANES), ns:]
        return ar * hr - ai * hi + xr, ar * hi + ai * hr + xi

    def body(j, carry):
        hr, hi = carry
        t0 = (S5_CHUNK - 1 - 2 * j) if reverse else 2 * j
        t1 = t0 - 1 if reverse else t0 + 1
        hr0, hi0 = step(hr, hi, t0)
        hr1, hi1 = step(hr0, hi0, t1)
        lo = t1 if reverse else t0
        first_r, second_r = (hr1, hr0) if reverse else (hr0, hr1)
        first_i, second_i = (hi1, hi0) if reverse else (hi0, hi1)
        r = pl.multiple_of(lo * SUBLANES, 2 * SUBLANES)
        h_scr[pl.ds(r, 2 * SUBLANES), :ns] = jnp.concatenate([first_r, second_r], axis=0).astype(BF16)
        h_scr[pl.ds(r, 2 * SUBLANES), ns:] = jnp.concatenate([first_i, second_i], axis=0).astype(BF16)
        return hr1, hi1

    hr, hi = lax.fori_loop(0, S5_CHUNK // 2, body, (hr_scr[...], hi_scr[...]))
    hr_scr[...] = hr
    hi_scr[...] = hi
    y = jnp.dot(h_scr[...], wr_ref[...], preferred_element_type=F32)
    if not reverse:
        y = y + u * d_ref[...]
    y_ref[...] = y


def pl_s5_scan(u_tb, w_drive, w_read, a_re, a_im, d_skip, *, n_lat_steps, reverse):
    rows, ch = u_tb.shape
    rc = S5_CHUNK * SUBLANES
    n_chunks = rows // rc
    n_lat = n_lat_steps // S5_CHUNK
    n_ctx = n_chunks - n_lat
    assert rows % rc == 0 and n_lat_steps % S5_CHUNK == 0
    if reverse:
        def idx(i):
            return (n_chunks - 1 - i, 0)
    else:
        def idx(i):
            return (jnp.where(i < n_ctx, n_lat + i, i - n_ctx), 0)
    const = lambda i: (0, 0)
    ns2 = 2 * S5_STATES
    return pl.pallas_call(
        functools.partial(_s5_kernel, reverse=reverse),
        grid=(n_chunks,),
        in_specs=[pl.BlockSpec((rc, ch), idx),
                  pl.BlockSpec((ch, ns2), const),
                  pl.BlockSpec((ns2, ch), const),
                  pl.BlockSpec((SUBLANES, S5_STATES), const),
                  pl.BlockSpec((SUBLANES, S5_STATES), const),
                  pl.BlockSpec((1, ch), const)],
        out_specs=pl.BlockSpec((rc, ch), idx),
        out_shape=jax.ShapeDtypeStruct((rows, ch), F32),
        scratch_shapes=[pltpu.VMEM((rc, ns2), F32), pltpu.VMEM((rc, ns2), BF16),
                        pltpu.VMEM((SUBLANES, S5_STATES), F32), pltpu.VMEM((SUBLANES, S5_STATES), F32)],
        compiler_params=_params(1),
        name="s5_scan_rev" if reverse else "s5_scan_fwd",
    )(u_tb, w_drive.astype(BF16), w_read.astype(BF16),
      jnp.broadcast_to(a_re[None, :], (SUBLANES, S5_STATES)),
      jnp.broadcast_to(a_im[None, :], (SUBLANES, S5_STATES)),
      d_skip.reshape(1, ch))


FFT_N2 = 128
HYENA_VMEM_LIMIT = 56 * 1024 * 1024


def _fft_tables(L):
    N = 2 * L
    N1 = N // FFT_N2
    k1 = np.arange(N1)[:, None]
    n1 = np.arange(N1 // 2)[None, :]
    n2 = np.arange(FFT_N2)[:, None, None]
    ang = -2.0 * np.pi * (k1[None] * (n2 + FFT_N2 * n1[None])) / N
    mr, mi = np.cos(ang), np.sin(ang)
    ma = np.concatenate([np.concatenate([mr, -mi], axis=2), np.concatenate([mi, mr], axis=2)], axis=1)
    gr, gi = np.transpose(mr, (0, 2, 1)), -np.transpose(mi, (0, 2, 1))
    mainv = np.concatenate([np.concatenate([gr, -gi], axis=2), np.concatenate([gi, gr], axis=2)], axis=1)
    kk = np.arange(FFT_N2)
    a2 = -2.0 * np.pi * np.outer(kk, kk) / FFT_N2
    fr, fi = np.cos(a2), np.sin(a2)
    f_fwd = np.block([[fr, -fi], [fi, fr]])
    f_inv = np.block([[fr, fi], [-fi, fr]])
    return (jnp.asarray(ma, BF16), jnp.asarray(mainv, BF16), jnp.asarray(f_fwd, BF16), jnp.asarray(f_inv, BF16))


def _hyena_fft_kernel(z_ref, h_ref, ma_ref, mainv_ref, ff_ref, fi_ref, o_ref, a_scr, b_scr, *, n1_count):
    half = n1_count // 2
    n2c = FFT_N2

    def stage_a(n2, c):
        xr = z_ref[0, pl.ds(n2, half, stride=n2c), :]
        xi = z_ref[1, pl.ds(n2, half, stride=n2c), :]
        x = jnp.concatenate([xr, xi], axis=0).astype(BF16)
        r = jnp.dot(ma_ref[n2], x, preferred_element_type=F32)
        a_scr[pl.ds(pl.multiple_of(n2 * 2 * n1_count, 2 * n1_count), 2 * n1_count), :] = r
        return c

    lax.fori_loop(0, n2c, stage_a, 0)

    def stage_c(k1, c):
        ar = a_scr[pl.ds(k1, n2c, stride=2 * n1_count), :]
        ai = a_scr[pl.ds(n1_count + k1, n2c, stride=2 * n1_count), :]
        x = jnp.concatenate([ar, ai], axis=0).astype(BF16)
        y = jnp.dot(ff_ref[...], x, preferred_element_type=F32)
        yr, yi = y[:n2c], y[n2c:]
        hr = h_ref[0, k1].astype(F32)
        hi = h_ref[1, k1].astype(F32)
        x2 = jnp.concatenate([yr * hr - yi * hi, yr * hi + yi * hr], axis=0).astype(BF16)
        b = jnp.dot(fi_ref[...], x2, preferred_element_type=F32)
        b_scr[pl.ds(pl.multiple_of(k1 * 2 * n2c, 2 * n2c), 2 * n2c), :] = b
        return c

    lax.fori_loop(0, n1_count, stage_c, 0)

    def stage_a_inv(n2, c):
        br = b_scr[pl.ds(n2, n1_count, stride=2 * n2c), :]
        bi = b_scr[pl.ds(n2c + n2, n1_count, stride=2 * n2c), :]
        x = jnp.concatenate([br, bi], axis=0).astype(BF16)
        r = jnp.dot(mainv_ref[n2], x, preferred_element_type=F32)
        o_ref[0, pl.ds(n2, half, stride=n2c), :] = r[:half]
        o_ref[1, pl.ds(n2, half, stride=n2c), :] = r[half:]
        return c

    lax.fori_loop(0, n2c, stage_a_inv, 0)


def pl_hyena_conv(z, filt, L):
    B, Lt, C = z.shape
    N = 2 * L
    N1 = N // FFT_N2
    hf = jnp.fft.fft(filt, axis=0) / N
    h2 = hf.reshape(FFT_N2, N1, C).transpose(1, 0, 2)
    h = jnp.stack([h2.real, h2.imag]).astype(BF16)
    ma, mainv, f_fwd, f_inv = _fft_tables(L)
    cw = LANES
    const3 = lambda ct, bp: (0, 0, 0)
    const2 = lambda ct, bp: (0, 0)
    return pl.pallas_call(
        functools.partial(_hyena_fft_kernel, n1_count=N1),
        grid=(C // cw, B // 2),
        in_specs=[pl.BlockSpec((2, L, cw), lambda ct, bp: (bp, 0, ct)),
                  pl.BlockSpec((2, N1, FFT_N2, cw), lambda ct, bp: (0, 0, 0, ct)),
                  pl.BlockSpec(ma.shape, const3), pl.BlockSpec(mainv.shape, const3),
                  pl.BlockSpec(f_fwd.shape, const2), pl.BlockSpec(f_inv.shape, const2)],
        out_specs=pl.BlockSpec((2, L, cw), lambda ct, bp: (bp, 0, ct)),
        out_shape=jax.ShapeDtypeStruct((B, Lt, C), F32),
        scratch_shapes=[pltpu.VMEM((FFT_N2 * 2 * N1, cw), F32), pltpu.VMEM((N1 * 2 * FFT_N2, cw), F32)],
        compiler_params=_params(2, HYENA_VMEM_LIMIT),
        name="hyena_fft_conv",
    )(z, h, ma, mainv, f_fwd, f_inv)


def hyena_filter(L, w1, b1, w2, b2, w3, freq):
    t = jnp.arange(L, dtype=F32) / L
    ang = (2.0 * math.pi) * t[:, None] * jnp.arange(1, HY_BANDS + 1, dtype=F32)
    feat = jnp.concatenate([t[:, None], jnp.cos(ang), jnp.sin(ang)], axis=-1)
    hp = lax.Precision.HIGHEST
    h = jnp.sin(freq * (jnp.dot(feat, w1, precision=hp) + b1))
    h = jnp.sin(freq * (jnp.dot(h, w2, precision=hp) + b2))
    h = jnp.dot(h, w3, precision=hp).reshape(L, 2, HY_CH)
    window = jnp.exp(-t[:, None] * jnp.linspace(HY_DECAY_MIN, HY_DECAY_MAX, HY_CH, dtype=F32))
    h = h * window[:, None, :]
    filt = jnp.concatenate([h[:, 0], jnp.zeros((1, HY_CH), F32), h[:0:-1, 1]], axis=0)
    return filt / (jnp.sum(jnp.abs(filt), axis=0, keepdims=True) + EPS)


def hyena_conv(z, filt_params, L):
    Lc = z.shape[1] - L
    y = pl_hyena_conv(z, hyena_filter(L, *filt_params), L)
    zc = z[:, L:]
    zf = jnp.fft.rfft(zc, n=2 * Lc, axis=1)
    ff = jnp.fft.rfft(hyena_filter(Lc, *filt_params), n=2 * Lc, axis=0)
    yc = jnp.fft.irfft(zf * ff[None], n=2 * Lc, axis=1)[:, :Lc]
    return lax.dynamic_update_slice(y, yc, (0, L, 0))


def _block_diag(blocks):
    G, r, c = blocks.shape
    eye = jnp.eye(G, dtype=blocks.dtype)
    return (eye[:, None, :, None] * blocks[:, :, None, :]).reshape(G * r, G * c)


def s5_scan(u_tb, a_re, a_im, log_dt, b_re, b_im, c_re, c_im, d_skip, n_lat_steps):
    outs = []
    for direction in range(2):
        A = lax.complex(a_re[direction], a_im[direction])
        dtA = jnp.exp(log_dt[direction])[:, None] * A
        a_bar = jnp.exp(dtA)
        b_bar = ((a_bar - 1.0) / A)[:, :, None] * lax.complex(b_re[direction], b_im[direction])
        bt_re = jnp.transpose(b_bar.real, (0, 2, 1))
        bt_im = jnp.transpose(b_bar.imag, (0, 2, 1))
        w_drive = jnp.concatenate([_block_diag(bt_re), _block_diag(bt_im)], axis=1)
        ct_re = jnp.transpose(c_re[direction], (0, 2, 1))
        ct_im = jnp.transpose(c_im[direction], (0, 2, 1))
        w_read = jnp.concatenate([_block_diag(ct_re), -_block_diag(ct_im)], axis=0)
        outs.append(pl_s5_scan(u_tb, w_drive, w_read, a_bar.real.reshape(-1), a_bar.imag.reshape(-1), d_skip,
                               n_lat_steps=n_lat_steps, reverse=direction == 1))
    return outs


def rope_tables(L, Lc):
    rows = L // GRID_W
    row = jnp.repeat(jnp.arange(rows, dtype=F32), GRID_W)
    col = jnp.tile(jnp.arange(GRID_W, dtype=F32), rows)
    inv = ROPE_BASE ** (-jnp.arange(ROPE_PAIRS_AXIS, dtype=F32) / ROPE_PAIRS_AXIS)
    ang = jnp.concatenate([row[:, None] * inv, col[:, None] * inv], axis=-1)
    cos, sin = jnp.cos(ang), jnp.sin(ang)
    n_maps = ATT_W // ATT_HEAD_DIM
    cosf = jnp.tile(jnp.concatenate([cos, cos], axis=-1), (1, n_maps))
    sinf = jnp.tile(jnp.concatenate([-sin, sin], axis=-1), (1, n_maps))
    return (jnp.concatenate([cosf, jnp.ones((Lc, ATT_W), F32)], axis=0),
            jnp.concatenate([sinf, jnp.zeros((Lc, ATT_W), F32)], axis=0))


def moe_dispatch(logits, b_g, b_e):
    T = logits.shape[0]
    g_logits = logits[:, :MOE_GROUPS] + b_g
    g_idx = jnp.argmax(g_logits, axis=-1)
    p_group = jnp.take_along_axis(jax.nn.softmax(g_logits, axis=-1), g_idx[:, None], axis=1)
    e_logits = (logits[:, MOE_GROUPS:MOE_GROUPS + N_EXPERTS] + b_e).reshape(T, MOE_GROUPS, MOE_EPG)
    e_logits = jnp.take_along_axis(e_logits, g_idx[:, None, None], axis=1)[:, 0]
    top_p, top_i = lax.top_k(jax.nn.softmax(e_logits, axis=-1), MOE_TOP_K)
    gate = p_group * top_p / jnp.sum(top_p, axis=-1, keepdims=True)
    expert = (g_idx[:, None] * MOE_EPG + top_i).reshape(-1).astype(jnp.int32)
    tok = jnp.repeat(jnp.arange(T, dtype=jnp.int32), MOE_TOP_K)
    n_assign = T * MOE_TOP_K
    n_blocks = -(-n_assign // MOE_BLOCK) + N_EXPERTS
    n_pad = n_blocks * MOE_BLOCK
    order = jnp.argsort(expert)
    onehot = (expert[:, None] == jnp.arange(N_EXPERTS, dtype=jnp.int32)[None, :]).astype(jnp.int32)
    csum = jnp.cumsum(onehot, axis=0)
    counts = csum[-1]
    rank = jnp.take_along_axis(csum, expert[:, None], axis=1)[:, 0] - 1
    start = jnp.cumsum(counts) - counts
    padded = (counts + MOE_BLOCK - 1) // MOE_BLOCK * MOE_BLOCK
    pad_end = jnp.cumsum(padded)
    pad_start = pad_end - padded
    slot_of_assign = (pad_start[expert] + rank).astype(jnp.int32)
    block_e = jnp.minimum(jnp.searchsorted(pad_end, jnp.arange(n_blocks) * MOE_BLOCK, side='right'),
                          N_EXPERTS - 1).astype(jnp.int32)
    slot_e = jnp.repeat(block_e, MOE_BLOCK)
    slot_r = jnp.arange(n_pad, dtype=jnp.int32) - pad_start[slot_e]
    slot_valid = (slot_r < counts[slot_e]) & (jnp.arange(n_pad) < pad_end[-1])
    sorted_pos = jnp.clip(start[slot_e] + slot_r, 0, n_assign - 1)
    slot_tok = jnp.where(slot_valid, tok[order[sorted_pos]], T).astype(jnp.int32)
    n_used = (pad_end[-1:] // MOE_BLOCK).astype(jnp.int32)
    return gate, slot_tok, slot_of_assign, block_e, n_used


def hier_moe(h2, logits, b_g, b_e, w1, w3, w2):
    T, D = h2.shape
    gate, slot_tok, slot_of_assign, block_e, n_used = moe_dispatch(logits, b_g, b_e)
    h_pad = jnp.concatenate([h2, jnp.zeros((1, D), BF16)], axis=0)
    xb = h_pad[slot_tok]
    yb = pl_moe_ffn(xb, block_e, n_used, w1.astype(BF16), w3.astype(BF16), w2.astype(BF16))
    ys = yb[slot_of_assign].reshape(T, MOE_TOP_K, D)
    return jnp.sum(ys * gate[:, :, None], axis=1)


def kernel(x, c, ctx, c_ctx, w_mod, b_mod, norm1_g, norm2_g, final_g, w_in, w_out, hy_conv_w, hy_conv_b, hy_ffn_w1, hy_ffn_b1, hy_ffn_w2, hy_ffn_b2, hy_ffn_w3, hy_freq, hy_skip, hy_norm_g, s5_a_re, s5_a_im, s5_log_dt, s5_b_re, s5_b_im, s5_c_re, s5_c_im, s5_d, s5_glu_w, s5_norm_g, att_lq1, att_lk1, att_lq2, att_lk2, att_subln_g, moe_wg, moe_bg, moe_we, moe_be, moe_w1, moe_w3, moe_w2):
    B, L, D = x.shape
    Lc = ctx.shape[1]
    Lt = L + Lc
    assert B == SUBLANES and Lc == TOKEN_TILE and L % ATT_TQ == 0
    n_lat_tiles = L // TOKEN_TILE
    cosf, sinf = rope_tables(L, Lc)
    silu_c = jax.nn.silu(c)
    silu_cc = jax.nn.silu(c_ctx)
    hp = lax.Precision.HIGHEST
    q_scale = ATT_HEAD_DIM ** -0.5 * math.log2(math.e)
    is_ctx_row = (jnp.arange(Lt) >= L)[None, :, None]
    xs = jnp.concatenate([x, ctx], axis=1)
    for l in range(DEPTH):
        lam_init = 0.8 - 0.6 * math.exp(-0.3 * l)
        mod = jnp.dot(silu_c, w_mod[l], precision=hp) + b_mod[l]
        cmod = jnp.dot(silu_cc, w_mod[l], precision=hp) + b_mod[l]
        modv = jnp.stack([mod, jnp.broadcast_to(cmod, mod.shape)], axis=1).reshape(2 * B, 1, N_MOD * D)

        u0, z, s5_u, q_p, q_r, k_r, v = pl_in_proj(xs, modv, norm1_g[l], w_in[l], hy_conv_w[l], hy_conv_b[l],
                                                   cosf, sinf, n_lat_tiles, q_scale)

        filt_params = (hy_ffn_w1[l], hy_ffn_b1[l], hy_ffn_w2[l], hy_ffn_b2[l], hy_ffn_w3[l], hy_freq[l])
        y_hy = hyena_conv(z, filt_params, L)

        s5_f, s5_b = s5_scan(s5_u.reshape(Lt * B, S5_CH), s5_a_re[l], s5_a_im[l], s5_log_dt[l], s5_b_re[l],
                             s5_b_im[l], s5_c_re[l], s5_c_im[l], s5_d[l], L)

        lam = (jnp.exp(jnp.sum(att_lq1[l] * att_lk1[l])) - jnp.exp(jnp.sum(att_lq2[l] * att_lk2[l])) + lam_init)
        g_scaled = (att_subln_g[l] * (1.0 - lam_init)).reshape(1, ATT_V_DIM)
        att = pl_diff_attention(q_p, q_r, k_r, v, g_scaled, lam, L)

        w_router = jnp.zeros((D, LANES), F32).at[:, :MOE_GROUPS].set(moe_wg[l])
        w_router = w_router.at[:, MOE_GROUPS:MOE_GROUPS + N_EXPERTS].set(moe_we[l])
        xs, h2, logits = pl_out_proj(xs, u0, z, y_hy, s5_f.reshape(Lt, B * S5_CH), s5_b.reshape(Lt, B * S5_CH), att,
                                     modv, hy_skip[l], hy_norm_g[l], s5_glu_w[l], s5_norm_g[l], w_out[l],
                                     norm2_g[l], w_router, n_lat_tiles)

        y = hier_moe(h2.reshape(B * Lt, D), logits.reshape(B * Lt, LANES), moe_bg[l], moe_be[l],
                     moe_w1[l], moe_w3[l], moe_w2[l]).reshape(B, Lt, D)
        g2 = jnp.where(is_ctx_row, cmod[None, None, 5 * D:], mod[:, None, 5 * D:])
        xs = xs + g2 * y
    xl = xs[:, :L]
    return xl * lax.rsqrt(jnp.mean(xl * xl, axis=-1, keepdims=True) + EPS) * final_g
```

```python
import functools
import math

import jax
import jax.numpy as jnp
import numpy as np
from jax import lax
from jax.experimental import pallas as pl
from jax.experimental.pallas import tpu as pltpu

D_MODEL = 1024
DEPTH = 4
GRID_W = 64
N_MOD = 6
EPS = 1e-6
HY_CH = D_MODEL // 4
S5_CH = D_MODEL // 4
ATT_W = D_MODEL // 2
HY_BANDS = 16
HY_DECAY_MIN = -math.log(1e-2) / 1.5
HY_DECAY_MAX = -math.log(1e-2) / 0.3
S5_GROUP = 16
S5_GROUPS = S5_CH // S5_GROUP
S5_STATE = 64
ATT_HEAD_DIM = 64
ATT_HEADS = ATT_W // (2 * ATT_HEAD_DIM)
ATT_V_DIM = 2 * ATT_HEAD_DIM
ROPE_HALF = ATT_HEAD_DIM // 2
ROPE_PAIRS_AXIS = ROPE_HALF // 2
ROPE_BASE = 10000.0
MOE_GROUPS = 4
MOE_EPG = 8
N_EXPERTS = MOE_GROUPS * MOE_EPG
MOE_TOP_K = 2
MOE_BLOCK = 256
IN_COLS = 3 * HY_CH + S5_CH + 3 * ATT_W
COL_S5 = 3 * HY_CH
COL_Q = COL_S5 + S5_CH
COL_K = COL_Q + ATT_W
COL_V = COL_K + ATT_W

LANES = 128
SUBLANES = 8
VMEM_LIMIT = 48 * 1024 * 1024
TOKEN_TILE = 256

F32 = jnp.float32
BF16 = jnp.bfloat16


def _params(n_axes, vmem=VMEM_LIMIT):
    return pltpu.CompilerParams(dimension_semantics=("arbitrary",) * n_axes, vmem_limit_bytes=vmem)


def _rms(x):
    return x * lax.rsqrt(jnp.mean(x * x, axis=-1, keepdims=True) + EPS)


def _in_kernel(x_ref, xp_ref, xn_ref, mod_ref, g_ref, w_ref, cw_ref, cb_ref, cos_ref, sin_ref,
               u0_ref, z_ref, s5_ref, qp_ref, qr_ref, k_ref, v_ref, *, n_lat_tiles, q_scale):
    i = pl.program_id(1)
    tm = x_ref.shape[1]
    d = x_ref.shape[2]
    g = g_ref[...]
    shift = mod_ref[0, :, 0:d]
    scale = mod_ref[0, :, d:2 * d]

    def norm_mod(xt):
        return (_rms(xt) * g) * (1.0 + scale) + shift

    h = jnp.concatenate([norm_mod(x_ref[0]), norm_mod(xp_ref[0]), norm_mod(xn_ref[0])], axis=0).astype(BF16)
    p = jnp.dot(h, w_ref[...], preferred_element_type=F32)

    hy = p[:tm, :COL_S5]
    is_ctx = i == n_lat_tiles
    has_prev = jnp.logical_and(i != 0, jnp.logical_not(is_ctx))
    has_next = jnp.logical_and(i != n_lat_tiles - 1, jnp.logical_not(is_ctx))
    prev_row = jnp.where(has_prev, p[tm + SUBLANES - 1:tm + SUBLANES, :COL_S5], 0.0)
    next_row = jnp.where(has_next, p[tm + SUBLANES:tm + SUBLANES + 1, :COL_S5], 0.0)
    rows = lax.broadcasted_iota(jnp.int32, (tm, 1), 0)
    up = jnp.where(rows == 0, prev_row, pltpu.roll(hy, 1, axis=0))
    dn = jnp.where(rows == tm - 1, next_row, pltpu.roll(hy, tm - 1, axis=0))
    u = up * cw_ref[0:1, :] + hy * cw_ref[1:2, :] + dn * cw_ref[2:3, :] + cb_ref[...]
    u0_ref[0] = u[:, :HY_CH]
    z_ref[0] = u[:, HY_CH:2 * HY_CH] * u[:, 2 * HY_CH:]

    s5_ref[...] = p[:tm, COL_S5:COL_Q]

    lane = lax.broadcasted_iota(jnp.int32, (1, ATT_W), 1)
    first_half = jnp.bitwise_and(lane, ATT_HEAD_DIM - 1) < ROPE_HALF
    cos = cos_ref[...]
    sin = sin_ref[...]

    def rope(t):
        partner = jnp.where(first_half, pltpu.roll(t, ATT_W - ROPE_HALF, axis=1), pltpu.roll(t, ROPE_HALF, axis=1))
        return t * cos + partner * sin

    q = p[:tm, COL_Q:COL_K] * q_scale
    qp_ref[0] = q.astype(BF16)
    qr_ref[0] = rope(q).astype(BF16)
    k_ref[0] = rope(p[:tm, COL_K:COL_V]).astype(BF16)
    v_ref[0] = p[:tm, COL_V:].astype(BF16)


def pl_in_proj(xs, modv, norm_g, w_in, conv_w, conv_b, cosf, sinf, n_lat_tiles, q_scale):
    B, Lt, D = xs.shape
    tm = TOKEN_TILE
    n_tiles = Lt // tm
    halo_per_tile = tm // SUBLANES
    n_halo_blocks = Lt // SUBLANES
    tok = lambda w: pl.BlockSpec((1, tm, w), lambda b, i: (b, i, 0))
    const = lambda shape: pl.BlockSpec(shape, lambda b, i: (0,) * len(shape))
    out_shapes = (jax.ShapeDtypeStruct((B, Lt, HY_CH), F32), jax.ShapeDtypeStruct((B, Lt, HY_CH), F32),
                  jax.ShapeDtypeStruct((Lt, B * S5_CH), F32),
                  jax.ShapeDtypeStruct((B, Lt, ATT_W), BF16), jax.ShapeDtypeStruct((B, Lt, ATT_W), BF16),
                  jax.ShapeDtypeStruct((B, Lt, ATT_W), BF16), jax.ShapeDtypeStruct((B, Lt, ATT_W), BF16))
    return pl.pallas_call(
        functools.partial(_in_kernel, n_lat_tiles=n_lat_tiles, q_scale=q_scale),
        grid=(B, n_tiles),
        in_specs=[tok(D),
                  pl.BlockSpec((1, SUBLANES, D), lambda b, i: (b, jnp.maximum(i * halo_per_tile - 1, 0), 0)),
                  pl.BlockSpec((1, SUBLANES, D),
                               lambda b, i: (b, jnp.minimum((i + 1) * halo_per_tile, n_halo_blocks - 1), 0)),
                  pl.BlockSpec((1, 1, N_MOD * D), lambda b, i: (2 * b + jnp.where(i >= n_lat_tiles, 1, 0), 0, 0)),
                  const((1, D)), const((D, IN_COLS)), const((3, COL_S5)), const((1, COL_S5)),
                  pl.BlockSpec((tm, ATT_W), lambda b, i: (i, 0)), pl.BlockSpec((tm, ATT_W), lambda b, i: (i, 0))],
        out_specs=(tok(HY_CH), tok(HY_CH), pl.BlockSpec((tm, S5_CH), lambda b, i: (i, b)),
                   tok(ATT_W), tok(ATT_W), tok(ATT_W), tok(ATT_W)),
        out_shape=out_shapes,
        compiler_params=_params(2),
        name="in_proj",
    )(xs, xs, xs, modv, norm_g.reshape(1, D), w_in.astype(BF16), conv_w, conv_b.reshape(1, COL_S5), cosf, sinf)


def _out_kernel(x_ref, u0_ref, z_ref, y_ref, sf_ref, sb_ref, att_ref, mod_ref, skip_ref, hg_ref, gw_ref, sg_ref,
                wo_ref, n2g_ref, rwh_ref, rwl_ref, xo_ref, h2_ref, lg_ref):
    d = x_ref.shape[2]
    hy = _rms(u0_ref[0] * (y_ref[0] + skip_ref[...] * z_ref[0])) * hg_ref[...]
    gl = jax.nn.gelu(sf_ref[...] + sb_ref[...])
    gate = jax.nn.sigmoid(jnp.dot(gl.astype(BF16), gw_ref[...], preferred_element_type=F32))
    s5 = _rms(gl * gate) * sg_ref[...]
    mix = jnp.concatenate([hy, s5, att_ref[0]], axis=1).astype(BF16)
    proj = jnp.dot(mix, wo_ref[...], preferred_element_type=F32)
    xn = x_ref[0] + mod_ref[0, :, 2 * d:3 * d] * proj
    xo_ref[0] = xn
    h2 = (_rms(xn) * n2g_ref[...]) * (1.0 + mod_ref[0, :, 4 * d:5 * d]) + mod_ref[0, :, 3 * d:4 * d]
    hh = h2.astype(BF16)
    hl = (h2 - hh.astype(F32)).astype(BF16)
    h2_ref[0] = hh
    lg = jnp.dot(hh, rwh_ref[...], preferred_element_type=F32)
    lg += jnp.dot(hl, rwh_ref[...], preferred_element_type=F32)
    lg += jnp.dot(hh, rwl_ref[...], preferred_element_type=F32)
    lg_ref[0] = lg


def pl_out_proj(xs, u0, z, y_hy, s5_f, s5_b, att, modv, hy_skip, hy_norm_g, glu_w, s5_norm_g, w_out, norm2_g,
                w_router, n_lat_tiles):
    B, Lt, D = xs.shape
    tm = TOKEN_TILE
    tok = lambda w: pl.BlockSpec((1, tm, w), lambda b, i: (b, i, 0))
    tb = pl.BlockSpec((tm, S5_CH), lambda b, i: (i, b))
    const = lambda shape: pl.BlockSpec(shape, lambda b, i: (0,) * len(shape))
    rwh = w_router.astype(BF16)
    rwl = (w_router - rwh.astype(F32)).astype(BF16)
    return pl.pallas_call(
        _out_kernel,
        grid=(B, Lt // tm),
        in_specs=[tok(D), tok(HY_CH), tok(HY_CH), tok(HY_CH), tb, tb, tok(ATT_W),
                  pl.BlockSpec((1, 1, N_MOD * D), lambda b, i: (2 * b + jnp.where(i >= n_lat_tiles, 1, 0), 0, 0)),
                  const((1, HY_CH)), const((1, HY_CH)), const((S5_CH, S5_CH)), const((1, S5_CH)),
                  const((D, D)), const((1, D)), const((D, LANES)), const((D, LANES))],
        out_specs=(tok(D), tok(D), tok(LANES)),
        out_shape=(jax.ShapeDtypeStruct((B, Lt, D), F32), jax.ShapeDtypeStruct((B, Lt, D), BF16),
                   jax.ShapeDtypeStruct((B, Lt, LANES), F32)),
        compiler_params=_params(2),
        name="out_proj",
    )(xs, u0, z, y_hy, s5_f, s5_b, att, modv, hy_skip.reshape(1, HY_CH), hy_norm_g.reshape(1, HY_CH),
      glu_w.astype(BF16), s5_norm_g.reshape(1, S5_CH), w_out.astype(BF16), norm2_g.reshape(1, D), rwh, rwl)


def _dot_nt(a, b):
    return lax.dot_general(a, b, (((1,), (1,)), ((), ())), preferred_element_type=F32)


ATT_TQ = 512
ATT_SUB = 256


def _attn_kernel(qp_ref, qr_ref, k_ref, v_ref, g_ref, lam_ref, *rest, n_lat):
    o_ref = rest[-1]
    tq = qp_ref.shape[1]
    first_map = lax.broadcasted_iota(jnp.int32, (1, LANES), 1) < ATT_HEAD_DIM
    zero = jnp.zeros((), BF16)
    sub = min(ATT_SUB, tq)
    for r0 in range(0, tq, sub):
        qp = qp_ref[0, r0:r0 + sub, :]
        qr = qr_ref[0, r0:r0 + sub, :]
        probs = []
        for m in range(2):
            in_map = first_map if m == 0 else jnp.logical_not(first_map)
            s_c = _dot_nt(jnp.where(in_map, qp, zero), k_ref[0, n_lat:, :])
            mx = jnp.max(s_c, axis=-1, keepdims=True)
            if n_lat:
                s_l = _dot_nt(jnp.where(in_map, qr, zero), k_ref[0, :n_lat, :])
                mx = jnp.maximum(mx, jnp.max(s_l, axis=-1, keepdims=True))
                p_l = jnp.exp2(s_l - mx)
            p_c = jnp.exp2(s_c - mx)
            den = jnp.sum(p_c, axis=-1, keepdims=True)
            if n_lat:
                den = den + jnp.sum(p_l, axis=-1, keepdims=True)
            probs.append((p_c, p_l if n_lat else None, 1.0 / den))
        w0 = probs[0][2]
        w1 = lam_ref[0:1, 0:1] * probs[1][2]
        a_c = (probs[0][0] * w0 - probs[1][0] * w1).astype(BF16)
        o = jnp.dot(a_c, v_ref[0, n_lat:, :], preferred_element_type=F32)
        if n_lat:
            a_l = (probs[0][1] * w0 - probs[1][1] * w1).astype(BF16)
            o = o + jnp.dot(a_l, v_ref[0, :n_lat, :], preferred_element_type=F32)
        o_ref[0, r0:r0 + sub, :] = _rms(o) * g_ref[...]


def pl_diff_attention(qp, qr, k, v, g_scaled, lam, n_lat):
    B, Lt, _ = qp.shape
    n_ctx = Lt - n_lat
    lam_arr = jnp.full((SUBLANES, LANES), lam, F32)
    small = [pl.BlockSpec((1, LANES), lambda b, h, i: (0, 0)), pl.BlockSpec((SUBLANES, LANES), lambda b, h, i: (0, 0))]
    tq = ATT_TQ
    qspec = pl.BlockSpec((1, tq, LANES), lambda b, h, i: (b, i, h))
    kspec = pl.BlockSpec((1, Lt, LANES), lambda b, h, i: (b, 0, h))
    out_lat = pl.pallas_call(
        functools.partial(_attn_kernel, n_lat=n_lat),
        grid=(B, ATT_HEADS, n_lat // tq),
        in_specs=[qspec, qspec, kspec, kspec] + small,
        out_specs=qspec,
        out_shape=jax.ShapeDtypeStruct((B, Lt, ATT_W), F32),
        compiler_params=_params(3),
        name="diff_attention",
    )(qp, qr, k, v, g_scaled, lam_arr)
    ctx_blk = n_lat // n_ctx
    cspec = pl.BlockSpec((1, n_ctx, LANES), lambda b, h, i: (b, ctx_blk, h))
    return pl.pallas_call(
        functools.partial(_attn_kernel, n_lat=0),
        grid=(B, ATT_HEADS, 1),
        in_specs=[cspec, cspec, cspec, cspec] + small + [pl.BlockSpec(memory_space=pl.ANY)],
        out_specs=cspec,
        out_shape=jax.ShapeDtypeStruct((B, Lt, ATT_W), F32),
        input_output_aliases={6: 0},
        compiler_params=_params(3),
        name="diff_attention_ctx",
    )(qp, qr, k, v, g_scaled, lam_arr, out_lat)


def _moe_kernel(be_ref, nb_ref, x_ref, w1_ref, w3_ref, w2_ref, o_ref, w1_scr, w3_scr, w2_scr):
    i = pl.program_id(0)
    new_expert = jnp.logical_or(i == 0, be_ref[i] != be_ref[jnp.maximum(i - 1, 0)])

    @pl.when(jnp.logical_and(i < nb_ref[0], new_expert))
    def _():
        w1_scr[...] = w1_ref[0].astype(BF16)
        w3_scr[...] = w3_ref[0].astype(BF16)
        w2_scr[...] = w2_ref[0].astype(BF16)

    @pl.when(i < nb_ref[0])
    def _():
        x = x_ref[...]
        a = jnp.dot(x, w1_scr[...], preferred_element_type=F32)
        b = jnp.dot(x, w3_scr[...], preferred_element_type=F32)
        h = (a * jax.nn.sigmoid(a)) * b
        o_ref[...] = jnp.dot(h.astype(BF16), w2_scr[...], preferred_element_type=F32).astype(o_ref.dtype)

    @pl.when(i >= nb_ref[0])
    def _():
        o_ref[...] = jnp.zeros_like(o_ref)


def pl_moe_ffn(xb, block_e, n_used, w1, w3, w2):
    n_pad, D = xb.shape
    n_blocks = n_pad // MOE_BLOCK
    F = w1.shape[-1]
    grid_spec = pltpu.PrefetchScalarGridSpec(
        num_scalar_prefetch=2,
        grid=(n_blocks,),
        in_specs=[pl.BlockSpec((MOE_BLOCK, D), lambda i, be, nb: (i, 0)),
                  pl.BlockSpec((1, D, F), lambda i, be, nb: (be[i], 0, 0)),
                  pl.BlockSpec((1, D, F), lambda i, be, nb: (be[i], 0, 0)),
                  pl.BlockSpec((1, F, D), lambda i, be, nb: (be[i], 0, 0))],
        out_specs=pl.BlockSpec((MOE_BLOCK, D), lambda i, be, nb: (i, 0)),
        scratch_shapes=[pltpu.VMEM((D, F), BF16), pltpu.VMEM((D, F), BF16), pltpu.VMEM((F, D), BF16)],
    )
    return pl.pallas_call(
        _moe_kernel,
        grid_spec=grid_spec,
        out_shape=jax.ShapeDtypeStruct((n_pad, D), BF16),
        compiler_params=_params(1),
        name="moe_ffn",
    )(block_e, n_used, xb, w1, w3, w2)


S5_STATES = S5_GROUPS * S5_STATE
S5_CHUNK = 64


def _s5_kernel(u_ref, wd_ref, wr_ref, ar_ref, ai_ref, d_ref, y_ref, x_scr, h_scr, hr_scr, hi_scr, *, reverse):
    ns = S5_STATES

    @pl.when(pl.program_id(0) == 0)
    def _():
        hr_scr[...] = jnp.zeros_like(hr_scr)
        hi_scr[...] = jnp.zeros_like(hi_scr)

    u = u_ref[...]
    x_scr[...] = jnp.dot(u.astype(BF16), wd_ref[...], preferred_element_type=F32)
    ar = ar_ref[...]
    ai = ai_ref[...]

    def step(hr, hi, t):
        r = pl.multiple_of(t * SUBLANES, SUBLANES)
        xr = x_scr[pl.ds(r, SUBLANES), :ns]
        xi = x_scr[pl.ds(r, SUBLANES), ns:]
        return ar * hr - ai * hi + xr, ar * hi + ai * hr + xi

    def body(j, carry):
        hr, hi = carry
        t0 = (S5_CHUNK - 1 - 2 * j) if reverse else 2 * j
        t1 = t0 - 1 if reverse else t0 + 1
        hr0, hi0 = step(hr, hi, t0)
        hr1, hi1 = step(hr0, hi0, t1)
        lo = t1 if reverse else t0
        first_r, second_r = (hr1, hr0) if reverse else (hr0, hr1)
        first_i, second_i = (hi1, hi0) if reverse else (hi0, hi1)
        r = pl.multiple_of(lo * SUBLANES, 2 * SUBLANES)
        h_scr[pl.ds(r, 2 * SUBLANES), :ns] = jnp.concatenate([first_r, second_r], axis=0).astype(BF16)
        h_scr[pl.ds(r, 2 * SUBLANES), ns:] = jnp.concatenate([first_i, second_i], axis=0).astype(BF16)
        return hr1, hi1

    hr, hi = lax.fori_loop(0, S5_CHUNK // 2, body, (hr_scr[...], hi_scr[...]))
    hr_scr[...] = hr
    hi_scr[...] = hi
    y = jnp.dot(h_scr[...], wr_ref[...], preferred_element_type=F32)
    if not reverse:
        y = y + u * d_ref[...]
    y_ref[...] = y


def pl_s5_scan(u_tb, w_drive, w_read, a_re, a_im, d_skip, *, n_lat_steps, reverse):
    rows, ch = u_tb.shape
    rc = S5_CHUNK * SUBLANES
    n_chunks = rows // rc
    n_lat = n_lat_steps // S5_CHUNK
    n_ctx = n_chunks - n_lat
    assert rows % rc == 0 and n_lat_steps % S5_CHUNK == 0
    if reverse:
        def idx(i):
            return (n_chunks - 1 - i, 0)
    else:
        def idx(i):
            return (jnp.where(i < n_ctx, n_lat + i, i - n_ctx), 0)
    const = lambda i: (0, 0)
    ns2 = 2 * S5_STATES
    return pl.pallas_call(
        functools.partial(_s5_kernel, reverse=reverse),
        grid=(n_chunks,),
        in_specs=[pl.BlockSpec((rc, ch), idx),
                  pl.BlockSpec((ch, ns2), const),
                  pl.BlockSpec((ns2, ch), const),
                  pl.BlockSpec((SUBLANES, S5_STATES), const),
                  pl.BlockSpec((SUBLANES, S5_STATES), const),
                  pl.BlockSpec((1, ch), const)],
        out_specs=pl.BlockSpec((rc, ch), idx),
        out_shape=jax.ShapeDtypeStruct((rows, ch), F32),
        scratch_shapes=[pltpu.VMEM((rc, ns2), F32), pltpu.VMEM((rc, ns2), BF16),
                        pltpu.VMEM((SUBLANES, S5_STATES), F32), pltpu.VMEM((SUBLANES, S5_STATES), F32)],
        compiler_params=_params(1),
        name="s5_scan_rev" if reverse else "s5_scan_fwd",
    )(u_tb, w_drive.astype(BF16), w_read.astype(BF16),
      jnp.broadcast_to(a_re[None, :], (SUBLANES, S5_STATES)),
      jnp.broadcast_to(a_im[None, :], (SUBLANES, S5_STATES)),
      d_skip.reshape(1, ch))


FFT_N2 = 128
HYENA_VMEM_LIMIT = 56 * 1024 * 1024


def _fft_tables(L):
    N = 2 * L
    N1 = N // FFT_N2
    k1 = np.arange(N1)[:, None]
    n1 = np.arange(N1 // 2)[None, :]
    n2 = np.arange(FFT_N2)[:, None, None]
    ang = -2.0 * np.pi * (k1[None] * (n2 + FFT_N2 * n1[None])) / N
    mr, mi = np.cos(ang), np.sin(ang)
    ma = np.concatenate([np.concatenate([mr, -mi], axis=2), np.concatenate([mi, mr], axis=2)], axis=1)
    gr, gi = np.transpose(mr, (0, 2, 1)), -np.transpose(mi, (0, 2, 1))
    mainv = np.concatenate([np.concatenate([gr, -gi], axis=2), np.concatenate([gi, gr], axis=2)], axis=1)
    kk = np.arange(FFT_N2)
    a2 = -2.0 * np.pi * np.outer(kk, kk) / FFT_N2
    fr, fi = np.cos(a2), np.sin(a2)
    f_fwd = np.block([[fr, -fi], [fi, fr]])
    f_inv = np.block([[fr, fi], [-fi, fr]])
    return (jnp.asarray(ma, BF16), jnp.asarray(mainv, BF16), jnp.asarray(f_fwd, BF16), jnp.asarray(f_inv, BF16))


def _hyena_fft_kernel(z_ref, h_ref, ma_ref, mainv_ref, ff_ref, fi_ref, o_ref, a_scr, b_scr, *, n1_count):
    half = n1_count // 2
    n2c = FFT_N2

    def stage_a(n2, c):
        xr = z_ref[0, pl.ds(n2, half, stride=n2c), :]
        xi = z_ref[1, pl.ds(n2, half, stride=n2c), :]
        x = jnp.concatenate([xr, xi], axis=0).astype(BF16)
        r = jnp.dot(ma_ref[n2], x, preferred_element_type=F32)
        a_scr[pl.ds(pl.multiple_of(n2 * 2 * n1_count, 2 * n1_count), 2 * n1_count), :] = r
        return c

    lax.fori_loop(0, n2c, stage_a, 0)

    def stage_c(k1, c):
        ar = a_scr[pl.ds(k1, n2c, stride=2 * n1_count), :]
        ai = a_scr[pl.ds(n1_count + k1, n2c, stride=2 * n1_count), :]
        x = jnp.concatenate([ar, ai], axis=0).astype(BF16)
        y = jnp.dot(ff_ref[...], x, preferred_element_type=F32)
        yr, yi = y[:n2c], y[n2c:]
        hr = h_ref[0, k1].astype(F32)
        hi = h_ref[1, k1].astype(F32)
        x2 = jnp.concatenate([yr * hr - yi * hi, yr * hi + yi * hr], axis=0).astype(BF16)
        b = jnp.dot(fi_ref[...], x2, preferred_element_type=F32)
        b_scr[pl.ds(pl.multiple_of(k1 * 2 * n2c, 2 * n2c), 2 * n2c), :] = b
        return c

    lax.fori_loop(0, n1_count, stage_c, 0)

    def stage_a_inv(n2, c):
        br = b_scr[pl.ds(n2, n1_count, stride=2 * n2c), :]
        bi = b_scr[pl.ds(n2c + n2, n1_count, stride=2 * n2c), :]
        x = jnp.concatenate([br, bi], axis=0).astype(BF16)
        r = jnp.dot(mainv_ref[n2], x, preferred_element_type=F32)
        o_ref[0, pl.ds(n2, half, stride=n2c), :] = r[:half]
        o_ref[1, pl.ds(n2, half, stride=n2c), :] = r[half:]
        return c

    lax.fori_loop(0, n2c, stage_a_inv, 0)


def pl_hyena_conv(z, filt, L):
    B, Lt, C = z.shape
    N = 2 * L
    N1 = N // FFT_N2
    hf = jnp.fft.fft(filt, axis=0) / N
    h2 = hf.reshape(FFT_N2, N1, C).transpose(1, 0, 2)
    h = jnp.stack([h2.real, h2.imag]).astype(BF16)
    ma, mainv, f_fwd, f_inv = _fft_tables(L)
    cw = LANES
    const3 = lambda ct, bp: (0, 0, 0)
    const2 = lambda ct, bp: (0, 0)
    return pl.pallas_call(
        functools.partial(_hyena_fft_kernel, n1_count=N1),
        grid=(C // cw, B // 2),
        in_specs=[pl.BlockSpec((2, L, cw), lambda ct, bp: (bp, 0, ct)),
                  pl.BlockSpec((2, N1, FFT_N2, cw), lambda ct, bp: (0, 0, 0, ct)),
                  pl.BlockSpec(ma.shape, const3), pl.BlockSpec(mainv.shape, const3),
                  pl.BlockSpec(f_fwd.shape, const2), pl.BlockSpec(f_inv.shape, const2)],
        out_specs=pl.BlockSpec((2, L, cw), lambda ct, bp: (bp, 0, ct)),
        out_shape=jax.ShapeDtypeStruct((B, Lt, C), F32),
        scratch_shapes=[pltpu.VMEM((FFT_N2 * 2 * N1, cw), F32), pltpu.VMEM((N1 * 2 * FFT_N2, cw), F32)],
        compiler_params=_params(2, HYENA_VMEM_LIMIT),
        name="hyena_fft_conv",
    )(z, h, ma, mainv, f_fwd, f_inv)


def hyena_filter(L, w1, b1, w2, b2, w3, freq):
    t = jnp.arange(L, dtype=F32) / L
    ang = (2.0 * math.pi) * t[:, None] * jnp.arange(1, HY_BANDS + 1, dtype=F32)
    feat = jnp.concatenate([t[:, None], jnp.cos(ang), jnp.sin(ang)], axis=-1)
    hp = lax.Precision.HIGHEST
    h = jnp.sin(freq * (jnp.dot(feat, w1, precision=hp) + b1))
    h = jnp.sin(freq * (jnp.dot(h, w2, precision=hp) + b2))
    h = jnp.dot(h, w3, precision=hp).reshape(L, 2, HY_CH)
    window = jnp.exp(-t[:, None] * jnp.linspace(HY_DECAY_MIN, HY_DECAY_MAX, HY_CH, dtype=F32))
    h = h * window[:, None, :]
    filt = jnp.concatenate([h[:, 0], jnp.zeros((1, HY_CH), F32), h[:0:-1, 1]], axis=0)
    return filt / (jnp.sum(jnp.abs(filt), axis=0, keepdims=True) + EPS)


def hyena_conv(z, filt_params, L):
    Lc = z.shape[1] - L
    y = pl_hyena_conv(z, hyena_filter(L, *filt_params), L)
    zc = z[:, L:]
    zf = jnp.fft.rfft(zc, n=2 * Lc, axis=1)
    ff = jnp.fft.rfft(hyena_filter(Lc, *filt_params), n=2 * Lc, axis=0)
    yc = jnp.fft.irfft(zf * ff[None], n=2 * Lc, axis=1)[:, :Lc]
    return lax.dynamic_update_slice(y, yc, (0, L, 0))


def _block_diag(blocks):
    G, r, c = blocks.shape
    eye = jnp.eye(G, dtype=blocks.dtype)
    return (eye[:, None, :, None] * blocks[:, :, None, :]).reshape(G * r, G * c)


def s5_scan(u_tb, a_re, a_im, log_dt, b_re, b_im, c_re, c_im, d_skip, n_lat_steps):
    outs = []
    for direction in range(2):
        A = lax.complex(a_re[direction], a_im[direction])
        dtA = jnp.exp(log_dt[direction])[:, None] * A
        a_bar = jnp.exp(dtA)
        b_bar = ((a_bar - 1.0) / A)[:, :, None] * lax.complex(b_re[direction], b_im[direction])
        bt_re = jnp.transpose(b_bar.real, (0, 2, 1))
        bt_im = jnp.transpose(b_bar.imag, (0, 2, 1))
        w_drive = jnp.concatenate([_block_diag(bt_re), _block_diag(bt_im)], axis=1)
        ct_re = jnp.transpose(c_re[direction], (0, 2, 1))
        ct_im = jnp.transpose(c_im[direction], (0, 2, 1))
        w_read = jnp.concatenate([_block_diag(ct_re), -_block_diag(ct_im)], axis=0)
        outs.append(pl_s5_scan(u_tb, w_drive, w_read, a_bar.real.reshape(-1), a_bar.imag.reshape(-1), d_skip,
                               n_lat_steps=n_lat_steps, reverse=direction == 1))
    return outs


def rope_tables(L, Lc):
    rows = L // GRID_W
    row = jnp.repeat(jnp.arange(rows, dtype=F32), GRID_W)
    col = jnp.tile(jnp.arange(GRID_W, dtype=F32), rows)
    inv = ROPE_BASE ** (-jnp.arange(ROPE_PAIRS_AXIS, dtype=F32) / ROPE_PAIRS_AXIS)
    ang = jnp.concatenate([row[:, None] * inv, col[:, None] * inv], axis=-1)
    cos, sin = jnp.cos(ang), jnp.sin(ang)
    n_maps = ATT_W // ATT_HEAD_DIM
    cosf = jnp.tile(jnp.concatenate([cos, cos], axis=-1), (1, n_maps))
    sinf = jnp.tile(jnp.concatenate([-sin, sin], axis=-1), (1, n_maps))
    return (jnp.concatenate([cosf, jnp.ones((Lc, ATT_W), F32)], axis=0),
            jnp.concatenate([sinf, jnp.zeros((Lc, ATT_W), F32)], axis=0))


def moe_dispatch(logits, b_g, b_e):
    T = logits.shape[0]
    g_logits = logits[:, :MOE_GROUPS] + b_g
    g_idx = jnp.argmax(g_logits, axis=-1)
    p_group = jnp.take_along_axis(jax.nn.softmax(g_logits, axis=-1), g_idx[:, None], axis=1)
    e_logits = (logits[:, MOE_GROUPS:MOE_GROUPS + N_EXPERTS] + b_e).reshape(T, MOE_GROUPS, MOE_EPG)
    e_logits = jnp.take_along_axis(e_logits, g_idx[:, None, None], axis=1)[:, 0]
    top_p, top_i = lax.top_k(jax.nn.softmax(e_logits, axis=-1), MOE_TOP_K)
    gate = p_group * top_p / jnp.sum(top_p, axis=-1, keepdims=True)
    expert = (g_idx[:, None] * MOE_EPG + top_i).reshape(-1).astype(jnp.int32)
    tok = jnp.repeat(jnp.arange(T, dtype=jnp.int32), MOE_TOP_K)
    n_assign = T * MOE_TOP_K
    n_blocks = -(-n_assign // MOE_BLOCK) + N_EXPERTS
    n_pad = n_blocks * MOE_BLOCK
    order = jnp.argsort(expert).astype(jnp.int32)
    sorted_rank = jnp.argsort(order).astype(jnp.int32)
    bounds = jnp.searchsorted(expert[order], jnp.arange(N_EXPERTS + 1, dtype=jnp.int32), side='left')
    start = bounds[:-1].astype(jnp.int32)
    counts = (bounds[1:] - bounds[:-1]).astype(jnp.int32)
    padded = (counts + MOE_BLOCK - 1) // MOE_BLOCK * MOE_BLOCK
    pad_end = jnp.cumsum(padded)
    pad_start = pad_end - padded
    slot_of_assign = (pad_start[expert] + sorted_rank - start[expert]).astype(jnp.int32)
    block_e = jnp.minimum(jnp.searchsorted(pad_end, jnp.arange(n_blocks) * MOE_BLOCK, side='right'),
                          N_EXPERTS - 1).astype(jnp.int32)
    slot_e = jnp.repeat(block_e, MOE_BLOCK)
    slot_r = jnp.arange(n_pad, dtype=jnp.int32) - pad_start[slot_e]
    slot_valid = (slot_r < counts[slot_e]) & (jnp.arange(n_pad) < pad_end[-1])
    sorted_pos = jnp.clip(start[slot_e] + slot_r, 0, n_assign - 1)
    slot_tok = jnp.where(slot_valid, tok[order[sorted_pos]], T).astype(jnp.int32)
    n_used = (pad_end[-1:] // MOE_BLOCK).astype(jnp.int32)
    return gate, slot_tok, slot_of_assign, block_e, n_used


def hier_moe(h2, logits, b_g, b_e, w1, w3, w2):
    T, D = h2.shape
    gate, slot_tok, slot_of_assign, block_e, n_used = moe_dispatch(logits, b_g, b_e)
    h_pad = jnp.concatenate([h2, jnp.zeros((1, D), BF16)], axis=0)
    xb = h_pad[slot_tok]
    yb = pl_moe_ffn(xb, block_e, n_used, w1, w3, w2)
    slots = slot_of_assign.reshape(T, MOE_TOP_K)
    return sum(yb[slots[:, j]].astype(F32) * gate[:, j:j + 1] for j in range(MOE_TOP_K))


def kernel(x, c, ctx, c_ctx, w_mod, b_mod, norm1_g, norm2_g, final_g, w_in, w_out, hy_conv_w, hy_conv_b, hy_ffn_w1, hy_ffn_b1, hy_ffn_w2, hy_ffn_b2, hy_ffn_w3, hy_freq, hy_skip, hy_norm_g, s5_a_re, s5_a_im, s5_log_dt, s5_b_re, s5_b_im, s5_c_re, s5_c_im, s5_d, s5_glu_w, s5_norm_g, att_lq1, att_lk1, att_lq2, att_lk2, att_subln_g, moe_wg, moe_bg, moe_we, moe_be, moe_w1, moe_w3, moe_w2):
    B, L, D = x.shape
    Lc = ctx.shape[1]
    Lt = L + Lc
    assert B == SUBLANES and Lc == TOKEN_TILE and L % ATT_TQ == 0
    n_lat_tiles = L // TOKEN_TILE
    cosf, sinf = rope_tables(L, Lc)
    silu_c = jax.nn.silu(c)
    silu_cc = jax.nn.silu(c_ctx)
    hp = lax.Precision.HIGHEST
    q_scale = ATT_HEAD_DIM ** -0.5 * math.log2(math.e)
    is_ctx_row = (jnp.arange(Lt) >= L)[None, :, None]
    xs = jnp.concatenate([x, ctx], axis=1)
    for l in range(DEPTH):
        lam_init = 0.8 - 0.6 * math.exp(-0.3 * l)
        mod = jnp.dot(silu_c, w_mod[l], precision=hp) + b_mod[l]
        cmod = jnp.dot(silu_cc, w_mod[l], precision=hp) + b_mod[l]
        modv = jnp.stack([mod, jnp.broadcast_to(cmod, mod.shape)], axis=1).reshape(2 * B, 1, N_MOD * D)

        u0, z, s5_u, q_p, q_r, k_r, v = pl_in_proj(xs, modv, norm1_g[l], w_in[l], hy_conv_w[l], hy_conv_b[l],
                                                   cosf, sinf, n_lat_tiles, q_scale)

        filt_params = (hy_ffn_w1[l], hy_ffn_b1[l], hy_ffn_w2[l], hy_ffn_b2[l], hy_ffn_w3[l], hy_freq[l])
        y_hy = hyena_conv(z, filt_params, L)

        s5_f, s5_b = s5_scan(s5_u.reshape(Lt * B, S5_CH), s5_a_re[l], s5_a_im[l], s5_log_dt[l], s5_b_re[l],
                             s5_b_im[l], s5_c_re[l], s5_c_im[l], s5_d[l], L)

        lam = (jnp.exp(jnp.sum(att_lq1[l] * att_lk1[l])) - jnp.exp(jnp.sum(att_lq2[l] * att_lk2[l])) + lam_init)
        g_scaled = (att_subln_g[l] * (1.0 - lam_init)).reshape(1, ATT_V_DIM)
        att = pl_diff_attention(q_p, q_r, k_r, v, g_scaled, lam, L)

        w_router = jnp.zeros((D, LANES), F32).at[:, :MOE_GROUPS].set(moe_wg[l])
        w_router = w_router.at[:, MOE_GROUPS:MOE_GROUPS + N_EXPERTS].set(moe_we[l])
        xs, h2, logits = pl_out_proj(xs, u0, z, y_hy, s5_f.reshape(Lt, B * S5_CH), s5_b.reshape(Lt, B * S5_CH), att,
                                     modv, hy_skip[l], hy_norm_g[l], s5_glu_w[l], s5_norm_g[l], w_out[l],
                                     norm2_g[l], w_router, n_lat_tiles)

        y = hier_moe(h2.reshape(B * Lt, D), logits.reshape(B * Lt, LANES), moe_bg[l], moe_be[l],
                     moe_w1[l], moe_w3[l], moe_w2[l]).reshape(B, Lt, D)
        g2 = jnp.where(is_ctx_row, cmod[None, None, 5 * D:], mod[:, None, 5 * D:])
        xs = xs + g2 * y
    xl = xs[:, :L]
    return xl * lax.rsqrt(jnp.mean(xl * xl, axis=-1, keepdims=True) + EPS) * final_g
```

```python
import functools
import math

import jax
import jax.numpy as jnp
import numpy as np
from jax import lax
from jax.experimental import pallas as pl
from jax.experimental.pallas import tpu as pltpu

D_MODEL = 1024
DEPTH = 4
GRID_W = 64
N_MOD = 6
EPS = 1e-6
HY_CH = D_MODEL // 4
S5_CH = D_MODEL // 4
ATT_W = D_MODEL // 2
HY_BANDS = 16
HY_DECAY_MIN = -math.log(1e-2) / 1.5
HY_DECAY_MAX = -math.log(1e-2) / 0.3
S5_GROUP = 16
S5_GROUPS = S5_CH // S5_GROUP
S5_STATE = 64
ATT_HEAD_DIM = 64
ATT_HEADS = ATT_W // (2 * ATT_HEAD_DIM)
ATT_V_DIM = 2 * ATT_HEAD_DIM
ROPE_HALF = ATT_HEAD_DIM // 2
ROPE_PAIRS_AXIS = ROPE_HALF // 2
ROPE_BASE = 10000.0
MOE_GROUPS = 4
MOE_EPG = 8
N_EXPERTS = MOE_GROUPS * MOE_EPG
MOE_TOP_K = 2
MOE_BLOCK = 256
IN_COLS = 3 * HY_CH + S5_CH + 3 * ATT_W
COL_S5 = 3 * HY_CH
COL_Q = COL_S5 + S5_CH
COL_K = COL_Q + ATT_W
COL_V = COL_K + ATT_W

LANES = 128
SUBLANES = 8
VMEM_LIMIT = 48 * 1024 * 1024
TOKEN_TILE = 256

F32 = jnp.float32
BF16 = jnp.bfloat16


def _params(n_axes, vmem=VMEM_LIMIT):
    return pltpu.CompilerParams(dimension_semantics=("arbitrary",) * n_axes, vmem_limit_bytes=vmem)


def _rms(x):
    return x * lax.rsqrt(jnp.mean(x * x, axis=-1, keepdims=True) + EPS)


def _in_kernel(x_ref, xp_ref, xn_ref, mod_ref, g_ref, w_ref, cw_ref, cb_ref, cos_ref, sin_ref,
               u0_ref, z_ref, s5_ref, qp_ref, qr_ref, k_ref, v_ref, *, n_lat_tiles, q_scale):
    i = pl.program_id(1)
    tm = x_ref.shape[1]
    d = x_ref.shape[2]
    g = g_ref[...]
    shift = mod_ref[0, :, 0:d]
    scale = mod_ref[0, :, d:2 * d]

    def norm_mod(xt):
        return (_rms(xt) * g) * (1.0 + scale) + shift

    h = jnp.concatenate([norm_mod(x_ref[0]), norm_mod(xp_ref[0]), norm_mod(xn_ref[0])], axis=0).astype(BF16)
    p = jnp.dot(h, w_ref[...], preferred_element_type=F32)

    hy = p[:tm, :COL_S5]
    is_ctx = i == n_lat_tiles
    has_prev = jnp.logical_and(i != 0, jnp.logical_not(is_ctx))
    has_next = jnp.logical_and(i != n_lat_tiles - 1, jnp.logical_not(is_ctx))
    prev_row = jnp.where(has_prev, p[tm + SUBLANES - 1:tm + SUBLANES, :COL_S5], 0.0)
    next_row = jnp.where(has_next, p[tm + SUBLANES:tm + SUBLANES + 1, :COL_S5], 0.0)
    rows = lax.broadcasted_iota(jnp.int32, (tm, 1), 0)
    up = jnp.where(rows == 0, prev_row, pltpu.roll(hy, 1, axis=0))
    dn = jnp.where(rows == tm - 1, next_row, pltpu.roll(hy, tm - 1, axis=0))
    u = up * cw_ref[0:1, :] + hy * cw_ref[1:2, :] + dn * cw_ref[2:3, :] + cb_ref[...]
    u0_ref[0] = u[:, :HY_CH]
    z_ref[0] = u[:, HY_CH:2 * HY_CH] * u[:, 2 * HY_CH:]

    s5_ref[...] = p[:tm, COL_S5:COL_Q]

    lane = lax.broadcasted_iota(jnp.int32, (1, ATT_W), 1)
    first_half = jnp.bitwise_and(lane, ATT_HEAD_DIM - 1) < ROPE_HALF
    cos = cos_ref[...]
    sin = sin_ref[...]

    def rope(t):
        partner = jnp.where(first_half, pltpu.roll(t, ATT_W - ROPE_HALF, axis=1), pltpu.roll(t, ROPE_HALF, axis=1))
        return t * cos + partner * sin

    q = p[:tm, COL_Q:COL_K] * q_scale
    qp_ref[0] = q.astype(BF16)
    qr_ref[0] = rope(q).astype(BF16)
    k_ref[0] = rope(p[:tm, COL_K:COL_V]).astype(BF16)
    v_ref[0] = p[:tm, COL_V:].astype(BF16)


def pl_in_proj(xs, modv, norm_g, w_in, conv_w, conv_b, cosf, sinf, n_lat_tiles, q_scale):
    B, Lt, D = xs.shape
    tm = TOKEN_TILE
    n_tiles = Lt // tm
    halo_per_tile = tm // SUBLANES
    n_halo_blocks = Lt // SUBLANES
    tok = lambda w: pl.BlockSpec((1, tm, w), lambda b, i: (b, i, 0))
    const = lambda shape: pl.BlockSpec(shape, lambda b, i: (0,) * len(shape))
    out_shapes = (jax.ShapeDtypeStruct((B, Lt, HY_CH), F32), jax.ShapeDtypeStruct((B, Lt, HY_CH), F32),
                  jax.ShapeDtypeStruct((Lt, B * S5_CH), F32),
                  jax.ShapeDtypeStruct((B, Lt, ATT_W), BF16), jax.ShapeDtypeStruct((B, Lt, ATT_W), BF16),
                  jax.ShapeDtypeStruct((B, Lt, ATT_W), BF16), jax.ShapeDtypeStruct((B, Lt, ATT_W), BF16))
    return pl.pallas_call(
        functools.partial(_in_kernel, n_lat_tiles=n_lat_tiles, q_scale=q_scale),
        grid=(B, n_tiles),
        in_specs=[tok(D),
                  pl.BlockSpec((1, SUBLANES, D), lambda b, i: (b, jnp.maximum(i * halo_per_tile - 1, 0), 0)),
                  pl.BlockSpec((1, SUBLANES, D),
                               lambda b, i: (b, jnp.minimum((i + 1) * halo_per_tile, n_halo_blocks - 1), 0)),
                  pl.BlockSpec((1, 1, N_MOD * D), lambda b, i: (2 * b + jnp.where(i >= n_lat_tiles, 1, 0), 0, 0)),
                  const((1, D)), const((D, IN_COLS)), const((3, COL_S5)), const((1, COL_S5)),
                  pl.BlockSpec((tm, ATT_W), lambda b, i: (i, 0)), pl.BlockSpec((tm, ATT_W), lambda b, i: (i, 0))],
        out_specs=(tok(HY_CH), tok(HY_CH), pl.BlockSpec((tm, S5_CH), lambda b, i: (i, b)),
                   tok(ATT_W), tok(ATT_W), tok(ATT_W), tok(ATT_W)),
        out_shape=out_shapes,
        compiler_params=_params(2),
        name="in_proj",
    )(xs, xs, xs, modv, norm_g.reshape(1, D), w_in.astype(BF16), conv_w, conv_b.reshape(1, COL_S5), cosf, sinf)


def _out_kernel(x_ref, u0_ref, z_ref, y_ref, sf_ref, sb_ref, att_ref, mod_ref, skip_ref, hg_ref, gw_ref, sg_ref,
                wo_ref, n2g_ref, rwh_ref, rwl_ref, xo_ref, h2_ref, lg_ref):
    d = x_ref.shape[2]
    hy = _rms(u0_ref[0] * (y_ref[0] + skip_ref[...] * z_ref[0])) * hg_ref[...]
    gl = jax.nn.gelu(sf_ref[...] + sb_ref[...])
    gate = jax.nn.sigmoid(jnp.dot(gl.astype(BF16), gw_ref[...], preferred_element_type=F32))
    s5 = _rms(gl * gate) * sg_ref[...]
    mix = jnp.concatenate([hy, s5, att_ref[0]], axis=1).astype(BF16)
    proj = jnp.dot(mix, wo_ref[...], preferred_element_type=F32)
    xn = x_ref[0] + mod_ref[0, :, 2 * d:3 * d] * proj
    xo_ref[0] = xn
    h2 = (_rms(xn) * n2g_ref[...]) * (1.0 + mod_ref[0, :, 4 * d:5 * d]) + mod_ref[0, :, 3 * d:4 * d]
    hh = h2.astype(BF16)
    hl = (h2 - hh.astype(F32)).astype(BF16)
    h2_ref[0] = hh
    lg = jnp.dot(hh, rwh_ref[...], preferred_element_type=F32)
    lg += jnp.dot(hl, rwh_ref[...], preferred_element_type=F32)
    lg += jnp.dot(hh, rwl_ref[...], preferred_element_type=F32)
    lg_ref[0] = lg


def pl_out_proj(xs, u0, z, y_hy, s5_f, s5_b, att, modv, hy_skip, hy_norm_g, glu_w, s5_norm_g, w_out, norm2_g,
                w_router, n_lat_tiles):
    B, Lt, D = xs.shape
    tm = TOKEN_TILE
    tok = lambda w: pl.BlockSpec((1, tm, w), lambda b, i: (b, i, 0))
    tb = pl.BlockSpec((tm, S5_CH), lambda b, i: (i, b))
    const = lambda shape: pl.BlockSpec(shape, lambda b, i: (0,) * len(shape))
    rwh = w_router.astype(BF16)
    rwl = (w_router - rwh.astype(F32)).astype(BF16)
    return pl.pallas_call(
        _out_kernel,
        grid=(B, Lt // tm),
        in_specs=[tok(D), tok(HY_CH), tok(HY_CH), tok(HY_CH), tb, tb, tok(ATT_W),
                  pl.BlockSpec((1, 1, N_MOD * D), lambda b, i: (2 * b + jnp.where(i >= n_lat_tiles, 1, 0), 0, 0)),
                  const((1, HY_CH)), const((1, HY_CH)), const((S5_CH, S5_CH)), const((1, S5_CH)),
                  const((D, D)), const((1, D)), const((D, LANES)), const((D, LANES))],
        out_specs=(tok(D), tok(D), tok(LANES)),
        out_shape=(jax.ShapeDtypeStruct((B, Lt, D), F32), jax.ShapeDtypeStruct((B, Lt, D), BF16),
                   jax.ShapeDtypeStruct((B, Lt, LANES), F32)),
        compiler_params=_params(2),
        name="out_proj",
    )(xs, u0, z, y_hy, s5_f, s5_b, att, modv, hy_skip.reshape(1, HY_CH), hy_norm_g.reshape(1, HY_CH),
      glu_w.astype(BF16), s5_norm_g.reshape(1, S5_CH), w_out.astype(BF16), norm2_g.reshape(1, D), rwh, rwl)


def _dot_nt(a, b):
    return lax.dot_general(a, b, (((1,), (1,)), ((), ())), preferred_element_type=F32)


ATT_TQ = 512
ATT_SUB = 256


def _attn_kernel(qp_ref, qr_ref, k_ref, v_ref, g_ref, lam_ref, *rest, n_lat):
    o_ref = rest[-1]
    tq = qp_ref.shape[1]
    first_map = lax.broadcasted_iota(jnp.int32, (1, LANES), 1) < ATT_HEAD_DIM
    zero = jnp.zeros((), BF16)
    sub = min(ATT_SUB, tq)
    for r0 in range(0, tq, sub):
        qp = qp_ref[0, r0:r0 + sub, :]
        qr = qr_ref[0, r0:r0 + sub, :]
        probs = []
        for m in range(2):
            in_map = first_map if m == 0 else jnp.logical_not(first_map)
            s_c = _dot_nt(jnp.where(in_map, qp, zero), k_ref[0, n_lat:, :])
            mx = jnp.max(s_c, axis=-1, keepdims=True)
            if n_lat:
                s_l = _dot_nt(jnp.where(in_map, qr, zero), k_ref[0, :n_lat, :])
                mx = jnp.maximum(mx, jnp.max(s_l, axis=-1, keepdims=True))
                p_l = jnp.exp2(s_l - mx)
            p_c = jnp.exp2(s_c - mx)
            den = jnp.sum(p_c, axis=-1, keepdims=True)
            if n_lat:
                den = den + jnp.sum(p_l, axis=-1, keepdims=True)
            probs.append((p_c, p_l if n_lat else None, 1.0 / den))
        w0 = probs[0][2]
        w1 = lam_ref[0:1, 0:1] * probs[1][2]
        a_c = (probs[0][0] * w0 - probs[1][0] * w1).astype(BF16)
        o = jnp.dot(a_c, v_ref[0, n_lat:, :], preferred_element_type=F32)
        if n_lat:
            a_l = (probs[0][1] * w0 - probs[1][1] * w1).astype(BF16)
            o = o + jnp.dot(a_l, v_ref[0, :n_lat, :], preferred_element_type=F32)
        o_ref[0, r0:r0 + sub, :] = _rms(o) * g_ref[...]


def pl_diff_attention(qp, qr, k, v, g_scaled, lam, n_lat):
    B, Lt, _ = qp.shape
    n_ctx = Lt - n_lat
    lam_arr = jnp.full((SUBLANES, LANES), lam, F32)
    small = [pl.BlockSpec((1, LANES), lambda b, h, i: (0, 0)), pl.BlockSpec((SUBLANES, LANES), lambda b, h, i: (0, 0))]
    tq = ATT_TQ
    qspec = pl.BlockSpec((1, tq, LANES), lambda b, h, i: (b, i, h))
    kspec = pl.BlockSpec((1, Lt, LANES), lambda b, h, i: (b, 0, h))
    out_lat = pl.pallas_call(
        functools.partial(_attn_kernel, n_lat=n_lat),
        grid=(B, ATT_HEADS, n_lat // tq),
        in_specs=[qspec, qspec, kspec, kspec] + small,
        out_specs=qspec,
        out_shape=jax.ShapeDtypeStruct((B, Lt, ATT_W), F32),
        compiler_params=_params(3),
        name="diff_attention",
    )(qp, qr, k, v, g_scaled, lam_arr)
    ctx_blk = n_lat // n_ctx
    cspec = pl.BlockSpec((1, n_ctx, LANES), lambda b, h, i: (b, ctx_blk, h))
    return pl.pallas_call(
        functools.partial(_attn_kernel, n_lat=0),
        grid=(B, ATT_HEADS, 1),
        in_specs=[cspec, cspec, cspec, cspec] + small + [pl.BlockSpec(memory_space=pl.ANY)],
        out_specs=cspec,
        out_shape=jax.ShapeDtypeStruct((B, Lt, ATT_W), F32),
        input_output_aliases={6: 0},
        compiler_params=_params(3),
        name="diff_attention_ctx",
    )(qp, qr, k, v, g_scaled, lam_arr, out_lat)


def _moe_kernel(be_ref, nb_ref, x_ref, gate_ref, w1_ref, w3_ref, w2_ref, o_ref, w1_scr, w3_scr, w2_scr):
    i = pl.program_id(0)
    new_expert = jnp.logical_or(i == 0, be_ref[i] != be_ref[jnp.maximum(i - 1, 0)])

    @pl.when(jnp.logical_and(i < nb_ref[0], new_expert))
    def _():
        w1_scr[...] = w1_ref[0, 0].astype(BF16)
        w3_scr[...] = w3_ref[0, 0].astype(BF16)
        w2_scr[...] = w2_ref[0, 0].astype(BF16)

    @pl.when(i < nb_ref[0])
    def _():
        x = x_ref[...]
        a = jnp.dot(x, w1_scr[...], preferred_element_type=F32)
        b = jnp.dot(x, w3_scr[...], preferred_element_type=F32)
        h = (a * jax.nn.sigmoid(a)) * b
        y = jnp.dot(h.astype(BF16), w2_scr[...], preferred_element_type=F32)
        rows = lax.broadcasted_iota(jnp.int32, (MOE_BLOCK, 1), 0)
        lane = lax.broadcasted_iota(jnp.int32, (1, LANES), 1)
        g_rows = gate_ref[0, 0:1, :]
        for r in range(1, MOE_BLOCK // LANES):
            g_rows = jnp.where(rows >= r * LANES, gate_ref[0, r:r + 1, :], g_rows)
        g_col = jnp.sum(jnp.where(lane == jnp.bitwise_and(rows, LANES - 1), g_rows, 0.0), axis=1, keepdims=True)
        o_ref[...] = (y * g_col).astype(o_ref.dtype)

    @pl.when(i >= nb_ref[0])
    def _():
        o_ref[...] = jnp.zeros_like(o_ref)


def pl_moe_ffn(xb, slot_gate, block_e, n_used, w1, w3, w2, layer):
    n_pad, D = xb.shape
    n_blocks = n_pad // MOE_BLOCK
    F = w1.shape[-1]
    grid_spec = pltpu.PrefetchScalarGridSpec(
        num_scalar_prefetch=2,
        grid=(n_blocks,),
        in_specs=[pl.BlockSpec((MOE_BLOCK, D), lambda i, be, nb: (i, 0)),
                  pl.BlockSpec((1, MOE_BLOCK // LANES, LANES), lambda i, be, nb: (i, 0, 0)),
                  pl.BlockSpec((1, 1, D, F), lambda i, be, nb: (layer, be[i], 0, 0)),
                  pl.BlockSpec((1, 1, D, F), lambda i, be, nb: (layer, be[i], 0, 0)),
                  pl.BlockSpec((1, 1, F, D), lambda i, be, nb: (layer, be[i], 0, 0))],
        out_specs=pl.BlockSpec((MOE_BLOCK, D), lambda i, be, nb: (i, 0)),
        scratch_shapes=[pltpu.VMEM((D, F), BF16), pltpu.VMEM((D, F), BF16), pltpu.VMEM((F, D), BF16)],
    )
    return pl.pallas_call(
        _moe_kernel,
        grid_spec=grid_spec,
        out_shape=jax.ShapeDtypeStruct((n_pad, D), BF16),
        compiler_params=_params(1),
        name="moe_ffn",
    )(block_e, n_used, xb, slot_gate.reshape(n_blocks, MOE_BLOCK // LANES, LANES), w1, w3, w2)


def _combine_kernel(x_ref, y0_ref, y1_ref, mod_ref, g_ref, o_ref, *, final):
    d = x_ref.shape[2]
    xn = x_ref[0] + mod_ref[0, :, 5 * d:6 * d] * (y0_ref[0].astype(F32) + y1_ref[0].astype(F32))
    o_ref[0] = _rms(xn) * g_ref[...] if final else xn


def pl_moe_combine(xs, y0, y1, modv, final_g, n_lat_tiles, final):
    B, Lt, D = xs.shape
    tm = TOKEN_TILE
    n_tiles = n_lat_tiles if final else Lt // tm
    tok = pl.BlockSpec((1, tm, D), lambda b, i: (b, i, 0))
    return pl.pallas_call(
        functools.partial(_combine_kernel, final=final),
        grid=(B, n_tiles),
        in_specs=[tok, tok, tok,
                  pl.BlockSpec((1, 1, N_MOD * D), lambda b, i: (2 * b + jnp.where(i >= n_lat_tiles, 1, 0), 0, 0)),
                  pl.BlockSpec((1, D), lambda b, i: (0, 0))],
        out_specs=tok,
        out_shape=jax.ShapeDtypeStruct((B, n_tiles * tm, D), F32),
        compiler_params=_params(2),
        name="moe_combine",
    )(xs, y0.reshape(B, Lt, D), y1.reshape(B, Lt, D), modv, final_g.reshape(1, D))


S5_STATES = S5_GROUPS * S5_STATE
S5_CHUNK = 64


def _s5_kernel(u_ref, wd_ref, wr_ref, ar_ref, ai_ref, d_ref, y_ref, x_scr, h_scr, hr_scr, hi_scr, *, reverse):
    ns = S5_STATES

    @pl.when(pl.program_id(0) == 0)
    def _():
        hr_scr[...] = jnp.zeros_like(hr_scr)
        hi_scr[...] = jnp.zeros_like(hi_scr)

    u = u_ref[...]
    x_scr[...] = jnp.dot(u.astype(BF16), wd_ref[...], preferred_element_type=F32)
    ar = ar_ref[...]
    ai = ai_ref[...]

    def step(hr, hi, t):
        r = pl.multiple_of(t * SUBLANES, SUBLANES)
        xr = x_scr[pl.ds(r, SUBLANES), :ns]
        xi = x_scr[pl.ds(r, SUBLANES), ns:]
        return ar * hr - ai * hi + xr, ar * hi + ai * hr + xi

    def body(j, carry):
        hr, hi = carry
        t0 = (S5_CHUNK - 1 - 2 * j) if reverse else 2 * j
        t1 = t0 - 1 if reverse else t0 + 1
        hr0, hi0 = step(hr, hi, t0)
        hr1, hi1 = step(hr0, hi0, t1)
        lo = t1 if reverse else t0
        first_r, second_r = (hr1, hr0) if reverse else (hr0, hr1)
        first_i, second_i = (hi1, hi0) if reverse else (hi0, hi1)
        r = pl.multiple_of(lo * SUBLANES, 2 * SUBLANES)
        h_scr[pl.ds(r, 2 * SUBLANES), :ns] = jnp.concatenate([first_r, second_r], axis=0).astype(BF16)
        h_scr[pl.ds(r, 2 * SUBLANES), ns:] = jnp.concatenate([first_i, second_i], axis=0).astype(BF16)
        return hr1, hi1

    hr, hi = lax.fori_loop(0, S5_CHUNK // 2, body, (hr_scr[...], hi_scr[...]))
    hr_scr[...] = hr
    hi_scr[...] = hi
    y = jnp.dot(h_scr[...], wr_ref[...], preferred_element_type=F32)
    if not reverse:
        y = y + u * d_ref[...]
    y_ref[...] = y


def pl_s5_scan(u_tb, w_drive, w_read, a_re, a_im, d_skip, *, n_lat_steps, reverse):
    rows, ch = u_tb.shape
    rc = S5_CHUNK * SUBLANES
    n_chunks = rows // rc
    n_lat = n_lat_steps // S5_CHUNK
    n_ctx = n_chunks - n_lat
    assert rows % rc == 0 and n_lat_steps % S5_CHUNK == 0
    if reverse:
        def idx(i):
            return (n_chunks - 1 - i, 0)
    else:
        def idx(i):
            return (jnp.where(i < n_ctx, n_lat + i, i - n_ctx), 0)
    const = lambda i: (0, 0)
    ns2 = 2 * S5_STATES
    return pl.pallas_call(
        functools.partial(_s5_kernel, reverse=reverse),
        grid=(n_chunks,),
        in_specs=[pl.BlockSpec((rc, ch), idx),
                  pl.BlockSpec((ch, ns2), const),
                  pl.BlockSpec((ns2, ch), const),
                  pl.BlockSpec((SUBLANES, S5_STATES), const),
                  pl.BlockSpec((SUBLANES, S5_STATES), const),
                  pl.BlockSpec((1, ch), const)],
        out_specs=pl.BlockSpec((rc, ch), idx),
        out_shape=jax.ShapeDtypeStruct((rows, ch), F32),
        scratch_shapes=[pltpu.VMEM((rc, ns2), F32), pltpu.VMEM((rc, ns2), BF16),
                        pltpu.VMEM((SUBLANES, S5_STATES), F32), pltpu.VMEM((SUBLANES, S5_STATES), F32)],
        compiler_params=_params(1),
        name="s5_scan_rev" if reverse else "s5_scan_fwd",
    )(u_tb, w_drive.astype(BF16), w_read.astype(BF16),
      jnp.broadcast_to(a_re[None, :], (SUBLANES, S5_STATES)),
      jnp.broadcast_to(a_im[None, :], (SUBLANES, S5_STATES)),
      d_skip.reshape(1, ch))


FFT_N2 = 128
HYENA_VMEM_LIMIT = 56 * 1024 * 1024


def _fft_tables(L):
    N = 2 * L
    N1 = N // FFT_N2
    k1 = np.arange(N1)[:, None]
    n1 = np.arange(N1 // 2)[None, :]
    n2 = np.arange(FFT_N2)[:, None, None]
    ang = -2.0 * np.pi * (k1[None] * (n2 + FFT_N2 * n1[None])) / N
    mr, mi = np.cos(ang), np.sin(ang)
    ma = np.concatenate([np.concatenate([mr, -mi], axis=2), np.concatenate([mi, mr], axis=2)], axis=1)
    gr, gi = np.transpose(mr, (0, 2, 1)), -np.transpose(mi, (0, 2, 1))
    mainv = np.concatenate([np.concatenate([gr, -gi], axis=2), np.concatenate([gi, gr], axis=2)], axis=1)
    kk = np.arange(FFT_N2)
    a2 = -2.0 * np.pi * np.outer(kk, kk) / FFT_N2
    fr, fi = np.cos(a2), np.sin(a2)
    f_fwd = np.block([[fr, -fi], [fi, fr]])
    f_inv = np.block([[fr, fi], [-fi, fr]])
    return (jnp.asarray(ma, BF16), jnp.asarray(mainv, BF16), jnp.asarray(f_fwd, BF16), jnp.asarray(f_inv, BF16))


def _hyena_fft_kernel(z_ref, h_ref, ma_ref, mainv_ref, ff_ref, fi_ref, o_ref, a_scr, b_scr, *, n1_count):
    half = n1_count // 2
    n2c = FFT_N2

    def stage_a(n2, c):
        xr = z_ref[0, pl.ds(n2, half, stride=n2c), :]
        xi = z_ref[1, pl.ds(n2, half, stride=n2c), :]
        x = jnp.concatenate([xr, xi], axis=0).astype(BF16)
        r = jnp.dot(ma_ref[n2], x, preferred_element_type=F32)
        a_scr[pl.ds(pl.multiple_of(n2 * 2 * n1_count, 2 * n1_count), 2 * n1_count), :] = r
        return c

    lax.fori_loop(0, n2c, stage_a, 0)

    def stage_c(k1, c):
        ar = a_scr[pl.ds(k1, n2c, stride=2 * n1_count), :]
        ai = a_scr[pl.ds(n1_count + k1, n2c, stride=2 * n1_count), :]
        x = jnp.concatenate([ar, ai], axis=0).astype(BF16)
        y = jnp.dot(ff_ref[...], x, preferred_element_type=F32)
        yr, yi = y[:n2c], y[n2c:]
        hr = h_ref[0, k1].astype(F32)
        hi = h_ref[1, k1].astype(F32)
        x2 = jnp.concatenate([yr * hr - yi * hi, yr * hi + yi * hr], axis=0).astype(BF16)
        b = jnp.dot(fi_ref[...], x2, preferred_element_type=F32)
        b_scr[pl.ds(pl.multiple_of(k1 * 2 * n2c, 2 * n2c), 2 * n2c), :] = b
        return c

    lax.fori_loop(0, n1_count, stage_c, 0)

    def stage_a_inv(n2, c):
        br = b_scr[pl.ds(n2, n1_count, stride=2 * n2c), :]
        bi = b_scr[pl.ds(n2c + n2, n1_count, stride=2 * n2c), :]
        x = jnp.concatenate([br, bi], axis=0).astype(BF16)
        r = jnp.dot(mainv_ref[n2], x, preferred_element_type=F32)
        o_ref[0, pl.ds(n2, half, stride=n2c), :] = r[:half]
        o_ref[1, pl.ds(n2, half, stride=n2c), :] = r[half:]
        return c

    lax.fori_loop(0, n2c, stage_a_inv, 0)


def pl_hyena_conv(z, filt, L):
    B, Lt, C = z.shape
    N = 2 * L
    N1 = N // FFT_N2
    hf = jnp.fft.fft(filt, axis=0) / N
    h2 = hf.reshape(FFT_N2, N1, C).transpose(1, 0, 2)
    h = jnp.stack([h2.real, h2.imag]).astype(BF16)
    ma, mainv, f_fwd, f_inv = _fft_tables(L)
    cw = LANES
    const3 = lambda ct, bp: (0, 0, 0)
    const2 = lambda ct, bp: (0, 0)
    return pl.pallas_call(
        functools.partial(_hyena_fft_kernel, n1_count=N1),
        grid=(C // cw, B // 2),
        in_specs=[pl.BlockSpec((2, L, cw), lambda ct, bp: (bp, 0, ct)),
                  pl.BlockSpec((2, N1, FFT_N2, cw), lambda ct, bp: (0, 0, 0, ct)),
                  pl.BlockSpec(ma.shape, const3), pl.BlockSpec(mainv.shape, const3),
                  pl.BlockSpec(f_fwd.shape, const2), pl.BlockSpec(f_inv.shape, const2)],
        out_specs=pl.BlockSpec((2, L, cw), lambda ct, bp: (bp, 0, ct)),
        out_shape=jax.ShapeDtypeStruct((B, Lt, C), F32),
        scratch_shapes=[pltpu.VMEM((FFT_N2 * 2 * N1, cw), F32), pltpu.VMEM((N1 * 2 * FFT_N2, cw), F32)],
        compiler_params=_params(2, HYENA_VMEM_LIMIT),
        name="hyena_fft_conv",
    )(z, h, ma, mainv, f_fwd, f_inv)


def hyena_filter(L, w1, b1, w2, b2, w3, freq):
    t = jnp.arange(L, dtype=F32) / L
    ang = (2.0 * math.pi) * t[:, None] * jnp.arange(1, HY_BANDS + 1, dtype=F32)
    feat = jnp.concatenate([t[:, None], jnp.cos(ang), jnp.sin(ang)], axis=-1)
    hp = lax.Precision.HIGHEST
    h = jnp.sin(freq * (jnp.dot(feat, w1, precision=hp) + b1))
    h = jnp.sin(freq * (jnp.dot(h, w2, precision=hp) + b2))
    h = jnp.dot(h, w3, precision=hp).reshape(L, 2, HY_CH)
    window = jnp.exp(-t[:, None] * jnp.linspace(HY_DECAY_MIN, HY_DECAY_MAX, HY_CH, dtype=F32))
    h = h * window[:, None, :]
    filt = jnp.concatenate([h[:, 0], jnp.zeros((1, HY_CH), F32), h[:0:-1, 1]], axis=0)
    return filt / (jnp.sum(jnp.abs(filt), axis=0, keepdims=True) + EPS)


def hyena_conv(z, filt_params, L):
    Lc = z.shape[1] - L
    y = pl_hyena_conv(z, hyena_filter(L, *filt_params), L)
    zc = z[:, L:]
    zf = jnp.fft.rfft(zc, n=2 * Lc, axis=1)
    ff = jnp.fft.rfft(hyena_filter(Lc, *filt_params), n=2 * Lc, axis=0)
    yc = jnp.fft.irfft(zf * ff[None], n=2 * Lc, axis=1)[:, :Lc]
    return lax.dynamic_update_slice(y, yc, (0, L, 0))


def _block_diag(blocks):
    G, r, c = blocks.shape
    eye = jnp.eye(G, dtype=blocks.dtype)
    return (eye[:, None, :, None] * blocks[:, :, None, :]).reshape(G * r, G * c)


def s5_scan(u_tb, a_re, a_im, log_dt, b_re, b_im, c_re, c_im, d_skip, n_lat_steps):
    outs = []
    for direction in range(2):
        A = lax.complex(a_re[direction], a_im[direction])
        dtA = jnp.exp(log_dt[direction])[:, None] * A
        a_bar = jnp.exp(dtA)
        b_bar = ((a_bar - 1.0) / A)[:, :, None] * lax.complex(b_re[direction], b_im[direction])
        bt_re = jnp.transpose(b_bar.real, (0, 2, 1))
        bt_im = jnp.transpose(b_bar.imag, (0, 2, 1))
        w_drive = jnp.concatenate([_block_diag(bt_re), _block_diag(bt_im)], axis=1)
        ct_re = jnp.transpose(c_re[direction], (0, 2, 1))
        ct_im = jnp.transpose(c_im[direction], (0, 2, 1))
        w_read = jnp.concatenate([_block_diag(ct_re), -_block_diag(ct_im)], axis=0)
        outs.append(pl_s5_scan(u_tb, w_drive, w_read, a_bar.real.reshape(-1), a_bar.imag.reshape(-1), d_skip,
                               n_lat_steps=n_lat_steps, reverse=direction == 1))
    return outs


def rope_tables(L, Lc):
    rows = L // GRID_W
    row = jnp.repeat(jnp.arange(rows, dtype=F32), GRID_W)
    col = jnp.tile(jnp.arange(GRID_W, dtype=F32), rows)
    inv = ROPE_BASE ** (-jnp.arange(ROPE_PAIRS_AXIS, dtype=F32) / ROPE_PAIRS_AXIS)
    ang = jnp.concatenate([row[:, None] * inv, col[:, None] * inv], axis=-1)
    cos, sin = jnp.cos(ang), jnp.sin(ang)
    n_maps = ATT_W // ATT_HEAD_DIM
    cosf = jnp.tile(jnp.concatenate([cos, cos], axis=-1), (1, n_maps))
    sinf = jnp.tile(jnp.concatenate([-sin, sin], axis=-1), (1, n_maps))
    return (jnp.concatenate([cosf, jnp.ones((Lc, ATT_W), F32)], axis=0),
            jnp.concatenate([sinf, jnp.zeros((Lc, ATT_W), F32)], axis=0))


def moe_dispatch(logits, b_g, b_e):
    T = logits.shape[0]
    g_logits = logits[:, :MOE_GROUPS] + b_g
    g_idx = jnp.argmax(g_logits, axis=-1)
    p_group = jnp.take_along_axis(jax.nn.softmax(g_logits, axis=-1), g_idx[:, None], axis=1)
    e_logits = (logits[:, MOE_GROUPS:MOE_GROUPS + N_EXPERTS] + b_e).reshape(T, MOE_GROUPS, MOE_EPG)
    e_logits = jnp.take_along_axis(e_logits, g_idx[:, None, None], axis=1)[:, 0]
    top_p, top_i = lax.top_k(jax.nn.softmax(e_logits, axis=-1), MOE_TOP_K)
    gate = p_group * top_p / jnp.sum(top_p, axis=-1, keepdims=True)
    expert = (g_idx[:, None] * MOE_EPG + top_i).reshape(-1).astype(jnp.int32)
    tok = jnp.repeat(jnp.arange(T, dtype=jnp.int32), MOE_TOP_K)
    n_assign = T * MOE_TOP_K
    n_blocks = -(-n_assign // MOE_BLOCK) + N_EXPERTS
    n_pad = n_blocks * MOE_BLOCK
    order = jnp.argsort(expert).astype(jnp.int32)
    sorted_rank = jnp.argsort(order).astype(jnp.int32)
    e_ids = jnp.arange(N_EXPERTS + 1, dtype=jnp.int32)
    bounds = jnp.sum((expert[:, None] < e_ids[None, :]).astype(jnp.int32), axis=0)
    start = bounds[:-1]
    counts = bounds[1:] - bounds[:-1]
    padded = (counts + MOE_BLOCK - 1) // MOE_BLOCK * MOE_BLOCK
    pad_end = jnp.cumsum(padded)
    pad_start = pad_end - padded
    slot_of_assign = (pad_start[expert] + sorted_rank - start[expert]).astype(jnp.int32)
    block_first = jnp.arange(n_blocks, dtype=jnp.int32) * MOE_BLOCK
    block_e = jnp.minimum(jnp.sum((pad_end[None, :] <= block_first[:, None]).astype(jnp.int32), axis=1),
                          N_EXPERTS - 1).astype(jnp.int32)
    slot_e = jnp.repeat(block_e, MOE_BLOCK)
    slot_r = jnp.arange(n_pad, dtype=jnp.int32) - pad_start[slot_e]
    slot_valid = (slot_r < counts[slot_e]) & (jnp.arange(n_pad) < pad_end[-1])
    slot_assign = order[jnp.clip(start[slot_e] + slot_r, 0, n_assign - 1)]
    slot_tok = tok[slot_assign]
    slot_gate = jnp.where(slot_valid, gate.reshape(-1)[slot_assign], 0.0)
    n_used = (pad_end[-1:] // MOE_BLOCK).astype(jnp.int32)
    return slot_tok, slot_gate, slot_of_assign, block_e, n_used


def hier_moe(h2, logits, b_g, b_e, w1, w3, w2, layer):
    T, D = h2.shape
    slot_tok, slot_gate, slot_of_assign, block_e, n_used = moe_dispatch(logits, b_g, b_e)
    yb = pl_moe_ffn(h2[slot_tok], slot_gate, block_e, n_used, w1, w3, w2, layer)
    slots = slot_of_assign.reshape(T, MOE_TOP_K)
    return yb[slots[:, 0]], yb[slots[:, 1]]


def kernel(x, c, ctx, c_ctx, w_mod, b_mod, norm1_g, norm2_g, final_g, w_in, w_out, hy_conv_w, hy_conv_b, hy_ffn_w1, hy_ffn_b1, hy_ffn_w2, hy_ffn_b2, hy_ffn_w3, hy_freq, hy_skip, hy_norm_g, s5_a_re, s5_a_im, s5_log_dt, s5_b_re, s5_b_im, s5_c_re, s5_c_im, s5_d, s5_glu_w, s5_norm_g, att_lq1, att_lk1, att_lq2, att_lk2, att_subln_g, moe_wg, moe_bg, moe_we, moe_be, moe_w1, moe_w3, moe_w2):
    B, L, D = x.shape
    Lc = ctx.shape[1]
    Lt = L + Lc
    assert B == SUBLANES and Lc == TOKEN_TILE and L % ATT_TQ == 0
    n_lat_tiles = L // TOKEN_TILE
    cosf, sinf = rope_tables(L, Lc)
    silu_c = jax.nn.silu(c)
    silu_cc = jax.nn.silu(c_ctx)
    hp = lax.Precision.HIGHEST
    q_scale = ATT_HEAD_DIM ** -0.5 * math.log2(math.e)
    xs = jnp.concatenate([x, ctx], axis=1)
    for l in range(DEPTH):
        lam_init = 0.8 - 0.6 * math.exp(-0.3 * l)
        mod = jnp.dot(silu_c, w_mod[l], precision=hp) + b_mod[l]
        cmod = jnp.dot(silu_cc, w_mod[l], precision=hp) + b_mod[l]
        modv = jnp.stack([mod, jnp.broadcast_to(cmod, mod.shape)], axis=1).reshape(2 * B, 1, N_MOD * D)

        u0, z, s5_u, q_p, q_r, k_r, v = pl_in_proj(xs, modv, norm1_g[l], w_in[l], hy_conv_w[l], hy_conv_b[l],
                                                   cosf, sinf, n_lat_tiles, q_scale)

        filt_params = (hy_ffn_w1[l], hy_ffn_b1[l], hy_ffn_w2[l], hy_ffn_b2[l], hy_ffn_w3[l], hy_freq[l])
        y_hy = hyena_conv(z, filt_params, L)

        s5_f, s5_b = s5_scan(s5_u.reshape(Lt * B, S5_CH), s5_a_re[l], s5_a_im[l], s5_log_dt[l], s5_b_re[l],
                             s5_b_im[l], s5_c_re[l], s5_c_im[l], s5_d[l], L)

        lam = (jnp.exp(jnp.sum(att_lq1[l] * att_lk1[l])) - jnp.exp(jnp.sum(att_lq2[l] * att_lk2[l])) + lam_init)
        g_scaled = (att_subln_g[l] * (1.0 - lam_init)).reshape(1, ATT_V_DIM)
        att = pl_diff_attention(q_p, q_r, k_r, v, g_scaled, lam, L)

        w_router = jnp.zeros((D, LANES), F32).at[:, :MOE_GROUPS].set(moe_wg[l])
        w_router = w_router.at[:, MOE_GROUPS:MOE_GROUPS + N_EXPERTS].set(moe_we[l])
        xs, h2, logits = pl_out_proj(xs, u0, z, y_hy, s5_f.reshape(Lt, B * S5_CH), s5_b.reshape(Lt, B * S5_CH), att,
                                     modv, hy_skip[l], hy_norm_g[l], s5_glu_w[l], s5_norm_g[l], w_out[l],
                                     norm2_g[l], w_router, n_lat_tiles)

        y0, y1 = hier_moe(h2.reshape(B * Lt, D), logits.reshape(B * Lt, LANES), moe_bg[l], moe_be[l],
                          moe_w1, moe_w3, moe_w2, l)
        xs = pl_moe_combine(xs, y0, y1, modv, final_g, n_lat_tiles, final=l == DEPTH - 1)
    return xs
```

```python
import functools
import math

import jax
import jax.numpy as jnp
import numpy as np
from jax import lax
from jax.experimental import pallas as pl
from jax.experimental.pallas import tpu as pltpu

D_MODEL = 1024
DEPTH = 4
GRID_W = 64
N_MOD = 6
EPS = 1e-6
HY_CH = D_MODEL // 4
S5_CH = D_MODEL // 4
ATT_W = D_MODEL // 2
HY_BANDS = 16
HY_DECAY_MIN = -math.log(1e-2) / 1.5
HY_DECAY_MAX = -math.log(1e-2) / 0.3
S5_GROUP = 16
S5_GROUPS = S5_CH // S5_GROUP
S5_STATE = 64
ATT_HEAD_DIM = 64
ATT_HEADS = ATT_W // (2 * ATT_HEAD_DIM)
ATT_V_DIM = 2 * ATT_HEAD_DIM
ROPE_HALF = ATT_HEAD_DIM // 2
ROPE_PAIRS_AXIS = ROPE_HALF // 2
ROPE_BASE = 10000.0
MOE_GROUPS = 4
MOE_EPG = 8
N_EXPERTS = MOE_GROUPS * MOE_EPG
MOE_TOP_K = 2
MOE_BLOCK = 256
IN_COLS = 3 * HY_CH + S5_CH + 3 * ATT_W
COL_S5 = 3 * HY_CH
COL_Q = COL_S5 + S5_CH
COL_K = COL_Q + ATT_W
COL_V = COL_K + ATT_W

LANES = 128
SUBLANES = 8
VMEM_LIMIT = 48 * 1024 * 1024
TOKEN_TILE = 256

F32 = jnp.float32
BF16 = jnp.bfloat16


def _params(n_axes, vmem=VMEM_LIMIT):
    return pltpu.CompilerParams(dimension_semantics=("arbitrary",) * n_axes, vmem_limit_bytes=vmem)


def _rms(x):
    return x * lax.rsqrt(jnp.mean(x * x, axis=-1, keepdims=True) + EPS)


def _in_kernel(x_ref, xp_ref, xn_ref, mod_ref, g_ref, w_ref, cw_ref, cb_ref, cos_ref, sin_ref,
               u0_ref, z_ref, s5_ref, qp_ref, qr_ref, k_ref, v_ref, *, n_lat_tiles, q_scale):
    i = pl.program_id(1)
    tm = x_ref.shape[1]
    d = x_ref.shape[2]
    g = g_ref[...]
    shift = mod_ref[0, :, 0:d]
    scale = mod_ref[0, :, d:2 * d]

    def norm_mod(xt):
        return (_rms(xt) * g) * (1.0 + scale) + shift

    h = jnp.concatenate([norm_mod(x_ref[0]), norm_mod(xp_ref[0]), norm_mod(xn_ref[0])], axis=0).astype(BF16)
    p = jnp.dot(h, w_ref[...], preferred_element_type=F32)

    hy = p[:tm, :COL_S5]
    is_ctx = i == n_lat_tiles
    has_prev = jnp.logical_and(i != 0, jnp.logical_not(is_ctx))
    has_next = jnp.logical_and(i != n_lat_tiles - 1, jnp.logical_not(is_ctx))
    prev_row = jnp.where(has_prev, p[tm + SUBLANES - 1:tm + SUBLANES, :COL_S5], 0.0)
    next_row = jnp.where(has_next, p[tm + SUBLANES:tm + SUBLANES + 1, :COL_S5], 0.0)
    rows = lax.broadcasted_iota(jnp.int32, (tm, 1), 0)
    up = jnp.where(rows == 0, prev_row, pltpu.roll(hy, 1, axis=0))
    dn = jnp.where(rows == tm - 1, next_row, pltpu.roll(hy, tm - 1, axis=0))
    u = up * cw_ref[0:1, :] + hy * cw_ref[1:2, :] + dn * cw_ref[2:3, :] + cb_ref[...]
    u0_ref[0] = u[:, :HY_CH]
    z = u[:, HY_CH:2 * HY_CH] * u[:, 2 * HY_CH:]
    for ct in range(HY_CH // LANES):
        z_ref[ct, 0] = z[:, ct * LANES:(ct + 1) * LANES]

    s5_ref[...] = p[:tm, COL_S5:COL_Q]

    lane = lax.broadcasted_iota(jnp.int32, (1, ATT_W), 1)
    first_half = jnp.bitwise_and(lane, ATT_HEAD_DIM - 1) < ROPE_HALF
    cos = cos_ref[...]
    sin = sin_ref[...]

    def rope(t):
        partner = jnp.where(first_half, pltpu.roll(t, ATT_W - ROPE_HALF, axis=1), pltpu.roll(t, ROPE_HALF, axis=1))
        return t * cos + partner * sin

    q = p[:tm, COL_Q:COL_K] * q_scale
    qp_ref[0] = q.astype(BF16)
    qr_ref[0] = rope(q).astype(BF16)
    k_ref[0] = rope(p[:tm, COL_K:COL_V]).astype(BF16)
    v_ref[0] = p[:tm, COL_V:].astype(BF16)


def pl_in_proj(xs, modv, norm_g, w_in, conv_w, conv_b, cosf, sinf, n_lat_tiles, q_scale):
    B, Lt, D = xs.shape
    tm = TOKEN_TILE
    n_tiles = Lt // tm
    halo_per_tile = tm // SUBLANES
    n_halo_blocks = Lt // SUBLANES
    tok = lambda w: pl.BlockSpec((1, tm, w), lambda b, i: (b, i, 0))
    const = lambda shape: pl.BlockSpec(shape, lambda b, i: (0,) * len(shape))
    n_ct = HY_CH // LANES
    out_shapes = (jax.ShapeDtypeStruct((B, Lt, HY_CH), F32), jax.ShapeDtypeStruct((n_ct, B, Lt, LANES), F32),
                  jax.ShapeDtypeStruct((Lt, B * S5_CH), F32),
                  jax.ShapeDtypeStruct((B, Lt, ATT_W), BF16), jax.ShapeDtypeStruct((B, Lt, ATT_W), BF16),
                  jax.ShapeDtypeStruct((B, Lt, ATT_W), BF16), jax.ShapeDtypeStruct((B, Lt, ATT_W), BF16))
    return pl.pallas_call(
        functools.partial(_in_kernel, n_lat_tiles=n_lat_tiles, q_scale=q_scale),
        grid=(B, n_tiles),
        in_specs=[tok(D),
                  pl.BlockSpec((1, SUBLANES, D), lambda b, i: (b, jnp.maximum(i * halo_per_tile - 1, 0), 0)),
                  pl.BlockSpec((1, SUBLANES, D),
                               lambda b, i: (b, jnp.minimum((i + 1) * halo_per_tile, n_halo_blocks - 1), 0)),
                  pl.BlockSpec((1, 1, N_MOD * D), lambda b, i: (2 * b + jnp.where(i >= n_lat_tiles, 1, 0), 0, 0)),
                  const((1, D)), const((D, IN_COLS)), const((3, COL_S5)), const((1, COL_S5)),
                  pl.BlockSpec((tm, ATT_W), lambda b, i: (i, 0)), pl.BlockSpec((tm, ATT_W), lambda b, i: (i, 0))],
        out_specs=(tok(HY_CH), pl.BlockSpec((n_ct, 1, tm, LANES), lambda b, i: (0, b, i, 0)),
                   pl.BlockSpec((tm, S5_CH), lambda b, i: (i, b)),
                   tok(ATT_W), tok(ATT_W), tok(ATT_W), tok(ATT_W)),
        out_shape=out_shapes,
        compiler_params=_params(2),
        name="in_proj",
    )(xs, xs, xs, modv, norm_g.reshape(1, D), w_in.astype(BF16), conv_w, conv_b.reshape(1, COL_S5), cosf, sinf)


def _out_kernel(x_ref, u0_ref, z_ref, yl_ref, yc_ref, sf_ref, sb_ref, al_ref, ac_ref, mod_ref, skip_ref, hg_ref,
                gw_ref, sg_ref, wo_ref, n2g_ref, rwh_ref, rwl_ref, xo_ref, h2_ref, lg_ref, *, n_lat_tiles):
    d = x_ref.shape[2]
    is_ctx = pl.program_id(1) == n_lat_tiles
    n_ct = z_ref.shape[0]
    z = jnp.concatenate([z_ref[ct, 0] for ct in range(n_ct)], axis=1)
    y = jnp.where(is_ctx, yc_ref[0], jnp.concatenate([yl_ref[ct, 0] for ct in range(n_ct)], axis=1))
    att = jnp.where(is_ctx, ac_ref[0], al_ref[0])
    hy = _rms(u0_ref[0] * (y + skip_ref[...] * z)) * hg_ref[...]
    gl = jax.nn.gelu(sf_ref[...] + sb_ref[...])
    gate = jax.nn.sigmoid(jnp.dot(gl.astype(BF16), gw_ref[...], preferred_element_type=F32))
    s5 = _rms(gl * gate) * sg_ref[...]
    mix = jnp.concatenate([hy, s5, att], axis=1).astype(BF16)
    proj = jnp.dot(mix, wo_ref[...], preferred_element_type=F32)
    xn = x_ref[0] + mod_ref[0, :, 2 * d:3 * d] * proj
    xo_ref[0] = xn
    h2 = (_rms(xn) * n2g_ref[...]) * (1.0 + mod_ref[0, :, 4 * d:5 * d]) + mod_ref[0, :, 3 * d:4 * d]
    hh = h2.astype(BF16)
    hl = (h2 - hh.astype(F32)).astype(BF16)
    h2_ref[0] = hh
    lg = jnp.dot(hh, rwh_ref[...], preferred_element_type=F32)
    lg += jnp.dot(hl, rwh_ref[...], preferred_element_type=F32)
    lg += jnp.dot(hh, rwl_ref[...], preferred_element_type=F32)
    lg_ref[0] = lg


def pl_out_proj(xs, u0, zc, y_lat, y_ctx, s5_f, s5_b, att_lat, att_ctx, modv, hy_skip, hy_norm_g, glu_w, s5_norm_g,
                w_out, norm2_g, w_router, n_lat_tiles):
    B, Lt, D = xs.shape
    tm = TOKEN_TILE
    n_ct = zc.shape[0]
    tok = lambda w: pl.BlockSpec((1, tm, w), lambda b, i: (b, i, 0))
    lat_tok = lambda w: pl.BlockSpec((1, tm, w), lambda b, i: (b, jnp.minimum(i, n_lat_tiles - 1), 0))
    ctx_tok = lambda w: pl.BlockSpec((1, tm, w), lambda b, i: (b, 0, 0))
    tb = pl.BlockSpec((tm, S5_CH), lambda b, i: (i, b))
    const = lambda shape: pl.BlockSpec(shape, lambda b, i: (0,) * len(shape))
    rwh = w_router.astype(BF16)
    rwl = (w_router - rwh.astype(F32)).astype(BF16)
    return pl.pallas_call(
        functools.partial(_out_kernel, n_lat_tiles=n_lat_tiles),
        grid=(B, Lt // tm),
        in_specs=[tok(D), tok(HY_CH), pl.BlockSpec((n_ct, 1, tm, LANES), lambda b, i: (0, b, i, 0)),
                  pl.BlockSpec((n_ct, 1, tm, LANES), lambda b, i: (0, b, jnp.minimum(i, n_lat_tiles - 1), 0)),
                  ctx_tok(HY_CH), tb, tb, lat_tok(ATT_W), ctx_tok(ATT_W),
                  pl.BlockSpec((1, 1, N_MOD * D), lambda b, i: (2 * b + jnp.where(i >= n_lat_tiles, 1, 0), 0, 0)),
                  const((1, HY_CH)), const((1, HY_CH)), const((S5_CH, S5_CH)), const((1, S5_CH)),
                  const((D, D)), const((1, D)), const((D, LANES)), const((D, LANES))],
        out_specs=(tok(D), tok(D), tok(LANES)),
        out_shape=(jax.ShapeDtypeStruct((B, Lt, D), F32), jax.ShapeDtypeStruct((B, Lt, D), BF16),
                   jax.ShapeDtypeStruct((B, Lt, LANES), F32)),
        compiler_params=_params(2),
        name="out_proj",
    )(xs, u0, zc, y_lat, y_ctx, s5_f, s5_b, att_lat, att_ctx, modv, hy_skip.reshape(1, HY_CH),
      hy_norm_g.reshape(1, HY_CH), glu_w.astype(BF16), s5_norm_g.reshape(1, S5_CH), w_out.astype(BF16),
      norm2_g.reshape(1, D), rwh, rwl)


def _dot_nt(a, b):
    return lax.dot_general(a, b, (((1,), (1,)), ((), ())), preferred_element_type=F32)


ATT_TQ = 512
ATT_SUB = 256


def _attn_kernel(qp_ref, qr_ref, k_ref, v_ref, g_ref, lam_ref, o_ref, *, n_lat):
    tq = qp_ref.shape[1]
    first_map = lax.broadcasted_iota(jnp.int32, (1, LANES), 1) < ATT_HEAD_DIM
    zero = jnp.zeros((), BF16)
    sub = min(ATT_SUB, tq)
    for r0 in range(0, tq, sub):
        qp = qp_ref[0, r0:r0 + sub, :]
        qr = qr_ref[0, r0:r0 + sub, :]
        probs = []
        for m in range(2):
            in_map = first_map if m == 0 else jnp.logical_not(first_map)
            s_c = _dot_nt(jnp.where(in_map, qp, zero), k_ref[0, n_lat:, :])
            mx = jnp.max(s_c, axis=-1, keepdims=True)
            if n_lat:
                s_l = _dot_nt(jnp.where(in_map, qr, zero), k_ref[0, :n_lat, :])
                mx = jnp.maximum(mx, jnp.max(s_l, axis=-1, keepdims=True))
                p_l = jnp.exp2(s_l - mx)
            p_c = jnp.exp2(s_c - mx)
            den = jnp.sum(p_c, axis=-1, keepdims=True)
            if n_lat:
                den = den + jnp.sum(p_l, axis=-1, keepdims=True)
            probs.append((p_c, p_l if n_lat else None, 1.0 / den))
        w0 = probs[0][2]
        w1 = lam_ref[0:1, 0:1] * probs[1][2]
        a_c = (probs[0][0] * w0 - probs[1][0] * w1).astype(BF16)
        o = jnp.dot(a_c, v_ref[0, n_lat:, :], preferred_element_type=F32)
        if n_lat:
            a_l = (probs[0][1] * w0 - probs[1][1] * w1).astype(BF16)
            o = o + jnp.dot(a_l, v_ref[0, :n_lat, :], preferred_element_type=F32)
        o_ref[0, r0:r0 + sub, :] = _rms(o) * g_ref[...]


def pl_diff_attention(qp, qr, k, v, g_scaled, lam, n_lat):
    B, Lt, _ = qp.shape
    n_ctx = Lt - n_lat
    lam_arr = jnp.full((SUBLANES, LANES), lam, F32)
    small = [pl.BlockSpec((1, LANES), lambda b, h, i: (0, 0)), pl.BlockSpec((SUBLANES, LANES), lambda b, h, i: (0, 0))]
    tq = ATT_TQ
    qspec = pl.BlockSpec((1, tq, LANES), lambda b, h, i: (b, i, h))
    kspec = pl.BlockSpec((1, Lt, LANES), lambda b, h, i: (b, 0, h))
    out_lat = pl.pallas_call(
        functools.partial(_attn_kernel, n_lat=n_lat),
        grid=(B, ATT_HEADS, n_lat // tq),
        in_specs=[qspec, qspec, kspec, kspec] + small,
        out_specs=qspec,
        out_shape=jax.ShapeDtypeStruct((B, n_lat, ATT_W), F32),
        compiler_params=_params(3),
        name="diff_attention",
    )(qp, qr, k, v, g_scaled, lam_arr)
    ctx_blk = n_lat // n_ctx
    cspec = pl.BlockSpec((1, n_ctx, LANES), lambda b, h, i: (b, ctx_blk, h))
    out_ctx = pl.pallas_call(
        functools.partial(_attn_kernel, n_lat=0),
        grid=(B, ATT_HEADS, 1),
        in_specs=[cspec, cspec, cspec, cspec] + small,
        out_specs=pl.BlockSpec((1, n_ctx, LANES), lambda b, h, i: (b, 0, h)),
        out_shape=jax.ShapeDtypeStruct((B, n_ctx, ATT_W), F32),
        compiler_params=_params(3),
        name="diff_attention_ctx",
    )(qp, qr, k, v, g_scaled, lam_arr)
    return out_lat, out_ctx


def _moe_kernel(be_ref, nb_ref, x_ref, gate_ref, w1_ref, w3_ref, w2_ref, o_ref, w1_scr, w3_scr, w2_scr):
    i = pl.program_id(0)
    new_expert = jnp.logical_or(i == 0, be_ref[i] != be_ref[jnp.maximum(i - 1, 0)])

    @pl.when(jnp.logical_and(i < nb_ref[0], new_expert))
    def _():
        w1_scr[...] = w1_ref[0, 0].astype(BF16)
        w3_scr[...] = w3_ref[0, 0].astype(BF16)
        w2_scr[...] = w2_ref[0, 0].astype(BF16)

    @pl.when(i < nb_ref[0])
    def _():
        x = x_ref[...]
        a = jnp.dot(x, w1_scr[...], preferred_element_type=F32)
        b = jnp.dot(x, w3_scr[...], preferred_element_type=F32)
        h = (a * jax.nn.sigmoid(a)) * b
        y = jnp.dot(h.astype(BF16), w2_scr[...], preferred_element_type=F32)
        rows = lax.broadcasted_iota(jnp.int32, (MOE_BLOCK, 1), 0)
        lane = lax.broadcasted_iota(jnp.int32, (1, LANES), 1)
        g_rows = gate_ref[0, 0:1, :]
        for r in range(1, MOE_BLOCK // LANES):
            g_rows = jnp.where(rows >= r * LANES, gate_ref[0, r:r + 1, :], g_rows)
        g_col = jnp.sum(jnp.where(lane == jnp.bitwise_and(rows, LANES - 1), g_rows, 0.0), axis=1, keepdims=True)
        o_ref[...] = (y * g_col).astype(o_ref.dtype)

    @pl.when(i >= nb_ref[0])
    def _():
        o_ref[...] = jnp.zeros_like(o_ref)


def pl_moe_ffn(xb, slot_gate, block_e, n_used, w1, w3, w2, layer):
    n_pad, D = xb.shape
    n_blocks = n_pad // MOE_BLOCK
    F = w1.shape[-1]
    grid_spec = pltpu.PrefetchScalarGridSpec(
        num_scalar_prefetch=2,
        grid=(n_blocks,),
        in_specs=[pl.BlockSpec((MOE_BLOCK, D), lambda i, be, nb: (i, 0)),
                  pl.BlockSpec((1, MOE_BLOCK // LANES, LANES), lambda i, be, nb: (i, 0, 0)),
                  pl.BlockSpec((1, 1, D, F), lambda i, be, nb: (layer, be[i], 0, 0)),
                  pl.BlockSpec((1, 1, D, F), lambda i, be, nb: (layer, be[i], 0, 0)),
                  pl.BlockSpec((1, 1, F, D), lambda i, be, nb: (layer, be[i], 0, 0))],
        out_specs=pl.BlockSpec((MOE_BLOCK, D), lambda i, be, nb: (i, 0)),
        scratch_shapes=[pltpu.VMEM((D, F), BF16), pltpu.VMEM((D, F), BF16), pltpu.VMEM((F, D), BF16)],
    )
    return pl.pallas_call(
        _moe_kernel,
        grid_spec=grid_spec,
        out_shape=jax.ShapeDtypeStruct((n_pad, D), BF16),
        compiler_params=_params(1),
        name="moe_ffn",
    )(block_e, n_used, xb, slot_gate.reshape(n_blocks, MOE_BLOCK // LANES, LANES), w1, w3, w2)


def _combine_kernel(x_ref, y0_ref, y1_ref, mod_ref, g_ref, o_ref, *, final):
    d = x_ref.shape[2]
    xn = x_ref[0] + mod_ref[0, :, 5 * d:6 * d] * (y0_ref[0].astype(F32) + y1_ref[0].astype(F32))
    o_ref[0] = _rms(xn) * g_ref[...] if final else xn


def pl_moe_combine(xs, y0, y1, modv, final_g, n_lat_tiles, final):
    B, Lt, D = xs.shape
    tm = TOKEN_TILE
    n_tiles = n_lat_tiles if final else Lt // tm
    tok = pl.BlockSpec((1, tm, D), lambda b, i: (b, i, 0))
    return pl.pallas_call(
        functools.partial(_combine_kernel, final=final),
        grid=(B, n_tiles),
        in_specs=[tok, tok, tok,
                  pl.BlockSpec((1, 1, N_MOD * D), lambda b, i: (2 * b + jnp.where(i >= n_lat_tiles, 1, 0), 0, 0)),
                  pl.BlockSpec((1, D), lambda b, i: (0, 0))],
        out_specs=tok,
        out_shape=jax.ShapeDtypeStruct((B, n_tiles * tm, D), F32),
        compiler_params=_params(2),
        name="moe_combine",
    )(xs, y0.reshape(B, Lt, D), y1.reshape(B, Lt, D), modv, final_g.reshape(1, D))


S5_STATES = S5_GROUPS * S5_STATE
S5_CHUNK = 64


def _s5_kernel(u_ref, wd_ref, wr_ref, ar_ref, ai_ref, d_ref, y_ref, x_scr, h_scr, hr_scr, hi_scr, *, reverse):
    ns = S5_STATES

    @pl.when(pl.program_id(0) == 0)
    def _():
        hr_scr[...] = jnp.zeros_like(hr_scr)
        hi_scr[...] = jnp.zeros_like(hi_scr)

    u = u_ref[...]
    x_scr[...] = jnp.dot(u.astype(BF16), wd_ref[...], preferred_element_type=F32)
    ar = ar_ref[...]
    ai = ai_ref[...]

    def step(hr, hi, t):
        r = pl.multiple_of(t * SUBLANES, SUBLANES)
        xr = x_scr[pl.ds(r, SUBLANES), :ns]
        xi = x_scr[pl.ds(r, SUBLANES), ns:]
        return ar * hr - ai * hi + xr, ar * hi + ai * hr + xi

    def body(j, carry):
        hr, hi = carry
        t0 = (S5_CHUNK - 1 - 2 * j) if reverse else 2 * j
        t1 = t0 - 1 if reverse else t0 + 1
        hr0, hi0 = step(hr, hi, t0)
        hr1, hi1 = step(hr0, hi0, t1)
        lo = t1 if reverse else t0
        first_r, second_r = (hr1, hr0) if reverse else (hr0, hr1)
        first_i, second_i = (hi1, hi0) if reverse else (hi0, hi1)
        r = pl.multiple_of(lo * SUBLANES, 2 * SUBLANES)
        h_scr[pl.ds(r, 2 * SUBLANES), :ns] = jnp.concatenate([first_r, second_r], axis=0).astype(BF16)
        h_scr[pl.ds(r, 2 * SUBLANES), ns:] = jnp.concatenate([first_i, second_i], axis=0).astype(BF16)
        return hr1, hi1

    hr, hi = lax.fori_loop(0, S5_CHUNK // 2, body, (hr_scr[...], hi_scr[...]))
    hr_scr[...] = hr
    hi_scr[...] = hi
    y = jnp.dot(h_scr[...], wr_ref[...], preferred_element_type=F32)
    if not reverse:
        y = y + u * d_ref[...]
    y_ref[...] = y


def pl_s5_scan(u_tb, w_drive, w_read, a_re, a_im, d_skip, *, n_lat_steps, reverse):
    rows, ch = u_tb.shape
    rc = S5_CHUNK * SUBLANES
    n_chunks = rows // rc
    n_lat = n_lat_steps // S5_CHUNK
    n_ctx = n_chunks - n_lat
    assert rows % rc == 0 and n_lat_steps % S5_CHUNK == 0
    if reverse:
        def idx(i):
            return (n_chunks - 1 - i, 0)
    else:
        def idx(i):
            return (jnp.where(i < n_ctx, n_lat + i, i - n_ctx), 0)
    const = lambda i: (0, 0)
    ns2 = 2 * S5_STATES
    return pl.pallas_call(
        functools.partial(_s5_kernel, reverse=reverse),
        grid=(n_chunks,),
        in_specs=[pl.BlockSpec((rc, ch), idx),
                  pl.BlockSpec((ch, ns2), const),
                  pl.BlockSpec((ns2, ch), const),
                  pl.BlockSpec((SUBLANES, S5_STATES), const),
                  pl.BlockSpec((SUBLANES, S5_STATES), const),
                  pl.BlockSpec((1, ch), const)],
        out_specs=pl.BlockSpec((rc, ch), idx),
        out_shape=jax.ShapeDtypeStruct((rows, ch), F32),
        scratch_shapes=[pltpu.VMEM((rc, ns2), F32), pltpu.VMEM((rc, ns2), BF16),
                        pltpu.VMEM((SUBLANES, S5_STATES), F32), pltpu.VMEM((SUBLANES, S5_STATES), F32)],
        compiler_params=_params(1),
        name="s5_scan_rev" if reverse else "s5_scan_fwd",
    )(u_tb, w_drive.astype(BF16), w_read.astype(BF16),
      jnp.broadcast_to(a_re[None, :], (SUBLANES, S5_STATES)),
      jnp.broadcast_to(a_im[None, :], (SUBLANES, S5_STATES)),
      d_skip.reshape(1, ch))


FFT_N2 = 128


def _fft_tables(L):
    N = 2 * L
    N1 = N // FFT_N2
    k1 = np.arange(N1)[:, None]
    n1 = np.arange(N1 // 2)[None, :]
    n2 = np.arange(FFT_N2)[:, None, None]
    ang = -2.0 * np.pi * (k1[None] * (n2 + FFT_N2 * n1[None])) / N
    mr, mi = np.cos(ang), np.sin(ang)
    ma = np.concatenate([np.concatenate([mr, -mi], axis=2), np.concatenate([mi, mr], axis=2)], axis=1)
    gr, gi = np.transpose(mr, (0, 2, 1)), -np.transpose(mi, (0, 2, 1))
    mainv = np.concatenate([np.concatenate([gr, -gi], axis=2), np.concatenate([gi, gr], axis=2)], axis=1)
    kk = np.arange(FFT_N2)
    a2 = -2.0 * np.pi * np.outer(kk, kk) / FFT_N2
    fr, fi = np.cos(a2), np.sin(a2)
    f_fwd = np.block([[fr, -fi], [fi, fr]])
    f_inv = np.block([[fr, fi], [-fi, fr]])
    return (jnp.asarray(ma, BF16), jnp.asarray(mainv, BF16), jnp.asarray(f_fwd, BF16), jnp.asarray(f_inv, BF16))


def _hyena_stage_a_kernel(z_ref, ma_ref, o_ref):
    def body(n2, c):
        off = pl.multiple_of(n2 * LANES, LANES)
        x = jnp.concatenate([z_ref[0, 0, :, pl.ds(off, LANES)], z_ref[0, 1, :, pl.ds(off, LANES)]], axis=0)
        o_ref[0, 0, :, pl.ds(off, LANES)] = jnp.dot(ma_ref[n2], x.astype(BF16), preferred_element_type=F32)
        return c

    lax.fori_loop(0, FFT_N2, body, 0)


def _hyena_stage_c_kernel(a_ref, h_ref, ff_ref, fi_ref, o_ref):
    n1_count = a_ref.shape[2] // 2

    def body(k1, c):
        x = jnp.concatenate([a_ref[0, 0, k1], a_ref[0, 0, n1_count + k1]], axis=0).astype(BF16)
        y = jnp.dot(ff_ref[...], x, preferred_element_type=F32)
        yr, yi = y[:FFT_N2], y[FFT_N2:]
        hr = h_ref[0, 0, k1].astype(F32)
        hi = h_ref[0, 1, k1].astype(F32)
        x2 = jnp.concatenate([yr * hr - yi * hi, yr * hi + yi * hr], axis=0).astype(BF16)
        b = jnp.dot(fi_ref[...], x2, preferred_element_type=F32)
        o_ref[0, 0, k1] = b[:FFT_N2]
        o_ref[0, 0, n1_count + k1] = b[FFT_N2:]
        return c

    lax.fori_loop(0, n1_count, body, 0)


def _hyena_stage_a_inv_kernel(b_ref, mainv_ref, o_ref):
    half = o_ref.shape[2]

    def body(n2, c):
        off = pl.multiple_of(n2 * LANES, LANES)
        r = jnp.dot(mainv_ref[n2], b_ref[0, 0, :, pl.ds(off, LANES)].astype(BF16), preferred_element_type=F32)
        o_ref[0, 0, :, pl.ds(off, LANES)] = r[:half]
        o_ref[0, 1, :, pl.ds(off, LANES)] = r[half:]
        return c

    lax.fori_loop(0, FFT_N2, body, 0)


def pl_hyena_conv(zc, filt, L):
    n_ct, B, Lt, cw = zc.shape
    assert cw == LANES
    N = 2 * L
    N1 = N // FFT_N2
    half = N1 // 2
    flat = FFT_N2 * cw
    hf = jnp.fft.fft(filt, axis=0) / N
    h2 = hf.reshape(FFT_N2, N1, n_ct, cw).transpose(2, 1, 0, 3)
    h = jnp.stack([h2.real, h2.imag], axis=1).astype(BF16)
    ma, mainv, f_fwd, f_inv = _fft_tables(L)
    n_pairs = B // 2
    grid = (n_ct, n_pairs)
    full = lambda arr: pl.BlockSpec(arr.shape, lambda ct, bp: (0,) * arr.ndim)
    pair_rows = pl.BlockSpec((1, 2, half, flat), lambda ct, bp: (ct, bp, 0, 0))
    spec_flat = pl.BlockSpec((1, 1, 2 * N1, flat), lambda ct, bp: (ct, bp, 0, 0))
    spec_slab = pl.BlockSpec((1, 1, 2 * N1, FFT_N2, cw), lambda ct, bp: (ct, bp, 0, 0, 0))
    a = pl.pallas_call(
        _hyena_stage_a_kernel, grid=grid,
        in_specs=[pair_rows, full(ma)], out_specs=spec_flat,
        out_shape=jax.ShapeDtypeStruct((n_ct, n_pairs, 2 * N1, flat), F32),
        compiler_params=_params(2), name="hyena_dft_slow",
    )(zc.reshape(n_ct, B, Lt // FFT_N2, flat), ma)
    b = pl.pallas_call(
        _hyena_stage_c_kernel, grid=grid,
        in_specs=[spec_slab, pl.BlockSpec((1, 2, N1, FFT_N2, cw), lambda ct, bp: (ct, 0, 0, 0, 0)),
                  full(f_fwd), full(f_inv)],
        out_specs=spec_slab,
        out_shape=jax.ShapeDtypeStruct((n_ct, n_pairs, 2 * N1, FFT_N2, cw), F32),
        compiler_params=_params(2), name="hyena_dft_fast_filter",
    )(a.reshape(n_ct, n_pairs, 2 * N1, FFT_N2, cw), h, f_fwd, f_inv)
    y = pl.pallas_call(
        _hyena_stage_a_inv_kernel, grid=grid,
        in_specs=[spec_flat, full(mainv)], out_specs=pair_rows,
        out_shape=jax.ShapeDtypeStruct((n_ct, B, half, flat), F32),
        compiler_params=_params(2), name="hyena_idft_slow",
    )(b.reshape(n_ct, n_pairs, 2 * N1, flat), mainv)
    return y.reshape(n_ct, B, L, cw)


def hyena_filter(L, w1, b1, w2, b2, w3, freq):
    t = jnp.arange(L, dtype=F32) / L
    ang = (2.0 * math.pi) * t[:, None] * jnp.arange(1, HY_BANDS + 1, dtype=F32)
    feat = jnp.concatenate([t[:, None], jnp.cos(ang), jnp.sin(ang)], axis=-1)
    hp = lax.Precision.HIGHEST
    h = jnp.sin(freq * (jnp.dot(feat, w1, precision=hp) + b1))
    h = jnp.sin(freq * (jnp.dot(h, w2, precision=hp) + b2))
    h = jnp.dot(h, w3, precision=hp).reshape(L, 2, HY_CH)
    window = jnp.exp(-t[:, None] * jnp.linspace(HY_DECAY_MIN, HY_DECAY_MAX, HY_CH, dtype=F32))
    h = h * window[:, None, :]
    filt = jnp.concatenate([h[:, 0], jnp.zeros((1, HY_CH), F32), h[:0:-1, 1]], axis=0)
    return filt / (jnp.sum(jnp.abs(filt), axis=0, keepdims=True) + EPS)


def hyena_conv(zc, filt_params, L):
    n_ct, B, Lt, cw = zc.shape
    Lc = Lt - L
    y_lat = pl_hyena_conv(zc, hyena_filter(L, *filt_params), L)
    z_ctx = zc[:, :, L:].transpose(1, 2, 0, 3).reshape(B, Lc, n_ct * cw)
    zf = jnp.fft.rfft(z_ctx, n=2 * Lc, axis=1)
    ff = jnp.fft.rfft(hyena_filter(Lc, *filt_params), n=2 * Lc, axis=0)
    y_ctx = jnp.fft.irfft(zf * ff[None], n=2 * Lc, axis=1)[:, :Lc]
    return y_lat, y_ctx


def _block_diag(blocks):
    G, r, c = blocks.shape
    eye = jnp.eye(G, dtype=blocks.dtype)
    return (eye[:, None, :, None] * blocks[:, :, None, :]).reshape(G * r, G * c)


def s5_scan(u_tb, a_re, a_im, log_dt, b_re, b_im, c_re, c_im, d_skip, n_lat_steps):
    outs = []
    for direction in range(2):
        A = lax.complex(a_re[direction], a_im[direction])
        dtA = jnp.exp(log_dt[direction])[:, None] * A
        a_bar = jnp.exp(dtA)
        b_bar = ((a_bar - 1.0) / A)[:, :, None] * lax.complex(b_re[direction], b_im[direction])
        bt_re = jnp.transpose(b_bar.real, (0, 2, 1))
        bt_im = jnp.transpose(b_bar.imag, (0, 2, 1))
        w_drive = jnp.concatenate([_block_diag(bt_re), _block_diag(bt_im)], axis=1)
        ct_re = jnp.transpose(c_re[direction], (0, 2, 1))
        ct_im = jnp.transpose(c_im[direction], (0, 2, 1))
        w_read = jnp.concatenate([_block_diag(ct_re), -_block_diag(ct_im)], axis=0)
        outs.append(pl_s5_scan(u_tb, w_drive, w_read, a_bar.real.reshape(-1), a_bar.imag.reshape(-1), d_skip,
                               n_lat_steps=n_lat_steps, reverse=direction == 1))
    return outs


def rope_tables(L, Lc):
    rows = L // GRID_W
    row = jnp.repeat(jnp.arange(rows, dtype=F32), GRID_W)
    col = jnp.tile(jnp.arange(GRID_W, dtype=F32), rows)
    inv = ROPE_BASE ** (-jnp.arange(ROPE_PAIRS_AXIS, dtype=F32) / ROPE_PAIRS_AXIS)
    ang = jnp.concatenate([row[:, None] * inv, col[:, None] * inv], axis=-1)
    cos, sin = jnp.cos(ang), jnp.sin(ang)
    n_maps = ATT_W // ATT_HEAD_DIM
    cosf = jnp.tile(jnp.concatenate([cos, cos], axis=-1), (1, n_maps))
    sinf = jnp.tile(jnp.concatenate([-sin, sin], axis=-1), (1, n_maps))
    return (jnp.concatenate([cosf, jnp.ones((Lc, ATT_W), F32)], axis=0),
            jnp.concatenate([sinf, jnp.zeros((Lc, ATT_W), F32)], axis=0))


def moe_dispatch(logits, b_g, b_e):
    T = logits.shape[0]
    g_logits = logits[:, :MOE_GROUPS] + b_g
    g_idx = jnp.argmax(g_logits, axis=-1)
    p_group = jnp.take_along_axis(jax.nn.softmax(g_logits, axis=-1), g_idx[:, None], axis=1)
    e_logits = (logits[:, MOE_GROUPS:MOE_GROUPS + N_EXPERTS] + b_e).reshape(T, MOE_GROUPS, MOE_EPG)
    e_logits = jnp.take_along_axis(e_logits, g_idx[:, None, None], axis=1)[:, 0]
    top_p, top_i = lax.top_k(jax.nn.softmax(e_logits, axis=-1), MOE_TOP_K)
    gate = p_group * top_p / jnp.sum(top_p, axis=-1, keepdims=True)
    expert = (g_idx[:, None] * MOE_EPG + top_i).reshape(-1).astype(jnp.int32)
    tok = jnp.repeat(jnp.arange(T, dtype=jnp.int32), MOE_TOP_K)
    n_assign = T * MOE_TOP_K
    n_blocks = -(-n_assign // MOE_BLOCK) + N_EXPERTS
    n_pad = n_blocks * MOE_BLOCK
    order = jnp.argsort(expert).astype(jnp.int32)
    sorted_rank = jnp.argsort(order).astype(jnp.int32)
    e_ids = jnp.arange(N_EXPERTS + 1, dtype=jnp.int32)
    bounds = jnp.sum((expert[:, None] < e_ids[None, :]).astype(jnp.int32), axis=0)
    start = bounds[:-1]
    counts = bounds[1:] - bounds[:-1]
    padded = (counts + MOE_BLOCK - 1) // MOE_BLOCK * MOE_BLOCK
    pad_end = jnp.cumsum(padded)
    pad_start = pad_end - padded
    slot_of_assign = (pad_start[expert] + sorted_rank - start[expert]).astype(jnp.int32)
    block_first = jnp.arange(n_blocks, dtype=jnp.int32) * MOE_BLOCK
    block_e = jnp.minimum(jnp.sum((pad_end[None, :] <= block_first[:, None]).astype(jnp.int32), axis=1),
                          N_EXPERTS - 1).astype(jnp.int32)
    slot_e = jnp.repeat(block_e, MOE_BLOCK)
    slot_r = jnp.arange(n_pad, dtype=jnp.int32) - pad_start[slot_e]
    slot_valid = (slot_r < counts[slot_e]) & (jnp.arange(n_pad) < pad_end[-1])
    slot_assign = order[jnp.clip(start[slot_e] + slot_r, 0, n_assign - 1)]
    slot_tok = tok[slot_assign]
    slot_gate = jnp.where(slot_valid, gate.reshape(-1)[slot_assign], 0.0)
    n_used = (pad_end[-1:] // MOE_BLOCK).astype(jnp.int32)
    return slot_tok, slot_gate, slot_of_assign, block_e, n_used


def hier_moe(h2, logits, b_g, b_e, w1, w3, w2, layer):
    T, D = h2.shape
    slot_tok, slot_gate, slot_of_assign, block_e, n_used = moe_dispatch(logits, b_g, b_e)
    yb = pl_moe_ffn(h2[slot_tok], slot_gate, block_e, n_used, w1, w3, w2, layer)
    slots = slot_of_assign.reshape(T, MOE_TOP_K)
    return yb[slots[:, 0]], yb[slots[:, 1]]


def kernel(x, c, ctx, c_ctx, w_mod, b_mod, norm1_g, norm2_g, final_g, w_in, w_out, hy_conv_w, hy_conv_b, hy_ffn_w1, hy_ffn_b1, hy_ffn_w2, hy_ffn_b2, hy_ffn_w3, hy_freq, hy_skip, hy_norm_g, s5_a_re, s5_a_im, s5_log_dt, s5_b_re, s5_b_im, s5_c_re, s5_c_im, s5_d, s5_glu_w, s5_norm_g, att_lq1, att_lk1, att_lq2, att_lk2, att_subln_g, moe_wg, moe_bg, moe_we, moe_be, moe_w1, moe_w3, moe_w2):
    B, L, D = x.shape
    Lc = ctx.shape[1]
    Lt = L + Lc
    assert B == SUBLANES and Lc == TOKEN_TILE and L % ATT_TQ == 0
    n_lat_tiles = L // TOKEN_TILE
    cosf, sinf = rope_tables(L, Lc)
    silu_c = jax.nn.silu(c)
    silu_cc = jax.nn.silu(c_ctx)
    hp = lax.Precision.HIGHEST
    q_scale = ATT_HEAD_DIM ** -0.5 * math.log2(math.e)
    xs = jnp.concatenate([x, ctx], axis=1)
    for l in range(DEPTH):
        lam_init = 0.8 - 0.6 * math.exp(-0.3 * l)
        mod = jnp.dot(silu_c, w_mod[l], precision=hp) + b_mod[l]
        cmod = jnp.dot(silu_cc, w_mod[l], precision=hp) + b_mod[l]
        modv = jnp.stack([mod, jnp.broadcast_to(cmod, mod.shape)], axis=1).reshape(2 * B, 1, N_MOD * D)

        u0, z, s5_u, q_p, q_r, k_r, v = pl_in_proj(xs, modv, norm1_g[l], w_in[l], hy_conv_w[l], hy_conv_b[l],
                                                   cosf, sinf, n_lat_tiles, q_scale)

        filt_params = (hy_ffn_w1[l], hy_ffn_b1[l], hy_ffn_w2[l], hy_ffn_b2[l], hy_ffn_w3[l], hy_freq[l])
        y_lat, y_ctx = hyena_conv(z, filt_params, L)

        s5_f, s5_b = s5_scan(s5_u.reshape(Lt * B, S5_CH), s5_a_re[l], s5_a_im[l], s5_log_dt[l], s5_b_re[l],
                             s5_b_im[l], s5_c_re[l], s5_c_im[l], s5_d[l], L)

        lam = (jnp.exp(jnp.sum(att_lq1[l] * att_lk1[l])) - jnp.exp(jnp.sum(att_lq2[l] * att_lk2[l])) + lam_init)
        g_scaled = (att_subln_g[l] * (1.0 - lam_init)).reshape(1, ATT_V_DIM)
        att_lat, att_ctx = pl_diff_attention(q_p, q_r, k_r, v, g_scaled, lam, L)

        w_router = jnp.zeros((D, LANES), F32).at[:, :MOE_GROUPS].set(moe_wg[l])
        w_router = w_router.at[:, MOE_GROUPS:MOE_GROUPS + N_EXPERTS].set(moe_we[l])
        xs, h2, logits = pl_out_proj(xs, u0, z, y_lat, y_ctx, s5_f.reshape(Lt, B * S5_CH),
                                     s5_b.reshape(Lt, B * S5_CH), att_lat, att_ctx, modv, hy_skip[l], hy_norm_g[l],
                                     s5_glu_w[l], s5_norm_g[l], w_out[l], norm2_g[l], w_router, n_lat_tiles)

        y0, y1 = hier_moe(h2.reshape(B * Lt, D), logits.reshape(B * Lt, LANES), moe_bg[l], moe_be[l],
                          moe_w1, moe_w3, moe_w2, l)
        xs = pl_moe_combine(xs, y0, y1, modv, final_g, n_lat_tiles, final=l == DEPTH - 1)
    return xs
```

```python
import functools
import math

import jax
import jax.numpy as jnp
import numpy as np
from jax import lax
from jax.experimental import pallas as pl
from jax.experimental.pallas import tpu as pltpu

D_MODEL = 1024
DEPTH = 4
GRID_W = 64
N_MOD = 6
EPS = 1e-6
HY_CH = D_MODEL // 4
S5_CH = D_MODEL // 4
ATT_W = D_MODEL // 2
HY_BANDS = 16
HY_DECAY_MIN = -math.log(1e-2) / 1.5
HY_DECAY_MAX = -math.log(1e-2) / 0.3
S5_GROUP = 16
S5_GROUPS = S5_CH // S5_GROUP
S5_STATE = 64
ATT_HEAD_DIM = 64
ATT_HEADS = ATT_W // (2 * ATT_HEAD_DIM)
ATT_V_DIM = 2 * ATT_HEAD_DIM
ROPE_HALF = ATT_HEAD_DIM // 2
ROPE_PAIRS_AXIS = ROPE_HALF // 2
ROPE_BASE = 10000.0
MOE_GROUPS = 4
MOE_EPG = 8
N_EXPERTS = MOE_GROUPS * MOE_EPG
MOE_TOP_K = 2
MOE_BLOCK = 512
IN_COLS = 3 * HY_CH + S5_CH + 3 * ATT_W
COL_S5 = 3 * HY_CH
COL_Q = COL_S5 + S5_CH
COL_K = COL_Q + ATT_W
COL_V = COL_K + ATT_W

LANES = 128
SUBLANES = 8
VMEM_LIMIT = 48 * 1024 * 1024
TOKEN_TILE = 256

F32 = jnp.float32
BF16 = jnp.bfloat16


def _params(n_axes, vmem=VMEM_LIMIT):
    return pltpu.CompilerParams(dimension_semantics=("arbitrary",) * n_axes, vmem_limit_bytes=vmem)


def _rms(x):
    return x * lax.rsqrt(jnp.mean(x * x, axis=-1, keepdims=True) + EPS)


def _in_kernel(x_ref, xp_ref, xn_ref, mod_ref, g_ref, w_ref, cw_ref, cb_ref, cos_ref, sin_ref,
               u0_ref, z_ref, s5_ref, qp_ref, qr_ref, k_ref, v_ref, *, n_lat_tiles, q_scale):
    i = pl.program_id(1)
    tm = x_ref.shape[1]
    d = x_ref.shape[2]
    g = g_ref[...]
    shift = mod_ref[0, :, 0:d]
    scale = mod_ref[0, :, d:2 * d]

    def norm_mod(xt):
        return (_rms(xt) * g) * (1.0 + scale) + shift

    h = jnp.concatenate([norm_mod(x_ref[0]), norm_mod(xp_ref[0]), norm_mod(xn_ref[0])], axis=0).astype(BF16)
    p = jnp.dot(h, w_ref[...], preferred_element_type=F32)

    hy = p[:tm, :COL_S5]
    is_ctx = i == n_lat_tiles
    has_prev = jnp.logical_and(i != 0, jnp.logical_not(is_ctx))
    has_next = jnp.logical_and(i != n_lat_tiles - 1, jnp.logical_not(is_ctx))
    prev_row = jnp.where(has_prev, p[tm + SUBLANES - 1:tm + SUBLANES, :COL_S5], 0.0)
    next_row = jnp.where(has_next, p[tm + SUBLANES:tm + SUBLANES + 1, :COL_S5], 0.0)
    rows = lax.broadcasted_iota(jnp.int32, (tm, 1), 0)
    up = jnp.where(rows == 0, prev_row, pltpu.roll(hy, 1, axis=0))
    dn = jnp.where(rows == tm - 1, next_row, pltpu.roll(hy, tm - 1, axis=0))
    u = up * cw_ref[0:1, :] + hy * cw_ref[1:2, :] + dn * cw_ref[2:3, :] + cb_ref[...]
    u0_ref[0] = u[:, :HY_CH].astype(u0_ref.dtype)
    z = (u[:, HY_CH:2 * HY_CH] * u[:, 2 * HY_CH:]).astype(z_ref.dtype)
    for ct in range(HY_CH // LANES):
        z_ref[ct, 0] = z[:, ct * LANES:(ct + 1) * LANES]

    s5_ref[...] = p[:tm, COL_S5:COL_Q]

    lane = lax.broadcasted_iota(jnp.int32, (1, ATT_W), 1)
    first_half = jnp.bitwise_and(lane, ATT_HEAD_DIM - 1) < ROPE_HALF
    cos = cos_ref[...]
    sin = sin_ref[...]

    def rope(t):
        partner = jnp.where(first_half, pltpu.roll(t, ATT_W - ROPE_HALF, axis=1), pltpu.roll(t, ROPE_HALF, axis=1))
        return t * cos + partner * sin

    q = p[:tm, COL_Q:COL_K] * q_scale
    qp_ref[0] = q.astype(BF16)
    qr_ref[0] = rope(q).astype(BF16)
    k_ref[0] = rope(p[:tm, COL_K:COL_V]).astype(BF16)
    v_ref[0] = p[:tm, COL_V:].astype(BF16)


def pl_in_proj(xs, modv, norm_g, w_in, conv_w, conv_b, cosf, sinf, n_lat_tiles, q_scale):
    B, Lt, D = xs.shape
    tm = TOKEN_TILE
    n_tiles = Lt // tm
    halo_per_tile = tm // SUBLANES
    n_halo_blocks = Lt // SUBLANES
    tok = lambda w: pl.BlockSpec((1, tm, w), lambda b, i: (b, i, 0))
    const = lambda shape: pl.BlockSpec(shape, lambda b, i: (0,) * len(shape))
    n_ct = HY_CH // LANES
    out_shapes = (jax.ShapeDtypeStruct((B, Lt, HY_CH), BF16), jax.ShapeDtypeStruct((n_ct, B, Lt, LANES), BF16),
                  jax.ShapeDtypeStruct((Lt, B * S5_CH), F32),
                  jax.ShapeDtypeStruct((B, Lt, ATT_W), BF16), jax.ShapeDtypeStruct((B, Lt, ATT_W), BF16),
                  jax.ShapeDtypeStruct((B, Lt, ATT_W), BF16), jax.ShapeDtypeStruct((B, Lt, ATT_W), BF16))
    return pl.pallas_call(
        functools.partial(_in_kernel, n_lat_tiles=n_lat_tiles, q_scale=q_scale),
        grid=(B, n_tiles),
        in_specs=[tok(D),
                  pl.BlockSpec((1, SUBLANES, D), lambda b, i: (b, jnp.maximum(i * halo_per_tile - 1, 0), 0)),
                  pl.BlockSpec((1, SUBLANES, D),
                               lambda b, i: (b, jnp.minimum((i + 1) * halo_per_tile, n_halo_blocks - 1), 0)),
                  pl.BlockSpec((1, 1, N_MOD * D), lambda b, i: (2 * b + jnp.where(i >= n_lat_tiles, 1, 0), 0, 0)),
                  const((1, D)), const((D, IN_COLS)), const((3, COL_S5)), const((1, COL_S5)),
                  pl.BlockSpec((tm, ATT_W), lambda b, i: (i, 0)), pl.BlockSpec((tm, ATT_W), lambda b, i: (i, 0))],
        out_specs=(tok(HY_CH), pl.BlockSpec((n_ct, 1, tm, LANES), lambda b, i: (0, b, i, 0)),
                   pl.BlockSpec((tm, S5_CH), lambda b, i: (i, b)),
                   tok(ATT_W), tok(ATT_W), tok(ATT_W), tok(ATT_W)),
        out_shape=out_shapes,
        compiler_params=_params(2),
        name="in_proj",
    )(xs, xs, xs, modv, norm_g.reshape(1, D), w_in.astype(BF16), conv_w, conv_b.reshape(1, COL_S5), cosf, sinf)


def _out_kernel(x_ref, u0_ref, z_ref, yl_ref, yc_ref, sf_ref, sb_ref, al_ref, ac_ref, mod_ref, skip_ref, hg_ref,
                gw_ref, sg_ref, wo_ref, n2g_ref, rwh_ref, rwl_ref, xo_ref, h2_ref, lg_ref, *, n_lat_tiles):
    d = x_ref.shape[2]
    is_ctx = pl.program_id(1) == n_lat_tiles
    n_ct = z_ref.shape[0]
    z = jnp.concatenate([z_ref[ct, 0] for ct in range(n_ct)], axis=1).astype(F32)
    y_lat = jnp.concatenate([yl_ref[ct, 0] for ct in range(n_ct)], axis=1).astype(F32)
    y = jnp.where(is_ctx, yc_ref[0], y_lat)
    att = jnp.where(is_ctx, ac_ref[0], al_ref[0])
    hy = _rms(u0_ref[0].astype(F32) * (y + skip_ref[...] * z)) * hg_ref[...]
    gl = jax.nn.gelu(sf_ref[...].astype(F32) + sb_ref[...].astype(F32))
    gate = jax.nn.sigmoid(jnp.dot(gl.astype(BF16), gw_ref[...], preferred_element_type=F32))
    s5 = _rms(gl * gate) * sg_ref[...]
    mix = jnp.concatenate([hy.astype(BF16), s5.astype(BF16), att], axis=1)
    proj = jnp.dot(mix, wo_ref[...], preferred_element_type=F32)
    xn = x_ref[0] + mod_ref[0, :, 2 * d:3 * d] * proj
    xo_ref[0] = xn
    h2 = (_rms(xn) * n2g_ref[...]) * (1.0 + mod_ref[0, :, 4 * d:5 * d]) + mod_ref[0, :, 3 * d:4 * d]
    hh = h2.astype(BF16)
    hl = (h2 - hh.astype(F32)).astype(BF16)
    h2_ref[0] = hh
    lg = jnp.dot(hh, rwh_ref[...], preferred_element_type=F32)
    lg += jnp.dot(hl, rwh_ref[...], preferred_element_type=F32)
    lg += jnp.dot(hh, rwl_ref[...], preferred_element_type=F32)
    lg_ref[0] = lg


def pl_out_proj(xs, u0, zc, y_lat, y_ctx, s5_f, s5_b, att_lat, att_ctx, modv, hy_skip, hy_norm_g, glu_w, s5_norm_g,
                w_out, norm2_g, w_router, n_lat_tiles):
    B, Lt, D = xs.shape
    tm = TOKEN_TILE
    n_ct = zc.shape[0]
    tok = lambda w: pl.BlockSpec((1, tm, w), lambda b, i: (b, i, 0))
    lat_tok = lambda w: pl.BlockSpec((1, tm, w), lambda b, i: (b, jnp.minimum(i, n_lat_tiles - 1), 0))
    ctx_tok = lambda w: pl.BlockSpec((1, tm, w), lambda b, i: (b, 0, 0))
    tb = pl.BlockSpec((tm, S5_CH), lambda b, i: (i, b))
    const = lambda shape: pl.BlockSpec(shape, lambda b, i: (0,) * len(shape))
    rwh = w_router.astype(BF16)
    rwl = (w_router - rwh.astype(F32)).astype(BF16)
    return pl.pallas_call(
        functools.partial(_out_kernel, n_lat_tiles=n_lat_tiles),
        grid=(B, Lt // tm),
        in_specs=[tok(D), tok(HY_CH), pl.BlockSpec((n_ct, 1, tm, LANES), lambda b, i: (0, b, i, 0)),
                  pl.BlockSpec((n_ct, 1, tm, LANES), lambda b, i: (0, b, jnp.minimum(i, n_lat_tiles - 1), 0)),
                  ctx_tok(HY_CH), tb, tb, lat_tok(ATT_W), ctx_tok(ATT_W),
                  pl.BlockSpec((1, 1, N_MOD * D), lambda b, i: (2 * b + jnp.where(i >= n_lat_tiles, 1, 0), 0, 0)),
                  const((1, HY_CH)), const((1, HY_CH)), const((S5_CH, S5_CH)), const((1, S5_CH)),
                  const((D, D)), const((1, D)), const((D, LANES)), const((D, LANES))],
        out_specs=(tok(D), tok(D), tok(LANES)),
        out_shape=(jax.ShapeDtypeStruct((B, Lt, D), F32), jax.ShapeDtypeStruct((B, Lt, D), BF16),
                   jax.ShapeDtypeStruct((B, Lt, LANES), F32)),
        compiler_params=_params(2),
        name="out_proj",
    )(xs, u0, zc, y_lat, y_ctx, s5_f, s5_b, att_lat, att_ctx, modv, hy_skip.reshape(1, HY_CH),
      hy_norm_g.reshape(1, HY_CH), glu_w.astype(BF16), s5_norm_g.reshape(1, S5_CH), w_out.astype(BF16),
      norm2_g.reshape(1, D), rwh, rwl)


def _dot_nt(a, b):
    return lax.dot_general(a, b, (((1,), (1,)), ((), ())), preferred_element_type=F32)


ATT_TQ = 512
ATT_SUB = 256


def _attn_kernel(qp_ref, qr_ref, k_ref, v_ref, g_ref, lam_ref, o_ref, *, n_lat):
    tq = qp_ref.shape[1]
    first_map = lax.broadcasted_iota(jnp.int32, (1, LANES), 1) < ATT_HEAD_DIM
    zero = jnp.zeros((), BF16)
    sub = min(ATT_SUB, tq)
    for r0 in range(0, tq, sub):
        qp = qp_ref[0, r0:r0 + sub, :]
        qr = qr_ref[0, r0:r0 + sub, :]
        probs = []
        for m in range(2):
            in_map = first_map if m == 0 else jnp.logical_not(first_map)
            s_c = _dot_nt(jnp.where(in_map, qp, zero), k_ref[0, n_lat:, :])
            mx = jnp.max(s_c, axis=-1, keepdims=True)
            if n_lat:
                s_l = _dot_nt(jnp.where(in_map, qr, zero), k_ref[0, :n_lat, :])
                mx = jnp.maximum(mx, jnp.max(s_l, axis=-1, keepdims=True))
                p_l = jnp.exp2(s_l - mx)
            p_c = jnp.exp2(s_c - mx)
            den = jnp.sum(p_c, axis=-1, keepdims=True)
            if n_lat:
                den = den + jnp.sum(p_l, axis=-1, keepdims=True)
            probs.append((p_c, p_l if n_lat else None, 1.0 / den))
        w0 = probs[0][2]
        w1 = lam_ref[0:1, 0:1] * probs[1][2]
        a_c = (probs[0][0] * w0 - probs[1][0] * w1).astype(BF16)
        o = jnp.dot(a_c, v_ref[0, n_lat:, :], preferred_element_type=F32)
        if n_lat:
            a_l = (probs[0][1] * w0 - probs[1][1] * w1).astype(BF16)
            o = o + jnp.dot(a_l, v_ref[0, :n_lat, :], preferred_element_type=F32)
        o_ref[0, r0:r0 + sub, :] = (_rms(o) * g_ref[...]).astype(o_ref.dtype)


def pl_diff_attention(qp, qr, k, v, g_scaled, lam, n_lat):
    B, Lt, _ = qp.shape
    n_ctx = Lt - n_lat
    lam_arr = jnp.full((SUBLANES, LANES), lam, F32)
    small = [pl.BlockSpec((1, LANES), lambda b, h, i: (0, 0)), pl.BlockSpec((SUBLANES, LANES), lambda b, h, i: (0, 0))]
    tq = ATT_TQ
    qspec = pl.BlockSpec((1, tq, LANES), lambda b, h, i: (b, i, h))
    kspec = pl.BlockSpec((1, Lt, LANES), lambda b, h, i: (b, 0, h))
    out_lat = pl.pallas_call(
        functools.partial(_attn_kernel, n_lat=n_lat),
        grid=(B, ATT_HEADS, n_lat // tq),
        in_specs=[qspec, qspec, kspec, kspec] + small,
        out_specs=qspec,
        out_shape=jax.ShapeDtypeStruct((B, n_lat, ATT_W), BF16),
        compiler_params=_params(3),
        name="diff_attention",
    )(qp, qr, k, v, g_scaled, lam_arr)
    ctx_blk = n_lat // n_ctx
    cspec = pl.BlockSpec((1, n_ctx, LANES), lambda b, h, i: (b, ctx_blk, h))
    out_ctx = pl.pallas_call(
        functools.partial(_attn_kernel, n_lat=0),
        grid=(B, ATT_HEADS, 1),
        in_specs=[cspec, cspec, cspec, cspec] + small,
        out_specs=pl.BlockSpec((1, n_ctx, LANES), lambda b, h, i: (b, 0, h)),
        out_shape=jax.ShapeDtypeStruct((B, n_ctx, ATT_W), BF16),
        compiler_params=_params(3),
        name="diff_attention_ctx",
    )(qp, qr, k, v, g_scaled, lam_arr)
    return out_lat, out_ctx


def _moe_kernel(be_ref, nb_ref, x_ref, gate_ref, w1_ref, w3_ref, w2_ref, o_ref, w1_scr, w3_scr, w2_scr):
    i = pl.program_id(0)
    new_expert = jnp.logical_or(i == 0, be_ref[i] != be_ref[jnp.maximum(i - 1, 0)])

    @pl.when(jnp.logical_and(i < nb_ref[0], new_expert))
    def _():
        w1_scr[...] = w1_ref[0, 0].astype(BF16)
        w3_scr[...] = w3_ref[0, 0].astype(BF16)
        w2_scr[...] = w2_ref[0, 0].astype(BF16)

    @pl.when(i < nb_ref[0])
    def _():
        x = x_ref[...]
        a = jnp.dot(x, w1_scr[...], preferred_element_type=F32)
        b = jnp.dot(x, w3_scr[...], preferred_element_type=F32)
        h = (a * jax.nn.sigmoid(a)) * b
        y = jnp.dot(h.astype(BF16), w2_scr[...], preferred_element_type=F32)
        rows = lax.broadcasted_iota(jnp.int32, (MOE_BLOCK, 1), 0)
        lane = lax.broadcasted_iota(jnp.int32, (1, LANES), 1)
        g_rows = gate_ref[0, 0:1, :]
        for r in range(1, MOE_BLOCK // LANES):
            g_rows = jnp.where(rows >= r * LANES, gate_ref[0, r:r + 1, :], g_rows)
        g_col = jnp.sum(jnp.where(lane == jnp.bitwise_and(rows, LANES - 1), g_rows, 0.0), axis=1, keepdims=True)
        o_ref[...] = (y * g_col).astype(o_ref.dtype)

    @pl.when(i >= nb_ref[0])
    def _():
        o_ref[...] = jnp.zeros_like(o_ref)


def pl_moe_ffn(xb, slot_gate, block_e, n_used, w1, w3, w2, layer):
    n_pad, D = xb.shape
    n_blocks = n_pad // MOE_BLOCK
    F = w1.shape[-1]
    grid_spec = pltpu.PrefetchScalarGridSpec(
        num_scalar_prefetch=2,
        grid=(n_blocks,),
        in_specs=[pl.BlockSpec((MOE_BLOCK, D), lambda i, be, nb: (i, 0)),
                  pl.BlockSpec((1, MOE_BLOCK // LANES, LANES), lambda i, be, nb: (i, 0, 0)),
                  pl.BlockSpec((1, 1, D, F), lambda i, be, nb: (layer, be[i], 0, 0)),
                  pl.BlockSpec((1, 1, D, F), lambda i, be, nb: (layer, be[i], 0, 0)),
                  pl.BlockSpec((1, 1, F, D), lambda i, be, nb: (layer, be[i], 0, 0))],
        out_specs=pl.BlockSpec((MOE_BLOCK, D), lambda i, be, nb: (i, 0)),
        scratch_shapes=[pltpu.VMEM((D, F), BF16), pltpu.VMEM((D, F), BF16), pltpu.VMEM((F, D), BF16)],
    )
    return pl.pallas_call(
        _moe_kernel,
        grid_spec=grid_spec,
        out_shape=jax.ShapeDtypeStruct((n_pad, D), BF16),
        compiler_params=_params(1),
        name="moe_ffn",
    )(block_e, n_used, xb, slot_gate.reshape(n_blocks, MOE_BLOCK // LANES, LANES), w1, w3, w2)


def _combine_kernel(x_ref, y0_ref, y1_ref, mod_ref, g_ref, o_ref, *, final):
    d = x_ref.shape[2]
    xn = x_ref[0] + mod_ref[0, :, 5 * d:6 * d] * (y0_ref[0].astype(F32) + y1_ref[0].astype(F32))
    o_ref[0] = _rms(xn) * g_ref[...] if final else xn


def pl_moe_combine(xs, y0, y1, modv, final_g, n_lat_tiles, final):
    B, Lt, D = xs.shape
    tm = TOKEN_TILE
    n_tiles = n_lat_tiles if final else Lt // tm
    tok = pl.BlockSpec((1, tm, D), lambda b, i: (b, i, 0))
    return pl.pallas_call(
        functools.partial(_combine_kernel, final=final),
        grid=(B, n_tiles),
        in_specs=[tok, tok, tok,
                  pl.BlockSpec((1, 1, N_MOD * D), lambda b, i: (2 * b + jnp.where(i >= n_lat_tiles, 1, 0), 0, 0)),
                  pl.BlockSpec((1, D), lambda b, i: (0, 0))],
        out_specs=tok,
        out_shape=jax.ShapeDtypeStruct((B, n_tiles * tm, D), F32),
        compiler_params=_params(2),
        name="moe_combine",
    )(xs, y0.reshape(B, Lt, D), y1.reshape(B, Lt, D), modv, final_g.reshape(1, D))


S5_STATES = S5_GROUPS * S5_STATE
S5_CHUNK = 64


def _s5_kernel(u_ref, wd_ref, wr_ref, ar_ref, ai_ref, d_ref, y_ref, x_scr, h_scr, hr_scr, hi_scr, *, reverse):
    ns = S5_STATES

    @pl.when(pl.program_id(0) == 0)
    def _():
        hr_scr[...] = jnp.zeros_like(hr_scr)
        hi_scr[...] = jnp.zeros_like(hi_scr)

    u = u_ref[...]
    x_scr[...] = jnp.dot(u.astype(BF16), wd_ref[...], preferred_element_type=F32)
    ar = ar_ref[...]
    ai = ai_ref[...]

    def step(hr, hi, t):
        r = pl.multiple_of(t * SUBLANES, SUBLANES)
        xr = x_scr[pl.ds(r, SUBLANES), :ns]
        xi = x_scr[pl.ds(r, SUBLANES), ns:]
        return ar * hr - ai * hi + xr, ar * hi + ai * hr + xi

    def body(j, carry):
        hr, hi = carry
        t0 = (S5_CHUNK - 1 - 2 * j) if reverse else 2 * j
        t1 = t0 - 1 if reverse else t0 + 1
        hr0, hi0 = step(hr, hi, t0)
        hr1, hi1 = step(hr0, hi0, t1)
        lo = t1 if reverse else t0
        first_r, second_r = (hr1, hr0) if reverse else (hr0, hr1)
        first_i, second_i = (hi1, hi0) if reverse else (hi0, hi1)
        r = pl.multiple_of(lo * SUBLANES, 2 * SUBLANES)
        h_scr[pl.ds(r, 2 * SUBLANES), :ns] = jnp.concatenate([first_r, second_r], axis=0).astype(BF16)
        h_scr[pl.ds(r, 2 * SUBLANES), ns:] = jnp.concatenate([first_i, second_i], axis=0).astype(BF16)
        return hr1, hi1

    hr, hi = lax.fori_loop(0, S5_CHUNK // 2, body, (hr_scr[...], hi_scr[...]))
    hr_scr[...] = hr
    hi_scr[...] = hi
    y = jnp.dot(h_scr[...], wr_ref[...], preferred_element_type=F32)
    if not reverse:
        y = y + u * d_ref[...]
    y_ref[...] = y.astype(y_ref.dtype)


def pl_s5_scan(u_tb, w_drive, w_read, a_re, a_im, d_skip, *, n_lat_steps, reverse):
    rows, ch = u_tb.shape
    rc = S5_CHUNK * SUBLANES
    n_chunks = rows // rc
    n_lat = n_lat_steps // S5_CHUNK
    n_ctx = n_chunks - n_lat
    assert rows % rc == 0 and n_lat_steps % S5_CHUNK == 0
    if reverse:
        def idx(i):
            return (n_chunks - 1 - i, 0)
    else:
        def idx(i):
            return (jnp.where(i < n_ctx, n_lat + i, i - n_ctx), 0)
    const = lambda i: (0, 0)
    ns2 = 2 * S5_STATES
    return pl.pallas_call(
        functools.partial(_s5_kernel, reverse=reverse),
        grid=(n_chunks,),
        in_specs=[pl.BlockSpec((rc, ch), idx),
                  pl.BlockSpec((ch, ns2), const),
                  pl.BlockSpec((ns2, ch), const),
                  pl.BlockSpec((SUBLANES, S5_STATES), const),
                  pl.BlockSpec((SUBLANES, S5_STATES), const),
                  pl.BlockSpec((1, ch), const)],
        out_specs=pl.BlockSpec((rc, ch), idx),
        out_shape=jax.ShapeDtypeStruct((rows, ch), BF16),
        scratch_shapes=[pltpu.VMEM((rc, ns2), F32), pltpu.VMEM((rc, ns2), BF16),
                        pltpu.VMEM((SUBLANES, S5_STATES), F32), pltpu.VMEM((SUBLANES, S5_STATES), F32)],
        compiler_params=_params(1),
        name="s5_scan_rev" if reverse else "s5_scan_fwd",
    )(u_tb, w_drive.astype(BF16), w_read.astype(BF16),
      jnp.broadcast_to(a_re[None, :], (SUBLANES, S5_STATES)),
      jnp.broadcast_to(a_im[None, :], (SUBLANES, S5_STATES)),
      d_skip.reshape(1, ch))


FFT_N2 = 128


def _fft_tables(L):
    N = 2 * L
    N1 = N // FFT_N2
    k1 = np.arange(N1)[:, None]
    n1 = np.arange(N1 // 2)[None, :]
    n2 = np.arange(FFT_N2)[:, None, None]
    ang = -2.0 * np.pi * (k1[None] * (n2 + FFT_N2 * n1[None])) / N
    mr, mi = np.cos(ang), np.sin(ang)
    ma = np.concatenate([np.concatenate([mr, -mi], axis=2), np.concatenate([mi, mr], axis=2)], axis=1)
    gr, gi = np.transpose(mr, (0, 2, 1)), -np.transpose(mi, (0, 2, 1))
    mainv = np.concatenate([np.concatenate([gr, -gi], axis=2), np.concatenate([gi, gr], axis=2)], axis=1)
    kk = np.arange(FFT_N2)
    a2 = -2.0 * np.pi * np.outer(kk, kk) / FFT_N2
    fr, fi = np.cos(a2), np.sin(a2)
    f_fwd = np.block([[fr, -fi], [fi, fr]])
    f_inv = np.block([[fr, fi], [-fi, fr]])
    return (jnp.asarray(ma, BF16), jnp.asarray(mainv, BF16), jnp.asarray(f_fwd, BF16), jnp.asarray(f_inv, BF16))


def _hyena_stage_a_kernel(z_ref, ma_ref, o_ref):
    def body(n2, c):
        off = pl.multiple_of(n2 * LANES, LANES)
        x = jnp.concatenate([z_ref[0, 0, :, pl.ds(off, LANES)], z_ref[0, 1, :, pl.ds(off, LANES)]], axis=0)
        r = jnp.dot(ma_ref[n2], x.astype(BF16), preferred_element_type=F32)
        o_ref[0, 0, :, pl.ds(off, LANES)] = r.astype(o_ref.dtype)
        return c

    lax.fori_loop(0, FFT_N2, body, 0)


def _hyena_stage_c_kernel(a_ref, h_ref, ff_ref, fi_ref, o_ref):
    n1_count = a_ref.shape[2] // 2

    def body(k1, c):
        x = jnp.concatenate([a_ref[0, 0, k1], a_ref[0, 0, n1_count + k1]], axis=0).astype(BF16)
        y = jnp.dot(ff_ref[...], x, preferred_element_type=F32)
        yr, yi = y[:FFT_N2], y[FFT_N2:]
        hr = h_ref[0, 0, k1].astype(F32)
        hi = h_ref[0, 1, k1].astype(F32)
        x2 = jnp.concatenate([yr * hr - yi * hi, yr * hi + yi * hr], axis=0).astype(BF16)
        b = jnp.dot(fi_ref[...], x2, preferred_element_type=F32)
        o_ref[0, 0, k1] = b[:FFT_N2].astype(o_ref.dtype)
        o_ref[0, 0, n1_count + k1] = b[FFT_N2:].astype(o_ref.dtype)
        return c

    lax.fori_loop(0, n1_count, body, 0)


def _hyena_stage_a_inv_kernel(b_ref, mainv_ref, o_ref):
    half = o_ref.shape[2]

    def body(n2, c):
        off = pl.multiple_of(n2 * LANES, LANES)
        r = jnp.dot(mainv_ref[n2], b_ref[0, 0, :, pl.ds(off, LANES)].astype(BF16), preferred_element_type=F32)
        o_ref[0, 0, :, pl.ds(off, LANES)] = r[:half].astype(o_ref.dtype)
        o_ref[0, 1, :, pl.ds(off, LANES)] = r[half:].astype(o_ref.dtype)
        return c

    lax.fori_loop(0, FFT_N2, body, 0)


def pl_hyena_conv(zc, filt, L):
    n_ct, B, Lt, cw = zc.shape
    assert cw == LANES
    N = 2 * L
    N1 = N // FFT_N2
    half = N1 // 2
    flat = FFT_N2 * cw
    hf = jnp.fft.fft(filt, axis=0) / N
    h2 = hf.reshape(FFT_N2, N1, n_ct, cw).transpose(2, 1, 0, 3)
    h = jnp.stack([h2.real, h2.imag], axis=1).astype(BF16)
    ma, mainv, f_fwd, f_inv = _fft_tables(L)
    n_pairs = B // 2
    grid = (n_ct, n_pairs)
    full = lambda arr: pl.BlockSpec(arr.shape, lambda ct, bp: (0,) * arr.ndim)
    pair_rows = pl.BlockSpec((1, 2, half, flat), lambda ct, bp: (ct, bp, 0, 0))
    spec_flat = pl.BlockSpec((1, 1, 2 * N1, flat), lambda ct, bp: (ct, bp, 0, 0))
    spec_slab = pl.BlockSpec((1, 1, 2 * N1, FFT_N2, cw), lambda ct, bp: (ct, bp, 0, 0, 0))
    a = pl.pallas_call(
        _hyena_stage_a_kernel, grid=grid,
        in_specs=[pair_rows, full(ma)], out_specs=spec_flat,
        out_shape=jax.ShapeDtypeStruct((n_ct, n_pairs, 2 * N1, flat), BF16),
        compiler_params=_params(2), name="hyena_dft_slow",
    )(zc.reshape(n_ct, B, Lt // FFT_N2, flat), ma)
    b = pl.pallas_call(
        _hyena_stage_c_kernel, grid=grid,
        in_specs=[spec_slab, pl.BlockSpec((1, 2, N1, FFT_N2, cw), lambda ct, bp: (ct, 0, 0, 0, 0)),
                  full(f_fwd), full(f_inv)],
        out_specs=spec_slab,
        out_shape=jax.ShapeDtypeStruct((n_ct, n_pairs, 2 * N1, FFT_N2, cw), BF16),
        compiler_params=_params(2), name="hyena_dft_fast_filter",
    )(a.reshape(n_ct, n_pairs, 2 * N1, FFT_N2, cw), h, f_fwd, f_inv)
    y = pl.pallas_call(
        _hyena_stage_a_inv_kernel, grid=grid,
        in_specs=[spec_flat, full(mainv)], out_specs=pair_rows,
        out_shape=jax.ShapeDtypeStruct((n_ct, B, half, flat), BF16),
        compiler_params=_params(2), name="hyena_idft_slow",
    )(b.reshape(n_ct, n_pairs, 2 * N1, flat), mainv)
    return y.reshape(n_ct, B, L, cw)


def hyena_filter(L, w1, b1, w2, b2, w3, freq):
    t = jnp.arange(L, dtype=F32) / L
    ang = (2.0 * math.pi) * t[:, None] * jnp.arange(1, HY_BANDS + 1, dtype=F32)
    feat = jnp.concatenate([t[:, None], jnp.cos(ang), jnp.sin(ang)], axis=-1)
    hp = lax.Precision.HIGHEST
    h = jnp.sin(freq * (jnp.dot(feat, w1, precision=hp) + b1))
    h = jnp.sin(freq * (jnp.dot(h, w2, precision=hp) + b2))
    h = jnp.dot(h, w3, precision=hp).reshape(L, 2, HY_CH)
    window = jnp.exp(-t[:, None] * jnp.linspace(HY_DECAY_MIN, HY_DECAY_MAX, HY_CH, dtype=F32))
    h = h * window[:, None, :]
    filt = jnp.concatenate([h[:, 0], jnp.zeros((1, HY_CH), F32), h[:0:-1, 1]], axis=0)
    return filt / (jnp.sum(jnp.abs(filt), axis=0, keepdims=True) + EPS)


def hyena_conv(zc, filt_params, L):
    n_ct, B, Lt, cw = zc.shape
    Lc = Lt - L
    y_lat = pl_hyena_conv(zc, hyena_filter(L, *filt_params), L)
    z_ctx = zc[:, :, L:].astype(F32).transpose(1, 2, 0, 3).reshape(B, Lc, n_ct * cw)
    zf = jnp.fft.rfft(z_ctx, n=2 * Lc, axis=1)
    ff = jnp.fft.rfft(hyena_filter(Lc, *filt_params), n=2 * Lc, axis=0)
    y_ctx = jnp.fft.irfft(zf * ff[None], n=2 * Lc, axis=1)[:, :Lc]
    return y_lat, y_ctx


def _block_diag(blocks):
    G, r, c = blocks.shape
    eye = jnp.eye(G, dtype=blocks.dtype)
    return (eye[:, None, :, None] * blocks[:, :, None, :]).reshape(G * r, G * c)


def s5_scan(u_tb, a_re, a_im, log_dt, b_re, b_im, c_re, c_im, d_skip, n_lat_steps):
    outs = []
    for direction in range(2):
        A = lax.complex(a_re[direction], a_im[direction])
        dtA = jnp.exp(log_dt[direction])[:, None] * A
        a_bar = jnp.exp(dtA)
        b_bar = ((a_bar - 1.0) / A)[:, :, None] * lax.complex(b_re[direction], b_im[direction])
        bt_re = jnp.transpose(b_bar.real, (0, 2, 1))
        bt_im = jnp.transpose(b_bar.imag, (0, 2, 1))
        w_drive = jnp.concatenate([_block_diag(bt_re), _block_diag(bt_im)], axis=1)
        ct_re = jnp.transpose(c_re[direction], (0, 2, 1))
        ct_im = jnp.transpose(c_im[direction], (0, 2, 1))
        w_read = jnp.concatenate([_block_diag(ct_re), -_block_diag(ct_im)], axis=0)
        outs.append(pl_s5_scan(u_tb, w_drive, w_read, a_bar.real.reshape(-1), a_bar.imag.reshape(-1), d_skip,
                               n_lat_steps=n_lat_steps, reverse=direction == 1))
    return outs


def rope_tables(L, Lc):
    rows = L // GRID_W
    row = jnp.repeat(jnp.arange(rows, dtype=F32), GRID_W)
    col = jnp.tile(jnp.arange(GRID_W, dtype=F32), rows)
    inv = ROPE_BASE ** (-jnp.arange(ROPE_PAIRS_AXIS, dtype=F32) / ROPE_PAIRS_AXIS)
    ang = jnp.concatenate([row[:, None] * inv, col[:, None] * inv], axis=-1)
    cos, sin = jnp.cos(ang), jnp.sin(ang)
    n_maps = ATT_W // ATT_HEAD_DIM
    cosf = jnp.tile(jnp.concatenate([cos, cos], axis=-1), (1, n_maps))
    sinf = jnp.tile(jnp.concatenate([-sin, sin], axis=-1), (1, n_maps))
    return (jnp.concatenate([cosf, jnp.ones((Lc, ATT_W), F32)], axis=0),
            jnp.concatenate([sinf, jnp.zeros((Lc, ATT_W), F32)], axis=0))


def moe_dispatch(logits, b_g, b_e):
    T = logits.shape[0]
    g_logits = logits[:, :MOE_GROUPS] + b_g
    g_idx = jnp.argmax(g_logits, axis=-1)
    p_group = jnp.take_along_axis(jax.nn.softmax(g_logits, axis=-1), g_idx[:, None], axis=1)
    e_logits = (logits[:, MOE_GROUPS:MOE_GROUPS + N_EXPERTS] + b_e).reshape(T, MOE_GROUPS, MOE_EPG)
    e_logits = jnp.take_along_axis(e_logits, g_idx[:, None, None], axis=1)[:, 0]
    top_p, top_i = lax.top_k(jax.nn.softmax(e_logits, axis=-1), MOE_TOP_K)
    gate = p_group * top_p / jnp.sum(top_p, axis=-1, keepdims=True)
    expert = (g_idx[:, None] * MOE_EPG + top_i).reshape(-1).astype(jnp.int32)
    tok = jnp.repeat(jnp.arange(T, dtype=jnp.int32), MOE_TOP_K)
    n_assign = T * MOE_TOP_K
    n_blocks = -(-n_assign // MOE_BLOCK) + N_EXPERTS
    n_pad = n_blocks * MOE_BLOCK
    order = jnp.argsort(expert).astype(jnp.int32)
    sorted_rank = jnp.argsort(order).astype(jnp.int32)
    e_ids = jnp.arange(N_EXPERTS + 1, dtype=jnp.int32)
    bounds = jnp.sum((expert[:, None] < e_ids[None, :]).astype(jnp.int32), axis=0)
    start = bounds[:-1]
    counts = bounds[1:] - bounds[:-1]
    padded = (counts + MOE_BLOCK - 1) // MOE_BLOCK * MOE_BLOCK
    pad_end = jnp.cumsum(padded)
    pad_start = pad_end - padded
    slot_of_assign = (pad_start[expert] + sorted_rank - start[expert]).astype(jnp.int32)
    block_first = jnp.arange(n_blocks, dtype=jnp.int32) * MOE_BLOCK
    block_e = jnp.minimum(jnp.sum((pad_end[None, :] <= block_first[:, None]).astype(jnp.int32), axis=1),
                          N_EXPERTS - 1).astype(jnp.int32)
    slot_e = jnp.repeat(block_e, MOE_BLOCK)
    slot_r = jnp.arange(n_pad, dtype=jnp.int32) - pad_start[slot_e]
    slot_valid = (slot_r < counts[slot_e]) & (jnp.arange(n_pad) < pad_end[-1])
    slot_assign = order[jnp.clip(start[slot_e] + slot_r, 0, n_assign - 1)]
    slot_tok = tok[slot_assign]
    slot_gate = jnp.where(slot_valid, gate.reshape(-1)[slot_assign], 0.0)
    n_used = (pad_end[-1:] // MOE_BLOCK).astype(jnp.int32)
    return slot_tok, slot_gate, slot_of_assign, block_e, n_used


def hier_moe(h2, logits, b_g, b_e, w1, w3, w2, layer):
    T, D = h2.shape
    slot_tok, slot_gate, slot_of_assign, block_e, n_used = moe_dispatch(logits, b_g, b_e)
    yb = pl_moe_ffn(h2[slot_tok], slot_gate, block_e, n_used, w1, w3, w2, layer)
    slots = slot_of_assign.reshape(T, MOE_TOP_K)
    return yb[slots[:, 0]], yb[slots[:, 1]]


def kernel(x, c, ctx, c_ctx, w_mod, b_mod, norm1_g, norm2_g, final_g, w_in, w_out, hy_conv_w, hy_conv_b, hy_ffn_w1, hy_ffn_b1, hy_ffn_w2, hy_ffn_b2, hy_ffn_w3, hy_freq, hy_skip, hy_norm_g, s5_a_re, s5_a_im, s5_log_dt, s5_b_re, s5_b_im, s5_c_re, s5_c_im, s5_d, s5_glu_w, s5_norm_g, att_lq1, att_lk1, att_lq2, att_lk2, att_subln_g, moe_wg, moe_bg, moe_we, moe_be, moe_w1, moe_w3, moe_w2):
    B, L, D = x.shape
    Lc = ctx.shape[1]
    Lt = L + Lc
    assert B == SUBLANES and Lc == TOKEN_TILE and L % ATT_TQ == 0
    n_lat_tiles = L // TOKEN_TILE
    cosf, sinf = rope_tables(L, Lc)
    silu_c = jax.nn.silu(c)
    silu_cc = jax.nn.silu(c_ctx)
    hp = lax.Precision.HIGHEST
    q_scale = ATT_HEAD_DIM ** -0.5 * math.log2(math.e)
    xs = jnp.concatenate([x, ctx], axis=1)
    for l in range(DEPTH):
        lam_init = 0.8 - 0.6 * math.exp(-0.3 * l)
        mod = jnp.dot(silu_c, w_mod[l], precision=hp) + b_mod[l]
        cmod = jnp.dot(silu_cc, w_mod[l], precision=hp) + b_mod[l]
        modv = jnp.stack([mod, jnp.broadcast_to(cmod, mod.shape)], axis=1).reshape(2 * B, 1, N_MOD * D)

        u0, z, s5_u, q_p, q_r, k_r, v = pl_in_proj(xs, modv, norm1_g[l], w_in[l], hy_conv_w[l], hy_conv_b[l],
                                                   cosf, sinf, n_lat_tiles, q_scale)

        filt_params = (hy_ffn_w1[l], hy_ffn_b1[l], hy_ffn_w2[l], hy_ffn_b2[l], hy_ffn_w3[l], hy_freq[l])
        y_lat, y_ctx = hyena_conv(z, filt_params, L)

        s5_f, s5_b = s5_scan(s5_u.reshape(Lt * B, S5_CH), s5_a_re[l], s5_a_im[l], s5_log_dt[l], s5_b_re[l],
                             s5_b_im[l], s5_c_re[l], s5_c_im[l], s5_d[l], L)

        lam = (jnp.exp(jnp.sum(att_lq1[l] * att_lk1[l])) - jnp.exp(jnp.sum(att_lq2[l] * att_lk2[l])) + lam_init)
        g_scaled = (att_subln_g[l] * (1.0 - lam_init)).reshape(1, ATT_V_DIM)
        att_lat, att_ctx = pl_diff_attention(q_p, q_r, k_r, v, g_scaled, lam, L)

        w_router = jnp.zeros((D, LANES), F32).at[:, :MOE_GROUPS].set(moe_wg[l])
        w_router = w_router.at[:, MOE_GROUPS:MOE_GROUPS + N_EXPERTS].set(moe_we[l])
        xs, h2, logits = pl_out_proj(xs, u0, z, y_lat, y_ctx, s5_f.reshape(Lt, B * S5_CH),
                                     s5_b.reshape(Lt, B * S5_CH), att_lat, att_ctx, modv, hy_skip[l], hy_norm_g[l],
                                     s5_glu_w[l], s5_norm_g[l], w_out[l], norm2_g[l], w_router, n_lat_tiles)

        y0, y1 = hier_moe(h2.reshape(B * Lt, D), logits.reshape(B * Lt, LANES), moe_bg[l], moe_be[l],
                          moe_w1, moe_w3, moe_w2, l)
        xs = pl_moe_combine(xs, y0, y1, modv, final_g, n_lat_tiles, final=l == DEPTH - 1)
    return xs
```

```python
import functools
import math

import jax
import jax.numpy as jnp
import numpy as np
from jax import lax
from jax.experimental import pallas as pl
from jax.experimental.pallas import tpu as pltpu

D_MODEL = 1024
DEPTH = 4
GRID_W = 64
N_MOD = 6
EPS = 1e-6
HY_CH = D_MODEL // 4
S5_CH = D_MODEL // 4
ATT_W = D_MODEL // 2
HY_BANDS = 16
HY_DECAY_MIN = -math.log(1e-2) / 1.5
HY_DECAY_MAX = -math.log(1e-2) / 0.3
S5_GROUP = 16
S5_GROUPS = S5_CH // S5_GROUP
S5_STATE = 64
ATT_HEAD_DIM = 64
ATT_HEADS = ATT_W // (2 * ATT_HEAD_DIM)
ATT_V_DIM = 2 * ATT_HEAD_DIM
ROPE_HALF = ATT_HEAD_DIM // 2
ROPE_PAIRS_AXIS = ROPE_HALF // 2
ROPE_BASE = 10000.0
MOE_GROUPS = 4
MOE_EPG = 8
N_EXPERTS = MOE_GROUPS * MOE_EPG
MOE_TOP_K = 2
MOE_BLOCK = 512
IN_COLS = 3 * HY_CH + S5_CH + 3 * ATT_W
COL_S5 = 3 * HY_CH
COL_Q = COL_S5 + S5_CH
COL_K = COL_Q + ATT_W
COL_V = COL_K + ATT_W

LANES = 128
SUBLANES = 8
VMEM_LIMIT = 48 * 1024 * 1024
TOKEN_TILE = 256

F32 = jnp.float32
BF16 = jnp.bfloat16


def _params(n_axes, vmem=VMEM_LIMIT):
    return pltpu.CompilerParams(dimension_semantics=("arbitrary",) * n_axes, vmem_limit_bytes=vmem)


def _rms(x):
    return x * lax.rsqrt(jnp.mean(x * x, axis=-1, keepdims=True) + EPS)


def _in_kernel(x_ref, xp_ref, xn_ref, mod_ref, g_ref, w_ref, cw_ref, cb_ref, cos_ref, sin_ref,
               u0_ref, z_ref, s5_ref, qp_ref, qr_ref, k_ref, v_ref, *, n_lat_tiles, q_scale):
    i = pl.program_id(1)
    tm = x_ref.shape[1]
    d = x_ref.shape[2]
    g = g_ref[...]
    shift = mod_ref[0, :, 0:d]
    scale = mod_ref[0, :, d:2 * d]

    def norm_mod(xt):
        return (_rms(xt) * g) * (1.0 + scale) + shift

    h = jnp.concatenate([norm_mod(x_ref[0]), norm_mod(xp_ref[0]), norm_mod(xn_ref[0])], axis=0).astype(BF16)
    p = jnp.dot(h, w_ref[...], preferred_element_type=F32)

    hy = p[:tm, :COL_S5]
    is_ctx = i == n_lat_tiles
    has_prev = jnp.logical_and(i != 0, jnp.logical_not(is_ctx))
    has_next = jnp.logical_and(i != n_lat_tiles - 1, jnp.logical_not(is_ctx))
    prev_row = jnp.where(has_prev, p[tm + SUBLANES - 1:tm + SUBLANES, :COL_S5], 0.0)
    next_row = jnp.where(has_next, p[tm + SUBLANES:tm + SUBLANES + 1, :COL_S5], 0.0)
    rows = lax.broadcasted_iota(jnp.int32, (tm, 1), 0)
    up = jnp.where(rows == 0, prev_row, pltpu.roll(hy, 1, axis=0))
    dn = jnp.where(rows == tm - 1, next_row, pltpu.roll(hy, tm - 1, axis=0))
    u = up * cw_ref[0:1, :] + hy * cw_ref[1:2, :] + dn * cw_ref[2:3, :] + cb_ref[...]
    u0_ref[0] = u[:, :HY_CH].astype(u0_ref.dtype)
    z = (u[:, HY_CH:2 * HY_CH] * u[:, 2 * HY_CH:]).astype(z_ref.dtype)
    for ct in range(HY_CH // LANES):
        z_ref[ct, 0] = z[:, ct * LANES:(ct + 1) * LANES]

    s5_ref[...] = p[:tm, COL_S5:COL_Q]

    lane = lax.broadcasted_iota(jnp.int32, (1, ATT_W), 1)
    first_half = jnp.bitwise_and(lane, ATT_HEAD_DIM - 1) < ROPE_HALF
    cos = cos_ref[...]
    sin = sin_ref[...]

    def rope(t):
        partner = jnp.where(first_half, pltpu.roll(t, ATT_W - ROPE_HALF, axis=1), pltpu.roll(t, ROPE_HALF, axis=1))
        return t * cos + partner * sin

    q = p[:tm, COL_Q:COL_K] * q_scale
    qp_ref[0] = q.astype(BF16)
    qr_ref[0] = rope(q).astype(BF16)
    k_ref[0] = rope(p[:tm, COL_K:COL_V]).astype(BF16)
    v_ref[0] = p[:tm, COL_V:].astype(BF16)


def pl_in_proj(xs, modv, norm_g, w_in, conv_w, conv_b, cosf, sinf, n_lat_tiles, q_scale):
    B, Lt, D = xs.shape
    tm = TOKEN_TILE
    n_tiles = Lt // tm
    halo_per_tile = tm // SUBLANES
    n_halo_blocks = Lt // SUBLANES
    tok = lambda w: pl.BlockSpec((1, tm, w), lambda b, i: (b, i, 0))
    const = lambda shape: pl.BlockSpec(shape, lambda b, i: (0,) * len(shape))
    n_ct = HY_CH // LANES
    out_shapes = (jax.ShapeDtypeStruct((B, Lt, HY_CH), BF16), jax.ShapeDtypeStruct((n_ct, B, Lt, LANES), BF16),
                  jax.ShapeDtypeStruct((Lt, B * S5_CH), F32),
                  jax.ShapeDtypeStruct((B, Lt, ATT_W), BF16), jax.ShapeDtypeStruct((B, Lt, ATT_W), BF16),
                  jax.ShapeDtypeStruct((B, Lt, ATT_W), BF16), jax.ShapeDtypeStruct((B, Lt, ATT_W), BF16))
    return pl.pallas_call(
        functools.partial(_in_kernel, n_lat_tiles=n_lat_tiles, q_scale=q_scale),
        grid=(B, n_tiles),
        in_specs=[tok(D),
                  pl.BlockSpec((1, SUBLANES, D), lambda b, i: (b, jnp.maximum(i * halo_per_tile - 1, 0), 0)),
                  pl.BlockSpec((1, SUBLANES, D),
                               lambda b, i: (b, jnp.minimum((i + 1) * halo_per_tile, n_halo_blocks - 1), 0)),
                  pl.BlockSpec((1, 1, N_MOD * D), lambda b, i: (2 * b + jnp.where(i >= n_lat_tiles, 1, 0), 0, 0)),
                  const((1, D)), const((D, IN_COLS)), const((3, COL_S5)), const((1, COL_S5)),
                  pl.BlockSpec((tm, ATT_W), lambda b, i: (i, 0)), pl.BlockSpec((tm, ATT_W), lambda b, i: (i, 0))],
        out_specs=(tok(HY_CH), pl.BlockSpec((n_ct, 1, tm, LANES), lambda b, i: (0, b, i, 0)),
                   pl.BlockSpec((tm, S5_CH), lambda b, i: (i, b)),
                   tok(ATT_W), tok(ATT_W), tok(ATT_W), tok(ATT_W)),
        out_shape=out_shapes,
        compiler_params=_params(2),
        name="in_proj",
    )(xs, xs, xs, modv, norm_g.reshape(1, D), w_in.astype(BF16), conv_w, conv_b.reshape(1, COL_S5), cosf, sinf)


def _out_kernel(x_ref, u0_ref, z_ref, yl_ref, yc_ref, sf_ref, sb_ref, al_ref, ac_ref, mod_ref, skip_ref, hg_ref,
                gw_ref, sg_ref, wo_ref, n2g_ref, rwh_ref, rwl_ref, xo_ref, h2_ref, lg_ref, *, n_lat_tiles):
    d = x_ref.shape[2]
    is_ctx = pl.program_id(1) == n_lat_tiles
    n_ct = z_ref.shape[0]
    z = jnp.concatenate([z_ref[ct, 0] for ct in range(n_ct)], axis=1).astype(F32)
    y_lat = jnp.concatenate([yl_ref[ct, 0] for ct in range(n_ct)], axis=1).astype(F32)
    y = jnp.where(is_ctx, yc_ref[0], y_lat)
    att = jnp.where(is_ctx, ac_ref[0], al_ref[0])
    hy = _rms(u0_ref[0].astype(F32) * (y + skip_ref[...] * z)) * hg_ref[...]
    gl = jax.nn.gelu(sf_ref[...].astype(F32) + sb_ref[...].astype(F32))
    gate = jax.nn.sigmoid(jnp.dot(gl.astype(BF16), gw_ref[...], preferred_element_type=F32))
    s5 = _rms(gl * gate) * sg_ref[...]
    mix = jnp.concatenate([hy.astype(BF16), s5.astype(BF16), att], axis=1)
    proj = jnp.dot(mix, wo_ref[...], preferred_element_type=F32)
    xn = x_ref[0] + mod_ref[0, :, 2 * d:3 * d] * proj
    xo_ref[0] = xn
    h2 = (_rms(xn) * n2g_ref[...]) * (1.0 + mod_ref[0, :, 4 * d:5 * d]) + mod_ref[0, :, 3 * d:4 * d]
    hh = h2.astype(BF16)
    hl = (h2 - hh.astype(F32)).astype(BF16)
    h2_ref[0] = hh
    lg = jnp.dot(hh, rwh_ref[...], preferred_element_type=F32)
    lg += jnp.dot(hl, rwh_ref[...], preferred_element_type=F32)
    lg += jnp.dot(hh, rwl_ref[...], preferred_element_type=F32)
    lg_ref[0] = lg


def pl_out_proj(xs, u0, zc, y_lat, y_ctx, s5_f, s5_b, att_lat, att_ctx, modv, hy_skip, hy_norm_g, glu_w, s5_norm_g,
                w_out, norm2_g, w_router, n_lat_tiles):
    B, Lt, D = xs.shape
    tm = TOKEN_TILE
    n_ct = zc.shape[0]
    tok = lambda w: pl.BlockSpec((1, tm, w), lambda b, i: (b, i, 0))
    lat_tok = lambda w: pl.BlockSpec((1, tm, w), lambda b, i: (b, jnp.minimum(i, n_lat_tiles - 1), 0))
    ctx_tok = lambda w: pl.BlockSpec((1, tm, w), lambda b, i: (b, 0, 0))
    tb = pl.BlockSpec((tm, S5_CH), lambda b, i: (i, b))
    const = lambda shape: pl.BlockSpec(shape, lambda b, i: (0,) * len(shape))
    rwh = w_router.astype(BF16)
    rwl = (w_router - rwh.astype(F32)).astype(BF16)
    return pl.pallas_call(
        functools.partial(_out_kernel, n_lat_tiles=n_lat_tiles),
        grid=(B, Lt // tm),
        in_specs=[tok(D), tok(HY_CH), pl.BlockSpec((n_ct, 1, tm, LANES), lambda b, i: (0, b, i, 0)),
                  pl.BlockSpec((n_ct, 1, tm, LANES), lambda b, i: (0, b, jnp.minimum(i, n_lat_tiles - 1), 0)),
                  ctx_tok(HY_CH), tb, tb, lat_tok(ATT_W), ctx_tok(ATT_W),
                  pl.BlockSpec((1, 1, N_MOD * D), lambda b, i: (2 * b + jnp.where(i >= n_lat_tiles, 1, 0), 0, 0)),
                  const((1, HY_CH)), const((1, HY_CH)), const((S5_CH, S5_CH)), const((1, S5_CH)),
                  const((D, D)), const((1, D)), const((D, LANES)), const((D, LANES))],
        out_specs=(tok(D), tok(D), tok(LANES)),
        out_shape=(jax.ShapeDtypeStruct((B, Lt, D), F32), jax.ShapeDtypeStruct((B, Lt, D), BF16),
                   jax.ShapeDtypeStruct((B, Lt, LANES), F32)),
        compiler_params=_params(2),
        name="out_proj",
    )(xs, u0, zc, y_lat, y_ctx, s5_f, s5_b, att_lat, att_ctx, modv, hy_skip.reshape(1, HY_CH),
      hy_norm_g.reshape(1, HY_CH), glu_w.astype(BF16), s5_norm_g.reshape(1, S5_CH), w_out.astype(BF16),
      norm2_g.reshape(1, D), rwh, rwl)


def _dot_nt(a, b):
    return lax.dot_general(a, b, (((1,), (1,)), ((), ())), preferred_element_type=F32)


ATT_TQ = 512
ATT_SUB = 256


def _attn_kernel(qp_ref, qr_ref, k_ref, v_ref, g_ref, lam_ref, o_ref, *, n_lat):
    tq = qp_ref.shape[1]
    first_map = lax.broadcasted_iota(jnp.int32, (1, LANES), 1) < ATT_HEAD_DIM
    zero = jnp.zeros((), BF16)
    sub = min(ATT_SUB, tq)
    for r0 in range(0, tq, sub):
        qp = qp_ref[0, r0:r0 + sub, :]
        qr = qr_ref[0, r0:r0 + sub, :]
        probs = []
        for m in range(2):
            in_map = first_map if m == 0 else jnp.logical_not(first_map)
            s_c = _dot_nt(jnp.where(in_map, qp, zero), k_ref[0, n_lat:, :])
            mx = jnp.max(s_c, axis=-1, keepdims=True)
            if n_lat:
                s_l = _dot_nt(jnp.where(in_map, qr, zero), k_ref[0, :n_lat, :])
                mx = jnp.maximum(mx, jnp.max(s_l, axis=-1, keepdims=True))
                p_l = jnp.exp2(s_l - mx)
            p_c = jnp.exp2(s_c - mx)
            den = jnp.sum(p_c, axis=-1, keepdims=True)
            if n_lat:
                den = den + jnp.sum(p_l, axis=-1, keepdims=True)
            probs.append((p_c, p_l if n_lat else None, 1.0 / den))
        w0 = probs[0][2]
        w1 = lam_ref[0:1, 0:1] * probs[1][2]
        a_c = (probs[0][0] * w0 - probs[1][0] * w1).astype(BF16)
        o = jnp.dot(a_c, v_ref[0, n_lat:, :], preferred_element_type=F32)
        if n_lat:
            a_l = (probs[0][1] * w0 - probs[1][1] * w1).astype(BF16)
            o = o + jnp.dot(a_l, v_ref[0, :n_lat, :], preferred_element_type=F32)
        o_ref[0, r0:r0 + sub, :] = (_rms(o) * g_ref[...]).astype(o_ref.dtype)


def pl_diff_attention(qp, qr, k, v, g_scaled, lam, n_lat):
    B, Lt, _ = qp.shape
    n_ctx = Lt - n_lat
    lam_arr = jnp.full((SUBLANES, LANES), lam, F32)
    small = [pl.BlockSpec((1, LANES), lambda b, h, i: (0, 0)), pl.BlockSpec((SUBLANES, LANES), lambda b, h, i: (0, 0))]
    tq = ATT_TQ
    qspec = pl.BlockSpec((1, tq, LANES), lambda b, h, i: (b, i, h))
    kspec = pl.BlockSpec((1, Lt, LANES), lambda b, h, i: (b, 0, h))
    out_lat = pl.pallas_call(
        functools.partial(_attn_kernel, n_lat=n_lat),
        grid=(B, ATT_HEADS, n_lat // tq),
        in_specs=[qspec, qspec, kspec, kspec] + small,
        out_specs=qspec,
        out_shape=jax.ShapeDtypeStruct((B, n_lat, ATT_W), BF16),
        compiler_params=_params(3),
        name="diff_attention",
    )(qp, qr, k, v, g_scaled, lam_arr)
    ctx_blk = n_lat // n_ctx
    cspec = pl.BlockSpec((1, n_ctx, LANES), lambda b, h, i: (b, ctx_blk, h))
    out_ctx = pl.pallas_call(
        functools.partial(_attn_kernel, n_lat=0),
        grid=(B, ATT_HEADS, 1),
        in_specs=[cspec, cspec, cspec, cspec] + small,
        out_specs=pl.BlockSpec((1, n_ctx, LANES), lambda b, h, i: (b, 0, h)),
        out_shape=jax.ShapeDtypeStruct((B, n_ctx, ATT_W), BF16),
        compiler_params=_params(3),
        name="diff_attention_ctx",
    )(qp, qr, k, v, g_scaled, lam_arr)
    return out_lat, out_ctx


def _moe_kernel(be_ref, nb_ref, x_ref, gate_ref, w1_ref, w3_ref, w2_ref, o_ref, w1_scr, w3_scr, w2_scr):
    i = pl.program_id(0)
    new_expert = jnp.logical_or(i == 0, be_ref[i] != be_ref[jnp.maximum(i - 1, 0)])

    @pl.when(jnp.logical_and(i < nb_ref[0], new_expert))
    def _():
        w1_scr[...] = w1_ref[0, 0].astype(BF16)
        w3_scr[...] = w3_ref[0, 0].astype(BF16)
        w2_scr[...] = w2_ref[0, 0].astype(BF16)

    @pl.when(i < nb_ref[0])
    def _():
        x = x_ref[...]
        a = jnp.dot(x, w1_scr[...], preferred_element_type=F32)
        b = jnp.dot(x, w3_scr[...], preferred_element_type=F32)
        h = (a * jax.nn.sigmoid(a)) * b
        y = jnp.dot(h.astype(BF16), w2_scr[...], preferred_element_type=F32)
        rows = lax.broadcasted_iota(jnp.int32, (MOE_BLOCK, 1), 0)
        lane = lax.broadcasted_iota(jnp.int32, (1, LANES), 1)
        g_rows = gate_ref[0, 0:1, :]
        for r in range(1, MOE_BLOCK // LANES):
            g_rows = jnp.where(rows >= r * LANES, gate_ref[0, r:r + 1, :], g_rows)
        g_col = jnp.sum(jnp.where(lane == jnp.bitwise_and(rows, LANES - 1), g_rows, 0.0), axis=1, keepdims=True)
        o_ref[...] = (y * g_col).astype(o_ref.dtype)

    @pl.when(i >= nb_ref[0])
    def _():
        o_ref[...] = jnp.zeros_like(o_ref)


def pl_moe_ffn(xb, slot_gate, block_e, n_used, w1, w3, w2, layer):
    n_pad, D = xb.shape
    n_blocks = n_pad // MOE_BLOCK
    F = w1.shape[-1]
    grid_spec = pltpu.PrefetchScalarGridSpec(
        num_scalar_prefetch=2,
        grid=(n_blocks,),
        in_specs=[pl.BlockSpec((MOE_BLOCK, D), lambda i, be, nb: (i, 0)),
                  pl.BlockSpec((1, MOE_BLOCK // LANES, LANES), lambda i, be, nb: (i, 0, 0)),
                  pl.BlockSpec((1, 1, D, F), lambda i, be, nb: (layer, be[i], 0, 0)),
                  pl.BlockSpec((1, 1, D, F), lambda i, be, nb: (layer, be[i], 0, 0)),
                  pl.BlockSpec((1, 1, F, D), lambda i, be, nb: (layer, be[i], 0, 0))],
        out_specs=pl.BlockSpec((MOE_BLOCK, D), lambda i, be, nb: (i, 0)),
        scratch_shapes=[pltpu.VMEM((D, F), BF16), pltpu.VMEM((D, F), BF16), pltpu.VMEM((F, D), BF16)],
    )
    return pl.pallas_call(
        _moe_kernel,
        grid_spec=grid_spec,
        out_shape=jax.ShapeDtypeStruct((n_pad, D), BF16),
        compiler_params=_params(1),
        name="moe_ffn",
    )(block_e, n_used, xb, slot_gate.reshape(n_blocks, MOE_BLOCK // LANES, LANES), w1, w3, w2)


def _combine_kernel(x_ref, y0_ref, y1_ref, mod_ref, g_ref, o_ref, *, final):
    d = x_ref.shape[2]
    xn = x_ref[0] + mod_ref[0, :, 5 * d:6 * d] * (y0_ref[0].astype(F32) + y1_ref[0].astype(F32))
    o_ref[0] = _rms(xn) * g_ref[...] if final else xn


def pl_moe_combine(xs, y0, y1, modv, final_g, n_lat_tiles, final):
    B, Lt, D = xs.shape
    tm = TOKEN_TILE
    n_tiles = n_lat_tiles if final else Lt // tm
    tok = pl.BlockSpec((1, tm, D), lambda b, i: (b, i, 0))
    return pl.pallas_call(
        functools.partial(_combine_kernel, final=final),
        grid=(B, n_tiles),
        in_specs=[tok, tok, tok,
                  pl.BlockSpec((1, 1, N_MOD * D), lambda b, i: (2 * b + jnp.where(i >= n_lat_tiles, 1, 0), 0, 0)),
                  pl.BlockSpec((1, D), lambda b, i: (0, 0))],
        out_specs=tok,
        out_shape=jax.ShapeDtypeStruct((B, n_tiles * tm, D), F32),
        compiler_params=_params(2),
        name="moe_combine",
    )(xs, y0.reshape(B, Lt, D), y1.reshape(B, Lt, D), modv, final_g.reshape(1, D))


S5_STATES = S5_GROUPS * S5_STATE
S5_CHUNK = 64


def _s5_kernel(u_ref, wd_ref, wr_ref, ar_ref, ai_ref, d_ref, y_ref, x_scr, h_scr, hr_scr, hi_scr, *, reverse):
    ns = S5_STATES

    @pl.when(pl.program_id(0) == 0)
    def _():
        hr_scr[...] = jnp.zeros_like(hr_scr)
        hi_scr[...] = jnp.zeros_like(hi_scr)

    u = u_ref[...]
    x_scr[...] = jnp.dot(u.astype(BF16), wd_ref[...], preferred_element_type=F32)
    ar = ar_ref[...]
    ai = ai_ref[...]

    def step(hr, hi, t):
        r = pl.multiple_of(t * SUBLANES, SUBLANES)
        xr = x_scr[pl.ds(r, SUBLANES), :ns]
        xi = x_scr[pl.ds(r, SUBLANES), ns:]
        return ar * hr - ai * hi + xr, ar * hi + ai * hr + xi

    def body(j, carry):
        hr, hi = carry
        t0 = (S5_CHUNK - 1 - 2 * j) if reverse else 2 * j
        t1 = t0 - 1 if reverse else t0 + 1
        hr0, hi0 = step(hr, hi, t0)
        hr1, hi1 = step(hr0, hi0, t1)
        lo = t1 if reverse else t0
        first_r, second_r = (hr1, hr0) if reverse else (hr0, hr1)
        first_i, second_i = (hi1, hi0) if reverse else (hi0, hi1)
        r = pl.multiple_of(lo * SUBLANES, 2 * SUBLANES)
        h_scr[pl.ds(r, 2 * SUBLANES), :ns] = jnp.concatenate([first_r, second_r], axis=0).astype(BF16)
        h_scr[pl.ds(r, 2 * SUBLANES), ns:] = jnp.concatenate([first_i, second_i], axis=0).astype(BF16)
        return hr1, hi1

    hr, hi = lax.fori_loop(0, S5_CHUNK // 2, body, (hr_scr[...], hi_scr[...]))
    hr_scr[...] = hr
    hi_scr[...] = hi
    y = jnp.dot(h_scr[...], wr_ref[...], preferred_element_type=F32)
    if not reverse:
        y = y + u * d_ref[...]
    y_ref[...] = y.astype(y_ref.dtype)


def pl_s5_scan(u_tb, w_drive, w_read, a_re, a_im, d_skip, *, n_lat_steps, reverse):
    rows, ch = u_tb.shape
    rc = S5_CHUNK * SUBLANES
    n_chunks = rows // rc
    n_lat = n_lat_steps // S5_CHUNK
    n_ctx = n_chunks - n_lat
    assert rows % rc == 0 and n_lat_steps % S5_CHUNK == 0
    if reverse:
        def idx(i):
            return (n_chunks - 1 - i, 0)
    else:
        def idx(i):
            return (jnp.where(i < n_ctx, n_lat + i, i - n_ctx), 0)
    const = lambda i: (0, 0)
    ns2 = 2 * S5_STATES
    return pl.pallas_call(
        functools.partial(_s5_kernel, reverse=reverse),
        grid=(n_chunks,),
        in_specs=[pl.BlockSpec((rc, ch), idx),
                  pl.BlockSpec((ch, ns2), const),
                  pl.BlockSpec((ns2, ch), const),
                  pl.BlockSpec((SUBLANES, S5_STATES), const),
                  pl.BlockSpec((SUBLANES, S5_STATES), const),
                  pl.BlockSpec((1, ch), const)],
        out_specs=pl.BlockSpec((rc, ch), idx),
        out_shape=jax.ShapeDtypeStruct((rows, ch), BF16),
        scratch_shapes=[pltpu.VMEM((rc, ns2), F32), pltpu.VMEM((rc, ns2), BF16),
                        pltpu.VMEM((SUBLANES, S5_STATES), F32), pltpu.VMEM((SUBLANES, S5_STATES), F32)],
        compiler_params=_params(1),
        name="s5_scan_rev" if reverse else "s5_scan_fwd",
    )(u_tb, w_drive.astype(BF16), w_read.astype(BF16),
      jnp.broadcast_to(a_re[None, :], (SUBLANES, S5_STATES)),
      jnp.broadcast_to(a_im[None, :], (SUBLANES, S5_STATES)),
      d_skip.reshape(1, ch))


FFT_N2 = 128


def _fft_tables(L):
    N = 2 * L
    N1 = N // FFT_N2
    k1 = np.arange(N1)[:, None]
    n1 = np.arange(N1 // 2)[None, :]
    n2 = np.arange(FFT_N2)[:, None, None]
    ang = -2.0 * np.pi * (k1[None] * (n2 + FFT_N2 * n1[None])) / N
    mr, mi = np.cos(ang), np.sin(ang)
    ma = np.concatenate([np.concatenate([mr, -mi], axis=2), np.concatenate([mi, mr], axis=2)], axis=1)
    gr, gi = np.transpose(mr, (0, 2, 1)), -np.transpose(mi, (0, 2, 1))
    mainv = np.concatenate([np.concatenate([gr, -gi], axis=2), np.concatenate([gi, gr], axis=2)], axis=1)
    kk = np.arange(FFT_N2)
    a2 = -2.0 * np.pi * np.outer(kk, kk) / FFT_N2
    fr, fi = np.cos(a2), np.sin(a2)
    f_fwd = np.block([[fr, -fi], [fi, fr]])
    f_inv = np.block([[fr, fi], [-fi, fr]])
    return (jnp.asarray(ma, BF16), jnp.asarray(mainv, BF16), jnp.asarray(f_fwd, BF16), jnp.asarray(f_inv, BF16))


def _hyena_stage_a_kernel(z_ref, ma_ref, o_ref):
    def body(n2, c):
        off = pl.multiple_of(n2 * LANES, LANES)
        x = jnp.concatenate([z_ref[0, 0, :, pl.ds(off, LANES)], z_ref[0, 1, :, pl.ds(off, LANES)]], axis=0)
        r = jnp.dot(ma_ref[n2], x.astype(BF16), preferred_element_type=F32)
        o_ref[0, 0, :, pl.ds(off, LANES)] = r.astype(o_ref.dtype)
        return c

    lax.fori_loop(0, FFT_N2, body, 0)


def _hyena_stage_c_kernel(a_ref, h_ref, ff_ref, fi_ref, o_ref):
    n1_count = a_ref.shape[2] // 2

    def body(k1, c):
        x = jnp.concatenate([a_ref[0, 0, k1], a_ref[0, 0, n1_count + k1]], axis=0).astype(BF16)
        y = jnp.dot(ff_ref[...], x, preferred_element_type=F32)
        yr, yi = y[:FFT_N2], y[FFT_N2:]
        hr = h_ref[0, 0, k1].astype(F32)
        hi = h_ref[0, 1, k1].astype(F32)
        x2 = jnp.concatenate([yr * hr - yi * hi, yr * hi + yi * hr], axis=0).astype(BF16)
        b = jnp.dot(fi_ref[...], x2, preferred_element_type=F32)
        o_ref[0, 0, k1] = b[:FFT_N2].astype(o_ref.dtype)
        o_ref[0, 0, n1_count + k1] = b[FFT_N2:].astype(o_ref.dtype)
        return c

    lax.fori_loop(0, n1_count, body, 0)


def _hyena_stage_a_inv_kernel(b_ref, mainv_ref, o_ref):
    half = o_ref.shape[2]

    def body(n2, c):
        off = pl.multiple_of(n2 * LANES, LANES)
        r = jnp.dot(mainv_ref[n2], b_ref[0, 0, :, pl.ds(off, LANES)].astype(BF16), preferred_element_type=F32)
        o_ref[0, 0, :, pl.ds(off, LANES)] = r[:half].astype(o_ref.dtype)
        o_ref[0, 1, :, pl.ds(off, LANES)] = r[half:].astype(o_ref.dtype)
        return c

    lax.fori_loop(0, FFT_N2, body, 0)


def pl_hyena_conv(zc, filt, L):
    n_ct, B, Lt, cw = zc.shape
    assert cw == LANES
    N = 2 * L
    N1 = N // FFT_N2
    half = N1 // 2
    flat = FFT_N2 * cw
    hf = jnp.fft.fft(filt, axis=0) / N
    h2 = hf.reshape(FFT_N2, N1, n_ct, cw).transpose(2, 1, 0, 3)
    h = jnp.stack([h2.real, h2.imag], axis=1).astype(BF16)
    ma, mainv, f_fwd, f_inv = _fft_tables(L)
    n_pairs = B // 2
    grid = (n_ct, n_pairs)
    full = lambda arr: pl.BlockSpec(arr.shape, lambda ct, bp: (0,) * arr.ndim)
    pair_rows = pl.BlockSpec((1, 2, half, flat), lambda ct, bp: (ct, bp, 0, 0))
    spec_flat = pl.BlockSpec((1, 1, 2 * N1, flat), lambda ct, bp: (ct, bp, 0, 0))
    spec_slab = pl.BlockSpec((1, 1, 2 * N1, FFT_N2, cw), lambda ct, bp: (ct, bp, 0, 0, 0))
    a = pl.pallas_call(
        _hyena_stage_a_kernel, grid=grid,
        in_specs=[pair_rows, full(ma)], out_specs=spec_flat,
        out_shape=jax.ShapeDtypeStruct((n_ct, n_pairs, 2 * N1, flat), BF16),
        compiler_params=_params(2), name="hyena_dft_slow",
    )(zc.reshape(n_ct, B, Lt // FFT_N2, flat), ma)
    b = pl.pallas_call(
        _hyena_stage_c_kernel, grid=grid,
        in_specs=[spec_slab, pl.BlockSpec((1, 2, N1, FFT_N2, cw), lambda ct, bp: (ct, 0, 0, 0, 0)),
                  full(f_fwd), full(f_inv)],
        out_specs=spec_slab,
        out_shape=jax.ShapeDtypeStruct((n_ct, n_pairs, 2 * N1, FFT_N2, cw), BF16),
        compiler_params=_params(2), name="hyena_dft_fast_filter",
    )(a.reshape(n_ct, n_pairs, 2 * N1, FFT_N2, cw), h, f_fwd, f_inv)
    y = pl.pallas_call(
        _hyena_stage_a_inv_kernel, grid=grid,
        in_specs=[spec_flat, full(mainv)], out_specs=pair_rows,
        out_shape=jax.ShapeDtypeStruct((n_ct, B, half, flat), BF16),
        compiler_params=_params(2), name="hyena_idft_slow",
    )(b.reshape(n_ct, n_pairs, 2 * N1, flat), mainv)
    return y.reshape(n_ct, B, L, cw)


def hyena_filter(L, w1, b1, w2, b2, w3, freq):
    t = jnp.arange(L, dtype=F32) / L
    ang = (2.0 * math.pi) * t[:, None] * jnp.arange(1, HY_BANDS + 1, dtype=F32)
    feat = jnp.concatenate([t[:, None], jnp.cos(ang), jnp.sin(ang)], axis=-1)
    hp = lax.Precision.HIGHEST
    h = jnp.sin(freq * (jnp.dot(feat, w1, precision=hp) + b1))
    h = jnp.sin(freq * (jnp.dot(h, w2, precision=hp) + b2))
    h = jnp.dot(h, w3, precision=hp).reshape(L, 2, HY_CH)
    window = jnp.exp(-t[:, None] * jnp.linspace(HY_DECAY_MIN, HY_DECAY_MAX, HY_CH, dtype=F32))
    h = h * window[:, None, :]
    filt = jnp.concatenate([h[:, 0], jnp.zeros((1, HY_CH), F32), h[:0:-1, 1]], axis=0)
    return filt / (jnp.sum(jnp.abs(filt), axis=0, keepdims=True) + EPS)


def hyena_conv(zc, filt_params, L):
    n_ct, B, Lt, cw = zc.shape
    Lc = Lt - L
    y_lat = pl_hyena_conv(zc, hyena_filter(L, *filt_params), L)
    z_ctx = zc[:, :, L:].astype(F32).transpose(1, 2, 0, 3).reshape(B, Lc, n_ct * cw)
    zf = jnp.fft.rfft(z_ctx, n=2 * Lc, axis=1)
    ff = jnp.fft.rfft(hyena_filter(Lc, *filt_params), n=2 * Lc, axis=0)
    y_ctx = jnp.fft.irfft(zf * ff[None], n=2 * Lc, axis=1)[:, :Lc]
    return y_lat, y_ctx


def _block_diag(blocks):
    G, r, c = blocks.shape
    eye = jnp.eye(G, dtype=blocks.dtype)
    return (eye[:, None, :, None] * blocks[:, :, None, :]).reshape(G * r, G * c)


def s5_scan(u_tb, a_re, a_im, log_dt, b_re, b_im, c_re, c_im, d_skip, n_lat_steps):
    outs = []
    for direction in range(2):
        A = lax.complex(a_re[direction], a_im[direction])
        dtA = jnp.exp(log_dt[direction])[:, None] * A
        a_bar = jnp.exp(dtA)
        b_bar = ((a_bar - 1.0) / A)[:, :, None] * lax.complex(b_re[direction], b_im[direction])
        bt_re = jnp.transpose(b_bar.real, (0, 2, 1))
        bt_im = jnp.transpose(b_bar.imag, (0, 2, 1))
        w_drive = jnp.concatenate([_block_diag(bt_re), _block_diag(bt_im)], axis=1)
        ct_re = jnp.transpose(c_re[direction], (0, 2, 1))
        ct_im = jnp.transpose(c_im[direction], (0, 2, 1))
        w_read = jnp.concatenate([_block_diag(ct_re), -_block_diag(ct_im)], axis=0)
        outs.append(pl_s5_scan(u_tb, w_drive, w_read, a_bar.real.reshape(-1), a_bar.imag.reshape(-1), d_skip,
                               n_lat_steps=n_lat_steps, reverse=direction == 1))
    return outs


def rope_tables(L, Lc):
    rows = L // GRID_W
    row = jnp.repeat(jnp.arange(rows, dtype=F32), GRID_W)
    col = jnp.tile(jnp.arange(GRID_W, dtype=F32), rows)
    inv = ROPE_BASE ** (-jnp.arange(ROPE_PAIRS_AXIS, dtype=F32) / ROPE_PAIRS_AXIS)
    ang = jnp.concatenate([row[:, None] * inv, col[:, None] * inv], axis=-1)
    cos, sin = jnp.cos(ang), jnp.sin(ang)
    n_maps = ATT_W // ATT_HEAD_DIM
    cosf = jnp.tile(jnp.concatenate([cos, cos], axis=-1), (1, n_maps))
    sinf = jnp.tile(jnp.concatenate([-sin, sin], axis=-1), (1, n_maps))
    return (jnp.concatenate([cosf, jnp.ones((Lc, ATT_W), F32)], axis=0),
            jnp.concatenate([sinf, jnp.zeros((Lc, ATT_W), F32)], axis=0))


def moe_dispatch(logits, b_g, b_e):
    T = logits.shape[0]
    g_logits = logits[:, :MOE_GROUPS] + b_g
    g_idx = jnp.argmax(g_logits, axis=-1)
    p_group = jnp.take_along_axis(jax.nn.softmax(g_logits, axis=-1), g_idx[:, None], axis=1)
    e_logits = (logits[:, MOE_GROUPS:MOE_GROUPS + N_EXPERTS] + b_e).reshape(T, MOE_GROUPS, MOE_EPG)
    e_logits = jnp.take_along_axis(e_logits, g_idx[:, None, None], axis=1)[:, 0]
    top_p, top_i = lax.top_k(jax.nn.softmax(e_logits, axis=-1), MOE_TOP_K)
    gate = p_group * top_p / jnp.sum(top_p, axis=-1, keepdims=True)
    expert = (g_idx[:, None] * MOE_EPG + top_i).reshape(-1).astype(jnp.int32)
    tok = jnp.repeat(jnp.arange(T, dtype=jnp.int32), MOE_TOP_K)
    n_assign = T * MOE_TOP_K
    n_blocks = -(-n_assign // MOE_BLOCK) + N_EXPERTS
    n_pad = n_blocks * MOE_BLOCK
    order = jnp.argsort(expert).astype(jnp.int32)
    sorted_rank = jnp.argsort(order).astype(jnp.int32)
    e_ids = jnp.arange(N_EXPERTS + 1, dtype=jnp.int32)
    bounds = jnp.sum((expert[:, None] < e_ids[None, :]).astype(jnp.int32), axis=0)
    start = bounds[:-1]
    counts = bounds[1:] - bounds[:-1]
    padded = (counts + MOE_BLOCK - 1) // MOE_BLOCK * MOE_BLOCK
    pad_end = jnp.cumsum(padded)
    pad_start = pad_end - padded
    slot_of_assign = (pad_start[expert] + sorted_rank - start[expert]).astype(jnp.int32)
    block_first = jnp.arange(n_blocks, dtype=jnp.int32) * MOE_BLOCK
    block_e = jnp.minimum(jnp.sum((pad_end[None, :] <= block_first[:, None]).astype(jnp.int32), axis=1),
                          N_EXPERTS - 1).astype(jnp.int32)
    slot_e = jnp.repeat(block_e, MOE_BLOCK)
    slot_r = jnp.arange(n_pad, dtype=jnp.int32) - pad_start[slot_e]
    slot_valid = (slot_r < counts[slot_e]) & (jnp.arange(n_pad) < pad_end[-1])
    slot_assign = order[jnp.clip(start[slot_e] + slot_r, 0, n_assign - 1)]
    slot_tok = jnp.where(slot_valid, tok[slot_assign], jnp.arange(n_pad, dtype=jnp.int32) % T)
    slot_gate = jnp.where(slot_valid, gate.reshape(-1)[slot_assign], 0.0)
    n_used = (pad_end[-1:] // MOE_BLOCK).astype(jnp.int32)
    return slot_tok, slot_gate, slot_of_assign, block_e, n_used


def hier_moe(h2, logits, b_g, b_e, w1, w3, w2, layer):
    T, D = h2.shape
    slot_tok, slot_gate, slot_of_assign, block_e, n_used = moe_dispatch(logits, b_g, b_e)
    yb = pl_moe_ffn(h2[slot_tok], slot_gate, block_e, n_used, w1, w3, w2, layer)
    slots = slot_of_assign.reshape(T, MOE_TOP_K)
    return yb[slots[:, 0]], yb[slots[:, 1]]


def kernel(x, c, ctx, c_ctx, w_mod, b_mod, norm1_g, norm2_g, final_g, w_in, w_out, hy_conv_w, hy_conv_b, hy_ffn_w1, hy_ffn_b1, hy_ffn_w2, hy_ffn_b2, hy_ffn_w3, hy_freq, hy_skip, hy_norm_g, s5_a_re, s5_a_im, s5_log_dt, s5_b_re, s5_b_im, s5_c_re, s5_c_im, s5_d, s5_glu_w, s5_norm_g, att_lq1, att_lk1, att_lq2, att_lk2, att_subln_g, moe_wg, moe_bg, moe_we, moe_be, moe_w1, moe_w3, moe_w2):
    B, L, D = x.shape
    Lc = ctx.shape[1]
    Lt = L + Lc
    assert B == SUBLANES and Lc == TOKEN_TILE and L % ATT_TQ == 0
    n_lat_tiles = L // TOKEN_TILE
    cosf, sinf = rope_tables(L, Lc)
    silu_c = jax.nn.silu(c)
    silu_cc = jax.nn.silu(c_ctx)
    hp = lax.Precision.HIGHEST
    q_scale = ATT_HEAD_DIM ** -0.5 * math.log2(math.e)
    xs = jnp.concatenate([x, ctx], axis=1)
    for l in range(DEPTH):
        lam_init = 0.8 - 0.6 * math.exp(-0.3 * l)
        mod = jnp.dot(silu_c, w_mod[l], precision=hp) + b_mod[l]
        cmod = jnp.dot(silu_cc, w_mod[l], precision=hp) + b_mod[l]
        modv = jnp.stack([mod, jnp.broadcast_to(cmod, mod.shape)], axis=1).reshape(2 * B, 1, N_MOD * D)

        u0, z, s5_u, q_p, q_r, k_r, v = pl_in_proj(xs, modv, norm1_g[l], w_in[l], hy_conv_w[l], hy_conv_b[l],
                                                   cosf, sinf, n_lat_tiles, q_scale)

        filt_params = (hy_ffn_w1[l], hy_ffn_b1[l], hy_ffn_w2[l], hy_ffn_b2[l], hy_ffn_w3[l], hy_freq[l])
        y_lat, y_ctx = hyena_conv(z, filt_params, L)

        s5_f, s5_b = s5_scan(s5_u.reshape(Lt * B, S5_CH), s5_a_re[l], s5_a_im[l], s5_log_dt[l], s5_b_re[l],
                             s5_b_im[l], s5_c_re[l], s5_c_im[l], s5_d[l], L)

        lam = (jnp.exp(jnp.sum(att_lq1[l] * att_lk1[l])) - jnp.exp(jnp.sum(att_lq2[l] * att_lk2[l])) + lam_init)
        g_scaled = (att_subln_g[l] * (1.0 - lam_init)).reshape(1, ATT_V_DIM)
        att_lat, att_ctx = pl_diff_attention(q_p, q_r, k_r, v, g_scaled, lam, L)

        w_router = jnp.zeros((D, LANES), F32).at[:, :MOE_GROUPS].set(moe_wg[l])
        w_router = w_router.at[:, MOE_GROUPS:MOE_GROUPS + N_EXPERTS].set(moe_we[l])
        xs, h2, logits = pl_out_proj(xs, u0, z, y_lat, y_ctx, s5_f.reshape(Lt, B * S5_CH),
                                     s5_b.reshape(Lt, B * S5_CH), att_lat, att_ctx, modv, hy_skip[l], hy_norm_g[l],
                                     s5_glu_w[l], s5_norm_g[l], w_out[l], norm2_g[l], w_router, n_lat_tiles)

        y0, y1 = hier_moe(h2.reshape(B * Lt, D), logits.reshape(B * Lt, LANES), moe_bg[l], moe_be[l],
                          moe_w1, moe_w3, moe_w2, l)
        xs = pl_moe_combine(xs, y0, y1, modv, final_g, n_lat_tiles, final=l == DEPTH - 1)
    return xs
```

```python
import functools
import math

import jax
import jax.numpy as jnp
import numpy as np
from jax import lax
from jax.experimental import pallas as pl
from jax.experimental.pallas import tpu as pltpu

D_MODEL = 1024
DEPTH = 4
GRID_W = 64
N_MOD = 6
EPS = 1e-6
HY_CH = D_MODEL // 4
S5_CH = D_MODEL // 4
ATT_W = D_MODEL // 2
HY_BANDS = 16
HY_DECAY_MIN = -math.log(1e-2) / 1.5
HY_DECAY_MAX = -math.log(1e-2) / 0.3
S5_GROUP = 16
S5_GROUPS = S5_CH // S5_GROUP
S5_STATE = 64
ATT_HEAD_DIM = 64
ATT_HEADS = ATT_W // (2 * ATT_HEAD_DIM)
ATT_V_DIM = 2 * ATT_HEAD_DIM
ROPE_HALF = ATT_HEAD_DIM // 2
ROPE_PAIRS_AXIS = ROPE_HALF // 2
ROPE_BASE = 10000.0
MOE_GROUPS = 4
MOE_EPG = 8
N_EXPERTS = MOE_GROUPS * MOE_EPG
MOE_TOP_K = 2
MOE_BLOCK = 512
IN_COLS = 3 * HY_CH + S5_CH + 3 * ATT_W
COL_S5 = 3 * HY_CH
COL_Q = COL_S5 + S5_CH
COL_K = COL_Q + ATT_W
COL_V = COL_K + ATT_W

LANES = 128
SUBLANES = 8
VMEM_LIMIT = 48 * 1024 * 1024
TOKEN_TILE = 256

F32 = jnp.float32
BF16 = jnp.bfloat16


def _params(n_axes, vmem=VMEM_LIMIT):
    return pltpu.CompilerParams(dimension_semantics=("arbitrary",) * n_axes, vmem_limit_bytes=vmem)


def _rms(x):
    return x * lax.rsqrt(jnp.mean(x * x, axis=-1, keepdims=True) + EPS)


def _in_kernel(x_ref, xp_ref, xn_ref, mod_ref, g_ref, w_ref, cw_ref, cb_ref, cos_ref, sin_ref,
               u0_ref, z_ref, s5_ref, qp_ref, qr_ref, k_ref, v_ref, *, n_lat_tiles, q_scale):
    i = pl.program_id(1)
    tm = x_ref.shape[1]
    d = x_ref.shape[2]
    g = g_ref[...]
    shift = mod_ref[0, :, 0:d]
    scale = mod_ref[0, :, d:2 * d]

    def norm_mod(xt):
        return (_rms(xt) * g) * (1.0 + scale) + shift

    h = jnp.concatenate([norm_mod(x_ref[0]), norm_mod(xp_ref[0]), norm_mod(xn_ref[0])], axis=0).astype(BF16)
    p = jnp.dot(h, w_ref[...], preferred_element_type=F32)

    hy = p[:tm, :COL_S5]
    is_ctx = i == n_lat_tiles
    has_prev = jnp.logical_and(i != 0, jnp.logical_not(is_ctx))
    has_next = jnp.logical_and(i != n_lat_tiles - 1, jnp.logical_not(is_ctx))
    prev_row = jnp.where(has_prev, p[tm + SUBLANES - 1:tm + SUBLANES, :COL_S5], 0.0)
    next_row = jnp.where(has_next, p[tm + SUBLANES:tm + SUBLANES + 1, :COL_S5], 0.0)
    rows = lax.broadcasted_iota(jnp.int32, (tm, 1), 0)
    up = jnp.where(rows == 0, prev_row, pltpu.roll(hy, 1, axis=0))
    dn = jnp.where(rows == tm - 1, next_row, pltpu.roll(hy, tm - 1, axis=0))
    u = up * cw_ref[0:1, :] + hy * cw_ref[1:2, :] + dn * cw_ref[2:3, :] + cb_ref[...]
    u0_ref[0] = u[:, :HY_CH].astype(u0_ref.dtype)
    z_ref[0] = u[:, HY_CH:2 * HY_CH] * u[:, 2 * HY_CH:]

    s5_ref[...] = p[:tm, COL_S5:COL_Q]

    lane = lax.broadcasted_iota(jnp.int32, (1, ATT_W), 1)
    first_half = jnp.bitwise_and(lane, ATT_HEAD_DIM - 1) < ROPE_HALF
    cos = cos_ref[...]
    sin = sin_ref[...]

    def rope(t):
        partner = jnp.where(first_half, pltpu.roll(t, ATT_W - ROPE_HALF, axis=1), pltpu.roll(t, ROPE_HALF, axis=1))
        return t * cos + partner * sin

    q = p[:tm, COL_Q:COL_K] * q_scale
    qp_ref[0] = q.astype(BF16)
    qr_ref[0] = rope(q).astype(BF16)
    k_ref[0] = rope(p[:tm, COL_K:COL_V]).astype(BF16)
    v_ref[0] = p[:tm, COL_V:].astype(BF16)


def pl_in_proj(xs, modv, norm_g, w_in, conv_w, conv_b, cosf, sinf, n_lat_tiles, q_scale):
    B, Lt, D = xs.shape
    tm = TOKEN_TILE
    n_tiles = Lt // tm
    halo_per_tile = tm // SUBLANES
    n_halo_blocks = Lt // SUBLANES
    tok = lambda w: pl.BlockSpec((1, tm, w), lambda b, i: (b, i, 0))
    const = lambda shape: pl.BlockSpec(shape, lambda b, i: (0,) * len(shape))
    out_shapes = (jax.ShapeDtypeStruct((B, Lt, HY_CH), BF16), jax.ShapeDtypeStruct((B, Lt, HY_CH), F32),
                  jax.ShapeDtypeStruct((Lt, B * S5_CH), F32),
                  jax.ShapeDtypeStruct((B, Lt, ATT_W), BF16), jax.ShapeDtypeStruct((B, Lt, ATT_W), BF16),
                  jax.ShapeDtypeStruct((B, Lt, ATT_W), BF16), jax.ShapeDtypeStruct((B, Lt, ATT_W), BF16))
    return pl.pallas_call(
        functools.partial(_in_kernel, n_lat_tiles=n_lat_tiles, q_scale=q_scale),
        grid=(B, n_tiles),
        in_specs=[tok(D),
                  pl.BlockSpec((1, SUBLANES, D), lambda b, i: (b, jnp.maximum(i * halo_per_tile - 1, 0), 0)),
                  pl.BlockSpec((1, SUBLANES, D),
                               lambda b, i: (b, jnp.minimum((i + 1) * halo_per_tile, n_halo_blocks - 1), 0)),
                  pl.BlockSpec((1, 1, N_MOD * D), lambda b, i: (2 * b + jnp.where(i >= n_lat_tiles, 1, 0), 0, 0)),
                  const((1, D)), const((D, IN_COLS)), const((3, COL_S5)), const((1, COL_S5)),
                  pl.BlockSpec((tm, ATT_W), lambda b, i: (i, 0)), pl.BlockSpec((tm, ATT_W), lambda b, i: (i, 0))],
        out_specs=(tok(HY_CH), tok(HY_CH), pl.BlockSpec((tm, S5_CH), lambda b, i: (i, b)),
                   tok(ATT_W), tok(ATT_W), tok(ATT_W), tok(ATT_W)),
        out_shape=out_shapes,
        compiler_params=_params(2),
        name="in_proj",
    )(xs, xs, xs, modv, norm_g.reshape(1, D), w_in.astype(BF16), conv_w, conv_b.reshape(1, COL_S5), cosf, sinf)


def _out_kernel(x_ref, u0_ref, z_ref, yl_ref, yc_ref, sf_ref, sb_ref, al_ref, ac_ref, mod_ref, skip_ref, hg_ref,
                gw_ref, sg_ref, wo_ref, n2g_ref, rwh_ref, rwl_ref, xo_ref, h2_ref, lg_ref, *, n_lat_tiles):
    d = x_ref.shape[2]
    is_ctx = pl.program_id(1) == n_lat_tiles
    y = jnp.where(is_ctx, yc_ref[0], yl_ref[0])
    att = jnp.where(is_ctx, ac_ref[0], al_ref[0])
    hy = _rms(u0_ref[0].astype(F32) * (y + skip_ref[...] * z_ref[0])) * hg_ref[...]
    gl = jax.nn.gelu(sf_ref[...].astype(F32) + sb_ref[...].astype(F32))
    gate = jax.nn.sigmoid(jnp.dot(gl.astype(BF16), gw_ref[...], preferred_element_type=F32))
    s5 = _rms(gl * gate) * sg_ref[...]
    mix = jnp.concatenate([hy.astype(BF16), s5.astype(BF16), att], axis=1)
    proj = jnp.dot(mix, wo_ref[...], preferred_element_type=F32)
    xn = x_ref[0] + mod_ref[0, :, 2 * d:3 * d] * proj
    xo_ref[0] = xn
    h2 = (_rms(xn) * n2g_ref[...]) * (1.0 + mod_ref[0, :, 4 * d:5 * d]) + mod_ref[0, :, 3 * d:4 * d]
    hh = h2.astype(BF16)
    hl = (h2 - hh.astype(F32)).astype(BF16)
    h2_ref[0] = hh
    lg = jnp.dot(hh, rwh_ref[...], preferred_element_type=F32)
    lg += jnp.dot(hl, rwh_ref[...], preferred_element_type=F32)
    lg += jnp.dot(hh, rwl_ref[...], preferred_element_type=F32)
    lg_ref[0] = lg


def pl_out_proj(xs, u0, z, y_lat, y_ctx, s5_f, s5_b, att_lat, att_ctx, modv, hy_skip, hy_norm_g, glu_w, s5_norm_g,
                w_out, norm2_g, w_router, n_lat_tiles):
    B, Lt, D = xs.shape
    tm = TOKEN_TILE
    tok = lambda w: pl.BlockSpec((1, tm, w), lambda b, i: (b, i, 0))
    lat_tok = lambda w: pl.BlockSpec((1, tm, w), lambda b, i: (b, jnp.minimum(i, n_lat_tiles - 1), 0))
    ctx_tok = lambda w: pl.BlockSpec((1, tm, w), lambda b, i: (b, 0, 0))
    tb = pl.BlockSpec((tm, S5_CH), lambda b, i: (i, b))
    const = lambda shape: pl.BlockSpec(shape, lambda b, i: (0,) * len(shape))
    rwh = w_router.astype(BF16)
    rwl = (w_router - rwh.astype(F32)).astype(BF16)
    return pl.pallas_call(
        functools.partial(_out_kernel, n_lat_tiles=n_lat_tiles),
        grid=(B, Lt // tm),
        in_specs=[tok(D), tok(HY_CH), tok(HY_CH), lat_tok(HY_CH), ctx_tok(HY_CH), tb, tb, lat_tok(ATT_W),
                  ctx_tok(ATT_W),
                  pl.BlockSpec((1, 1, N_MOD * D), lambda b, i: (2 * b + jnp.where(i >= n_lat_tiles, 1, 0), 0, 0)),
                  const((1, HY_CH)), const((1, HY_CH)), const((S5_CH, S5_CH)), const((1, S5_CH)),
                  const((D, D)), const((1, D)), const((D, LANES)), const((D, LANES))],
        out_specs=(tok(D), tok(D), tok(LANES)),
        out_shape=(jax.ShapeDtypeStruct((B, Lt, D), F32), jax.ShapeDtypeStruct((B, Lt, D), BF16),
                   jax.ShapeDtypeStruct((B, Lt, LANES), F32)),
        compiler_params=_params(2),
        name="out_proj",
    )(xs, u0, z, y_lat, y_ctx, s5_f, s5_b, att_lat, att_ctx, modv, hy_skip.reshape(1, HY_CH),
      hy_norm_g.reshape(1, HY_CH), glu_w.astype(BF16), s5_norm_g.reshape(1, S5_CH), w_out.astype(BF16),
      norm2_g.reshape(1, D), rwh, rwl)


def _dot_nt(a, b):
    return lax.dot_general(a, b, (((1,), (1,)), ((), ())), preferred_element_type=F32)


ATT_TQ = 512
ATT_SUB = 256


def _attn_kernel(qp_ref, qr_ref, k_ref, v_ref, g_ref, lam_ref, o_ref, *, n_lat):
    tq = qp_ref.shape[1]
    first_map = lax.broadcasted_iota(jnp.int32, (1, LANES), 1) < ATT_HEAD_DIM
    zero = jnp.zeros((), BF16)
    sub = min(ATT_SUB, tq)
    for r0 in range(0, tq, sub):
        qp = qp_ref[0, r0:r0 + sub, :]
        qr = qr_ref[0, r0:r0 + sub, :]
        probs = []
        for m in range(2):
            in_map = first_map if m == 0 else jnp.logical_not(first_map)
            s_c = _dot_nt(jnp.where(in_map, qp, zero), k_ref[0, n_lat:, :])
            mx = jnp.max(s_c, axis=-1, keepdims=True)
            if n_lat:
                s_l = _dot_nt(jnp.where(in_map, qr, zero), k_ref[0, :n_lat, :])
                mx = jnp.maximum(mx, jnp.max(s_l, axis=-1, keepdims=True))
                p_l = jnp.exp2(s_l - mx)
            p_c = jnp.exp2(s_c - mx)
            den = jnp.sum(p_c, axis=-1, keepdims=True)
            if n_lat:
                den = den + jnp.sum(p_l, axis=-1, keepdims=True)
            probs.append((p_c, p_l if n_lat else None, 1.0 / den))
        w0 = probs[0][2]
        w1 = lam_ref[0:1, 0:1] * probs[1][2]
        a_c = (probs[0][0] * w0 - probs[1][0] * w1).astype(BF16)
        o = jnp.dot(a_c, v_ref[0, n_lat:, :], preferred_element_type=F32)
        if n_lat:
            a_l = (probs[0][1] * w0 - probs[1][1] * w1).astype(BF16)
            o = o + jnp.dot(a_l, v_ref[0, :n_lat, :], preferred_element_type=F32)
        o_ref[0, r0:r0 + sub, :] = (_rms(o) * g_ref[...]).astype(o_ref.dtype)


def pl_diff_attention(qp, qr, k, v, g_scaled, lam, n_lat):
    B, Lt, _ = qp.shape
    n_ctx = Lt - n_lat
    lam_arr = jnp.full((SUBLANES, LANES), lam, F32)
    small = [pl.BlockSpec((1, LANES), lambda b, h, i: (0, 0)), pl.BlockSpec((SUBLANES, LANES), lambda b, h, i: (0, 0))]
    tq = ATT_TQ
    qspec = pl.BlockSpec((1, tq, LANES), lambda b, h, i: (b, i, h))
    kspec = pl.BlockSpec((1, Lt, LANES), lambda b, h, i: (b, 0, h))
    out_lat = pl.pallas_call(
        functools.partial(_attn_kernel, n_lat=n_lat),
        grid=(B, ATT_HEADS, n_lat // tq),
        in_specs=[qspec, qspec, kspec, kspec] + small,
        out_specs=qspec,
        out_shape=jax.ShapeDtypeStruct((B, n_lat, ATT_W), BF16),
        compiler_params=_params(3),
        name="diff_attention",
    )(qp, qr, k, v, g_scaled, lam_arr)
    ctx_blk = n_lat // n_ctx
    cspec = pl.BlockSpec((1, n_ctx, LANES), lambda b, h, i: (b, ctx_blk, h))
    out_ctx = pl.pallas_call(
        functools.partial(_attn_kernel, n_lat=0),
        grid=(B, ATT_HEADS, 1),
        in_specs=[cspec, cspec, cspec, cspec] + small,
        out_specs=pl.BlockSpec((1, n_ctx, LANES), lambda b, h, i: (b, 0, h)),
        out_shape=jax.ShapeDtypeStruct((B, n_ctx, ATT_W), BF16),
        compiler_params=_params(3),
        name="diff_attention_ctx",
    )(qp, qr, k, v, g_scaled, lam_arr)
    return out_lat, out_ctx


def _moe_kernel(be_ref, nb_ref, x_ref, gate_ref, w1_ref, w3_ref, w2_ref, o_ref, w1_scr, w3_scr, w2_scr):
    i = pl.program_id(0)
    new_expert = jnp.logical_or(i == 0, be_ref[i] != be_ref[jnp.maximum(i - 1, 0)])

    @pl.when(jnp.logical_and(i < nb_ref[0], new_expert))
    def _():
        w1_scr[...] = w1_ref[0, 0].astype(BF16)
        w3_scr[...] = w3_ref[0, 0].astype(BF16)
        w2_scr[...] = w2_ref[0, 0].astype(BF16)

    @pl.when(i < nb_ref[0])
    def _():
        x = x_ref[...]
        a = jnp.dot(x, w1_scr[...], preferred_element_type=F32)
        b = jnp.dot(x, w3_scr[...], preferred_element_type=F32)
        h = (a * jax.nn.sigmoid(a)) * b
        y = jnp.dot(h.astype(BF16), w2_scr[...], preferred_element_type=F32)
        rows = lax.broadcasted_iota(jnp.int32, (MOE_BLOCK, 1), 0)
        lane = lax.broadcasted_iota(jnp.int32, (1, LANES), 1)
        g_rows = gate_ref[0, 0:1, :]
        for r in range(1, MOE_BLOCK // LANES):
            g_rows = jnp.where(rows >= r * LANES, gate_ref[0, r:r + 1, :], g_rows)
        g_col = jnp.sum(jnp.where(lane == jnp.bitwise_and(rows, LANES - 1), g_rows, 0.0), axis=1, keepdims=True)
        o_ref[...] = (y * g_col).astype(o_ref.dtype)

    @pl.when(i >= nb_ref[0])
    def _():
        o_ref[...] = jnp.zeros_like(o_ref)


def pl_moe_ffn(xb, slot_gate, block_e, n_used, w1, w3, w2, layer):
    n_pad, D = xb.shape
    n_blocks = n_pad // MOE_BLOCK
    F = w1.shape[-1]
    grid_spec = pltpu.PrefetchScalarGridSpec(
        num_scalar_prefetch=2,
        grid=(n_blocks,),
        in_specs=[pl.BlockSpec((MOE_BLOCK, D), lambda i, be, nb: (i, 0)),
                  pl.BlockSpec((1, MOE_BLOCK // LANES, LANES), lambda i, be, nb: (i, 0, 0)),
                  pl.BlockSpec((1, 1, D, F), lambda i, be, nb: (layer, be[i], 0, 0)),
                  pl.BlockSpec((1, 1, D, F), lambda i, be, nb: (layer, be[i], 0, 0)),
                  pl.BlockSpec((1, 1, F, D), lambda i, be, nb: (layer, be[i], 0, 0))],
        out_specs=pl.BlockSpec((MOE_BLOCK, D), lambda i, be, nb: (i, 0)),
        scratch_shapes=[pltpu.VMEM((D, F), BF16), pltpu.VMEM((D, F), BF16), pltpu.VMEM((F, D), BF16)],
    )
    return pl.pallas_call(
        _moe_kernel,
        grid_spec=grid_spec,
        out_shape=jax.ShapeDtypeStruct((n_pad, D), BF16),
        compiler_params=_params(1),
        name="moe_ffn",
    )(block_e, n_used, xb, slot_gate.reshape(n_blocks, MOE_BLOCK // LANES, LANES), w1, w3, w2)


def _combine_kernel(x_ref, y0_ref, y1_ref, mod_ref, g_ref, o_ref, *, final):
    d = x_ref.shape[2]
    xn = x_ref[0] + mod_ref[0, :, 5 * d:6 * d] * (y0_ref[0].astype(F32) + y1_ref[0].astype(F32))
    o_ref[0] = _rms(xn) * g_ref[...] if final else xn


def pl_moe_combine(xs, y0, y1, modv, final_g, n_lat_tiles, final):
    B, Lt, D = xs.shape
    tm = TOKEN_TILE
    n_tiles = n_lat_tiles if final else Lt // tm
    tok = pl.BlockSpec((1, tm, D), lambda b, i: (b, i, 0))
    return pl.pallas_call(
        functools.partial(_combine_kernel, final=final),
        grid=(B, n_tiles),
        in_specs=[tok, tok, tok,
                  pl.BlockSpec((1, 1, N_MOD * D), lambda b, i: (2 * b + jnp.where(i >= n_lat_tiles, 1, 0), 0, 0)),
                  pl.BlockSpec((1, D), lambda b, i: (0, 0))],
        out_specs=tok,
        out_shape=jax.ShapeDtypeStruct((B, n_tiles * tm, D), F32),
        compiler_params=_params(2),
        name="moe_combine",
    )(xs, y0.reshape(B, Lt, D), y1.reshape(B, Lt, D), modv, final_g.reshape(1, D))


S5_STATES = S5_GROUPS * S5_STATE
S5_CHUNK = 64


def _s5_kernel(u_ref, wd_ref, wr_ref, ar_ref, ai_ref, d_ref, y_ref, x_scr, h_scr, hr_scr, hi_scr, *, reverse):
    ns = S5_STATES

    @pl.when(pl.program_id(0) == 0)
    def _():
        hr_scr[...] = jnp.zeros_like(hr_scr)
        hi_scr[...] = jnp.zeros_like(hi_scr)

    u = u_ref[...]
    x_scr[...] = jnp.dot(u.astype(BF16), wd_ref[...], preferred_element_type=F32)
    ar = ar_ref[...]
    ai = ai_ref[...]

    def step(hr, hi, t):
        r = pl.multiple_of(t * SUBLANES, SUBLANES)
        xr = x_scr[pl.ds(r, SUBLANES), :ns]
        xi = x_scr[pl.ds(r, SUBLANES), ns:]
        return ar * hr - ai * hi + xr, ar * hi + ai * hr + xi

    def body(j, carry):
        hr, hi = carry
        t0 = (S5_CHUNK - 1 - 2 * j) if reverse else 2 * j
        t1 = t0 - 1 if reverse else t0 + 1
        hr0, hi0 = step(hr, hi, t0)
        hr1, hi1 = step(hr0, hi0, t1)
        lo = t1 if reverse else t0
        first_r, second_r = (hr1, hr0) if reverse else (hr0, hr1)
        first_i, second_i = (hi1, hi0) if reverse else (hi0, hi1)
        r = pl.multiple_of(lo * SUBLANES, 2 * SUBLANES)
        h_scr[pl.ds(r, 2 * SUBLANES), :ns] = jnp.concatenate([first_r, second_r], axis=0).astype(BF16)
        h_scr[pl.ds(r, 2 * SUBLANES), ns:] = jnp.concatenate([first_i, second_i], axis=0).astype(BF16)
        return hr1, hi1

    hr, hi = lax.fori_loop(0, S5_CHUNK // 2, body, (hr_scr[...], hi_scr[...]))
    hr_scr[...] = hr
    hi_scr[...] = hi
    y = jnp.dot(h_scr[...], wr_ref[...], preferred_element_type=F32)
    if not reverse:
        y = y + u * d_ref[...]
    y_ref[...] = y.astype(y_ref.dtype)


def pl_s5_scan(u_tb, w_drive, w_read, a_re, a_im, d_skip, *, n_lat_steps, reverse):
    rows, ch = u_tb.shape
    rc = S5_CHUNK * SUBLANES
    n_chunks = rows // rc
    n_lat = n_lat_steps // S5_CHUNK
    n_ctx = n_chunks - n_lat
    assert rows % rc == 0 and n_lat_steps % S5_CHUNK == 0
    if reverse:
        def idx(i):
            return (n_chunks - 1 - i, 0)
    else:
        def idx(i):
            return (jnp.where(i < n_ctx, n_lat + i, i - n_ctx), 0)
    const = lambda i: (0, 0)
    ns2 = 2 * S5_STATES
    return pl.pallas_call(
        functools.partial(_s5_kernel, reverse=reverse),
        grid=(n_chunks,),
        in_specs=[pl.BlockSpec((rc, ch), idx),
                  pl.BlockSpec((ch, ns2), const),
                  pl.BlockSpec((ns2, ch), const),
                  pl.BlockSpec((SUBLANES, S5_STATES), const),
                  pl.BlockSpec((SUBLANES, S5_STATES), const),
                  pl.BlockSpec((1, ch), const)],
        out_specs=pl.BlockSpec((rc, ch), idx),
        out_shape=jax.ShapeDtypeStruct((rows, ch), BF16),
        scratch_shapes=[pltpu.VMEM((rc, ns2), F32), pltpu.VMEM((rc, ns2), BF16),
                        pltpu.VMEM((SUBLANES, S5_STATES), F32), pltpu.VMEM((SUBLANES, S5_STATES), F32)],
        compiler_params=_params(1),
        name="s5_scan_rev" if reverse else "s5_scan_fwd",
    )(u_tb, w_drive.astype(BF16), w_read.astype(BF16),
      jnp.broadcast_to(a_re[None, :], (SUBLANES, S5_STATES)),
      jnp.broadcast_to(a_im[None, :], (SUBLANES, S5_STATES)),
      d_skip.reshape(1, ch))


FFT_N2 = 128


def _fft_tables(L):
    N = 2 * L
    N1 = N // FFT_N2
    k1 = np.arange(N1)[:, None]
    n1 = np.arange(N1 // 2)[None, :]
    n2 = np.arange(FFT_N2)[:, None, None]
    ang = -2.0 * np.pi * (k1[None] * (n2 + FFT_N2 * n1[None])) / N
    mr, mi = np.cos(ang), np.sin(ang)
    ma = np.concatenate([np.concatenate([mr, -mi], axis=2), np.concatenate([mi, mr], axis=2)], axis=1)
    gr, gi = np.transpose(mr, (0, 2, 1)), -np.transpose(mi, (0, 2, 1))
    mainv = np.concatenate([np.concatenate([gr, -gi], axis=2), np.concatenate([gi, gr], axis=2)], axis=1)
    kk = np.arange(FFT_N2)
    a2 = -2.0 * np.pi * np.outer(kk, kk) / FFT_N2
    fr, fi = np.cos(a2), np.sin(a2)
    f_fwd = np.block([[fr, -fi], [fi, fr]])
    f_inv = np.block([[fr, fi], [-fi, fr]])
    return (jnp.asarray(ma, BF16), jnp.asarray(mainv, BF16), jnp.asarray(f_fwd, BF16), jnp.asarray(f_inv, BF16))


HYENA_VMEM_LIMIT = 56 * 1024 * 1024
FFT_UNROLL = 8


def _hyena_fft_kernel(z_ref, h_ref, ma_ref, mainv_ref, ff_ref, fi_ref, o_ref, a_scr, b_scr, *, n1_count):
    half = n1_count // 2
    n2c = FFT_N2

    def stage_a(n2, c):
        xr = z_ref[0, pl.ds(n2, half, stride=n2c), :]
        xi = z_ref[1, pl.ds(n2, half, stride=n2c), :]
        x = jnp.concatenate([xr, xi], axis=0).astype(BF16)
        r = jnp.dot(ma_ref[n2], x, preferred_element_type=F32)
        a_scr[pl.ds(pl.multiple_of(n2 * 2 * n1_count, 2 * n1_count), 2 * n1_count), :] = r
        return c

    lax.fori_loop(0, n2c, stage_a, 0, unroll=FFT_UNROLL)

    def stage_c(k1, c):
        ar = a_scr[pl.ds(k1, n2c, stride=2 * n1_count), :]
        ai = a_scr[pl.ds(n1_count + k1, n2c, stride=2 * n1_count), :]
        x = jnp.concatenate([ar, ai], axis=0).astype(BF16)
        y = jnp.dot(ff_ref[...], x, preferred_element_type=F32)
        yr, yi = y[:n2c], y[n2c:]
        hr = h_ref[0, k1].astype(F32)
        hi = h_ref[1, k1].astype(F32)
        x2 = jnp.concatenate([yr * hr - yi * hi, yr * hi + yi * hr], axis=0).astype(BF16)
        b = jnp.dot(fi_ref[...], x2, preferred_element_type=F32)
        b_scr[pl.ds(pl.multiple_of(k1 * 2 * n2c, 2 * n2c), 2 * n2c), :] = b
        return c

    lax.fori_loop(0, n1_count, stage_c, 0, unroll=FFT_UNROLL)

    def stage_a_inv(n2, c):
        br = b_scr[pl.ds(n2, n1_count, stride=2 * n2c), :]
        bi = b_scr[pl.ds(n2c + n2, n1_count, stride=2 * n2c), :]
        x = jnp.concatenate([br, bi], axis=0).astype(BF16)
        r = jnp.dot(mainv_ref[n2], x, preferred_element_type=F32)
        o_ref[0, pl.ds(n2, half, stride=n2c), :] = r[:half]
        o_ref[1, pl.ds(n2, half, stride=n2c), :] = r[half:]
        return c

    lax.fori_loop(0, n2c, stage_a_inv, 0, unroll=FFT_UNROLL)


def pl_hyena_conv(z, filt, L):
    B, _, C = z.shape
    N = 2 * L
    N1 = N // FFT_N2
    hf = jnp.fft.fft(filt, axis=0) / N
    h2 = hf.reshape(FFT_N2, N1, C).transpose(1, 0, 2)
    h = jnp.stack([h2.real, h2.imag]).astype(BF16)
    ma, mainv, f_fwd, f_inv = _fft_tables(L)
    cw = LANES
    full = lambda arr: pl.BlockSpec(arr.shape, lambda ct, bp: (0,) * arr.ndim)
    return pl.pallas_call(
        functools.partial(_hyena_fft_kernel, n1_count=N1),
        grid=(C // cw, B // 2),
        in_specs=[pl.BlockSpec((2, L, cw), lambda ct, bp: (bp, 0, ct)),
                  pl.BlockSpec((2, N1, FFT_N2, cw), lambda ct, bp: (0, 0, 0, ct)),
                  full(ma), full(mainv), full(f_fwd), full(f_inv)],
        out_specs=pl.BlockSpec((2, L, cw), lambda ct, bp: (bp, 0, ct)),
        out_shape=jax.ShapeDtypeStruct((B, L, C), F32),
        scratch_shapes=[pltpu.VMEM((FFT_N2 * 2 * N1, cw), F32), pltpu.VMEM((N1 * 2 * FFT_N2, cw), F32)],
        compiler_params=_params(2, HYENA_VMEM_LIMIT),
        name="hyena_fft_conv",
    )(z, h, ma, mainv, f_fwd, f_inv)


def hyena_filter(L, w1, b1, w2, b2, w3, freq):
    t = jnp.arange(L, dtype=F32) / L
    ang = (2.0 * math.pi) * t[:, None] * jnp.arange(1, HY_BANDS + 1, dtype=F32)
    feat = jnp.concatenate([t[:, None], jnp.cos(ang), jnp.sin(ang)], axis=-1)
    hp = lax.Precision.HIGHEST
    h = jnp.sin(freq * (jnp.dot(feat, w1, precision=hp) + b1))
    h = jnp.sin(freq * (jnp.dot(h, w2, precision=hp) + b2))
    h = jnp.dot(h, w3, precision=hp).reshape(L, 2, HY_CH)
    window = jnp.exp(-t[:, None] * jnp.linspace(HY_DECAY_MIN, HY_DECAY_MAX, HY_CH, dtype=F32))
    h = h * window[:, None, :]
    filt = jnp.concatenate([h[:, 0], jnp.zeros((1, HY_CH), F32), h[:0:-1, 1]], axis=0)
    return filt / (jnp.sum(jnp.abs(filt), axis=0, keepdims=True) + EPS)


def hyena_conv(z, filt_params, L):
    Lc = z.shape[1] - L
    y_lat = pl_hyena_conv(z, hyena_filter(L, *filt_params), L)
    zf = jnp.fft.rfft(z[:, L:], n=2 * Lc, axis=1)
    ff = jnp.fft.rfft(hyena_filter(Lc, *filt_params), n=2 * Lc, axis=0)
    y_ctx = jnp.fft.irfft(zf * ff[None], n=2 * Lc, axis=1)[:, :Lc]
    return y_lat, y_ctx


def _block_diag(blocks):
    G, r, c = blocks.shape
    eye = jnp.eye(G, dtype=blocks.dtype)
    return (eye[:, None, :, None] * blocks[:, :, None, :]).reshape(G * r, G * c)


def s5_scan(u_tb, a_re, a_im, log_dt, b_re, b_im, c_re, c_im, d_skip, n_lat_steps):
    outs = []
    for direction in range(2):
        A = lax.complex(a_re[direction], a_im[direction])
        dtA = jnp.exp(log_dt[direction])[:, None] * A
        a_bar = jnp.exp(dtA)
        b_bar = ((a_bar - 1.0) / A)[:, :, None] * lax.complex(b_re[direction], b_im[direction])
        bt_re = jnp.transpose(b_bar.real, (0, 2, 1))
        bt_im = jnp.transpose(b_bar.imag, (0, 2, 1))
        w_drive = jnp.concatenate([_block_diag(bt_re), _block_diag(bt_im)], axis=1)
        ct_re = jnp.transpose(c_re[direction], (0, 2, 1))
        ct_im = jnp.transpose(c_im[direction], (0, 2, 1))
        w_read = jnp.concatenate([_block_diag(ct_re), -_block_diag(ct_im)], axis=0)
        outs.append(pl_s5_scan(u_tb, w_drive, w_read, a_bar.real.reshape(-1), a_bar.imag.reshape(-1), d_skip,
                               n_lat_steps=n_lat_steps, reverse=direction == 1))
    return outs


def rope_tables(L, Lc):
    rows = L // GRID_W
    row = jnp.repeat(jnp.arange(rows, dtype=F32), GRID_W)
    col = jnp.tile(jnp.arange(GRID_W, dtype=F32), rows)
    inv = ROPE_BASE ** (-jnp.arange(ROPE_PAIRS_AXIS, dtype=F32) / ROPE_PAIRS_AXIS)
    ang = jnp.concatenate([row[:, None] * inv, col[:, None] * inv], axis=-1)
    cos, sin = jnp.cos(ang), jnp.sin(ang)
    n_maps = ATT_W // ATT_HEAD_DIM
    cosf = jnp.tile(jnp.concatenate([cos, cos], axis=-1), (1, n_maps))
    sinf = jnp.tile(jnp.concatenate([-sin, sin], axis=-1), (1, n_maps))
    return (jnp.concatenate([cosf, jnp.ones((Lc, ATT_W), F32)], axis=0),
            jnp.concatenate([sinf, jnp.zeros((Lc, ATT_W), F32)], axis=0))


def moe_dispatch(logits, b_g, b_e):
    T = logits.shape[0]
    g_logits = logits[:, :MOE_GROUPS] + b_g
    g_idx = jnp.argmax(g_logits, axis=-1)
    p_group = jnp.take_along_axis(jax.nn.softmax(g_logits, axis=-1), g_idx[:, None], axis=1)
    e_logits = (logits[:, MOE_GROUPS:MOE_GROUPS + N_EXPERTS] + b_e).reshape(T, MOE_GROUPS, MOE_EPG)
    e_logits = jnp.take_along_axis(e_logits, g_idx[:, None, None], axis=1)[:, 0]
    top_p, top_i = lax.top_k(jax.nn.softmax(e_logits, axis=-1), MOE_TOP_K)
    gate = p_group * top_p / jnp.sum(top_p, axis=-1, keepdims=True)
    expert = (g_idx[:, None] * MOE_EPG + top_i).reshape(-1).astype(jnp.int32)
    tok = jnp.repeat(jnp.arange(T, dtype=jnp.int32), MOE_TOP_K)
    n_assign = T * MOE_TOP_K
    n_blocks = -(-n_assign // MOE_BLOCK) + N_EXPERTS
    n_pad = n_blocks * MOE_BLOCK
    order = jnp.argsort(expert).astype(jnp.int32)
    sorted_rank = jnp.argsort(order).astype(jnp.int32)
    e_ids = jnp.arange(N_EXPERTS + 1, dtype=jnp.int32)
    bounds = jnp.sum((expert[:, None] < e_ids[None, :]).astype(jnp.int32), axis=0)
    start = bounds[:-1]
    counts = bounds[1:] - bounds[:-1]
    padded = (counts + MOE_BLOCK - 1) // MOE_BLOCK * MOE_BLOCK
    pad_end = jnp.cumsum(padded)
    pad_start = pad_end - padded
    slot_of_assign = (pad_start[expert] + sorted_rank - start[expert]).astype(jnp.int32)
    block_first = jnp.arange(n_blocks, dtype=jnp.int32) * MOE_BLOCK
    block_e = jnp.minimum(jnp.sum((pad_end[None, :] <= block_first[:, None]).astype(jnp.int32), axis=1),
                          N_EXPERTS - 1).astype(jnp.int32)
    slot_e = jnp.repeat(block_e, MOE_BLOCK)
    slot_r = jnp.arange(n_pad, dtype=jnp.int32) - pad_start[slot_e]
    slot_valid = (slot_r < counts[slot_e]) & (jnp.arange(n_pad) < pad_end[-1])
    slot_assign = order[jnp.clip(start[slot_e] + slot_r, 0, n_assign - 1)]
    slot_tok = jnp.where(slot_valid, tok[slot_assign], jnp.arange(n_pad, dtype=jnp.int32) % T)
    slot_gate = jnp.where(slot_valid, gate.reshape(-1)[slot_assign], 0.0)
    n_used = (pad_end[-1:] // MOE_BLOCK).astype(jnp.int32)
    return slot_tok, slot_gate, slot_of_assign, block_e, n_used


def hier_moe(h2, logits, b_g, b_e, w1, w3, w2, layer):
    T, D = h2.shape
    slot_tok, slot_gate, slot_of_assign, block_e, n_used = moe_dispatch(logits, b_g, b_e)
    yb = pl_moe_ffn(h2[slot_tok], slot_gate, block_e, n_used, w1, w3, w2, layer)
    slots = slot_of_assign.reshape(T, MOE_TOP_K)
    return yb[slots[:, 0]], yb[slots[:, 1]]


def kernel(x, c, ctx, c_ctx, w_mod, b_mod, norm1_g, norm2_g, final_g, w_in, w_out, hy_conv_w, hy_conv_b, hy_ffn_w1, hy_ffn_b1, hy_ffn_w2, hy_ffn_b2, hy_ffn_w3, hy_freq, hy_skip, hy_norm_g, s5_a_re, s5_a_im, s5_log_dt, s5_b_re, s5_b_im, s5_c_re, s5_c_im, s5_d, s5_glu_w, s5_norm_g, att_lq1, att_lk1, att_lq2, att_lk2, att_subln_g, moe_wg, moe_bg, moe_we, moe_be, moe_w1, moe_w3, moe_w2):
    B, L, D = x.shape
    Lc = ctx.shape[1]
    Lt = L + Lc
    assert B == SUBLANES and Lc == TOKEN_TILE and L % ATT_TQ == 0
    n_lat_tiles = L // TOKEN_TILE
    cosf, sinf = rope_tables(L, Lc)
    silu_c = jax.nn.silu(c)
    silu_cc = jax.nn.silu(c_ctx)
    hp = lax.Precision.HIGHEST
    q_scale = ATT_HEAD_DIM ** -0.5 * math.log2(math.e)
    xs = jnp.concatenate([x, ctx], axis=1)
    for l in range(DEPTH):
        lam_init = 0.8 - 0.6 * math.exp(-0.3 * l)
        mod = jnp.dot(silu_c, w_mod[l], precision=hp) + b_mod[l]
        cmod = jnp.dot(silu_cc, w_mod[l], precision=hp) + b_mod[l]
        modv = jnp.stack([mod, jnp.broadcast_to(cmod, mod.shape)], axis=1).reshape(2 * B, 1, N_MOD * D)

        u0, z, s5_u, q_p, q_r, k_r, v = pl_in_proj(xs, modv, norm1_g[l], w_in[l], hy_conv_w[l], hy_conv_b[l],
                                                   cosf, sinf, n_lat_tiles, q_scale)

        filt_params = (hy_ffn_w1[l], hy_ffn_b1[l], hy_ffn_w2[l], hy_ffn_b2[l], hy_ffn_w3[l], hy_freq[l])
        y_lat, y_ctx = hyena_conv(z, filt_params, L)

        s5_f, s5_b = s5_scan(s5_u.reshape(Lt * B, S5_CH), s5_a_re[l], s5_a_im[l], s5_log_dt[l], s5_b_re[l],
                             s5_b_im[l], s5_c_re[l], s5_c_im[l], s5_d[l], L)

        lam = (jnp.exp(jnp.sum(att_lq1[l] * att_lk1[l])) - jnp.exp(jnp.sum(att_lq2[l] * att_lk2[l])) + lam_init)
        g_scaled = (att_subln_g[l] * (1.0 - lam_init)).reshape(1, ATT_V_DIM)
        att_lat, att_ctx = pl_diff_attention(q_p, q_r, k_r, v, g_scaled, lam, L)

        w_router = jnp.zeros((D, LANES), F32).at[:, :MOE_GROUPS].set(moe_wg[l])
        w_router = w_router.at[:, MOE_GROUPS:MOE_GROUPS + N_EXPERTS].set(moe_we[l])
        xs, h2, logits = pl_out_proj(xs, u0, z, y_lat, y_ctx, s5_f.reshape(Lt, B * S5_CH),
                                     s5_b.reshape(Lt, B * S5_CH), att_lat, att_ctx, modv, hy_skip[l], hy_norm_g[l],
                                     s5_glu_w[l], s5_norm_g[l], w_out[l], norm2_g[l], w_router, n_lat_tiles)

        y0, y1 = hier_moe(h2.reshape(B * Lt, D), logits.reshape(B * Lt, LANES), moe_bg[l], moe_be[l],
                          moe_w1, moe_w3, moe_w2, l)
        xs = pl_moe_combine(xs, y0, y1, modv, final_g, n_lat_tiles, final=l == DEPTH - 1)
    return xs
```

```python
import functools
import math

import jax
import jax.numpy as jnp
import numpy as np
from jax import lax
from jax.experimental import pallas as pl
from jax.experimental.pallas import tpu as pltpu

D_MODEL = 1024
DEPTH = 4
GRID_W = 64
N_MOD = 6
EPS = 1e-6
HY_CH = D_MODEL // 4
S5_CH = D_MODEL // 4
ATT_W = D_MODEL // 2
HY_BANDS = 16
HY_DECAY_MIN = -math.log(1e-2) / 1.5
HY_DECAY_MAX = -math.log(1e-2) / 0.3
S5_GROUP = 16
S5_GROUPS = S5_CH // S5_GROUP
S5_STATE = 64
ATT_HEAD_DIM = 64
ATT_HEADS = ATT_W // (2 * ATT_HEAD_DIM)
ATT_V_DIM = 2 * ATT_HEAD_DIM
ROPE_HALF = ATT_HEAD_DIM // 2
ROPE_PAIRS_AXIS = ROPE_HALF // 2
ROPE_BASE = 10000.0
MOE_GROUPS = 4
MOE_EPG = 8
N_EXPERTS = MOE_GROUPS * MOE_EPG
MOE_TOP_K = 2
MOE_BLOCK = 512
IN_COLS = 3 * HY_CH + S5_CH + 3 * ATT_W
COL_S5 = 3 * HY_CH
COL_Q = COL_S5 + S5_CH
COL_K = COL_Q + ATT_W
COL_V = COL_K + ATT_W

LANES = 128
SUBLANES = 8
VMEM_LIMIT = 48 * 1024 * 1024
TOKEN_TILE = 256

F32 = jnp.float32
BF16 = jnp.bfloat16


def _params(n_axes, vmem=VMEM_LIMIT):
    return pltpu.CompilerParams(dimension_semantics=("arbitrary",) * n_axes, vmem_limit_bytes=vmem)


def _rms(x):
    return x * lax.rsqrt(jnp.mean(x * x, axis=-1, keepdims=True) + EPS)


def _in_kernel(x_ref, xp_ref, xn_ref, mod_ref, g_ref, w_ref, cw_ref, cb_ref, cos_ref, sin_ref,
               u0_ref, z_ref, s5_ref, qp_ref, qr_ref, k_ref, v_ref, *, n_lat_tiles, q_scale):
    i = pl.program_id(1)
    tm = x_ref.shape[1]
    d = x_ref.shape[2]
    g = g_ref[...]
    shift = mod_ref[0, :, 0:d]
    scale = mod_ref[0, :, d:2 * d]

    def norm_mod(xt):
        return (_rms(xt) * g) * (1.0 + scale) + shift

    h = jnp.concatenate([norm_mod(x_ref[0]), norm_mod(xp_ref[0]), norm_mod(xn_ref[0])], axis=0).astype(BF16)
    p = jnp.dot(h, w_ref[...], preferred_element_type=F32)

    hy = p[:tm, :COL_S5]
    is_ctx = i == n_lat_tiles
    has_prev = jnp.logical_and(i != 0, jnp.logical_not(is_ctx))
    has_next = jnp.logical_and(i != n_lat_tiles - 1, jnp.logical_not(is_ctx))
    prev_row = jnp.where(has_prev, p[tm + SUBLANES - 1:tm + SUBLANES, :COL_S5], 0.0)
    next_row = jnp.where(has_next, p[tm + SUBLANES:tm + SUBLANES + 1, :COL_S5], 0.0)
    rows = lax.broadcasted_iota(jnp.int32, (tm, 1), 0)
    up = jnp.where(rows == 0, prev_row, pltpu.roll(hy, 1, axis=0))
    dn = jnp.where(rows == tm - 1, next_row, pltpu.roll(hy, tm - 1, axis=0))
    u = up * cw_ref[0:1, :] + hy * cw_ref[1:2, :] + dn * cw_ref[2:3, :] + cb_ref[...]
    u0_ref[0] = u[:, :HY_CH].astype(u0_ref.dtype)
    z_ref[0] = u[:, HY_CH:2 * HY_CH] * u[:, 2 * HY_CH:]

    s5_ref[...] = p[:tm, COL_S5:COL_Q]

    lane = lax.broadcasted_iota(jnp.int32, (1, ATT_W), 1)
    first_half = jnp.bitwise_and(lane, ATT_HEAD_DIM - 1) < ROPE_HALF
    cos = cos_ref[...]
    sin = sin_ref[...]

    def rope(t):
        partner = jnp.where(first_half, pltpu.roll(t, ATT_W - ROPE_HALF, axis=1), pltpu.roll(t, ROPE_HALF, axis=1))
        return t * cos + partner * sin

    q = p[:tm, COL_Q:COL_K] * q_scale
    qp_ref[0] = q.astype(BF16)
    qr_ref[0] = rope(q).astype(BF16)
    k_ref[0] = rope(p[:tm, COL_K:COL_V]).astype(BF16)
    v_ref[0] = p[:tm, COL_V:].astype(BF16)


def pl_in_proj(xs, modv, norm_g, w_in, conv_w, conv_b, cosf, sinf, n_lat_tiles, q_scale):
    B, Lt, D = xs.shape
    tm = TOKEN_TILE
    n_tiles = Lt // tm
    halo_per_tile = tm // SUBLANES
    n_halo_blocks = Lt // SUBLANES
    tok = lambda w: pl.BlockSpec((1, tm, w), lambda b, i: (b, i, 0))
    const = lambda shape: pl.BlockSpec(shape, lambda b, i: (0,) * len(shape))
    out_shapes = (jax.ShapeDtypeStruct((B, Lt, HY_CH), BF16), jax.ShapeDtypeStruct((B, Lt, HY_CH), F32),
                  jax.ShapeDtypeStruct((Lt, B * S5_CH), F32),
                  jax.ShapeDtypeStruct((B, Lt, ATT_W), BF16), jax.ShapeDtypeStruct((B, Lt, ATT_W), BF16),
                  jax.ShapeDtypeStruct((B, Lt, ATT_W), BF16), jax.ShapeDtypeStruct((B, Lt, ATT_W), BF16))
    return pl.pallas_call(
        functools.partial(_in_kernel, n_lat_tiles=n_lat_tiles, q_scale=q_scale),
        grid=(B, n_tiles),
        in_specs=[tok(D),
                  pl.BlockSpec((1, SUBLANES, D), lambda b, i: (b, jnp.maximum(i * halo_per_tile - 1, 0), 0)),
                  pl.BlockSpec((1, SUBLANES, D),
                               lambda b, i: (b, jnp.minimum((i + 1) * halo_per_tile, n_halo_blocks - 1), 0)),
                  pl.BlockSpec((1, 1, N_MOD * D), lambda b, i: (2 * b + jnp.where(i >= n_lat_tiles, 1, 0), 0, 0)),
                  const((1, D)), const((D, IN_COLS)), const((3, COL_S5)), const((1, COL_S5)),
                  pl.BlockSpec((tm, ATT_W), lambda b, i: (i, 0)), pl.BlockSpec((tm, ATT_W), lambda b, i: (i, 0))],
        out_specs=(tok(HY_CH), tok(HY_CH), pl.BlockSpec((tm, S5_CH), lambda b, i: (i, b)),
                   tok(ATT_W), tok(ATT_W), tok(ATT_W), tok(ATT_W)),
        out_shape=out_shapes,
        compiler_params=_params(2),
        name="in_proj",
    )(xs, xs, xs, modv, norm_g.reshape(1, D), w_in, conv_w, conv_b.reshape(1, COL_S5), cosf, sinf)


def _out_kernel(x_ref, u0_ref, z_ref, yl_ref, yc_ref, s5_ref, al_ref, ac_ref, mod_ref, skip_ref, hg_ref,
                gw_ref, sg_ref, wo_ref, n2g_ref, rwh_ref, rwl_ref, xo_ref, h2_ref, lg_ref, *, n_lat_tiles):
    d = x_ref.shape[2]
    is_ctx = pl.program_id(1) == n_lat_tiles
    y = jnp.where(is_ctx, yc_ref[0], yl_ref[0])
    att = jnp.where(is_ctx, ac_ref[0], al_ref[0])
    hy = _rms(u0_ref[0].astype(F32) * (y + skip_ref[...] * z_ref[0])) * hg_ref[...]
    gl = jax.nn.gelu(s5_ref[...].astype(F32))
    gate = jax.nn.sigmoid(jnp.dot(gl.astype(BF16), gw_ref[...], preferred_element_type=F32))
    s5 = _rms(gl * gate) * sg_ref[...]
    mix = jnp.concatenate([hy.astype(BF16), s5.astype(BF16), att], axis=1)
    proj = jnp.dot(mix, wo_ref[...], preferred_element_type=F32)
    xn = x_ref[0] + mod_ref[0, :, 2 * d:3 * d] * proj
    xo_ref[0] = xn
    h2 = (_rms(xn) * n2g_ref[...]) * (1.0 + mod_ref[0, :, 4 * d:5 * d]) + mod_ref[0, :, 3 * d:4 * d]
    hh = h2.astype(BF16)
    hl = (h2 - hh.astype(F32)).astype(BF16)
    h2_ref[0] = hh
    lg = jnp.dot(hh, rwh_ref[...], preferred_element_type=F32)
    lg += jnp.dot(hl, rwh_ref[...], preferred_element_type=F32)
    lg += jnp.dot(hh, rwl_ref[...], preferred_element_type=F32)
    lg_ref[0] = lg


def pl_out_proj(xs, u0, z, y_lat, y_ctx, s5_y, att_lat, att_ctx, modv, hy_skip, hy_norm_g, glu_w, s5_norm_g,
                w_out, norm2_g, rwh, rwl, n_lat_tiles):
    B, Lt, D = xs.shape
    tm = TOKEN_TILE
    tok = lambda w: pl.BlockSpec((1, tm, w), lambda b, i: (b, i, 0))
    lat_tok = lambda w: pl.BlockSpec((1, tm, w), lambda b, i: (b, jnp.minimum(i, n_lat_tiles - 1), 0))
    ctx_tok = lambda w: pl.BlockSpec((1, tm, w), lambda b, i: (b, 0, 0))
    tb = pl.BlockSpec((tm, S5_CH), lambda b, i: (i, b))
    const = lambda shape: pl.BlockSpec(shape, lambda b, i: (0,) * len(shape))
    return pl.pallas_call(
        functools.partial(_out_kernel, n_lat_tiles=n_lat_tiles),
        grid=(B, Lt // tm),
        in_specs=[tok(D), tok(HY_CH), tok(HY_CH), lat_tok(HY_CH), ctx_tok(HY_CH), tb, lat_tok(ATT_W),
                  ctx_tok(ATT_W),
                  pl.BlockSpec((1, 1, N_MOD * D), lambda b, i: (2 * b + jnp.where(i >= n_lat_tiles, 1, 0), 0, 0)),
                  const((1, HY_CH)), const((1, HY_CH)), const((S5_CH, S5_CH)), const((1, S5_CH)),
                  const((D, D)), const((1, D)), const((D, LANES)), const((D, LANES))],
        out_specs=(tok(D), tok(D), tok(LANES)),
        out_shape=(jax.ShapeDtypeStruct((B, Lt, D), F32), jax.ShapeDtypeStruct((B, Lt, D), BF16),
                   jax.ShapeDtypeStruct((B, Lt, LANES), F32)),
        compiler_params=_params(2),
        name="out_proj",
    )(xs, u0, z, y_lat, y_ctx, s5_y, att_lat, att_ctx, modv, hy_skip.reshape(1, HY_CH),
      hy_norm_g.reshape(1, HY_CH), glu_w, s5_norm_g.reshape(1, S5_CH), w_out, norm2_g.reshape(1, D), rwh, rwl)


def _dot_nt(a, b):
    return lax.dot_general(a, b, (((1,), (1,)), ((), ())), preferred_element_type=F32)


ATT_TQ = 512
ATT_SUB = 256


def _attn_kernel(qp_ref, qr_ref, k_ref, v_ref, g_ref, lam_ref, o_ref, *, n_lat):
    tq = qp_ref.shape[1]
    first_map = lax.broadcasted_iota(jnp.int32, (1, LANES), 1) < ATT_HEAD_DIM
    zero = jnp.zeros((), BF16)
    sub = min(ATT_SUB, tq)
    for r0 in range(0, tq, sub):
        qp = qp_ref[0, r0:r0 + sub, :]
        qr = qr_ref[0, r0:r0 + sub, :]
        probs = []
        for m in range(2):
            in_map = first_map if m == 0 else jnp.logical_not(first_map)
            s_c = _dot_nt(jnp.where(in_map, qp, zero), k_ref[0, n_lat:, :])
            mx = jnp.max(s_c, axis=-1, keepdims=True)
            if n_lat:
                s_l = _dot_nt(jnp.where(in_map, qr, zero), k_ref[0, :n_lat, :])
                mx = jnp.maximum(mx, jnp.max(s_l, axis=-1, keepdims=True))
                p_l = jnp.exp2(s_l - mx)
            p_c = jnp.exp2(s_c - mx)
            den = jnp.sum(p_c, axis=-1, keepdims=True)
            if n_lat:
                den = den + jnp.sum(p_l, axis=-1, keepdims=True)
            probs.append((p_c, p_l if n_lat else None, 1.0 / den))
        w0 = probs[0][2]
        w1 = lam_ref[0:1, 0:1] * probs[1][2]
        a_c = (probs[0][0] * w0 - probs[1][0] * w1).astype(BF16)
        o = jnp.dot(a_c, v_ref[0, n_lat:, :], preferred_element_type=F32)
        if n_lat:
            a_l = (probs[0][1] * w0 - probs[1][1] * w1).astype(BF16)
            o = o + jnp.dot(a_l, v_ref[0, :n_lat, :], preferred_element_type=F32)
        o_ref[0, r0:r0 + sub, :] = (_rms(o) * g_ref[...]).astype(o_ref.dtype)


def pl_diff_attention(qp, qr, k, v, g_scaled, lam, n_lat):
    B, Lt, _ = qp.shape
    n_ctx = Lt - n_lat
    lam_arr = jnp.full((SUBLANES, LANES), lam, F32)
    small = [pl.BlockSpec((1, LANES), lambda b, h, i: (0, 0)), pl.BlockSpec((SUBLANES, LANES), lambda b, h, i: (0, 0))]
    tq = ATT_TQ
    qspec = pl.BlockSpec((1, tq, LANES), lambda b, h, i: (b, i, h))
    kspec = pl.BlockSpec((1, Lt, LANES), lambda b, h, i: (b, 0, h))
    out_lat = pl.pallas_call(
        functools.partial(_attn_kernel, n_lat=n_lat),
        grid=(B, ATT_HEADS, n_lat // tq),
        in_specs=[qspec, qspec, kspec, kspec] + small,
        out_specs=qspec,
        out_shape=jax.ShapeDtypeStruct((B, n_lat, ATT_W), BF16),
        compiler_params=_params(3),
        name="diff_attention",
    )(qp, qr, k, v, g_scaled, lam_arr)
    ctx_blk = n_lat // n_ctx
    cspec = pl.BlockSpec((1, n_ctx, LANES), lambda b, h, i: (b, ctx_blk, h))
    out_ctx = pl.pallas_call(
        functools.partial(_attn_kernel, n_lat=0),
        grid=(B, ATT_HEADS, 1),
        in_specs=[cspec, cspec, cspec, cspec] + small,
        out_specs=pl.BlockSpec((1, n_ctx, LANES), lambda b, h, i: (b, 0, h)),
        out_shape=jax.ShapeDtypeStruct((B, n_ctx, ATT_W), BF16),
        compiler_params=_params(3),
        name="diff_attention_ctx",
    )(qp, qr, k, v, g_scaled, lam_arr)
    return out_lat, out_ctx


def _moe_kernel(be_ref, nb_ref, x_ref, gate_ref, w1_ref, w3_ref, w2_ref, o_ref, w1_scr, w3_scr, w2_scr):
    i = pl.program_id(0)
    new_expert = jnp.logical_or(i == 0, be_ref[i] != be_ref[jnp.maximum(i - 1, 0)])

    @pl.when(jnp.logical_and(i < nb_ref[0], new_expert))
    def _():
        w1_scr[...] = w1_ref[0, 0].astype(BF16)
        w3_scr[...] = w3_ref[0, 0].astype(BF16)
        w2_scr[...] = w2_ref[0, 0].astype(BF16)

    @pl.when(i < nb_ref[0])
    def _():
        x = x_ref[...]
        a = jnp.dot(x, w1_scr[...], preferred_element_type=F32)
        b = jnp.dot(x, w3_scr[...], preferred_element_type=F32)
        h = (a * jax.nn.sigmoid(a)) * b
        y = jnp.dot(h.astype(BF16), w2_scr[...], preferred_element_type=F32)
        rows = lax.broadcasted_iota(jnp.int32, (MOE_BLOCK, 1), 0)
        lane = lax.broadcasted_iota(jnp.int32, (1, LANES), 1)
        g_rows = gate_ref[0, 0:1, :]
        for r in range(1, MOE_BLOCK // LANES):
            g_rows = jnp.where(rows >= r * LANES, gate_ref[0, r:r + 1, :], g_rows)
        g_col = jnp.sum(jnp.where(lane == jnp.bitwise_and(rows, LANES - 1), g_rows, 0.0), axis=1, keepdims=True)
        o_ref[...] = (y * g_col).astype(o_ref.dtype)

    @pl.when(i >= nb_ref[0])
    def _():
        o_ref[...] = jnp.zeros_like(o_ref)


def pl_moe_ffn(xb, slot_gate, block_e, n_used, w1, w3, w2, layer):
    n_pad, D = xb.shape
    n_blocks = n_pad // MOE_BLOCK
    F = w1.shape[-1]
    grid_spec = pltpu.PrefetchScalarGridSpec(
        num_scalar_prefetch=2,
        grid=(n_blocks,),
        in_specs=[pl.BlockSpec((MOE_BLOCK, D), lambda i, be, nb: (i, 0)),
                  pl.BlockSpec((1, MOE_BLOCK // LANES, LANES), lambda i, be, nb: (i, 0, 0)),
                  pl.BlockSpec((1, 1, D, F), lambda i, be, nb: (layer, be[i], 0, 0)),
                  pl.BlockSpec((1, 1, D, F), lambda i, be, nb: (layer, be[i], 0, 0)),
                  pl.BlockSpec((1, 1, F, D), lambda i, be, nb: (layer, be[i], 0, 0))],
        out_specs=pl.BlockSpec((MOE_BLOCK, D), lambda i, be, nb: (i, 0)),
        scratch_shapes=[pltpu.VMEM((D, F), BF16), pltpu.VMEM((D, F), BF16), pltpu.VMEM((F, D), BF16)],
    )
    return pl.pallas_call(
        _moe_kernel,
        grid_spec=grid_spec,
        out_shape=jax.ShapeDtypeStruct((n_pad, D), BF16),
        compiler_params=_params(1),
        name="moe_ffn",
    )(block_e, n_used, xb, slot_gate.reshape(n_blocks, MOE_BLOCK // LANES, LANES), w1, w3, w2)


def _combine_kernel(x_ref, y0_ref, y1_ref, mod_ref, g_ref, o_ref, *, final):
    d = x_ref.shape[2]
    xn = x_ref[0] + mod_ref[0, :, 5 * d:6 * d] * (y0_ref[0].astype(F32) + y1_ref[0].astype(F32))
    o_ref[0] = _rms(xn) * g_ref[...] if final else xn


def pl_moe_combine(xs, y0, y1, modv, final_g, n_lat_tiles, final):
    B, Lt, D = xs.shape
    tm = TOKEN_TILE
    n_tiles = n_lat_tiles if final else Lt // tm
    tok = pl.BlockSpec((1, tm, D), lambda b, i: (b, i, 0))
    return pl.pallas_call(
        functools.partial(_combine_kernel, final=final),
        grid=(B, n_tiles),
        in_specs=[tok, tok, tok,
                  pl.BlockSpec((1, 1, N_MOD * D), lambda b, i: (2 * b + jnp.where(i >= n_lat_tiles, 1, 0), 0, 0)),
                  pl.BlockSpec((1, D), lambda b, i: (0, 0))],
        out_specs=tok,
        out_shape=jax.ShapeDtypeStruct((B, n_tiles * tm, D), F32),
        compiler_params=_params(2),
        name="moe_combine",
    )(xs, y0.reshape(B, Lt, D), y1.reshape(B, Lt, D), modv, final_g.reshape(1, D))


S5_STATES = S5_GROUPS * S5_STATE
S5_CHUNK = 64


def _s5_kernel(u_ref, wd_ref, wr_ref, ar_ref, ai_ref, d_ref, y_ref, x_scr, h_scr, hr_scr, hi_scr, *, reverse):
    ns = S5_STATES

    @pl.when(pl.program_id(0) == 0)
    def _():
        hr_scr[...] = jnp.zeros_like(hr_scr)
        hi_scr[...] = jnp.zeros_like(hi_scr)

    u = u_ref[...]
    x_scr[...] = jnp.dot(u.astype(BF16), wd_ref[...], preferred_element_type=F32)
    ar = ar_ref[...]
    ai = ai_ref[...]

    def step(hr, hi, t):
        r = pl.multiple_of(t * SUBLANES, SUBLANES)
        xr = x_scr[pl.ds(r, SUBLANES), :ns]
        xi = x_scr[pl.ds(r, SUBLANES), ns:]
        return ar * hr - ai * hi + xr, ar * hi + ai * hr + xi

    def body(j, carry):
        hr, hi = carry
        t0 = (S5_CHUNK - 1 - 2 * j) if reverse else 2 * j
        t1 = t0 - 1 if reverse else t0 + 1
        hr0, hi0 = step(hr, hi, t0)
        hr1, hi1 = step(hr0, hi0, t1)
        lo = t1 if reverse else t0
        first_r, second_r = (hr1, hr0) if reverse else (hr0, hr1)
        first_i, second_i = (hi1, hi0) if reverse else (hi0, hi1)
        r = pl.multiple_of(lo * SUBLANES, 2 * SUBLANES)
        h_scr[pl.ds(r, 2 * SUBLANES), :ns] = jnp.concatenate([first_r, second_r], axis=0).astype(BF16)
        h_scr[pl.ds(r, 2 * SUBLANES), ns:] = jnp.concatenate([first_i, second_i], axis=0).astype(BF16)
        return hr1, hi1

    hr, hi = lax.fori_loop(0, S5_CHUNK // 2, body, (hr_scr[...], hi_scr[...]))
    hr_scr[...] = hr
    hi_scr[...] = hi
    y = jnp.dot(h_scr[...], wr_ref[...], preferred_element_type=F32)
    y = y + (d_ref[...].astype(F32) if reverse else u * d_ref[...])
    y_ref[...] = y.astype(y_ref.dtype)


def pl_s5_scan(u_tb, w_drive, w_read, a_re, a_im, addend, *, n_lat_steps, reverse):
    rows, ch = u_tb.shape
    rc = S5_CHUNK * SUBLANES
    n_chunks = rows // rc
    n_lat = n_lat_steps // S5_CHUNK
    n_ctx = n_chunks - n_lat
    assert rows % rc == 0 and n_lat_steps % S5_CHUNK == 0
    if reverse:
        def idx(i):
            return (n_chunks - 1 - i, 0)
    else:
        def idx(i):
            return (jnp.where(i < n_ctx, n_lat + i, i - n_ctx), 0)
    const = lambda i: (0, 0)
    ns2 = 2 * S5_STATES
    return pl.pallas_call(
        functools.partial(_s5_kernel, reverse=reverse),
        grid=(n_chunks,),
        in_specs=[pl.BlockSpec((rc, ch), idx),
                  pl.BlockSpec((ch, ns2), const),
                  pl.BlockSpec((ns2, ch), const),
                  pl.BlockSpec((SUBLANES, S5_STATES), const),
                  pl.BlockSpec((SUBLANES, S5_STATES), const),
                  pl.BlockSpec((rc, ch), idx) if reverse else pl.BlockSpec((1, ch), const)],
        out_specs=pl.BlockSpec((rc, ch), idx),
        out_shape=jax.ShapeDtypeStruct((rows, ch), BF16),
        scratch_shapes=[pltpu.VMEM((rc, ns2), F32), pltpu.VMEM((rc, ns2), BF16),
                        pltpu.VMEM((SUBLANES, S5_STATES), F32), pltpu.VMEM((SUBLANES, S5_STATES), F32)],
        compiler_params=_params(1),
        name="s5_scan_rev" if reverse else "s5_scan_fwd",
    )(u_tb, w_drive, w_read,
      jnp.broadcast_to(a_re[None, :], (SUBLANES, S5_STATES)),
      jnp.broadcast_to(a_im[None, :], (SUBLANES, S5_STATES)),
      addend if reverse else addend.reshape(1, ch))


FFT_N2 = 128


def _fft_tables(L):
    N = 2 * L
    N1 = N // FFT_N2
    k1 = np.arange(N1)[:, None]
    n1 = np.arange(N1 // 2)[None, :]
    n2 = np.arange(FFT_N2)[:, None, None]
    ang = -2.0 * np.pi * (k1[None] * (n2 + FFT_N2 * n1[None])) / N
    mr, mi = np.cos(ang), np.sin(ang)
    ma = np.concatenate([np.concatenate([mr, -mi], axis=2), np.concatenate([mi, mr], axis=2)], axis=1)
    gr, gi = np.transpose(mr, (0, 2, 1)), -np.transpose(mi, (0, 2, 1))
    mainv = np.concatenate([np.concatenate([gr, -gi], axis=2), np.concatenate([gi, gr], axis=2)], axis=1)
    kk = np.arange(FFT_N2)
    a2 = -2.0 * np.pi * np.outer(kk, kk) / FFT_N2
    fr, fi = np.cos(a2), np.sin(a2)
    f_fwd = np.block([[fr, -fi], [fi, fr]])
    f_inv = np.block([[fr, fi], [-fi, fr]])
    return (jnp.asarray(ma, BF16), jnp.asarray(mainv, BF16), jnp.asarray(f_fwd, BF16), jnp.asarray(f_inv, BF16))


HYENA_VMEM_LIMIT = 56 * 1024 * 1024
FFT_UNROLL = 8


def _hyena_fft_kernel(z_ref, h_ref, ma_ref, mainv_ref, ff_ref, fi_ref, o_ref, a_scr, b_scr, *, n1_count):
    half = n1_count // 2
    n2c = FFT_N2

    def stage_a(n2, c):
        xr = z_ref[0, pl.ds(n2, half, stride=n2c), :]
        xi = z_ref[1, pl.ds(n2, half, stride=n2c), :]
        x = jnp.concatenate([xr, xi], axis=0).astype(BF16)
        r = jnp.dot(ma_ref[n2], x, preferred_element_type=F32)
        a_scr[pl.ds(pl.multiple_of(n2 * 2 * n1_count, 2 * n1_count), 2 * n1_count), :] = r
        return c

    lax.fori_loop(0, n2c, stage_a, 0, unroll=FFT_UNROLL)

    def stage_c(k1, c):
        ar = a_scr[pl.ds(k1, n2c, stride=2 * n1_count), :]
        ai = a_scr[pl.ds(n1_count + k1, n2c, stride=2 * n1_count), :]
        x = jnp.concatenate([ar, ai], axis=0).astype(BF16)
        y = jnp.dot(ff_ref[...], x, preferred_element_type=F32)
        yr, yi = y[:n2c], y[n2c:]
        hr = h_ref[0, k1].astype(F32)
        hi = h_ref[1, k1].astype(F32)
        x2 = jnp.concatenate([yr * hr - yi * hi, yr * hi + yi * hr], axis=0).astype(BF16)
        b = jnp.dot(fi_ref[...], x2, preferred_element_type=F32)
        b_scr[pl.ds(pl.multiple_of(k1 * 2 * n2c, 2 * n2c), 2 * n2c), :] = b
        return c

    lax.fori_loop(0, n1_count, stage_c, 0, unroll=FFT_UNROLL)

    def stage_a_inv(n2, c):
        br = b_scr[pl.ds(n2, n1_count, stride=2 * n2c), :]
        bi = b_scr[pl.ds(n2c + n2, n1_count, stride=2 * n2c), :]
        x = jnp.concatenate([br, bi], axis=0).astype(BF16)
        r = jnp.dot(mainv_ref[n2], x, preferred_element_type=F32)
        o_ref[0, pl.ds(n2, half, stride=n2c), :] = r[:half]
        o_ref[1, pl.ds(n2, half, stride=n2c), :] = r[half:]
        return c

    lax.fori_loop(0, n2c, stage_a_inv, 0, unroll=FFT_UNROLL)


def hyena_spectrum(filt):
    N, C = filt.shape
    N1 = N // FFT_N2
    h2 = (jnp.fft.fft(filt, axis=0) / N).reshape(FFT_N2, N1, C).transpose(1, 0, 2)
    return jnp.stack([h2.real, h2.imag]).astype(BF16)


def pl_hyena_conv(z, h, dft_tables, L):
    B, _, C = z.shape
    N1 = 2 * L // FFT_N2
    ma, mainv, f_fwd, f_inv = dft_tables
    cw = LANES
    full = lambda arr: pl.BlockSpec(arr.shape, lambda ct, bp: (0,) * arr.ndim)
    return pl.pallas_call(
        functools.partial(_hyena_fft_kernel, n1_count=N1),
        grid=(C // cw, B // 2),
        in_specs=[pl.BlockSpec((2, L, cw), lambda ct, bp: (bp, 0, ct)),
                  pl.BlockSpec((2, N1, FFT_N2, cw), lambda ct, bp: (0, 0, 0, ct)),
                  full(ma), full(mainv), full(f_fwd), full(f_inv)],
        out_specs=pl.BlockSpec((2, L, cw), lambda ct, bp: (bp, 0, ct)),
        out_shape=jax.ShapeDtypeStruct((B, L, C), F32),
        scratch_shapes=[pltpu.VMEM((FFT_N2 * 2 * N1, cw), F32), pltpu.VMEM((N1 * 2 * FFT_N2, cw), F32)],
        compiler_params=_params(2, HYENA_VMEM_LIMIT),
        name="hyena_fft_conv",
    )(z, h, ma, mainv, f_fwd, f_inv)


def hyena_filter(L, w1, b1, w2, b2, w3, freq):
    t = jnp.arange(L, dtype=F32) / L
    ang = (2.0 * math.pi) * t[:, None] * jnp.arange(1, HY_BANDS + 1, dtype=F32)
    feat = jnp.concatenate([t[:, None], jnp.cos(ang), jnp.sin(ang)], axis=-1)
    hp = lax.Precision.HIGHEST
    h = jnp.sin(freq * (jnp.dot(feat, w1, precision=hp) + b1))
    h = jnp.sin(freq * (jnp.dot(h, w2, precision=hp) + b2))
    h = jnp.dot(h, w3, precision=hp).reshape(L, 2, HY_CH)
    window = jnp.exp(-t[:, None] * jnp.linspace(HY_DECAY_MIN, HY_DECAY_MAX, HY_CH, dtype=F32))
    h = h * window[:, None, :]
    filt = jnp.concatenate([h[:, 0], jnp.zeros((1, HY_CH), F32), h[:0:-1, 1]], axis=0)
    return filt / (jnp.sum(jnp.abs(filt), axis=0, keepdims=True) + EPS)


def hyena_conv(z, h_lat, ff_ctx, dft_tables, L):
    Lc = z.shape[1] - L
    y_lat = pl_hyena_conv(z, h_lat, dft_tables, L)
    zf = jnp.fft.rfft(z[:, L:], n=2 * Lc, axis=1)
    y_ctx = jnp.fft.irfft(zf * ff_ctx[None], n=2 * Lc, axis=1)[:, :Lc]
    return y_lat, y_ctx


def _block_diag(blocks):
    G, r, c = blocks.shape
    eye = jnp.eye(G, dtype=blocks.dtype)
    return (eye[:, None, :, None] * blocks[:, :, None, :]).reshape(G * r, G * c)


def s5_tables(a_re, a_im, log_dt, b_re, b_im, c_re, c_im):
    A = lax.complex(a_re, a_im)
    dtA = jnp.exp(log_dt)[:, None] * A
    a_bar = jnp.exp(dtA)
    b_bar = ((a_bar - 1.0) / A)[:, :, None] * lax.complex(b_re, b_im)
    bt_re = jnp.transpose(b_bar.real, (0, 2, 1))
    bt_im = jnp.transpose(b_bar.imag, (0, 2, 1))
    w_drive = jnp.concatenate([_block_diag(bt_re), _block_diag(bt_im)], axis=1)
    ct_re = jnp.transpose(c_re, (0, 2, 1))
    ct_im = jnp.transpose(c_im, (0, 2, 1))
    w_read = jnp.concatenate([_block_diag(ct_re), -_block_diag(ct_im)], axis=0)
    return w_drive.astype(BF16), w_read.astype(BF16), a_bar.real.reshape(-1), a_bar.imag.reshape(-1)


def s5_scan(u_tb, tables, d_skip, n_lat_steps):
    y = d_skip
    for direction in range(2):
        w_drive, w_read, a_re, a_im = (t[direction] for t in tables)
        y = pl_s5_scan(u_tb, w_drive, w_read, a_re, a_im, y, n_lat_steps=n_lat_steps, reverse=direction == 1)
    return y


def rope_tables(L, Lc):
    rows = L // GRID_W
    row = jnp.repeat(jnp.arange(rows, dtype=F32), GRID_W)
    col = jnp.tile(jnp.arange(GRID_W, dtype=F32), rows)
    inv = ROPE_BASE ** (-jnp.arange(ROPE_PAIRS_AXIS, dtype=F32) / ROPE_PAIRS_AXIS)
    ang = jnp.concatenate([row[:, None] * inv, col[:, None] * inv], axis=-1)
    cos, sin = jnp.cos(ang), jnp.sin(ang)
    n_maps = ATT_W // ATT_HEAD_DIM
    cosf = jnp.tile(jnp.concatenate([cos, cos], axis=-1), (1, n_maps))
    sinf = jnp.tile(jnp.concatenate([-sin, sin], axis=-1), (1, n_maps))
    return (jnp.concatenate([cosf, jnp.ones((Lc, ATT_W), F32)], axis=0),
            jnp.concatenate([sinf, jnp.zeros((Lc, ATT_W), F32)], axis=0))


def moe_dispatch(logits, b_g, b_e):
    T = logits.shape[0]
    g_logits = logits[:, :MOE_GROUPS] + b_g
    g_idx = jnp.argmax(g_logits, axis=-1)
    p_group = jnp.take_along_axis(jax.nn.softmax(g_logits, axis=-1), g_idx[:, None], axis=1)
    e_logits = (logits[:, MOE_GROUPS:MOE_GROUPS + N_EXPERTS] + b_e).reshape(T, MOE_GROUPS, MOE_EPG)
    e_logits = jnp.take_along_axis(e_logits, g_idx[:, None, None], axis=1)[:, 0]
    top_p, top_i = lax.top_k(jax.nn.softmax(e_logits, axis=-1), MOE_TOP_K)
    gate = p_group * top_p / jnp.sum(top_p, axis=-1, keepdims=True)
    expert = (g_idx[:, None] * MOE_EPG + top_i).reshape(-1).astype(jnp.int32)
    tok = jnp.repeat(jnp.arange(T, dtype=jnp.int32), MOE_TOP_K)
    n_assign = T * MOE_TOP_K
    n_blocks = -(-n_assign // MOE_BLOCK) + N_EXPERTS
    n_pad = n_blocks * MOE_BLOCK
    order = jnp.argsort(expert).astype(jnp.int32)
    sorted_rank = jnp.argsort(order).astype(jnp.int32)
    e_ids = jnp.arange(N_EXPERTS + 1, dtype=jnp.int32)
    bounds = jnp.sum((expert[:, None] < e_ids[None, :]).astype(jnp.int32), axis=0)
    start = bounds[:-1]
    counts = bounds[1:] - bounds[:-1]
    padded = (counts + MOE_BLOCK - 1) // MOE_BLOCK * MOE_BLOCK
    pad_end = jnp.cumsum(padded)
    pad_start = pad_end - padded
    slot_of_assign = (pad_start[expert] + sorted_rank - start[expert]).astype(jnp.int32)
    block_first = jnp.arange(n_blocks, dtype=jnp.int32) * MOE_BLOCK
    block_e = jnp.minimum(jnp.sum((pad_end[None, :] <= block_first[:, None]).astype(jnp.int32), axis=1),
                          N_EXPERTS - 1).astype(jnp.int32)
    slot_e = jnp.repeat(block_e, MOE_BLOCK)
    slot_r = jnp.arange(n_pad, dtype=jnp.int32) - pad_start[slot_e]
    slot_valid = (slot_r < counts[slot_e]) & (jnp.arange(n_pad) < pad_end[-1])
    slot_assign = order[jnp.clip(start[slot_e] + slot_r, 0, n_assign - 1)]
    slot_tok = jnp.where(slot_valid, tok[slot_assign], jnp.arange(n_pad, dtype=jnp.int32) % T)
    slot_gate = jnp.where(slot_valid, gate.reshape(-1)[slot_assign], 0.0)
    n_used = (pad_end[-1:] // MOE_BLOCK).astype(jnp.int32)
    return slot_tok, slot_gate, slot_of_assign, block_e, n_used


def hier_moe(h2, logits, b_g, b_e, w1, w3, w2, layer):
    T, D = h2.shape
    slot_tok, slot_gate, slot_of_assign, block_e, n_used = moe_dispatch(logits, b_g, b_e)
    yb = pl_moe_ffn(h2[slot_tok], slot_gate, block_e, n_used, w1, w3, w2, layer)
    slots = slot_of_assign.reshape(T, MOE_TOP_K)
    return yb[slots[:, 0]], yb[slots[:, 1]]


def kernel(x, c, ctx, c_ctx, w_mod, b_mod, norm1_g, norm2_g, final_g, w_in, w_out, hy_conv_w, hy_conv_b, hy_ffn_w1, hy_ffn_b1, hy_ffn_w2, hy_ffn_b2, hy_ffn_w3, hy_freq, hy_skip, hy_norm_g, s5_a_re, s5_a_im, s5_log_dt, s5_b_re, s5_b_im, s5_c_re, s5_c_im, s5_d, s5_glu_w, s5_norm_g, att_lq1, att_lk1, att_lq2, att_lk2, att_subln_g, moe_wg, moe_bg, moe_we, moe_be, moe_w1, moe_w3, moe_w2):
    B, L, D = x.shape
    Lc = ctx.shape[1]
    Lt = L + Lc
    assert B == SUBLANES and Lc == TOKEN_TILE and L % ATT_TQ == 0
    n_lat_tiles = L // TOKEN_TILE
    cosf, sinf = rope_tables(L, Lc)
    hp = lax.Precision.HIGHEST
    q_scale = ATT_HEAD_DIM ** -0.5 * math.log2(math.e)

    mods = jnp.einsum('bd,ldk->lbk', jax.nn.silu(c), w_mod, precision=hp) + b_mod[:, None, :]
    cmods = jnp.einsum('d,ldk->lk', jax.nn.silu(c_ctx), w_mod, precision=hp) + b_mod
    modvs = jnp.stack([mods, jnp.broadcast_to(cmods[:, None, :], mods.shape)], axis=2)
    modvs = modvs.reshape(DEPTH, 2 * B, 1, N_MOD * D)
    filt_args = (hy_ffn_w1, hy_ffn_b1, hy_ffn_w2, hy_ffn_b2, hy_ffn_w3, hy_freq)
    h_lat = jax.vmap(lambda *p: hyena_spectrum(hyena_filter(L, *p)))(*filt_args)
    ff_ctx = jax.vmap(lambda *p: jnp.fft.rfft(hyena_filter(Lc, *p), n=2 * Lc, axis=0))(*filt_args)
    dft_tables = _fft_tables(L)
    s5_tabs = jax.vmap(jax.vmap(s5_tables))(s5_a_re, s5_a_im, s5_log_dt, s5_b_re, s5_b_im, s5_c_re, s5_c_im)
    lam_inits = [0.8 - 0.6 * math.exp(-0.3 * l) for l in range(DEPTH)]
    lams = (jnp.exp(jnp.sum(att_lq1 * att_lk1, axis=-1)) - jnp.exp(jnp.sum(att_lq2 * att_lk2, axis=-1))
            + jnp.asarray(lam_inits, F32))
    w_router = jnp.zeros((DEPTH, D, LANES), F32).at[:, :, :MOE_GROUPS].set(moe_wg)
    w_router = w_router.at[:, :, MOE_GROUPS:MOE_GROUPS + N_EXPERTS].set(moe_we)
    rwh = w_router.astype(BF16)
    rwl = (w_router - rwh.astype(F32)).astype(BF16)
    w_in_b, w_out_b, glu_w_b = w_in.astype(BF16), w_out.astype(BF16), s5_glu_w.astype(BF16)

    xs = jnp.concatenate([x, ctx], axis=1)
    for l in range(DEPTH):
        modv = modvs[l]
        u0, z, s5_u, q_p, q_r, k_r, v = pl_in_proj(xs, modv, norm1_g[l], w_in_b[l], hy_conv_w[l], hy_conv_b[l],
                                                   cosf, sinf, n_lat_tiles, q_scale)

        y_lat, y_ctx = hyena_conv(z, h_lat[l], ff_ctx[l], dft_tables, L)

        s5_y = s5_scan(s5_u.reshape(Lt * B, S5_CH), tuple(t[l] for t in s5_tabs), s5_d[l], L)

        g_scaled = (att_subln_g[l] * (1.0 - lam_inits[l])).reshape(1, ATT_V_DIM)
        att_lat, att_ctx = pl_diff_attention(q_p, q_r, k_r, v, g_scaled, lams[l], L)

        xs, h2, logits = pl_out_proj(xs, u0, z, y_lat, y_ctx, s5_y.reshape(Lt, B * S5_CH), att_lat, att_ctx, modv,
                                     hy_skip[l], hy_norm_g[l], glu_w_b[l], s5_norm_g[l], w_out_b[l], norm2_g[l],
                                     rwh[l], rwl[l], n_lat_tiles)

        y0, y1 = hier_moe(h2.reshape(B * Lt, D), logits.reshape(B * Lt, LANES), moe_bg[l], moe_be[l],
                          moe_w1, moe_w3, moe_w2, l)
        xs = pl_moe_combine(xs, y0, y1, modv, final_g, n_lat_tiles, final=l == DEPTH - 1)
    return xs
```

```python
import functools
import math

import jax
import jax.numpy as jnp
import numpy as np
from jax import lax
from jax.experimental import pallas as pl
from jax.experimental.pallas import tpu as pltpu

D_MODEL = 1024
DEPTH = 4
GRID_W = 64
N_MOD = 6
EPS = 1e-6
HY_CH = D_MODEL // 4
S5_CH = D_MODEL // 4
ATT_W = D_MODEL // 2
HY_BANDS = 16
HY_DECAY_MIN = -math.log(1e-2) / 1.5
HY_DECAY_MAX = -math.log(1e-2) / 0.3
S5_GROUP = 16
S5_GROUPS = S5_CH // S5_GROUP
S5_STATE = 64
ATT_HEAD_DIM = 64
ATT_HEADS = ATT_W // (2 * ATT_HEAD_DIM)
ATT_V_DIM = 2 * ATT_HEAD_DIM
ROPE_HALF = ATT_HEAD_DIM // 2
ROPE_PAIRS_AXIS = ROPE_HALF // 2
ROPE_BASE = 10000.0
MOE_GROUPS = 4
MOE_EPG = 8
N_EXPERTS = MOE_GROUPS * MOE_EPG
MOE_TOP_K = 2
MOE_BLOCK = 512
IN_COLS = 3 * HY_CH + S5_CH + 3 * ATT_W
COL_S5 = 3 * HY_CH
COL_Q = COL_S5 + S5_CH
COL_K = COL_Q + ATT_W
COL_V = COL_K + ATT_W

LANES = 128
SUBLANES = 8
VMEM_LIMIT = 48 * 1024 * 1024
TOKEN_TILE = 256

F32 = jnp.float32
BF16 = jnp.bfloat16


def _params(n_axes, vmem=VMEM_LIMIT):
    return pltpu.CompilerParams(dimension_semantics=("arbitrary",) * n_axes, vmem_limit_bytes=vmem)


def _rms(x):
    return x * lax.rsqrt(jnp.mean(x * x, axis=-1, keepdims=True) + EPS)


def _in_kernel(x_ref, xp_ref, xn_ref, mod_ref, g_ref, w_ref, cw_ref, cb_ref, cos_ref, sin_ref,
               u0_ref, z_ref, s5_ref, qp_ref, qr_ref, k_ref, v_ref, *, n_lat_tiles, q_scale):
    i = pl.program_id(1)
    tm = x_ref.shape[1]
    d = x_ref.shape[2]
    g = g_ref[...]
    shift = mod_ref[0, :, 0:d]
    scale = mod_ref[0, :, d:2 * d]

    def norm_mod(xt):
        return (_rms(xt) * g) * (1.0 + scale) + shift

    h = jnp.concatenate([norm_mod(x_ref[0]), norm_mod(xp_ref[0]), norm_mod(xn_ref[0])], axis=0).astype(BF16)
    p = jnp.dot(h, w_ref[...], preferred_element_type=F32)

    hy = p[:tm, :COL_S5]
    is_ctx = i == n_lat_tiles
    has_prev = jnp.logical_and(i != 0, jnp.logical_not(is_ctx))
    has_next = jnp.logical_and(i != n_lat_tiles - 1, jnp.logical_not(is_ctx))
    prev_row = jnp.where(has_prev, p[tm + SUBLANES - 1:tm + SUBLANES, :COL_S5], 0.0)
    next_row = jnp.where(has_next, p[tm + SUBLANES:tm + SUBLANES + 1, :COL_S5], 0.0)
    rows = lax.broadcasted_iota(jnp.int32, (tm, 1), 0)
    up = jnp.where(rows == 0, prev_row, pltpu.roll(hy, 1, axis=0))
    dn = jnp.where(rows == tm - 1, next_row, pltpu.roll(hy, tm - 1, axis=0))
    u = up * cw_ref[0:1, :] + hy * cw_ref[1:2, :] + dn * cw_ref[2:3, :] + cb_ref[...]
    u0_ref[0] = u[:, :HY_CH].astype(u0_ref.dtype)
    z_ref[0] = u[:, HY_CH:2 * HY_CH] * u[:, 2 * HY_CH:]

    s5_ref[...] = p[:tm, COL_S5:COL_Q]

    lane = lax.broadcasted_iota(jnp.int32, (1, ATT_W), 1)
    first_half = jnp.bitwise_and(lane, ATT_HEAD_DIM - 1) < ROPE_HALF
    cos = cos_ref[...]
    sin = sin_ref[...]

    def rope(t):
        partner = jnp.where(first_half, pltpu.roll(t, ATT_W - ROPE_HALF, axis=1), pltpu.roll(t, ROPE_HALF, axis=1))
        return t * cos + partner * sin

    q = p[:tm, COL_Q:COL_K] * q_scale
    qp_ref[0] = q.astype(BF16)
    qr_ref[0] = rope(q).astype(BF16)
    k_ref[0] = rope(p[:tm, COL_K:COL_V]).astype(BF16)
    v_ref[0] = p[:tm, COL_V:].astype(BF16)


def pl_in_proj(xs, modv, norm_g, w_in, conv_w, conv_b, cosf, sinf, n_lat_tiles, q_scale):
    B, Lt, D = xs.shape
    tm = TOKEN_TILE
    n_tiles = Lt // tm
    halo_per_tile = tm // SUBLANES
    n_halo_blocks = Lt // SUBLANES
    tok = lambda w: pl.BlockSpec((1, tm, w), lambda b, i: (b, i, 0))
    const = lambda shape: pl.BlockSpec(shape, lambda b, i: (0,) * len(shape))
    out_shapes = (jax.ShapeDtypeStruct((B, Lt, HY_CH), BF16), jax.ShapeDtypeStruct((B, Lt, HY_CH), F32),
                  jax.ShapeDtypeStruct((Lt, B * S5_CH), F32),
                  jax.ShapeDtypeStruct((B, Lt, ATT_W), BF16), jax.ShapeDtypeStruct((B, Lt, ATT_W), BF16),
                  jax.ShapeDtypeStruct((B, Lt, ATT_W), BF16), jax.ShapeDtypeStruct((B, Lt, ATT_W), BF16))
    return pl.pallas_call(
        functools.partial(_in_kernel, n_lat_tiles=n_lat_tiles, q_scale=q_scale),
        grid=(B, n_tiles),
        in_specs=[tok(D),
                  pl.BlockSpec((1, SUBLANES, D), lambda b, i: (b, jnp.maximum(i * halo_per_tile - 1, 0), 0)),
                  pl.BlockSpec((1, SUBLANES, D),
                               lambda b, i: (b, jnp.minimum((i + 1) * halo_per_tile, n_halo_blocks - 1), 0)),
                  pl.BlockSpec((1, 1, N_MOD * D), lambda b, i: (2 * b + jnp.where(i >= n_lat_tiles, 1, 0), 0, 0)),
                  const((1, D)), const((D, IN_COLS)), const((3, COL_S5)), const((1, COL_S5)),
                  pl.BlockSpec((tm, ATT_W), lambda b, i: (i, 0)), pl.BlockSpec((tm, ATT_W), lambda b, i: (i, 0))],
        out_specs=(tok(HY_CH), tok(HY_CH), pl.BlockSpec((tm, S5_CH), lambda b, i: (i, b)),
                   tok(ATT_W), tok(ATT_W), tok(ATT_W), tok(ATT_W)),
        out_shape=out_shapes,
        compiler_params=_params(2),
        name="in_proj",
    )(xs, xs, xs, modv, norm_g.reshape(1, D), w_in, conv_w, conv_b.reshape(1, COL_S5), cosf, sinf)


def _out_kernel(x_ref, u0_ref, z_ref, yl_ref, yc_ref, s5_ref, al_ref, ac_ref, mod_ref, skip_ref, hg_ref,
                gw_ref, sg_ref, wo_ref, n2g_ref, rwh_ref, rwl_ref, xo_ref, h2_ref, lg_ref, *, n_lat_tiles):
    d = x_ref.shape[2]
    is_ctx = pl.program_id(1) == n_lat_tiles
    y = jnp.where(is_ctx, yc_ref[0], yl_ref[0])
    att = jnp.where(is_ctx, ac_ref[0], al_ref[0])
    hy = _rms(u0_ref[0].astype(F32) * (y + skip_ref[...] * z_ref[0])) * hg_ref[...]
    gl = jax.nn.gelu(s5_ref[...].astype(F32))
    gate = jax.nn.sigmoid(jnp.dot(gl.astype(BF16), gw_ref[...], preferred_element_type=F32))
    s5 = _rms(gl * gate) * sg_ref[...]
    mix = jnp.concatenate([hy.astype(BF16), s5.astype(BF16), att], axis=1)
    proj = jnp.dot(mix, wo_ref[...], preferred_element_type=F32)
    xn = x_ref[0] + mod_ref[0, :, 2 * d:3 * d] * proj
    xo_ref[0] = xn
    h2 = (_rms(xn) * n2g_ref[...]) * (1.0 + mod_ref[0, :, 4 * d:5 * d]) + mod_ref[0, :, 3 * d:4 * d]
    hh = h2.astype(BF16)
    hl = (h2 - hh.astype(F32)).astype(BF16)
    h2_ref[0] = hh
    lg = jnp.dot(hh, rwh_ref[...], preferred_element_type=F32)
    lg += jnp.dot(hl, rwh_ref[...], preferred_element_type=F32)
    lg += jnp.dot(hh, rwl_ref[...], preferred_element_type=F32)
    lg_ref[0] = lg


def pl_out_proj(xs, u0, z, y_lat, y_ctx, s5_y, att_lat, att_ctx, modv, hy_skip, hy_norm_g, glu_w, s5_norm_g,
                w_out, norm2_g, rwh, rwl, n_lat_tiles):
    B, Lt, D = xs.shape
    tm = TOKEN_TILE
    tok = lambda w: pl.BlockSpec((1, tm, w), lambda b, i: (b, i, 0))
    lat_tok = lambda w: pl.BlockSpec((1, tm, w), lambda b, i: (b, jnp.minimum(i, n_lat_tiles - 1), 0))
    ctx_tok = lambda w: pl.BlockSpec((1, tm, w), lambda b, i: (b, 0, 0))
    tb = pl.BlockSpec((tm, S5_CH), lambda b, i: (i, b))
    const = lambda shape: pl.BlockSpec(shape, lambda b, i: (0,) * len(shape))
    return pl.pallas_call(
        functools.partial(_out_kernel, n_lat_tiles=n_lat_tiles),
        grid=(B, Lt // tm),
        in_specs=[tok(D), tok(HY_CH), tok(HY_CH), lat_tok(HY_CH), ctx_tok(HY_CH), tb, lat_tok(ATT_W),
                  ctx_tok(ATT_W),
                  pl.BlockSpec((1, 1, N_MOD * D), lambda b, i: (2 * b + jnp.where(i >= n_lat_tiles, 1, 0), 0, 0)),
                  const((1, HY_CH)), const((1, HY_CH)), const((S5_CH, S5_CH)), const((1, S5_CH)),
                  const((D, D)), const((1, D)), const((D, LANES)), const((D, LANES))],
        out_specs=(tok(D), tok(D), tok(LANES)),
        out_shape=(jax.ShapeDtypeStruct((B, Lt, D), F32), jax.ShapeDtypeStruct((B, Lt, D), BF16),
                   jax.ShapeDtypeStruct((B, Lt, LANES), F32)),
        compiler_params=_params(2),
        name="out_proj",
    )(xs, u0, z, y_lat, y_ctx, s5_y, att_lat, att_ctx, modv, hy_skip.reshape(1, HY_CH),
      hy_norm_g.reshape(1, HY_CH), glu_w, s5_norm_g.reshape(1, S5_CH), w_out, norm2_g.reshape(1, D), rwh, rwl)


def _dot_nt(a, b):
    return lax.dot_general(a, b, (((1,), (1,)), ((), ())), preferred_element_type=F32)


ATT_TQ = 512
ATT_SUB = 256


def _attn_kernel(qp_ref, qr_ref, k_ref, v_ref, g_ref, lam_ref, o_ref, *, n_lat):
    tq = qp_ref.shape[1]
    first_map = lax.broadcasted_iota(jnp.int32, (1, LANES), 1) < ATT_HEAD_DIM
    zero = jnp.zeros((), BF16)
    sub = min(ATT_SUB, tq)
    for r0 in range(0, tq, sub):
        qp = qp_ref[0, r0:r0 + sub, :]
        qr = qr_ref[0, r0:r0 + sub, :]
        probs = []
        for m in range(2):
            in_map = first_map if m == 0 else jnp.logical_not(first_map)
            s_c = _dot_nt(jnp.where(in_map, qp, zero), k_ref[0, n_lat:, :])
            mx = jnp.max(s_c, axis=-1, keepdims=True)
            if n_lat:
                s_l = _dot_nt(jnp.where(in_map, qr, zero), k_ref[0, :n_lat, :])
                mx = jnp.maximum(mx, jnp.max(s_l, axis=-1, keepdims=True))
                p_l = jnp.exp2(s_l - mx)
            p_c = jnp.exp2(s_c - mx)
            den = jnp.sum(p_c, axis=-1, keepdims=True)
            if n_lat:
                den = den + jnp.sum(p_l, axis=-1, keepdims=True)
            probs.append((p_c, p_l if n_lat else None, 1.0 / den))
        w0 = probs[0][2]
        w1 = lam_ref[0:1, 0:1] * probs[1][2]
        a_c = (probs[0][0] * w0 - probs[1][0] * w1).astype(BF16)
        o = jnp.dot(a_c, v_ref[0, n_lat:, :], preferred_element_type=F32)
        if n_lat:
            a_l = (probs[0][1] * w0 - probs[1][1] * w1).astype(BF16)
            o = o + jnp.dot(a_l, v_ref[0, :n_lat, :], preferred_element_type=F32)
        o_ref[0, r0:r0 + sub, :] = (_rms(o) * g_ref[...]).astype(o_ref.dtype)


def pl_diff_attention(qp, qr, k, v, g_scaled, lam, n_lat):
    B, Lt, _ = qp.shape
    n_ctx = Lt - n_lat
    lam_arr = jnp.full((SUBLANES, LANES), lam, F32)
    small = [pl.BlockSpec((1, LANES), lambda b, h, i: (0, 0)), pl.BlockSpec((SUBLANES, LANES), lambda b, h, i: (0, 0))]
    tq = ATT_TQ
    qspec = pl.BlockSpec((1, tq, LANES), lambda b, h, i: (b, i, h))
    kspec = pl.BlockSpec((1, Lt, LANES), lambda b, h, i: (b, 0, h))
    out_lat = pl.pallas_call(
        functools.partial(_attn_kernel, n_lat=n_lat),
        grid=(B, ATT_HEADS, n_lat // tq),
        in_specs=[qspec, qspec, kspec, kspec] + small,
        out_specs=qspec,
        out_shape=jax.ShapeDtypeStruct((B, n_lat, ATT_W), BF16),
        compiler_params=_params(3),
        name="diff_attention",
    )(qp, qr, k, v, g_scaled, lam_arr)
    ctx_blk = n_lat // n_ctx
    cspec = pl.BlockSpec((1, n_ctx, LANES), lambda b, h, i: (b, ctx_blk, h))
    out_ctx = pl.pallas_call(
        functools.partial(_attn_kernel, n_lat=0),
        grid=(B, ATT_HEADS, 1),
        in_specs=[cspec, cspec, cspec, cspec] + small,
        out_specs=pl.BlockSpec((1, n_ctx, LANES), lambda b, h, i: (b, 0, h)),
        out_shape=jax.ShapeDtypeStruct((B, n_ctx, ATT_W), BF16),
        compiler_params=_params(3),
        name="diff_attention_ctx",
    )(qp, qr, k, v, g_scaled, lam_arr)
    return out_lat, out_ctx


def _moe_kernel(be_ref, nb_ref, x_ref, gate_ref, w1_ref, w3_ref, w2_ref, o_ref, w1_scr, w3_scr, w2_scr):
    i = pl.program_id(0)
    new_expert = jnp.logical_or(i == 0, be_ref[i] != be_ref[jnp.maximum(i - 1, 0)])

    @pl.when(jnp.logical_and(i < nb_ref[0], new_expert))
    def _():
        w1_scr[...] = w1_ref[0, 0].astype(BF16)
        w3_scr[...] = w3_ref[0, 0].astype(BF16)
        w2_scr[...] = w2_ref[0, 0].astype(BF16)

    @pl.when(i < nb_ref[0])
    def _():
        x = x_ref[...]
        a = jnp.dot(x, w1_scr[...], preferred_element_type=F32)
        b = jnp.dot(x, w3_scr[...], preferred_element_type=F32)
        h = (a * jax.nn.sigmoid(a)) * b
        y = jnp.dot(h.astype(BF16), w2_scr[...], preferred_element_type=F32)
        rows = lax.broadcasted_iota(jnp.int32, (MOE_BLOCK, 1), 0)
        lane = lax.broadcasted_iota(jnp.int32, (1, LANES), 1)
        g_rows = gate_ref[0, 0:1, :]
        for r in range(1, MOE_BLOCK // LANES):
            g_rows = jnp.where(rows >= r * LANES, gate_ref[0, r:r + 1, :], g_rows)
        g_col = jnp.sum(jnp.where(lane == jnp.bitwise_and(rows, LANES - 1), g_rows, 0.0), axis=1, keepdims=True)
        o_ref[...] = (y * g_col).astype(o_ref.dtype)

    @pl.when(i >= nb_ref[0])
    def _():
        o_ref[...] = jnp.zeros_like(o_ref)


def pl_moe_ffn(xb, slot_gate, block_e, n_used, w1, w3, w2, layer):
    n_pad, D = xb.shape
    n_blocks = n_pad // MOE_BLOCK
    F = w1.shape[-1]
    grid_spec = pltpu.PrefetchScalarGridSpec(
        num_scalar_prefetch=2,
        grid=(n_blocks,),
        in_specs=[pl.BlockSpec((MOE_BLOCK, D), lambda i, be, nb: (i, 0)),
                  pl.BlockSpec((1, MOE_BLOCK // LANES, LANES), lambda i, be, nb: (i, 0, 0)),
                  pl.BlockSpec((1, 1, D, F), lambda i, be, nb: (layer, be[i], 0, 0)),
                  pl.BlockSpec((1, 1, D, F), lambda i, be, nb: (layer, be[i], 0, 0)),
                  pl.BlockSpec((1, 1, F, D), lambda i, be, nb: (layer, be[i], 0, 0))],
        out_specs=pl.BlockSpec((MOE_BLOCK, D), lambda i, be, nb: (i, 0)),
        scratch_shapes=[pltpu.VMEM((D, F), BF16), pltpu.VMEM((D, F), BF16), pltpu.VMEM((F, D), BF16)],
    )
    return pl.pallas_call(
        _moe_kernel,
        grid_spec=grid_spec,
        out_shape=jax.ShapeDtypeStruct((n_pad, D), BF16),
        compiler_params=_params(1),
        name="moe_ffn",
    )(block_e, n_used, xb, slot_gate.reshape(n_blocks, MOE_BLOCK // LANES, LANES), w1, w3, w2)


def _combine_kernel(x_ref, y0_ref, y1_ref, mod_ref, g_ref, o_ref, *, final):
    d = x_ref.shape[2]
    xn = x_ref[0] + mod_ref[0, :, 5 * d:6 * d] * (y0_ref[0].astype(F32) + y1_ref[0].astype(F32))
    o_ref[0] = _rms(xn) * g_ref[...] if final else xn


def pl_moe_combine(xs, y0, y1, modv, final_g, n_lat_tiles, final):
    B, Lt, D = xs.shape
    tm = TOKEN_TILE
    n_tiles = n_lat_tiles if final else Lt // tm
    tok = pl.BlockSpec((1, tm, D), lambda b, i: (b, i, 0))
    return pl.pallas_call(
        functools.partial(_combine_kernel, final=final),
        grid=(B, n_tiles),
        in_specs=[tok, tok, tok,
                  pl.BlockSpec((1, 1, N_MOD * D), lambda b, i: (2 * b + jnp.where(i >= n_lat_tiles, 1, 0), 0, 0)),
                  pl.BlockSpec((1, D), lambda b, i: (0, 0))],
        out_specs=tok,
        out_shape=jax.ShapeDtypeStruct((B, n_tiles * tm, D), F32),
        compiler_params=_params(2),
        name="moe_combine",
    )(xs, y0.reshape(B, Lt, D), y1.reshape(B, Lt, D), modv, final_g.reshape(1, D))


S5_STATES = S5_GROUPS * S5_STATE
S5_CHUNK = 64


def _s5_kernel(u_ref, wd_ref, wr_ref, ar_ref, ai_ref, d_ref, y_ref, x_scr, h_scr, hr_scr, hi_scr, *, reverse):
    ns = S5_STATES

    @pl.when(pl.program_id(0) == 0)
    def _():
        hr_scr[...] = jnp.zeros_like(hr_scr)
        hi_scr[...] = jnp.zeros_like(hi_scr)

    u = u_ref[...]
    x_scr[...] = jnp.dot(u.astype(BF16), wd_ref[...], preferred_element_type=F32)
    ar = ar_ref[...]
    ai = ai_ref[...]

    def step(hr, hi, t):
        r = pl.multiple_of(t * SUBLANES, SUBLANES)
        xr = x_scr[pl.ds(r, SUBLANES), :ns]
        xi = x_scr[pl.ds(r, SUBLANES), ns:]
        return ar * hr - ai * hi + xr, ar * hi + ai * hr + xi

    def body(j, carry):
        hr, hi = carry
        t0 = (S5_CHUNK - 1 - 2 * j) if reverse else 2 * j
        t1 = t0 - 1 if reverse else t0 + 1
        hr0, hi0 = step(hr, hi, t0)
        hr1, hi1 = step(hr0, hi0, t1)
        lo = t1 if reverse else t0
        first_r, second_r = (hr1, hr0) if reverse else (hr0, hr1)
        first_i, second_i = (hi1, hi0) if reverse else (hi0, hi1)
        r = pl.multiple_of(lo * SUBLANES, 2 * SUBLANES)
        h_scr[pl.ds(r, 2 * SUBLANES), :ns] = jnp.concatenate([first_r, second_r], axis=0).astype(BF16)
        h_scr[pl.ds(r, 2 * SUBLANES), ns:] = jnp.concatenate([first_i, second_i], axis=0).astype(BF16)
        return hr1, hi1

    hr, hi = lax.fori_loop(0, S5_CHUNK // 2, body, (hr_scr[...], hi_scr[...]))
    hr_scr[...] = hr
    hi_scr[...] = hi
    y = jnp.dot(h_scr[...], wr_ref[...], preferred_element_type=F32)
    y = y + (d_ref[...].astype(F32) if reverse else u * d_ref[...])
    y_ref[...] = y.astype(y_ref.dtype)


def pl_s5_scan(u_tb, w_drive, w_read, a_re, a_im, addend, *, n_lat_steps, reverse):
    rows, ch = u_tb.shape
    rc = S5_CHUNK * SUBLANES
    n_chunks = rows // rc
    n_lat = n_lat_steps // S5_CHUNK
    n_ctx = n_chunks - n_lat
    assert rows % rc == 0 and n_lat_steps % S5_CHUNK == 0
    if reverse:
        def idx(i):
            return (n_chunks - 1 - i, 0)
    else:
        def idx(i):
            return (jnp.where(i < n_ctx, n_lat + i, i - n_ctx), 0)
    const = lambda i: (0, 0)
    ns2 = 2 * S5_STATES
    return pl.pallas_call(
        functools.partial(_s5_kernel, reverse=reverse),
        grid=(n_chunks,),
        in_specs=[pl.BlockSpec((rc, ch), idx),
                  pl.BlockSpec((ch, ns2), const),
                  pl.BlockSpec((ns2, ch), const),
                  pl.BlockSpec((SUBLANES, S5_STATES), const),
                  pl.BlockSpec((SUBLANES, S5_STATES), const),
                  pl.BlockSpec((rc, ch), idx) if reverse else pl.BlockSpec((1, ch), const)],
        out_specs=pl.BlockSpec((rc, ch), idx),
        out_shape=jax.ShapeDtypeStruct((rows, ch), BF16),
        scratch_shapes=[pltpu.VMEM((rc, ns2), F32), pltpu.VMEM((rc, ns2), BF16),
                        pltpu.VMEM((SUBLANES, S5_STATES), F32), pltpu.VMEM((SUBLANES, S5_STATES), F32)],
        compiler_params=_params(1),
        name="s5_scan_rev" if reverse else "s5_scan_fwd",
    )(u_tb, w_drive, w_read,
      jnp.broadcast_to(a_re[None, :], (SUBLANES, S5_STATES)),
      jnp.broadcast_to(a_im[None, :], (SUBLANES, S5_STATES)),
      addend if reverse else addend.reshape(1, ch))


FFT_N2 = 128


def _fft_tables(L):
    N = 2 * L
    N1 = N // FFT_N2
    k1 = np.arange(N1)[:, None]
    n1 = np.arange(N1 // 2)[None, :]
    n2 = np.arange(FFT_N2)[:, None, None]
    ang = -2.0 * np.pi * (k1[None] * (n2 + FFT_N2 * n1[None])) / N
    mr, mi = np.cos(ang), np.sin(ang)
    ma = np.concatenate([np.concatenate([mr, -mi], axis=2), np.concatenate([mi, mr], axis=2)], axis=1)
    gr, gi = np.transpose(mr, (0, 2, 1)), -np.transpose(mi, (0, 2, 1))
    mainv = np.concatenate([np.concatenate([gr, -gi], axis=2), np.concatenate([gi, gr], axis=2)], axis=1)
    kk = np.arange(FFT_N2)
    a2 = -2.0 * np.pi * np.outer(kk, kk) / FFT_N2
    fr, fi = np.cos(a2), np.sin(a2)
    f_fwd = np.block([[fr, -fi], [fi, fr]])
    f_inv = np.block([[fr, fi], [-fi, fr]])
    return (jnp.asarray(ma, BF16), jnp.asarray(mainv, BF16), jnp.asarray(f_fwd, BF16), jnp.asarray(f_inv, BF16))


HYENA_VMEM_LIMIT = 56 * 1024 * 1024
FFT_UNROLL = 8


def _hyena_fft_kernel(z_ref, h_ref, ma_ref, mainv_ref, ff_ref, fi_ref, o_ref, a_scr, b_scr, *, n1_count):
    half = n1_count // 2
    n2c = FFT_N2

    def stage_a(n2, c):
        xr = z_ref[0, pl.ds(n2, half, stride=n2c), :]
        xi = z_ref[1, pl.ds(n2, half, stride=n2c), :]
        x = jnp.concatenate([xr, xi], axis=0).astype(BF16)
        r = jnp.dot(ma_ref[n2], x, preferred_element_type=F32)
        a_scr[pl.ds(pl.multiple_of(n2 * 2 * n1_count, 2 * n1_count), 2 * n1_count), :] = r
        return c

    lax.fori_loop(0, n2c, stage_a, 0, unroll=FFT_UNROLL)

    def stage_c(k1, c):
        ar = a_scr[pl.ds(k1, n2c, stride=2 * n1_count), :]
        ai = a_scr[pl.ds(n1_count + k1, n2c, stride=2 * n1_count), :]
        x = jnp.concatenate([ar, ai], axis=0).astype(BF16)
        y = jnp.dot(ff_ref[...], x, preferred_element_type=F32)
        yr, yi = y[:n2c], y[n2c:]
        hr = h_ref[0, k1].astype(F32)
        hi = h_ref[1, k1].astype(F32)
        x2 = jnp.concatenate([yr * hr - yi * hi, yr * hi + yi * hr], axis=0).astype(BF16)
        b = jnp.dot(fi_ref[...], x2, preferred_element_type=F32)
        b_scr[pl.ds(pl.multiple_of(k1 * 2 * n2c, 2 * n2c), 2 * n2c), :] = b
        return c

    lax.fori_loop(0, n1_count, stage_c, 0, unroll=FFT_UNROLL)

    def stage_a_inv(n2, c):
        br = b_scr[pl.ds(n2, n1_count, stride=2 * n2c), :]
        bi = b_scr[pl.ds(n2c + n2, n1_count, stride=2 * n2c), :]
        x = jnp.concatenate([br, bi], axis=0).astype(BF16)
        r = jnp.dot(mainv_ref[n2], x, preferred_element_type=F32)
        o_ref[0, pl.ds(n2, half, stride=n2c), :] = r[:half]
        o_ref[1, pl.ds(n2, half, stride=n2c), :] = r[half:]
        return c

    lax.fori_loop(0, n2c, stage_a_inv, 0, unroll=FFT_UNROLL)


def hyena_spectrum(filt):
    N, C = filt.shape
    N1 = N // FFT_N2
    h2 = (jnp.fft.fft(filt, axis=0) / N).reshape(FFT_N2, N1, C).transpose(1, 0, 2)
    return jnp.stack([h2.real, h2.imag]).astype(BF16)


def pl_hyena_conv(z, h, dft_tables, L):
    B, _, C = z.shape
    N1 = 2 * L // FFT_N2
    ma, mainv, f_fwd, f_inv = dft_tables
    cw = LANES
    full = lambda arr: pl.BlockSpec(arr.shape, lambda ct, bp: (0,) * arr.ndim)
    return pl.pallas_call(
        functools.partial(_hyena_fft_kernel, n1_count=N1),
        grid=(C // cw, B // 2),
        in_specs=[pl.BlockSpec((2, L, cw), lambda ct, bp: (bp, 0, ct)),
                  pl.BlockSpec((2, N1, FFT_N2, cw), lambda ct, bp: (0, 0, 0, ct)),
                  full(ma), full(mainv), full(f_fwd), full(f_inv)],
        out_specs=pl.BlockSpec((2, L, cw), lambda ct, bp: (bp, 0, ct)),
        out_shape=jax.ShapeDtypeStruct((B, L, C), F32),
        scratch_shapes=[pltpu.VMEM((FFT_N2 * 2 * N1, cw), F32), pltpu.VMEM((N1 * 2 * FFT_N2, cw), F32)],
        compiler_params=_params(2, HYENA_VMEM_LIMIT),
        name="hyena_fft_conv",
    )(z, h, ma, mainv, f_fwd, f_inv)


def hyena_filter(L, w1, b1, w2, b2, w3, freq):
    t = jnp.arange(L, dtype=F32) / L
    ang = (2.0 * math.pi) * t[:, None] * jnp.arange(1, HY_BANDS + 1, dtype=F32)
    feat = jnp.concatenate([t[:, None], jnp.cos(ang), jnp.sin(ang)], axis=-1)
    hp = lax.Precision.HIGHEST
    h = jnp.sin(freq * (jnp.dot(feat, w1, precision=hp) + b1))
    h = jnp.sin(freq * (jnp.dot(h, w2, precision=hp) + b2))
    h = jnp.dot(h, w3, precision=hp).reshape(L, 2, HY_CH)
    window = jnp.exp(-t[:, None] * jnp.linspace(HY_DECAY_MIN, HY_DECAY_MAX, HY_CH, dtype=F32))
    h = h * window[:, None, :]
    filt = jnp.concatenate([h[:, 0], jnp.zeros((1, HY_CH), F32), h[:0:-1, 1]], axis=0)
    return filt / (jnp.sum(jnp.abs(filt), axis=0, keepdims=True) + EPS)


def hyena_conv(z, h_lat, ff_ctx, dft_tables, L):
    Lc = z.shape[1] - L
    y_lat = pl_hyena_conv(z, h_lat, dft_tables, L)
    zf = jnp.fft.rfft(z[:, L:], n=2 * Lc, axis=1)
    y_ctx = jnp.fft.irfft(zf * ff_ctx[None], n=2 * Lc, axis=1)[:, :Lc]
    return y_lat, y_ctx


def _block_diag(blocks):
    G, r, c = blocks.shape
    eye = jnp.eye(G, dtype=blocks.dtype)
    return (eye[:, None, :, None] * blocks[:, :, None, :]).reshape(G * r, G * c)


def s5_tables(a_re, a_im, log_dt, b_re, b_im, c_re, c_im):
    A = lax.complex(a_re, a_im)
    dtA = jnp.exp(log_dt)[:, None] * A
    a_bar = jnp.exp(dtA)
    b_bar = ((a_bar - 1.0) / A)[:, :, None] * lax.complex(b_re, b_im)
    bt_re = jnp.transpose(b_bar.real, (0, 2, 1))
    bt_im = jnp.transpose(b_bar.imag, (0, 2, 1))
    w_drive = jnp.concatenate([_block_diag(bt_re), _block_diag(bt_im)], axis=1)
    ct_re = jnp.transpose(c_re, (0, 2, 1))
    ct_im = jnp.transpose(c_im, (0, 2, 1))
    w_read = jnp.concatenate([_block_diag(ct_re), -_block_diag(ct_im)], axis=0)
    return w_drive.astype(BF16), w_read.astype(BF16), a_bar.real.reshape(-1), a_bar.imag.reshape(-1)


def s5_scan(u_tb, tables, d_skip, n_lat_steps):
    y = d_skip
    for direction in range(2):
        w_drive, w_read, a_re, a_im = (t[direction] for t in tables)
        y = pl_s5_scan(u_tb, w_drive, w_read, a_re, a_im, y, n_lat_steps=n_lat_steps, reverse=direction == 1)
    return y


def rope_tables(L, Lc):
    rows = L // GRID_W
    row = jnp.repeat(jnp.arange(rows, dtype=F32), GRID_W)
    col = jnp.tile(jnp.arange(GRID_W, dtype=F32), rows)
    inv = ROPE_BASE ** (-jnp.arange(ROPE_PAIRS_AXIS, dtype=F32) / ROPE_PAIRS_AXIS)
    ang = jnp.concatenate([row[:, None] * inv, col[:, None] * inv], axis=-1)
    cos, sin = jnp.cos(ang), jnp.sin(ang)
    n_maps = ATT_W // ATT_HEAD_DIM
    cosf = jnp.tile(jnp.concatenate([cos, cos], axis=-1), (1, n_maps))
    sinf = jnp.tile(jnp.concatenate([-sin, sin], axis=-1), (1, n_maps))
    return (jnp.concatenate([cosf, jnp.ones((Lc, ATT_W), F32)], axis=0),
            jnp.concatenate([sinf, jnp.zeros((Lc, ATT_W), F32)], axis=0))


def moe_dispatch(logits, b_g, b_e):
    T = logits.shape[0]
    g_logits = logits[:, :MOE_GROUPS] + b_g
    g_idx = jnp.argmax(g_logits, axis=-1)
    p_group = jnp.max(jax.nn.softmax(g_logits, axis=-1), axis=-1, keepdims=True)
    e_ids = jnp.arange(N_EXPERTS, dtype=jnp.int32)
    in_group = (e_ids // MOE_EPG)[None, :] == g_idx[:, None]
    e_all = logits[:, MOE_GROUPS:MOE_GROUPS + N_EXPERTS] + b_e
    probs = jnp.where(in_group, jax.nn.softmax(jnp.where(in_group, e_all, -jnp.inf), axis=-1), -1.0)
    top_i, top_p = [], []
    for _ in range(MOE_TOP_K):
        best = jnp.argmax(probs, axis=-1)
        top_i.append(best)
        top_p.append(jnp.max(probs, axis=-1))
        probs = jnp.where(e_ids[None, :] == best[:, None], -1.0, probs)
    top_p = jnp.stack(top_p, axis=1)
    gate = p_group * top_p / jnp.sum(top_p, axis=-1, keepdims=True)
    expert = jnp.stack(top_i, axis=1).reshape(-1).astype(jnp.int32)
    tok = jnp.repeat(jnp.arange(T, dtype=jnp.int32), MOE_TOP_K)
    n_assign = T * MOE_TOP_K
    n_blocks = -(-n_assign // MOE_BLOCK) + N_EXPERTS
    n_pad = n_blocks * MOE_BLOCK
    order = jnp.argsort(expert).astype(jnp.int32)
    sorted_rank = jnp.argsort(order).astype(jnp.int32)
    e_ids = jnp.arange(N_EXPERTS + 1, dtype=jnp.int32)
    bounds = jnp.sum((expert[:, None] < e_ids[None, :]).astype(jnp.int32), axis=0)
    start = bounds[:-1]
    counts = bounds[1:] - bounds[:-1]
    padded = (counts + MOE_BLOCK - 1) // MOE_BLOCK * MOE_BLOCK
    pad_end = jnp.cumsum(padded)
    pad_start = pad_end - padded
    slot_of_assign = (pad_start[expert] + sorted_rank - start[expert]).astype(jnp.int32)
    block_first = jnp.arange(n_blocks, dtype=jnp.int32) * MOE_BLOCK
    block_e = jnp.minimum(jnp.sum((pad_end[None, :] <= block_first[:, None]).astype(jnp.int32), axis=1),
                          N_EXPERTS - 1).astype(jnp.int32)
    slot_e = jnp.repeat(block_e, MOE_BLOCK)
    slot_r = jnp.arange(n_pad, dtype=jnp.int32) - pad_start[slot_e]
    slot_valid = (slot_r < counts[slot_e]) & (jnp.arange(n_pad) < pad_end[-1])
    slot_assign = order[jnp.clip(start[slot_e] + slot_r, 0, n_assign - 1)]
    slot_tok = jnp.where(slot_valid, tok[slot_assign], jnp.arange(n_pad, dtype=jnp.int32) % T)
    slot_gate = jnp.where(slot_valid, gate.reshape(-1)[slot_assign], 0.0)
    n_used = (pad_end[-1:] // MOE_BLOCK).astype(jnp.int32)
    return slot_tok, slot_gate, slot_of_assign, block_e, n_used


def hier_moe(h2, logits, b_g, b_e, w1, w3, w2, layer):
    T, D = h2.shape
    slot_tok, slot_gate, slot_of_assign, block_e, n_used = moe_dispatch(logits, b_g, b_e)
    yb = pl_moe_ffn(h2[slot_tok], slot_gate, block_e, n_used, w1, w3, w2, layer)
    slots = slot_of_assign.reshape(T, MOE_TOP_K)
    return yb[slots[:, 0]], yb[slots[:, 1]]


def kernel(x, c, ctx, c_ctx, w_mod, b_mod, norm1_g, norm2_g, final_g, w_in, w_out, hy_conv_w, hy_conv_b, hy_ffn_w1, hy_ffn_b1, hy_ffn_w2, hy_ffn_b2, hy_ffn_w3, hy_freq, hy_skip, hy_norm_g, s5_a_re, s5_a_im, s5_log_dt, s5_b_re, s5_b_im, s5_c_re, s5_c_im, s5_d, s5_glu_w, s5_norm_g, att_lq1, att_lk1, att_lq2, att_lk2, att_subln_g, moe_wg, moe_bg, moe_we, moe_be, moe_w1, moe_w3, moe_w2):
    B, L, D = x.shape
    Lc = ctx.shape[1]
    Lt = L + Lc
    assert B == SUBLANES and Lc == TOKEN_TILE and L % ATT_TQ == 0
    n_lat_tiles = L // TOKEN_TILE
    cosf, sinf = rope_tables(L, Lc)
    hp = lax.Precision.HIGHEST
    q_scale = ATT_HEAD_DIM ** -0.5 * math.log2(math.e)

    mods = jnp.einsum('bd,ldk->lbk', jax.nn.silu(c), w_mod, precision=hp) + b_mod[:, None, :]
    cmods = jnp.einsum('d,ldk->lk', jax.nn.silu(c_ctx), w_mod, precision=hp) + b_mod
    modvs = jnp.stack([mods, jnp.broadcast_to(cmods[:, None, :], mods.shape)], axis=2)
    modvs = modvs.reshape(DEPTH, 2 * B, 1, N_MOD * D)
    filt_args = (hy_ffn_w1, hy_ffn_b1, hy_ffn_w2, hy_ffn_b2, hy_ffn_w3, hy_freq)
    h_lat = jax.vmap(lambda *p: hyena_spectrum(hyena_filter(L, *p)))(*filt_args)
    ff_ctx = jax.vmap(lambda *p: jnp.fft.rfft(hyena_filter(Lc, *p), n=2 * Lc, axis=0))(*filt_args)
    dft_tables = _fft_tables(L)
    s5_tabs = jax.vmap(jax.vmap(s5_tables))(s5_a_re, s5_a_im, s5_log_dt, s5_b_re, s5_b_im, s5_c_re, s5_c_im)
    lam_inits = [0.8 - 0.6 * math.exp(-0.3 * l) for l in range(DEPTH)]
    lams = (jnp.exp(jnp.sum(att_lq1 * att_lk1, axis=-1)) - jnp.exp(jnp.sum(att_lq2 * att_lk2, axis=-1))
            + jnp.asarray(lam_inits, F32))
    w_router = jnp.zeros((DEPTH, D, LANES), F32).at[:, :, :MOE_GROUPS].set(moe_wg)
    w_router = w_router.at[:, :, MOE_GROUPS:MOE_GROUPS + N_EXPERTS].set(moe_we)
    rwh = w_router.astype(BF16)
    rwl = (w_router - rwh.astype(F32)).astype(BF16)
    w_in_b, w_out_b, glu_w_b = w_in.astype(BF16), w_out.astype(BF16), s5_glu_w.astype(BF16)

    xs = jnp.concatenate([x, ctx], axis=1)
    for l in range(DEPTH):
        modv = modvs[l]
        u0, z, s5_u, q_p, q_r, k_r, v = pl_in_proj(xs, modv, norm1_g[l], w_in_b[l], hy_conv_w[l], hy_conv_b[l],
                                                   cosf, sinf, n_lat_tiles, q_scale)

        y_lat, y_ctx = hyena_conv(z, h_lat[l], ff_ctx[l], dft_tables, L)

        s5_y = s5_scan(s5_u.reshape(Lt * B, S5_CH), tuple(t[l] for t in s5_tabs), s5_d[l], L)

        g_scaled = (att_subln_g[l] * (1.0 - lam_inits[l])).reshape(1, ATT_V_DIM)
        att_lat, att_ctx = pl_diff_attention(q_p, q_r, k_r, v, g_scaled, lams[l], L)

        xs, h2, logits = pl_out_proj(xs, u0, z, y_lat, y_ctx, s5_y.reshape(Lt, B * S5_CH), att_lat, att_ctx, modv,
                                     hy_skip[l], hy_norm_g[l], glu_w_b[l], s5_norm_g[l], w_out_b[l], norm2_g[l],
                                     rwh[l], rwl[l], n_lat_tiles)

        y0, y1 = hier_moe(h2.reshape(B * Lt, D), logits.reshape(B * Lt, LANES), moe_bg[l], moe_be[l],
                          moe_w1, moe_w3, moe_w2, l)
        xs = pl_moe_combine(xs, y0, y1, modv, final_g, n_lat_tiles, final=l == DEPTH - 1)
    return xs
```

```python
import functools
import math

import jax
import jax.numpy as jnp
import numpy as np
from jax import lax
from jax.experimental import pallas as pl
from jax.experimental.pallas import tpu as pltpu

D_MODEL = 1024
DEPTH = 4
GRID_W = 64
N_MOD = 6
EPS = 1e-6
HY_CH = D_MODEL // 4
S5_CH = D_MODEL // 4
ATT_W = D_MODEL // 2
HY_BANDS = 16
HY_DECAY_MIN = -math.log(1e-2) / 1.5
HY_DECAY_MAX = -math.log(1e-2) / 0.3
S5_GROUP = 16
S5_GROUPS = S5_CH // S5_GROUP
S5_STATE = 64
ATT_HEAD_DIM = 64
ATT_HEADS = ATT_W // (2 * ATT_HEAD_DIM)
ATT_V_DIM = 2 * ATT_HEAD_DIM
ROPE_HALF = ATT_HEAD_DIM // 2
ROPE_PAIRS_AXIS = ROPE_HALF // 2
ROPE_BASE = 10000.0
MOE_GROUPS = 4
MOE_EPG = 8
N_EXPERTS = MOE_GROUPS * MOE_EPG
MOE_TOP_K = 2
MOE_BLOCK = 512
IN_COLS = 3 * HY_CH + S5_CH + 3 * ATT_W
COL_S5 = 3 * HY_CH
COL_Q = COL_S5 + S5_CH
COL_K = COL_Q + ATT_W
COL_V = COL_K + ATT_W

LANES = 128
SUBLANES = 8
VMEM_LIMIT = 48 * 1024 * 1024
TOKEN_TILE = 256

F32 = jnp.float32
BF16 = jnp.bfloat16


def _params(n_axes, vmem=VMEM_LIMIT):
    return pltpu.CompilerParams(dimension_semantics=("arbitrary",) * n_axes, vmem_limit_bytes=vmem)


def _rms(x):
    return x * lax.rsqrt(jnp.mean(x * x, axis=-1, keepdims=True) + EPS)


def _in_kernel(x_ref, xp_ref, xn_ref, mod_ref, g_ref, w_ref, cw_ref, cb_ref, cos_ref, sin_ref,
               u0_ref, z_ref, s5_ref, qp_ref, qr_ref, k_ref, v_ref, *, n_lat_tiles, q_scale):
    i = pl.program_id(1)
    tm = x_ref.shape[1]
    d = x_ref.shape[2]
    g = g_ref[...]
    shift = mod_ref[0, :, 0:d]
    scale = mod_ref[0, :, d:2 * d]

    def norm_mod(xt):
        return (_rms(xt) * g) * (1.0 + scale) + shift

    h = jnp.concatenate([norm_mod(x_ref[0]), norm_mod(xp_ref[0]), norm_mod(xn_ref[0])], axis=0).astype(BF16)
    p = jnp.dot(h, w_ref[...], preferred_element_type=F32)

    hy = p[:tm, :COL_S5]
    is_ctx = i == n_lat_tiles
    has_prev = jnp.logical_and(i != 0, jnp.logical_not(is_ctx))
    has_next = jnp.logical_and(i != n_lat_tiles - 1, jnp.logical_not(is_ctx))
    prev_row = jnp.where(has_prev, p[tm + SUBLANES - 1:tm + SUBLANES, :COL_S5], 0.0)
    next_row = jnp.where(has_next, p[tm + SUBLANES:tm + SUBLANES + 1, :COL_S5], 0.0)
    rows = lax.broadcasted_iota(jnp.int32, (tm, 1), 0)
    up = jnp.where(rows == 0, prev_row, pltpu.roll(hy, 1, axis=0))
    dn = jnp.where(rows == tm - 1, next_row, pltpu.roll(hy, tm - 1, axis=0))
    u = up * cw_ref[0:1, :] + hy * cw_ref[1:2, :] + dn * cw_ref[2:3, :] + cb_ref[...]
    u0_ref[0] = u[:, :HY_CH].astype(u0_ref.dtype)
    z_ref[0] = u[:, HY_CH:2 * HY_CH] * u[:, 2 * HY_CH:]

    s5_ref[...] = p[:tm, COL_S5:COL_Q]

    lane = lax.broadcasted_iota(jnp.int32, (1, ATT_W), 1)
    first_half = jnp.bitwise_and(lane, ATT_HEAD_DIM - 1) < ROPE_HALF
    cos = cos_ref[...]
    sin = sin_ref[...]

    def rope(t):
        partner = jnp.where(first_half, pltpu.roll(t, ATT_W - ROPE_HALF, axis=1), pltpu.roll(t, ROPE_HALF, axis=1))
        return t * cos + partner * sin

    q = p[:tm, COL_Q:COL_K] * q_scale
    qp_ref[0] = q.astype(BF16)
    qr_ref[0] = rope(q).astype(BF16)
    k_ref[0] = rope(p[:tm, COL_K:COL_V]).astype(BF16)
    v_ref[0] = p[:tm, COL_V:].astype(BF16)


def pl_in_proj(xs, modv, norm_g, w_in, conv_w, conv_b, cosf, sinf, n_lat_tiles, q_scale):
    B, Lt, D = xs.shape
    tm = TOKEN_TILE
    n_tiles = Lt // tm
    halo_per_tile = tm // SUBLANES
    n_halo_blocks = Lt // SUBLANES
    tok = lambda w: pl.BlockSpec((1, tm, w), lambda b, i: (b, i, 0))
    const = lambda shape: pl.BlockSpec(shape, lambda b, i: (0,) * len(shape))
    out_shapes = (jax.ShapeDtypeStruct((B, Lt, HY_CH), BF16), jax.ShapeDtypeStruct((B, Lt, HY_CH), F32),
                  jax.ShapeDtypeStruct((Lt, B * S5_CH), F32),
                  jax.ShapeDtypeStruct((B, Lt, ATT_W), BF16), jax.ShapeDtypeStruct((B, Lt, ATT_W), BF16),
                  jax.ShapeDtypeStruct((B, Lt, ATT_W), BF16), jax.ShapeDtypeStruct((B, Lt, ATT_W), BF16))
    return pl.pallas_call(
        functools.partial(_in_kernel, n_lat_tiles=n_lat_tiles, q_scale=q_scale),
        grid=(B, n_tiles),
        in_specs=[tok(D),
                  pl.BlockSpec((1, SUBLANES, D), lambda b, i: (b, jnp.maximum(i * halo_per_tile - 1, 0), 0)),
                  pl.BlockSpec((1, SUBLANES, D),
                               lambda b, i: (b, jnp.minimum((i + 1) * halo_per_tile, n_halo_blocks - 1), 0)),
                  pl.BlockSpec((1, 1, N_MOD * D), lambda b, i: (2 * b + jnp.where(i >= n_lat_tiles, 1, 0), 0, 0)),
                  const((1, D)), const((D, IN_COLS)), const((3, COL_S5)), const((1, COL_S5)),
                  pl.BlockSpec((tm, ATT_W), lambda b, i: (i, 0)), pl.BlockSpec((tm, ATT_W), lambda b, i: (i, 0))],
        out_specs=(tok(HY_CH), tok(HY_CH), pl.BlockSpec((tm, S5_CH), lambda b, i: (i, b)),
                   tok(ATT_W), tok(ATT_W), tok(ATT_W), tok(ATT_W)),
        out_shape=out_shapes,
        compiler_params=_params(2),
        name="in_proj",
    )(xs, xs, xs, modv, norm_g.reshape(1, D), w_in, conv_w, conv_b.reshape(1, COL_S5), cosf, sinf)


def _out_kernel(x_ref, u0_ref, z_ref, yl_ref, yc_ref, s5_ref, al_ref, ac_ref, mod_ref, skip_ref, hg_ref,
                gw_ref, sg_ref, wo_ref, n2g_ref, rwh_ref, rwl_ref, rb_ref, tri_ref, xo_ref, h2_ref, rt_ref, cnt_ref,
                cnt_scr, *, n_lat_tiles):
    d = x_ref.shape[2]
    is_ctx = pl.program_id(1) == n_lat_tiles
    y = jnp.where(is_ctx, yc_ref[0], yl_ref[0])
    att = jnp.where(is_ctx, ac_ref[0], al_ref[0])
    hy = _rms(u0_ref[0].astype(F32) * (y + skip_ref[...] * z_ref[0])) * hg_ref[...]
    gl = jax.nn.gelu(s5_ref[...].astype(F32))
    gate = jax.nn.sigmoid(jnp.dot(gl.astype(BF16), gw_ref[...], preferred_element_type=F32))
    s5 = _rms(gl * gate) * sg_ref[...]
    mix = jnp.concatenate([hy.astype(BF16), s5.astype(BF16), att], axis=1)
    proj = jnp.dot(mix, wo_ref[...], preferred_element_type=F32)
    xn = x_ref[0] + mod_ref[0, :, 2 * d:3 * d] * proj
    xo_ref[0] = xn
    h2 = (_rms(xn) * n2g_ref[...]) * (1.0 + mod_ref[0, :, 4 * d:5 * d]) + mod_ref[0, :, 3 * d:4 * d]
    hh = h2.astype(BF16)
    hl = (h2 - hh.astype(F32)).astype(BF16)
    h2_ref[0] = hh
    lg = jnp.dot(hh, rwh_ref[...], preferred_element_type=F32)
    lg += jnp.dot(hl, rwh_ref[...], preferred_element_type=F32)
    lg += jnp.dot(hh, rwl_ref[...], preferred_element_type=F32)
    lg = lg + rb_ref[...]

    tm = lg.shape[0]
    lane = lax.broadcasted_iota(jnp.int32, (1, LANES), 1)
    neg_inf = jnp.float32(-jnp.inf)

    def first_max(v):
        m = jnp.max(v, axis=1, keepdims=True)
        return m, jnp.min(jnp.where(v == m, lane, LANES), axis=1, keepdims=True)

    g_logit = jnp.where(lane < MOE_GROUPS, lg, neg_inf)
    g_max, g_idx = first_max(g_logit)
    p_group = 1.0 / jnp.sum(jnp.exp(g_logit - g_max), axis=1, keepdims=True)
    e_lane = lane - MOE_GROUPS
    in_group = jnp.logical_and(jnp.logical_and(e_lane >= 0, e_lane < N_EXPERTS),
                               jnp.right_shift(e_lane, MOE_EPG.bit_length() - 1) == g_idx)
    e_logit = jnp.where(in_group, lg, neg_inf)
    e_exp = jnp.exp(e_logit - jnp.max(e_logit, axis=1, keepdims=True))
    probs = jnp.where(in_group, e_exp / jnp.sum(e_exp, axis=1, keepdims=True), -1.0)
    p1, l1 = first_max(probs)
    p2, l2 = first_max(jnp.where(lane == l1, -1.0, probs))
    hit1 = lane == l1
    hit2 = lane == l2
    onehot = jnp.where(hit1, 1.0, 0.0) + jnp.where(hit2, 1.0, 0.0)

    @pl.when(jnp.logical_and(pl.program_id(0) == 0, pl.program_id(1) == 0))
    def _():
        cnt_scr[...] = jnp.zeros_like(cnt_scr)

    earlier = jnp.dot(tri_ref[...], onehot.astype(BF16), preferred_element_type=F32)
    before = cnt_scr[0:1, :] + earlier
    r1 = jnp.sum(jnp.where(hit1, before, 0.0), axis=1, keepdims=True)
    r2 = jnp.sum(jnp.where(hit2, before, 0.0), axis=1, keepdims=True)
    cnt_scr[...] = jnp.broadcast_to(cnt_scr[0:1, :] + jnp.sum(onehot, axis=0, keepdims=True), cnt_scr.shape)
    cnt_ref[...] = cnt_scr[...]
    cols = [(l1 - MOE_GROUPS).astype(F32), (l2 - MOE_GROUPS).astype(F32), p_group * p1 / (p1 + p2),
            p_group * p2 / (p1 + p2), r1, r2]
    route = jnp.zeros((tm, LANES), F32)
    for j, col in enumerate(cols):
        route = jnp.where(lane == j, col, route)
    rt_ref[0] = route


def pl_out_proj(xs, u0, z, y_lat, y_ctx, s5_y, att_lat, att_ctx, modv, hy_skip, hy_norm_g, glu_w, s5_norm_g,
                w_out, norm2_g, rwh, rwl, router_bias, n_lat_tiles):
    B, Lt, D = xs.shape
    tm = TOKEN_TILE
    tri = jnp.asarray(np.tril(np.ones((tm, tm), np.float32), -1), BF16)
    tok = lambda w: pl.BlockSpec((1, tm, w), lambda b, i: (b, i, 0))
    lat_tok = lambda w: pl.BlockSpec((1, tm, w), lambda b, i: (b, jnp.minimum(i, n_lat_tiles - 1), 0))
    ctx_tok = lambda w: pl.BlockSpec((1, tm, w), lambda b, i: (b, 0, 0))
    tb = pl.BlockSpec((tm, S5_CH), lambda b, i: (i, b))
    const = lambda shape: pl.BlockSpec(shape, lambda b, i: (0,) * len(shape))
    return pl.pallas_call(
        functools.partial(_out_kernel, n_lat_tiles=n_lat_tiles),
        grid=(B, Lt // tm),
        in_specs=[tok(D), tok(HY_CH), tok(HY_CH), lat_tok(HY_CH), ctx_tok(HY_CH), tb, lat_tok(ATT_W),
                  ctx_tok(ATT_W),
                  pl.BlockSpec((1, 1, N_MOD * D), lambda b, i: (2 * b + jnp.where(i >= n_lat_tiles, 1, 0), 0, 0)),
                  const((1, HY_CH)), const((1, HY_CH)), const((S5_CH, S5_CH)), const((1, S5_CH)),
                  const((D, D)), const((1, D)), const((D, LANES)), const((D, LANES)), const((1, LANES)),
                  const((tm, tm))],
        out_specs=(tok(D), tok(D), tok(LANES), const((SUBLANES, LANES))),
        out_shape=(jax.ShapeDtypeStruct((B, Lt, D), F32), jax.ShapeDtypeStruct((B, Lt, D), BF16),
                   jax.ShapeDtypeStruct((B, Lt, LANES), F32), jax.ShapeDtypeStruct((SUBLANES, LANES), F32)),
        scratch_shapes=[pltpu.VMEM((SUBLANES, LANES), F32)],
        compiler_params=_params(2),
        name="out_proj",
    )(xs, u0, z, y_lat, y_ctx, s5_y, att_lat, att_ctx, modv, hy_skip.reshape(1, HY_CH),
      hy_norm_g.reshape(1, HY_CH), glu_w, s5_norm_g.reshape(1, S5_CH), w_out, norm2_g.reshape(1, D), rwh, rwl,
      router_bias, tri)


def _dot_nt(a, b):
    return lax.dot_general(a, b, (((1,), (1,)), ((), ())), preferred_element_type=F32)


ATT_TQ = 512
ATT_SUB = 256


def _attn_kernel(qp_ref, qr_ref, k_ref, v_ref, g_ref, lam_ref, o_ref, *, n_lat):
    tq = qp_ref.shape[1]
    first_map = lax.broadcasted_iota(jnp.int32, (1, LANES), 1) < ATT_HEAD_DIM
    zero = jnp.zeros((), BF16)
    sub = min(ATT_SUB, tq)
    for r0 in range(0, tq, sub):
        qp = qp_ref[0, r0:r0 + sub, :]
        qr = qr_ref[0, r0:r0 + sub, :]
        probs = []
        for m in range(2):
            in_map = first_map if m == 0 else jnp.logical_not(first_map)
            s_c = _dot_nt(jnp.where(in_map, qp, zero), k_ref[0, n_lat:, :])
            mx = jnp.max(s_c, axis=-1, keepdims=True)
            if n_lat:
                s_l = _dot_nt(jnp.where(in_map, qr, zero), k_ref[0, :n_lat, :])
                mx = jnp.maximum(mx, jnp.max(s_l, axis=-1, keepdims=True))
                p_l = jnp.exp2(s_l - mx)
            p_c = jnp.exp2(s_c - mx)
            den = jnp.sum(p_c, axis=-1, keepdims=True)
            if n_lat:
                den = den + jnp.sum(p_l, axis=-1, keepdims=True)
            probs.append((p_c, p_l if n_lat else None, 1.0 / den))
        w0 = probs[0][2]
        w1 = lam_ref[0:1, 0:1] * probs[1][2]
        a_c = (probs[0][0] * w0 - probs[1][0] * w1).astype(BF16)
        o = jnp.dot(a_c, v_ref[0, n_lat:, :], preferred_element_type=F32)
        if n_lat:
            a_l = (probs[0][1] * w0 - probs[1][1] * w1).astype(BF16)
            o = o + jnp.dot(a_l, v_ref[0, :n_lat, :], preferred_element_type=F32)
        o_ref[0, r0:r0 + sub, :] = (_rms(o) * g_ref[...]).astype(o_ref.dtype)


def pl_diff_attention(qp, qr, k, v, g_scaled, lam, n_lat):
    B, Lt, _ = qp.shape
    n_ctx = Lt - n_lat
    lam_arr = jnp.full((SUBLANES, LANES), lam, F32)
    small = [pl.BlockSpec((1, LANES), lambda b, h, i: (0, 0)), pl.BlockSpec((SUBLANES, LANES), lambda b, h, i: (0, 0))]
    tq = ATT_TQ
    qspec = pl.BlockSpec((1, tq, LANES), lambda b, h, i: (b, i, h))
    kspec = pl.BlockSpec((1, Lt, LANES), lambda b, h, i: (b, 0, h))
    out_lat = pl.pallas_call(
        functools.partial(_attn_kernel, n_lat=n_lat),
        grid=(B, ATT_HEADS, n_lat // tq),
        in_specs=[qspec, qspec, kspec, kspec] + small,
        out_specs=qspec,
        out_shape=jax.ShapeDtypeStruct((B, n_lat, ATT_W), BF16),
        compiler_params=_params(3),
        name="diff_attention",
    )(qp, qr, k, v, g_scaled, lam_arr)
    ctx_blk = n_lat // n_ctx
    cspec = pl.BlockSpec((1, n_ctx, LANES), lambda b, h, i: (b, ctx_blk, h))
    out_ctx = pl.pallas_call(
        functools.partial(_attn_kernel, n_lat=0),
        grid=(B, ATT_HEADS, 1),
        in_specs=[cspec, cspec, cspec, cspec] + small,
        out_specs=pl.BlockSpec((1, n_ctx, LANES), lambda b, h, i: (b, 0, h)),
        out_shape=jax.ShapeDtypeStruct((B, n_ctx, ATT_W), BF16),
        compiler_params=_params(3),
        name="diff_attention_ctx",
    )(qp, qr, k, v, g_scaled, lam_arr)
    return out_lat, out_ctx


def _moe_kernel(be_ref, nb_ref, x_ref, gate_ref, w1_ref, w3_ref, w2_ref, o_ref, w1_scr, w3_scr, w2_scr):
    i = pl.program_id(0)
    new_expert = jnp.logical_or(i == 0, be_ref[i] != be_ref[jnp.maximum(i - 1, 0)])

    @pl.when(jnp.logical_and(i < nb_ref[0], new_expert))
    def _():
        w1_scr[...] = w1_ref[0, 0].astype(BF16)
        w3_scr[...] = w3_ref[0, 0].astype(BF16)
        w2_scr[...] = w2_ref[0, 0].astype(BF16)

    @pl.when(i < nb_ref[0])
    def _():
        x = x_ref[...]
        a = jnp.dot(x, w1_scr[...], preferred_element_type=F32)
        b = jnp.dot(x, w3_scr[...], preferred_element_type=F32)
        h = (a * jax.nn.sigmoid(a)) * b
        y = jnp.dot(h.astype(BF16), w2_scr[...], preferred_element_type=F32)
        rows = lax.broadcasted_iota(jnp.int32, (MOE_BLOCK, 1), 0)
        lane = lax.broadcasted_iota(jnp.int32, (1, LANES), 1)
        g_rows = gate_ref[0, 0:1, :]
        for r in range(1, MOE_BLOCK // LANES):
            g_rows = jnp.where(rows >= r * LANES, gate_ref[0, r:r + 1, :], g_rows)
        g_col = jnp.sum(jnp.where(lane == jnp.bitwise_and(rows, LANES - 1), g_rows, 0.0), axis=1, keepdims=True)
        o_ref[...] = (y * g_col).astype(o_ref.dtype)

    @pl.when(i >= nb_ref[0])
    def _():
        o_ref[...] = jnp.zeros_like(o_ref)


def pl_moe_ffn(xb, slot_gate, block_e, n_used, w1, w3, w2, layer):
    n_pad, D = xb.shape
    n_blocks = n_pad // MOE_BLOCK
    F = w1.shape[-1]
    grid_spec = pltpu.PrefetchScalarGridSpec(
        num_scalar_prefetch=2,
        grid=(n_blocks,),
        in_specs=[pl.BlockSpec((MOE_BLOCK, D), lambda i, be, nb: (i, 0)),
                  pl.BlockSpec((1, MOE_BLOCK // LANES, LANES), lambda i, be, nb: (i, 0, 0)),
                  pl.BlockSpec((1, 1, D, F), lambda i, be, nb: (layer, be[i], 0, 0)),
                  pl.BlockSpec((1, 1, D, F), lambda i, be, nb: (layer, be[i], 0, 0)),
                  pl.BlockSpec((1, 1, F, D), lambda i, be, nb: (layer, be[i], 0, 0))],
        out_specs=pl.BlockSpec((MOE_BLOCK, D), lambda i, be, nb: (i, 0)),
        scratch_shapes=[pltpu.VMEM((D, F), BF16), pltpu.VMEM((D, F), BF16), pltpu.VMEM((F, D), BF16)],
    )
    return pl.pallas_call(
        _moe_kernel,
        grid_spec=grid_spec,
        out_shape=jax.ShapeDtypeStruct((n_pad, D), BF16),
        compiler_params=_params(1),
        name="moe_ffn",
    )(block_e, n_used, xb, slot_gate.reshape(n_blocks, MOE_BLOCK // LANES, LANES), w1, w3, w2)


def _combine_kernel(x_ref, y0_ref, y1_ref, mod_ref, g_ref, o_ref, *, final):
    d = x_ref.shape[2]
    xn = x_ref[0] + mod_ref[0, :, 5 * d:6 * d] * (y0_ref[0].astype(F32) + y1_ref[0].astype(F32))
    o_ref[0] = _rms(xn) * g_ref[...] if final else xn


def pl_moe_combine(xs, y0, y1, modv, final_g, n_lat_tiles, final):
    B, Lt, D = xs.shape
    tm = TOKEN_TILE
    n_tiles = n_lat_tiles if final else Lt // tm
    tok = pl.BlockSpec((1, tm, D), lambda b, i: (b, i, 0))
    return pl.pallas_call(
        functools.partial(_combine_kernel, final=final),
        grid=(B, n_tiles),
        in_specs=[tok, tok, tok,
                  pl.BlockSpec((1, 1, N_MOD * D), lambda b, i: (2 * b + jnp.where(i >= n_lat_tiles, 1, 0), 0, 0)),
                  pl.BlockSpec((1, D), lambda b, i: (0, 0))],
        out_specs=tok,
        out_shape=jax.ShapeDtypeStruct((B, n_tiles * tm, D), F32),
        compiler_params=_params(2),
        name="moe_combine",
    )(xs, y0.reshape(B, Lt, D), y1.reshape(B, Lt, D), modv, final_g.reshape(1, D))


S5_STATES = S5_GROUPS * S5_STATE
S5_CHUNK = 64


def _s5_kernel(u_ref, wd_ref, wr_ref, ar_ref, ai_ref, d_ref, y_ref, x_scr, h_scr, hr_scr, hi_scr, *, reverse):
    ns = S5_STATES

    @pl.when(pl.program_id(0) == 0)
    def _():
        hr_scr[...] = jnp.zeros_like(hr_scr)
        hi_scr[...] = jnp.zeros_like(hi_scr)

    u = u_ref[...]
    x_scr[...] = jnp.dot(u.astype(BF16), wd_ref[...], preferred_element_type=F32)
    ar = ar_ref[...]
    ai = ai_ref[...]

    def step(hr, hi, t):
        r = pl.multiple_of(t * SUBLANES, SUBLANES)
        xr = x_scr[pl.ds(r, SUBLANES), :ns]
        xi = x_scr[pl.ds(r, SUBLANES), ns:]
        return ar * hr - ai * hi + xr, ar * hi + ai * hr + xi

    def body(j, carry):
        hr, hi = carry
        t0 = (S5_CHUNK - 1 - 2 * j) if reverse else 2 * j
        t1 = t0 - 1 if reverse else t0 + 1
        hr0, hi0 = step(hr, hi, t0)
        hr1, hi1 = step(hr0, hi0, t1)
        lo = t1 if reverse else t0
        first_r, second_r = (hr1, hr0) if reverse else (hr0, hr1)
        first_i, second_i = (hi1, hi0) if reverse else (hi0, hi1)
        r = pl.multiple_of(lo * SUBLANES, 2 * SUBLANES)
        h_scr[pl.ds(r, 2 * SUBLANES), :ns] = jnp.concatenate([first_r, second_r], axis=0).astype(BF16)
        h_scr[pl.ds(r, 2 * SUBLANES), ns:] = jnp.concatenate([first_i, second_i], axis=0).astype(BF16)
        return hr1, hi1

    hr, hi = lax.fori_loop(0, S5_CHUNK // 2, body, (hr_scr[...], hi_scr[...]))
    hr_scr[...] = hr
    hi_scr[...] = hi
    y = jnp.dot(h_scr[...], wr_ref[...], preferred_element_type=F32)
    y = y + (d_ref[...].astype(F32) if reverse else u * d_ref[...])
    y_ref[...] = y.astype(y_ref.dtype)


def pl_s5_scan(u_tb, w_drive, w_read, a_re, a_im, addend, *, n_lat_steps, reverse):
    rows, ch = u_tb.shape
    rc = S5_CHUNK * SUBLANES
    n_chunks = rows // rc
    n_lat = n_lat_steps // S5_CHUNK
    n_ctx = n_chunks - n_lat
    assert rows % rc == 0 and n_lat_steps % S5_CHUNK == 0
    if reverse:
        def idx(i):
            return (n_chunks - 1 - i, 0)
    else:
        def idx(i):
            return (jnp.where(i < n_ctx, n_lat + i, i - n_ctx), 0)
    const = lambda i: (0, 0)
    ns2 = 2 * S5_STATES
    return pl.pallas_call(
        functools.partial(_s5_kernel, reverse=reverse),
        grid=(n_chunks,),
        in_specs=[pl.BlockSpec((rc, ch), idx),
                  pl.BlockSpec((ch, ns2), const),
                  pl.BlockSpec((ns2, ch), const),
                  pl.BlockSpec((SUBLANES, S5_STATES), const),
                  pl.BlockSpec((SUBLANES, S5_STATES), const),
                  pl.BlockSpec((rc, ch), idx) if reverse else pl.BlockSpec((1, ch), const)],
        out_specs=pl.BlockSpec((rc, ch), idx),
        out_shape=jax.ShapeDtypeStruct((rows, ch), BF16),
        scratch_shapes=[pltpu.VMEM((rc, ns2), F32), pltpu.VMEM((rc, ns2), BF16),
                        pltpu.VMEM((SUBLANES, S5_STATES), F32), pltpu.VMEM((SUBLANES, S5_STATES), F32)],
        compiler_params=_params(1),
        name="s5_scan_rev" if reverse else "s5_scan_fwd",
    )(u_tb, w_drive, w_read,
      jnp.broadcast_to(a_re[None, :], (SUBLANES, S5_STATES)),
      jnp.broadcast_to(a_im[None, :], (SUBLANES, S5_STATES)),
      addend if reverse else addend.reshape(1, ch))


FFT_N2 = 128


def _fft_tables(L):
    N = 2 * L
    N1 = N // FFT_N2
    k1 = np.arange(N1)[:, None]
    n1 = np.arange(N1 // 2)[None, :]
    n2 = np.arange(FFT_N2)[:, None, None]
    ang = -2.0 * np.pi * (k1[None] * (n2 + FFT_N2 * n1[None])) / N
    mr, mi = np.cos(ang), np.sin(ang)
    ma = np.concatenate([np.concatenate([mr, -mi], axis=2), np.concatenate([mi, mr], axis=2)], axis=1)
    gr, gi = np.transpose(mr, (0, 2, 1)), -np.transpose(mi, (0, 2, 1))
    mainv = np.concatenate([np.concatenate([gr, -gi], axis=2), np.concatenate([gi, gr], axis=2)], axis=1)
    kk = np.arange(FFT_N2)
    a2 = -2.0 * np.pi * np.outer(kk, kk) / FFT_N2
    fr, fi = np.cos(a2), np.sin(a2)
    f_fwd = np.block([[fr, -fi], [fi, fr]])
    f_inv = np.block([[fr, fi], [-fi, fr]])
    return (jnp.asarray(ma, BF16), jnp.asarray(mainv, BF16), jnp.asarray(f_fwd, BF16), jnp.asarray(f_inv, BF16))


HYENA_VMEM_LIMIT = 56 * 1024 * 1024
FFT_UNROLL = 8


def _hyena_fft_kernel(z_ref, h_ref, ma_ref, mainv_ref, ff_ref, fi_ref, o_ref, a_scr, b_scr, *, n1_count):
    half = n1_count // 2
    n2c = FFT_N2

    def stage_a(n2, c):
        xr = z_ref[0, pl.ds(n2, half, stride=n2c), :]
        xi = z_ref[1, pl.ds(n2, half, stride=n2c), :]
        x = jnp.concatenate([xr, xi], axis=0).astype(BF16)
        r = jnp.dot(ma_ref[n2], x, preferred_element_type=F32)
        a_scr[pl.ds(pl.multiple_of(n2 * 2 * n1_count, 2 * n1_count), 2 * n1_count), :] = r
        return c

    lax.fori_loop(0, n2c, stage_a, 0, unroll=FFT_UNROLL)

    def stage_c(k1, c):
        ar = a_scr[pl.ds(k1, n2c, stride=2 * n1_count), :]
        ai = a_scr[pl.ds(n1_count + k1, n2c, stride=2 * n1_count), :]
        x = jnp.concatenate([ar, ai], axis=0).astype(BF16)
        y = jnp.dot(ff_ref[...], x, preferred_element_type=F32)
        yr, yi = y[:n2c], y[n2c:]
        hr = h_ref[0, k1].astype(F32)
        hi = h_ref[1, k1].astype(F32)
        x2 = jnp.concatenate([yr * hr - yi * hi, yr * hi + yi * hr], axis=0).astype(BF16)
        b = jnp.dot(fi_ref[...], x2, preferred_element_type=F32)
        b_scr[pl.ds(pl.multiple_of(k1 * 2 * n2c, 2 * n2c), 2 * n2c), :] = b
        return c

    lax.fori_loop(0, n1_count, stage_c, 0, unroll=FFT_UNROLL)

    def stage_a_inv(n2, c):
        br = b_scr[pl.ds(n2, n1_count, stride=2 * n2c), :]
        bi = b_scr[pl.ds(n2c + n2, n1_count, stride=2 * n2c), :]
        x = jnp.concatenate([br, bi], axis=0).astype(BF16)
        r = jnp.dot(mainv_ref[n2], x, preferred_element_type=F32)
        o_ref[0, pl.ds(n2, half, stride=n2c), :] = r[:half]
        o_ref[1, pl.ds(n2, half, stride=n2c), :] = r[half:]
        return c

    lax.fori_loop(0, n2c, stage_a_inv, 0, unroll=FFT_UNROLL)


def hyena_spectrum(filt):
    N, C = filt.shape
    N1 = N // FFT_N2
    h2 = (jnp.fft.fft(filt, axis=0) / N).reshape(FFT_N2, N1, C).transpose(1, 0, 2)
    return jnp.stack([h2.real, h2.imag]).astype(BF16)


def pl_hyena_conv(z, h, dft_tables, L):
    B, _, C = z.shape
    N1 = 2 * L // FFT_N2
    ma, mainv, f_fwd, f_inv = dft_tables
    cw = LANES
    full = lambda arr: pl.BlockSpec(arr.shape, lambda ct, bp: (0,) * arr.ndim)
    return pl.pallas_call(
        functools.partial(_hyena_fft_kernel, n1_count=N1),
        grid=(C // cw, B // 2),
        in_specs=[pl.BlockSpec((2, L, cw), lambda ct, bp: (bp, 0, ct)),
                  pl.BlockSpec((2, N1, FFT_N2, cw), lambda ct, bp: (0, 0, 0, ct)),
                  full(ma), full(mainv), full(f_fwd), full(f_inv)],
        out_specs=pl.BlockSpec((2, L, cw), lambda ct, bp: (bp, 0, ct)),
        out_shape=jax.ShapeDtypeStruct((B, L, C), F32),
        scratch_shapes=[pltpu.VMEM((FFT_N2 * 2 * N1, cw), F32), pltpu.VMEM((N1 * 2 * FFT_N2, cw), F32)],
        compiler_params=_params(2, HYENA_VMEM_LIMIT),
        name="hyena_fft_conv",
    )(z, h, ma, mainv, f_fwd, f_inv)


def hyena_filter(L, w1, b1, w2, b2, w3, freq):
    t = jnp.arange(L, dtype=F32) / L
    ang = (2.0 * math.pi) * t[:, None] * jnp.arange(1, HY_BANDS + 1, dtype=F32)
    feat = jnp.concatenate([t[:, None], jnp.cos(ang), jnp.sin(ang)], axis=-1)
    hp = lax.Precision.HIGHEST
    h = jnp.sin(freq * (jnp.dot(feat, w1, precision=hp) + b1))
    h = jnp.sin(freq * (jnp.dot(h, w2, precision=hp) + b2))
    h = jnp.dot(h, w3, precision=hp).reshape(L, 2, HY_CH)
    window = jnp.exp(-t[:, None] * jnp.linspace(HY_DECAY_MIN, HY_DECAY_MAX, HY_CH, dtype=F32))
    h = h * window[:, None, :]
    filt = jnp.concatenate([h[:, 0], jnp.zeros((1, HY_CH), F32), h[:0:-1, 1]], axis=0)
    return filt / (jnp.sum(jnp.abs(filt), axis=0, keepdims=True) + EPS)


def hyena_conv(z, h_lat, ff_ctx, dft_tables, L):
    Lc = z.shape[1] - L
    y_lat = pl_hyena_conv(z, h_lat, dft_tables, L)
    zf = jnp.fft.rfft(z[:, L:], n=2 * Lc, axis=1)
    y_ctx = jnp.fft.irfft(zf * ff_ctx[None], n=2 * Lc, axis=1)[:, :Lc]
    return y_lat, y_ctx


def _block_diag(blocks):
    G, r, c = blocks.shape
    eye = jnp.eye(G, dtype=blocks.dtype)
    return (eye[:, None, :, None] * blocks[:, :, None, :]).reshape(G * r, G * c)


def s5_tables(a_re, a_im, log_dt, b_re, b_im, c_re, c_im):
    A = lax.complex(a_re, a_im)
    dtA = jnp.exp(log_dt)[:, None] * A
    a_bar = jnp.exp(dtA)
    b_bar = ((a_bar - 1.0) / A)[:, :, None] * lax.complex(b_re, b_im)
    bt_re = jnp.transpose(b_bar.real, (0, 2, 1))
    bt_im = jnp.transpose(b_bar.imag, (0, 2, 1))
    w_drive = jnp.concatenate([_block_diag(bt_re), _block_diag(bt_im)], axis=1)
    ct_re = jnp.transpose(c_re, (0, 2, 1))
    ct_im = jnp.transpose(c_im, (0, 2, 1))
    w_read = jnp.concatenate([_block_diag(ct_re), -_block_diag(ct_im)], axis=0)
    return w_drive.astype(BF16), w_read.astype(BF16), a_bar.real.reshape(-1), a_bar.imag.reshape(-1)


def s5_scan(u_tb, tables, d_skip, n_lat_steps):
    y = d_skip
    for direction in range(2):
        w_drive, w_read, a_re, a_im = (t[direction] for t in tables)
        y = pl_s5_scan(u_tb, w_drive, w_read, a_re, a_im, y, n_lat_steps=n_lat_steps, reverse=direction == 1)
    return y


def rope_tables(L, Lc):
    rows = L // GRID_W
    row = jnp.repeat(jnp.arange(rows, dtype=F32), GRID_W)
    col = jnp.tile(jnp.arange(GRID_W, dtype=F32), rows)
    inv = ROPE_BASE ** (-jnp.arange(ROPE_PAIRS_AXIS, dtype=F32) / ROPE_PAIRS_AXIS)
    ang = jnp.concatenate([row[:, None] * inv, col[:, None] * inv], axis=-1)
    cos, sin = jnp.cos(ang), jnp.sin(ang)
    n_maps = ATT_W // ATT_HEAD_DIM
    cosf = jnp.tile(jnp.concatenate([cos, cos], axis=-1), (1, n_maps))
    sinf = jnp.tile(jnp.concatenate([-sin, sin], axis=-1), (1, n_maps))
    return (jnp.concatenate([cosf, jnp.ones((Lc, ATT_W), F32)], axis=0),
            jnp.concatenate([sinf, jnp.zeros((Lc, ATT_W), F32)], axis=0))


def moe_dispatch(route, counts_row):
    T = route.shape[0]
    expert = route[:, 0:MOE_TOP_K].astype(jnp.int32).reshape(-1)
    gate = route[:, MOE_TOP_K:2 * MOE_TOP_K]
    rank = route[:, 2 * MOE_TOP_K:3 * MOE_TOP_K].astype(jnp.int32).reshape(-1)
    counts = counts_row[MOE_GROUPS:MOE_GROUPS + N_EXPERTS].astype(jnp.int32)
    tok = jnp.repeat(jnp.arange(T, dtype=jnp.int32), MOE_TOP_K)
    n_assign = T * MOE_TOP_K
    n_blocks = -(-n_assign // MOE_BLOCK) + N_EXPERTS
    n_pad = n_blocks * MOE_BLOCK
    order = jnp.argsort(expert).astype(jnp.int32)
    start = jnp.cumsum(counts) - counts
    padded = (counts + MOE_BLOCK - 1) // MOE_BLOCK * MOE_BLOCK
    pad_end = jnp.cumsum(padded)
    pad_start = pad_end - padded
    slot_of_assign = (pad_start[expert] + rank).astype(jnp.int32)
    block_first = jnp.arange(n_blocks, dtype=jnp.int32) * MOE_BLOCK
    block_e = jnp.minimum(jnp.sum((pad_end[None, :] <= block_first[:, None]).astype(jnp.int32), axis=1),
                          N_EXPERTS - 1).astype(jnp.int32)
    slot_e = jnp.repeat(block_e, MOE_BLOCK)
    slot_r = jnp.arange(n_pad, dtype=jnp.int32) - pad_start[slot_e]
    slot_valid = (slot_r < counts[slot_e]) & (jnp.arange(n_pad) < pad_end[-1])
    slot_assign = order[jnp.clip(start[slot_e] + slot_r, 0, n_assign - 1)]
    slot_tok = jnp.where(slot_valid, tok[slot_assign], jnp.arange(n_pad, dtype=jnp.int32) % T)
    slot_gate = jnp.where(slot_valid, gate.reshape(-1)[slot_assign], 0.0)
    n_used = (pad_end[-1:] // MOE_BLOCK).astype(jnp.int32)
    return slot_tok, slot_gate, slot_of_assign, block_e, n_used


def hier_moe(h2, route, counts_row, w1, w3, w2, layer):
    T, D = h2.shape
    slot_tok, slot_gate, slot_of_assign, block_e, n_used = moe_dispatch(route, counts_row)
    yb = pl_moe_ffn(h2[slot_tok], slot_gate, block_e, n_used, w1, w3, w2, layer)
    slots = slot_of_assign.reshape(T, MOE_TOP_K)
    return yb[slots[:, 0]], yb[slots[:, 1]]


def kernel(x, c, ctx, c_ctx, w_mod, b_mod, norm1_g, norm2_g, final_g, w_in, w_out, hy_conv_w, hy_conv_b, hy_ffn_w1, hy_ffn_b1, hy_ffn_w2, hy_ffn_b2, hy_ffn_w3, hy_freq, hy_skip, hy_norm_g, s5_a_re, s5_a_im, s5_log_dt, s5_b_re, s5_b_im, s5_c_re, s5_c_im, s5_d, s5_glu_w, s5_norm_g, att_lq1, att_lk1, att_lq2, att_lk2, att_subln_g, moe_wg, moe_bg, moe_we, moe_be, moe_w1, moe_w3, moe_w2):
    B, L, D = x.shape
    Lc = ctx.shape[1]
    Lt = L + Lc
    assert B == SUBLANES and Lc == TOKEN_TILE and L % ATT_TQ == 0
    n_lat_tiles = L // TOKEN_TILE
    cosf, sinf = rope_tables(L, Lc)
    hp = lax.Precision.HIGHEST
    q_scale = ATT_HEAD_DIM ** -0.5 * math.log2(math.e)

    mods = jnp.einsum('bd,ldk->lbk', jax.nn.silu(c), w_mod, precision=hp) + b_mod[:, None, :]
    cmods = jnp.einsum('d,ldk->lk', jax.nn.silu(c_ctx), w_mod, precision=hp) + b_mod
    modvs = jnp.stack([mods, jnp.broadcast_to(cmods[:, None, :], mods.shape)], axis=2)
    modvs = modvs.reshape(DEPTH, 2 * B, 1, N_MOD * D)
    filt_args = (hy_ffn_w1, hy_ffn_b1, hy_ffn_w2, hy_ffn_b2, hy_ffn_w3, hy_freq)
    h_lat = jax.vmap(lambda *p: hyena_spectrum(hyena_filter(L, *p)))(*filt_args)
    ff_ctx = jax.vmap(lambda *p: jnp.fft.rfft(hyena_filter(Lc, *p), n=2 * Lc, axis=0))(*filt_args)
    dft_tables = _fft_tables(L)
    s5_tabs = jax.vmap(jax.vmap(s5_tables))(s5_a_re, s5_a_im, s5_log_dt, s5_b_re, s5_b_im, s5_c_re, s5_c_im)
    lam_inits = [0.8 - 0.6 * math.exp(-0.3 * l) for l in range(DEPTH)]
    lams = (jnp.exp(jnp.sum(att_lq1 * att_lk1, axis=-1)) - jnp.exp(jnp.sum(att_lq2 * att_lk2, axis=-1))
            + jnp.asarray(lam_inits, F32))
    w_router = jnp.zeros((DEPTH, D, LANES), F32).at[:, :, :MOE_GROUPS].set(moe_wg)
    w_router = w_router.at[:, :, MOE_GROUPS:MOE_GROUPS + N_EXPERTS].set(moe_we)
    rwh = w_router.astype(BF16)
    rwl = (w_router - rwh.astype(F32)).astype(BF16)
    router_bias = jnp.zeros((DEPTH, 1, LANES), F32).at[:, 0, :MOE_GROUPS].set(moe_bg)
    router_bias = router_bias.at[:, 0, MOE_GROUPS:MOE_GROUPS + N_EXPERTS].set(moe_be)
    w_in_b, w_out_b, glu_w_b = w_in.astype(BF16), w_out.astype(BF16), s5_glu_w.astype(BF16)

    xs = jnp.concatenate([x, ctx], axis=1)
    for l in range(DEPTH):
        modv = modvs[l]
        u0, z, s5_u, q_p, q_r, k_r, v = pl_in_proj(xs, modv, norm1_g[l], w_in_b[l], hy_conv_w[l], hy_conv_b[l],
                                                   cosf, sinf, n_lat_tiles, q_scale)

        y_lat, y_ctx = hyena_conv(z, h_lat[l], ff_ctx[l], dft_tables, L)

        s5_y = s5_scan(s5_u.reshape(Lt * B, S5_CH), tuple(t[l] for t in s5_tabs), s5_d[l], L)

        g_scaled = (att_subln_g[l] * (1.0 - lam_inits[l])).reshape(1, ATT_V_DIM)
        att_lat, att_ctx = pl_diff_attention(q_p, q_r, k_r, v, g_scaled, lams[l], L)

        xs, h2, route, counts = pl_out_proj(xs, u0, z, y_lat, y_ctx, s5_y.reshape(Lt, B * S5_CH), att_lat, att_ctx,
                                            modv, hy_skip[l], hy_norm_g[l], glu_w_b[l], s5_norm_g[l], w_out_b[l],
                                            norm2_g[l], rwh[l], rwl[l], router_bias[l], n_lat_tiles)

        y0, y1 = hier_moe(h2.reshape(B * Lt, D), route.reshape(B * Lt, LANES), counts[0], moe_w1, moe_w3, moe_w2, l)
        xs = pl_moe_combine(xs, y0, y1, modv, final_g, n_lat_tiles, final=l == DEPTH - 1)
    return xs
```

```python
import functools
import math

import jax
import jax.numpy as jnp
import numpy as np
from jax import lax
from jax.experimental import pallas as pl
from jax.experimental.pallas import tpu as pltpu

D_MODEL = 1024
DEPTH = 4
GRID_W = 64
N_MOD = 6
EPS = 1e-6
HY_CH = D_MODEL // 4
S5_CH = D_MODEL // 4
ATT_W = D_MODEL // 2
HY_BANDS = 16
HY_DECAY_MIN = -math.log(1e-2) / 1.5
HY_DECAY_MAX = -math.log(1e-2) / 0.3
S5_GROUP = 16
S5_GROUPS = S5_CH // S5_GROUP
S5_STATE = 64
ATT_HEAD_DIM = 64
ATT_HEADS = ATT_W // (2 * ATT_HEAD_DIM)
ATT_V_DIM = 2 * ATT_HEAD_DIM
ROPE_HALF = ATT_HEAD_DIM // 2
ROPE_PAIRS_AXIS = ROPE_HALF // 2
ROPE_BASE = 10000.0
MOE_GROUPS = 4
MOE_EPG = 8
N_EXPERTS = MOE_GROUPS * MOE_EPG
MOE_TOP_K = 2
MOE_BLOCK = 512
IN_COLS = 3 * HY_CH + S5_CH + 3 * ATT_W
COL_S5 = 3 * HY_CH
COL_Q = COL_S5 + S5_CH
COL_K = COL_Q + ATT_W
COL_V = COL_K + ATT_W

LANES = 128
SUBLANES = 8
VMEM_LIMIT = 48 * 1024 * 1024
TOKEN_TILE = 256

F32 = jnp.float32
BF16 = jnp.bfloat16


def _params(n_axes, vmem=VMEM_LIMIT):
    return pltpu.CompilerParams(dimension_semantics=("arbitrary",) * n_axes, vmem_limit_bytes=vmem)


def _rms(x):
    return x * lax.rsqrt(jnp.mean(x * x, axis=-1, keepdims=True) + EPS)


def _in_kernel(x_ref, xp_ref, xn_ref, mod_ref, g_ref, w_ref, cw_ref, cb_ref, cos_ref, sin_ref,
               u0_ref, z_ref, s5_ref, qp_ref, qr_ref, k_ref, v_ref, *, n_lat_tiles, q_scale):
    i = pl.program_id(1)
    tm = x_ref.shape[1]
    d = x_ref.shape[2]
    g = g_ref[...]
    shift = mod_ref[0, :, 0:d]
    scale = mod_ref[0, :, d:2 * d]

    def norm_mod(xt):
        return (_rms(xt) * g) * (1.0 + scale) + shift

    h = jnp.concatenate([norm_mod(x_ref[0]), norm_mod(xp_ref[0]), norm_mod(xn_ref[0])], axis=0).astype(BF16)
    p = jnp.dot(h, w_ref[...], preferred_element_type=F32)

    hy = p[:tm, :COL_S5]
    is_ctx = i == n_lat_tiles
    has_prev = jnp.logical_and(i != 0, jnp.logical_not(is_ctx))
    has_next = jnp.logical_and(i != n_lat_tiles - 1, jnp.logical_not(is_ctx))
    prev_row = jnp.where(has_prev, p[tm + SUBLANES - 1:tm + SUBLANES, :COL_S5], 0.0)
    next_row = jnp.where(has_next, p[tm + SUBLANES:tm + SUBLANES + 1, :COL_S5], 0.0)
    rows = lax.broadcasted_iota(jnp.int32, (tm, 1), 0)
    up = jnp.where(rows == 0, prev_row, pltpu.roll(hy, 1, axis=0))
    dn = jnp.where(rows == tm - 1, next_row, pltpu.roll(hy, tm - 1, axis=0))
    u = up * cw_ref[0:1, :] + hy * cw_ref[1:2, :] + dn * cw_ref[2:3, :] + cb_ref[...]
    u0_ref[0] = u[:, :HY_CH].astype(u0_ref.dtype)
    z_ref[0] = u[:, HY_CH:2 * HY_CH] * u[:, 2 * HY_CH:]

    s5_ref[...] = p[:tm, COL_S5:COL_Q]

    lane = lax.broadcasted_iota(jnp.int32, (1, ATT_W), 1)
    first_half = jnp.bitwise_and(lane, ATT_HEAD_DIM - 1) < ROPE_HALF
    cos = cos_ref[...]
    sin = sin_ref[...]

    def rope(t):
        partner = jnp.where(first_half, pltpu.roll(t, ATT_W - ROPE_HALF, axis=1), pltpu.roll(t, ROPE_HALF, axis=1))
        return t * cos + partner * sin

    q = p[:tm, COL_Q:COL_K] * q_scale
    qp_ref[0] = q.astype(BF16)
    qr_ref[0] = rope(q).astype(BF16)
    k_ref[0] = rope(p[:tm, COL_K:COL_V]).astype(BF16)
    v_ref[0] = p[:tm, COL_V:].astype(BF16)


def pl_in_proj(xs, modv, norm_g, w_in, conv_w, conv_b, cosf, sinf, n_lat_tiles, q_scale):
    B, Lt, D = xs.shape
    tm = TOKEN_TILE
    n_tiles = Lt // tm
    halo_per_tile = tm // SUBLANES
    n_halo_blocks = Lt // SUBLANES
    tok = lambda w: pl.BlockSpec((1, tm, w), lambda b, i: (b, i, 0))
    const = lambda shape: pl.BlockSpec(shape, lambda b, i: (0,) * len(shape))
    out_shapes = (jax.ShapeDtypeStruct((B, Lt, HY_CH), BF16), jax.ShapeDtypeStruct((B, Lt, HY_CH), F32),
                  jax.ShapeDtypeStruct((Lt, B * S5_CH), F32),
                  jax.ShapeDtypeStruct((B, Lt, ATT_W), BF16), jax.ShapeDtypeStruct((B, Lt, ATT_W), BF16),
                  jax.ShapeDtypeStruct((B, Lt, ATT_W), BF16), jax.ShapeDtypeStruct((B, Lt, ATT_W), BF16))
    return pl.pallas_call(
        functools.partial(_in_kernel, n_lat_tiles=n_lat_tiles, q_scale=q_scale),
        grid=(B, n_tiles),
        in_specs=[tok(D),
                  pl.BlockSpec((1, SUBLANES, D), lambda b, i: (b, jnp.maximum(i * halo_per_tile - 1, 0), 0)),
                  pl.BlockSpec((1, SUBLANES, D),
                               lambda b, i: (b, jnp.minimum((i + 1) * halo_per_tile, n_halo_blocks - 1), 0)),
                  pl.BlockSpec((1, 1, N_MOD * D), lambda b, i: (2 * b + jnp.where(i >= n_lat_tiles, 1, 0), 0, 0)),
                  const((1, D)), const((D, IN_COLS)), const((3, COL_S5)), const((1, COL_S5)),
                  pl.BlockSpec((tm, ATT_W), lambda b, i: (i, 0)), pl.BlockSpec((tm, ATT_W), lambda b, i: (i, 0))],
        out_specs=(tok(HY_CH), tok(HY_CH), pl.BlockSpec((tm, S5_CH), lambda b, i: (i, b)),
                   tok(ATT_W), tok(ATT_W), tok(ATT_W), tok(ATT_W)),
        out_shape=out_shapes,
        compiler_params=_params(2),
        name="in_proj",
    )(xs, xs, xs, modv, norm_g.reshape(1, D), w_in, conv_w, conv_b.reshape(1, COL_S5), cosf, sinf)


def _out_kernel(x_ref, u0_ref, z_ref, yl_ref, yc_ref, s5_ref, al_ref, ac_ref, mod_ref, skip_ref, hg_ref,
                gw_ref, sg_ref, wo_ref, n2g_ref, rwh_ref, rwl_ref, rb_ref, tri_ref, xo_ref, h2_ref, rt_ref, cnt_ref,
                cnt_scr, *, n_lat_tiles):
    d = x_ref.shape[2]
    is_ctx = pl.program_id(1) == n_lat_tiles
    y = jnp.where(is_ctx, yc_ref[0], yl_ref[0])
    att = jnp.where(is_ctx, ac_ref[0], al_ref[0])
    hy = _rms(u0_ref[0].astype(F32) * (y + skip_ref[...] * z_ref[0])) * hg_ref[...]
    gl = jax.nn.gelu(s5_ref[...].astype(F32))
    gate = jax.nn.sigmoid(jnp.dot(gl.astype(BF16), gw_ref[...], preferred_element_type=F32))
    s5 = _rms(gl * gate) * sg_ref[...]
    mix = jnp.concatenate([hy.astype(BF16), s5.astype(BF16), att], axis=1)
    proj = jnp.dot(mix, wo_ref[...], preferred_element_type=F32)
    xn = x_ref[0] + mod_ref[0, :, 2 * d:3 * d] * proj
    xo_ref[0] = xn
    h2 = (_rms(xn) * n2g_ref[...]) * (1.0 + mod_ref[0, :, 4 * d:5 * d]) + mod_ref[0, :, 3 * d:4 * d]
    hh = h2.astype(BF16)
    hl = (h2 - hh.astype(F32)).astype(BF16)
    h2_ref[0] = hh
    lg = jnp.dot(hh, rwh_ref[...], preferred_element_type=F32)
    lg += jnp.dot(hl, rwh_ref[...], preferred_element_type=F32)
    lg += jnp.dot(hh, rwl_ref[...], preferred_element_type=F32)
    lg = lg + rb_ref[...]

    tm = lg.shape[0]
    lane = lax.broadcasted_iota(jnp.int32, (1, LANES), 1)
    neg_inf = jnp.float32(-jnp.inf)

    def first_max(v):
        m = jnp.max(v, axis=1, keepdims=True)
        return m, jnp.min(jnp.where(v == m, lane, LANES), axis=1, keepdims=True)

    g_logit = jnp.where(lane < MOE_GROUPS, lg, neg_inf)
    g_max, g_idx = first_max(g_logit)
    p_group = 1.0 / jnp.sum(jnp.exp(g_logit - g_max), axis=1, keepdims=True)
    e_lane = lane - MOE_GROUPS
    in_group = jnp.logical_and(jnp.logical_and(e_lane >= 0, e_lane < N_EXPERTS),
                               jnp.right_shift(e_lane, MOE_EPG.bit_length() - 1) == g_idx)
    e_logit = jnp.where(in_group, lg, neg_inf)
    e_exp = jnp.exp(e_logit - jnp.max(e_logit, axis=1, keepdims=True))
    probs = jnp.where(in_group, e_exp / jnp.sum(e_exp, axis=1, keepdims=True), -1.0)
    p1, l1 = first_max(probs)
    p2, l2 = first_max(jnp.where(lane == l1, -1.0, probs))
    hit1 = lane == l1
    hit2 = lane == l2
    onehot = jnp.where(hit1, 1.0, 0.0) + jnp.where(hit2, 1.0, 0.0)

    @pl.when(jnp.logical_and(pl.program_id(0) == 0, pl.program_id(1) == 0))
    def _():
        cnt_scr[...] = jnp.zeros_like(cnt_scr)

    earlier = jnp.dot(tri_ref[...], onehot.astype(BF16), preferred_element_type=F32)
    before = cnt_scr[0:1, :] + earlier
    r1 = jnp.sum(jnp.where(hit1, before, 0.0), axis=1, keepdims=True)
    r2 = jnp.sum(jnp.where(hit2, before, 0.0), axis=1, keepdims=True)
    cnt_scr[...] = jnp.broadcast_to(cnt_scr[0:1, :] + jnp.sum(onehot, axis=0, keepdims=True), cnt_scr.shape)
    cnt_ref[...] = cnt_scr[...]
    cols = [(l1 - MOE_GROUPS).astype(F32), (l2 - MOE_GROUPS).astype(F32), p_group * p1 / (p1 + p2),
            p_group * p2 / (p1 + p2), r1, r2]
    route = jnp.zeros((tm, LANES), F32)
    for j, col in enumerate(cols):
        route = jnp.where(lane == j, col, route)
    rt_ref[0] = route


def pl_out_proj(xs, u0, z, y_lat, y_ctx, s5_y, att_lat, att_ctx, modv, hy_skip, hy_norm_g, glu_w, s5_norm_g,
                w_out, norm2_g, rwh, rwl, router_bias, n_lat_tiles):
    B, Lt, D = xs.shape
    tm = TOKEN_TILE
    tri = jnp.asarray(np.tril(np.ones((tm, tm), np.float32), -1), BF16)
    tok = lambda w: pl.BlockSpec((1, tm, w), lambda b, i: (b, i, 0))
    lat_tok = lambda w: pl.BlockSpec((1, tm, w), lambda b, i: (b, jnp.minimum(i, n_lat_tiles - 1), 0))
    ctx_tok = lambda w: pl.BlockSpec((1, tm, w), lambda b, i: (b, 0, 0))
    tb = pl.BlockSpec((tm, S5_CH), lambda b, i: (i, b))
    const = lambda shape: pl.BlockSpec(shape, lambda b, i: (0,) * len(shape))
    return pl.pallas_call(
        functools.partial(_out_kernel, n_lat_tiles=n_lat_tiles),
        grid=(B, Lt // tm),
        in_specs=[tok(D), tok(HY_CH), tok(HY_CH), lat_tok(HY_CH), ctx_tok(HY_CH), tb, lat_tok(ATT_W),
                  ctx_tok(ATT_W),
                  pl.BlockSpec((1, 1, N_MOD * D), lambda b, i: (2 * b + jnp.where(i >= n_lat_tiles, 1, 0), 0, 0)),
                  const((1, HY_CH)), const((1, HY_CH)), const((S5_CH, S5_CH)), const((1, S5_CH)),
                  const((D, D)), const((1, D)), const((D, LANES)), const((D, LANES)), const((1, LANES)),
                  const((tm, tm))],
        out_specs=(tok(D), tok(D), tok(LANES), const((SUBLANES, LANES))),
        out_shape=(jax.ShapeDtypeStruct((B, Lt, D), F32), jax.ShapeDtypeStruct((B, Lt, D), BF16),
                   jax.ShapeDtypeStruct((B, Lt, LANES), F32), jax.ShapeDtypeStruct((SUBLANES, LANES), F32)),
        scratch_shapes=[pltpu.VMEM((SUBLANES, LANES), F32)],
        compiler_params=_params(2),
        name="out_proj",
    )(xs, u0, z, y_lat, y_ctx, s5_y, att_lat, att_ctx, modv, hy_skip.reshape(1, HY_CH),
      hy_norm_g.reshape(1, HY_CH), glu_w, s5_norm_g.reshape(1, S5_CH), w_out, norm2_g.reshape(1, D), rwh, rwl,
      router_bias, tri)


def _dot_nt(a, b):
    return lax.dot_general(a, b, (((1,), (1,)), ((), ())), preferred_element_type=F32)


ATT_TQ = 512
ATT_SUB = 256


def _attn_kernel(qp_ref, qr_ref, k_ref, v_ref, g_ref, lam_ref, o_ref, *, n_lat):
    tq = qp_ref.shape[1]
    first_map = lax.broadcasted_iota(jnp.int32, (1, LANES), 1) < ATT_HEAD_DIM
    zero = jnp.zeros((), BF16)
    sub = min(ATT_SUB, tq)
    for r0 in range(0, tq, sub):
        qp = qp_ref[0, r0:r0 + sub, :]
        qr = qr_ref[0, r0:r0 + sub, :]
        probs = []
        for m in range(2):
            in_map = first_map if m == 0 else jnp.logical_not(first_map)
            s_c = _dot_nt(jnp.where(in_map, qp, zero), k_ref[0, n_lat:, :])
            mx = jnp.max(s_c, axis=-1, keepdims=True)
            if n_lat:
                s_l = _dot_nt(jnp.where(in_map, qr, zero), k_ref[0, :n_lat, :])
                mx = jnp.maximum(mx, jnp.max(s_l, axis=-1, keepdims=True))
                p_l = jnp.exp2(s_l - mx)
            p_c = jnp.exp2(s_c - mx)
            den = jnp.sum(p_c, axis=-1, keepdims=True)
            if n_lat:
                den = den + jnp.sum(p_l, axis=-1, keepdims=True)
            probs.append((p_c.astype(BF16), p_l.astype(BF16) if n_lat else None, 1.0 / den))
        w0 = probs[0][2].astype(BF16)
        w1 = (lam_ref[0:1, 0:1] * probs[1][2]).astype(BF16)
        a_c = probs[0][0] * w0 - probs[1][0] * w1
        o = jnp.dot(a_c, v_ref[0, n_lat:, :], preferred_element_type=F32)
        if n_lat:
            a_l = probs[0][1] * w0 - probs[1][1] * w1
            o = o + jnp.dot(a_l, v_ref[0, :n_lat, :], preferred_element_type=F32)
        o_ref[0, r0:r0 + sub, :] = (_rms(o) * g_ref[...]).astype(o_ref.dtype)


def pl_diff_attention(qp, qr, k, v, g_scaled, lam, n_lat):
    B, Lt, _ = qp.shape
    n_ctx = Lt - n_lat
    lam_arr = jnp.full((SUBLANES, LANES), lam, F32)
    small = [pl.BlockSpec((1, LANES), lambda b, h, i: (0, 0)), pl.BlockSpec((SUBLANES, LANES), lambda b, h, i: (0, 0))]
    tq = ATT_TQ
    qspec = pl.BlockSpec((1, tq, LANES), lambda b, h, i: (b, i, h))
    kspec = pl.BlockSpec((1, Lt, LANES), lambda b, h, i: (b, 0, h))
    out_lat = pl.pallas_call(
        functools.partial(_attn_kernel, n_lat=n_lat),
        grid=(B, ATT_HEADS, n_lat // tq),
        in_specs=[qspec, qspec, kspec, kspec] + small,
        out_specs=qspec,
        out_shape=jax.ShapeDtypeStruct((B, n_lat, ATT_W), BF16),
        compiler_params=_params(3),
        name="diff_attention",
    )(qp, qr, k, v, g_scaled, lam_arr)
    ctx_blk = n_lat // n_ctx
    cspec = pl.BlockSpec((1, n_ctx, LANES), lambda b, h, i: (b, ctx_blk, h))
    out_ctx = pl.pallas_call(
        functools.partial(_attn_kernel, n_lat=0),
        grid=(B, ATT_HEADS, 1),
        in_specs=[cspec, cspec, cspec, cspec] + small,
        out_specs=pl.BlockSpec((1, n_ctx, LANES), lambda b, h, i: (b, 0, h)),
        out_shape=jax.ShapeDtypeStruct((B, n_ctx, ATT_W), BF16),
        compiler_params=_params(3),
        name="diff_attention_ctx",
    )(qp, qr, k, v, g_scaled, lam_arr)
    return out_lat, out_ctx


def _moe_kernel(be_ref, nb_ref, x_ref, gate_ref, w1_ref, w3_ref, w2_ref, o_ref, w1_scr, w3_scr, w2_scr):
    i = pl.program_id(0)
    new_expert = jnp.logical_or(i == 0, be_ref[i] != be_ref[jnp.maximum(i - 1, 0)])

    @pl.when(jnp.logical_and(i < nb_ref[0], new_expert))
    def _():
        w1_scr[...] = w1_ref[0, 0].astype(BF16)
        w3_scr[...] = w3_ref[0, 0].astype(BF16)
        w2_scr[...] = w2_ref[0, 0].astype(BF16)

    @pl.when(i < nb_ref[0])
    def _():
        x = x_ref[...]
        a = jnp.dot(x, w1_scr[...], preferred_element_type=F32)
        b = jnp.dot(x, w3_scr[...], preferred_element_type=F32)
        h = (a * jax.nn.sigmoid(a)) * b
        y = jnp.dot(h.astype(BF16), w2_scr[...], preferred_element_type=F32)
        rows = lax.broadcasted_iota(jnp.int32, (MOE_BLOCK, 1), 0)
        lane = lax.broadcasted_iota(jnp.int32, (1, LANES), 1)
        g_rows = gate_ref[0, 0:1, :]
        for r in range(1, MOE_BLOCK // LANES):
            g_rows = jnp.where(rows >= r * LANES, gate_ref[0, r:r + 1, :], g_rows)
        g_col = jnp.sum(jnp.where(lane == jnp.bitwise_and(rows, LANES - 1), g_rows, 0.0), axis=1, keepdims=True)
        o_ref[...] = (y * g_col).astype(o_ref.dtype)

    @pl.when(i >= nb_ref[0])
    def _():
        o_ref[...] = jnp.zeros_like(o_ref)


def pl_moe_ffn(xb, slot_gate, block_e, n_used, w1, w3, w2, layer):
    n_pad, D = xb.shape
    n_blocks = n_pad // MOE_BLOCK
    F = w1.shape[-1]
    grid_spec = pltpu.PrefetchScalarGridSpec(
        num_scalar_prefetch=2,
        grid=(n_blocks,),
        in_specs=[pl.BlockSpec((MOE_BLOCK, D), lambda i, be, nb: (i, 0)),
                  pl.BlockSpec((1, MOE_BLOCK // LANES, LANES), lambda i, be, nb: (i, 0, 0)),
                  pl.BlockSpec((1, 1, D, F), lambda i, be, nb: (layer, be[i], 0, 0)),
                  pl.BlockSpec((1, 1, D, F), lambda i, be, nb: (layer, be[i], 0, 0)),
                  pl.BlockSpec((1, 1, F, D), lambda i, be, nb: (layer, be[i], 0, 0))],
        out_specs=pl.BlockSpec((MOE_BLOCK, D), lambda i, be, nb: (i, 0)),
        scratch_shapes=[pltpu.VMEM((D, F), BF16), pltpu.VMEM((D, F), BF16), pltpu.VMEM((F, D), BF16)],
    )
    return pl.pallas_call(
        _moe_kernel,
        grid_spec=grid_spec,
        out_shape=jax.ShapeDtypeStruct((n_pad, D), BF16),
        compiler_params=_params(1),
        name="moe_ffn",
    )(block_e, n_used, xb, slot_gate.reshape(n_blocks, MOE_BLOCK // LANES, LANES), w1, w3, w2)


def _combine_kernel(x_ref, y0_ref, y1_ref, mod_ref, g_ref, o_ref, *, final):
    d = x_ref.shape[2]
    xn = x_ref[0] + mod_ref[0, :, 5 * d:6 * d] * (y0_ref[0].astype(F32) + y1_ref[0].astype(F32))
    o_ref[0] = _rms(xn) * g_ref[...] if final else xn


def pl_moe_combine(xs, y0, y1, modv, final_g, n_lat_tiles, final):
    B, Lt, D = xs.shape
    tm = TOKEN_TILE
    n_tiles = n_lat_tiles if final else Lt // tm
    tok = pl.BlockSpec((1, tm, D), lambda b, i: (b, i, 0))
    return pl.pallas_call(
        functools.partial(_combine_kernel, final=final),
        grid=(B, n_tiles),
        in_specs=[tok, tok, tok,
                  pl.BlockSpec((1, 1, N_MOD * D), lambda b, i: (2 * b + jnp.where(i >= n_lat_tiles, 1, 0), 0, 0)),
                  pl.BlockSpec((1, D), lambda b, i: (0, 0))],
        out_specs=tok,
        out_shape=jax.ShapeDtypeStruct((B, n_tiles * tm, D), F32),
        compiler_params=_params(2),
        name="moe_combine",
    )(xs, y0.reshape(B, Lt, D), y1.reshape(B, Lt, D), modv, final_g.reshape(1, D))


S5_STATES = S5_GROUPS * S5_STATE
S5_CHUNK = 64


def _s5_kernel(u_ref, wd_ref, wr_ref, ar_ref, ai_ref, d_ref, y_ref, x_scr, h_scr, hr_scr, hi_scr, *, reverse):
    ns = S5_STATES

    @pl.when(pl.program_id(0) == 0)
    def _():
        hr_scr[...] = jnp.zeros_like(hr_scr)
        hi_scr[...] = jnp.zeros_like(hi_scr)

    u = u_ref[...]
    x_scr[...] = jnp.dot(u.astype(BF16), wd_ref[...], preferred_element_type=F32)
    ar = ar_ref[...]
    ai = ai_ref[...]

    def step(hr, hi, t):
        r = pl.multiple_of(t * SUBLANES, SUBLANES)
        xr = x_scr[pl.ds(r, SUBLANES), :ns]
        xi = x_scr[pl.ds(r, SUBLANES), ns:]
        return ar * hr - ai * hi + xr, ar * hi + ai * hr + xi

    def body(j, carry):
        hr, hi = carry
        t0 = (S5_CHUNK - 1 - 2 * j) if reverse else 2 * j
        t1 = t0 - 1 if reverse else t0 + 1
        hr0, hi0 = step(hr, hi, t0)
        hr1, hi1 = step(hr0, hi0, t1)
        lo = t1 if reverse else t0
        first_r, second_r = (hr1, hr0) if reverse else (hr0, hr1)
        first_i, second_i = (hi1, hi0) if reverse else (hi0, hi1)
        r = pl.multiple_of(lo * SUBLANES, 2 * SUBLANES)
        h_scr[pl.ds(r, 2 * SUBLANES), :ns] = jnp.concatenate([first_r, second_r], axis=0).astype(BF16)
        h_scr[pl.ds(r, 2 * SUBLANES), ns:] = jnp.concatenate([first_i, second_i], axis=0).astype(BF16)
        return hr1, hi1

    hr, hi = lax.fori_loop(0, S5_CHUNK // 2, body, (hr_scr[...], hi_scr[...]))
    hr_scr[...] = hr
    hi_scr[...] = hi
    y = jnp.dot(h_scr[...], wr_ref[...], preferred_element_type=F32)
    y = y + (d_ref[...].astype(F32) if reverse else u * d_ref[...])
    y_ref[...] = y.astype(y_ref.dtype)


def pl_s5_scan(u_tb, w_drive, w_read, a_re, a_im, addend, *, n_lat_steps, reverse):
    rows, ch = u_tb.shape
    rc = S5_CHUNK * SUBLANES
    n_chunks = rows // rc
    n_lat = n_lat_steps // S5_CHUNK
    n_ctx = n_chunks - n_lat
    assert rows % rc == 0 and n_lat_steps % S5_CHUNK == 0
    if reverse:
        def idx(i):
            return (n_chunks - 1 - i, 0)
    else:
        def idx(i):
            return (jnp.where(i < n_ctx, n_lat + i, i - n_ctx), 0)
    const = lambda i: (0, 0)
    ns2 = 2 * S5_STATES
    return pl.pallas_call(
        functools.partial(_s5_kernel, reverse=reverse),
        grid=(n_chunks,),
        in_specs=[pl.BlockSpec((rc, ch), idx),
                  pl.BlockSpec((ch, ns2), const),
                  pl.BlockSpec((ns2, ch), const),
                  pl.BlockSpec((SUBLANES, S5_STATES), const),
                  pl.BlockSpec((SUBLANES, S5_STATES), const),
                  pl.BlockSpec((rc, ch), idx) if reverse else pl.BlockSpec((1, ch), const)],
        out_specs=pl.BlockSpec((rc, ch), idx),
        out_shape=jax.ShapeDtypeStruct((rows, ch), BF16),
        scratch_shapes=[pltpu.VMEM((rc, ns2), F32), pltpu.VMEM((rc, ns2), BF16),
                        pltpu.VMEM((SUBLANES, S5_STATES), F32), pltpu.VMEM((SUBLANES, S5_STATES), F32)],
        compiler_params=_params(1),
        name="s5_scan_rev" if reverse else "s5_scan_fwd",
    )(u_tb, w_drive, w_read,
      jnp.broadcast_to(a_re[None, :], (SUBLANES, S5_STATES)),
      jnp.broadcast_to(a_im[None, :], (SUBLANES, S5_STATES)),
      addend if reverse else addend.reshape(1, ch))


FFT_N2 = 128


def _fft_tables(L):
    N = 2 * L
    N1 = N // FFT_N2
    k1 = np.arange(N1)[:, None]
    n1 = np.arange(N1 // 2)[None, :]
    n2 = np.arange(FFT_N2)[:, None, None]
    ang = -2.0 * np.pi * (k1[None] * (n2 + FFT_N2 * n1[None])) / N
    mr, mi = np.cos(ang), np.sin(ang)
    ma = np.concatenate([np.concatenate([mr, -mi], axis=2), np.concatenate([mi, mr], axis=2)], axis=1)
    gr, gi = np.transpose(mr, (0, 2, 1)), -np.transpose(mi, (0, 2, 1))
    mainv = np.concatenate([np.concatenate([gr, -gi], axis=2), np.concatenate([gi, gr], axis=2)], axis=1)
    kk = np.arange(FFT_N2)
    a2 = -2.0 * np.pi * np.outer(kk, kk) / FFT_N2
    fr, fi = np.cos(a2), np.sin(a2)
    f_fwd = np.block([[fr, -fi], [fi, fr]])
    f_inv = np.block([[fr, fi], [-fi, fr]])
    return (jnp.asarray(ma, BF16), jnp.asarray(mainv, BF16), jnp.asarray(f_fwd, BF16), jnp.asarray(f_inv, BF16))


HYENA_VMEM_LIMIT = 56 * 1024 * 1024
FFT_UNROLL = 8


def _hyena_fft_kernel(z_ref, h_ref, ma_ref, mainv_ref, ff_ref, fi_ref, o_ref, a_scr, b_scr, *, n1_count):
    half = n1_count // 2
    n2c = FFT_N2

    def stage_a(n2, c):
        xr = z_ref[0, pl.ds(n2, half, stride=n2c), :]
        xi = z_ref[1, pl.ds(n2, half, stride=n2c), :]
        x = jnp.concatenate([xr, xi], axis=0).astype(BF16)
        r = jnp.dot(ma_ref[n2], x, preferred_element_type=F32)
        a_scr[pl.ds(pl.multiple_of(n2 * 2 * n1_count, 2 * n1_count), 2 * n1_count), :] = r
        return c

    lax.fori_loop(0, n2c, stage_a, 0, unroll=FFT_UNROLL)

    def stage_c(k1, c):
        ar = a_scr[pl.ds(k1, n2c, stride=2 * n1_count), :]
        ai = a_scr[pl.ds(n1_count + k1, n2c, stride=2 * n1_count), :]
        x = jnp.concatenate([ar, ai], axis=0).astype(BF16)
        y = jnp.dot(ff_ref[...], x, preferred_element_type=F32)
        yr, yi = y[:n2c], y[n2c:]
        hr = h_ref[0, k1].astype(F32)
        hi = h_ref[1, k1].astype(F32)
        x2 = jnp.concatenate([yr * hr - yi * hi, yr * hi + yi * hr], axis=0).astype(BF16)
        b = jnp.dot(fi_ref[...], x2, preferred_element_type=F32)
        b_scr[pl.ds(pl.multiple_of(k1 * 2 * n2c, 2 * n2c), 2 * n2c), :] = b
        return c

    lax.fori_loop(0, n1_count, stage_c, 0, unroll=FFT_UNROLL)

    def stage_a_inv(n2, c):
        br = b_scr[pl.ds(n2, n1_count, stride=2 * n2c), :]
        bi = b_scr[pl.ds(n2c + n2, n1_count, stride=2 * n2c), :]
        x = jnp.concatenate([br, bi], axis=0).astype(BF16)
        r = jnp.dot(mainv_ref[n2], x, preferred_element_type=F32)
        o_ref[0, pl.ds(n2, half, stride=n2c), :] = r[:half]
        o_ref[1, pl.ds(n2, half, stride=n2c), :] = r[half:]
        return c

    lax.fori_loop(0, n2c, stage_a_inv, 0, unroll=FFT_UNROLL)


def hyena_spectrum(filt):
    N, C = filt.shape
    N1 = N // FFT_N2
    h2 = (jnp.fft.fft(filt, axis=0) / N).reshape(FFT_N2, N1, C).transpose(1, 0, 2)
    return jnp.stack([h2.real, h2.imag]).astype(BF16)


def pl_hyena_conv(z, h, dft_tables, L):
    B, _, C = z.shape
    N1 = 2 * L // FFT_N2
    ma, mainv, f_fwd, f_inv = dft_tables
    cw = LANES
    full = lambda arr: pl.BlockSpec(arr.shape, lambda ct, bp: (0,) * arr.ndim)
    return pl.pallas_call(
        functools.partial(_hyena_fft_kernel, n1_count=N1),
        grid=(C // cw, B // 2),
        in_specs=[pl.BlockSpec((2, L, cw), lambda ct, bp: (bp, 0, ct)),
                  pl.BlockSpec((2, N1, FFT_N2, cw), lambda ct, bp: (0, 0, 0, ct)),
                  full(ma), full(mainv), full(f_fwd), full(f_inv)],
        out_specs=pl.BlockSpec((2, L, cw), lambda ct, bp: (bp, 0, ct)),
        out_shape=jax.ShapeDtypeStruct((B, L, C), F32),
        scratch_shapes=[pltpu.VMEM((FFT_N2 * 2 * N1, cw), F32), pltpu.VMEM((N1 * 2 * FFT_N2, cw), F32)],
        compiler_params=_params(2, HYENA_VMEM_LIMIT),
        name="hyena_fft_conv",
    )(z, h, ma, mainv, f_fwd, f_inv)


def hyena_filter(L, w1, b1, w2, b2, w3, freq):
    t = jnp.arange(L, dtype=F32) / L
    ang = (2.0 * math.pi) * t[:, None] * jnp.arange(1, HY_BANDS + 1, dtype=F32)
    feat = jnp.concatenate([t[:, None], jnp.cos(ang), jnp.sin(ang)], axis=-1)
    hp = lax.Precision.HIGHEST
    h = jnp.sin(freq * (jnp.dot(feat, w1, precision=hp) + b1))
    h = jnp.sin(freq * (jnp.dot(h, w2, precision=hp) + b2))
    h = jnp.dot(h, w3, precision=hp).reshape(L, 2, HY_CH)
    window = jnp.exp(-t[:, None] * jnp.linspace(HY_DECAY_MIN, HY_DECAY_MAX, HY_CH, dtype=F32))
    h = h * window[:, None, :]
    filt = jnp.concatenate([h[:, 0], jnp.zeros((1, HY_CH), F32), h[:0:-1, 1]], axis=0)
    return filt / (jnp.sum(jnp.abs(filt), axis=0, keepdims=True) + EPS)


def hyena_conv(z, h_lat, ff_ctx, dft_tables, L):
    Lc = z.shape[1] - L
    y_lat = pl_hyena_conv(z, h_lat, dft_tables, L)
    zf = jnp.fft.rfft(z[:, L:], n=2 * Lc, axis=1)
    y_ctx = jnp.fft.irfft(zf * ff_ctx[None], n=2 * Lc, axis=1)[:, :Lc]
    return y_lat, y_ctx


def _block_diag(blocks):
    G, r, c = blocks.shape
    eye = jnp.eye(G, dtype=blocks.dtype)
    return (eye[:, None, :, None] * blocks[:, :, None, :]).reshape(G * r, G * c)


def s5_tables(a_re, a_im, log_dt, b_re, b_im, c_re, c_im):
    A = lax.complex(a_re, a_im)
    dtA = jnp.exp(log_dt)[:, None] * A
    a_bar = jnp.exp(dtA)
    b_bar = ((a_bar - 1.0) / A)[:, :, None] * lax.complex(b_re, b_im)
    bt_re = jnp.transpose(b_bar.real, (0, 2, 1))
    bt_im = jnp.transpose(b_bar.imag, (0, 2, 1))
    w_drive = jnp.concatenate([_block_diag(bt_re), _block_diag(bt_im)], axis=1)
    ct_re = jnp.transpose(c_re, (0, 2, 1))
    ct_im = jnp.transpose(c_im, (0, 2, 1))
    w_read = jnp.concatenate([_block_diag(ct_re), -_block_diag(ct_im)], axis=0)
    return w_drive.astype(BF16), w_read.astype(BF16), a_bar.real.reshape(-1), a_bar.imag.reshape(-1)


def s5_scan(u_tb, tables, d_skip, n_lat_steps):
    y = d_skip
    for direction in range(2):
        w_drive, w_read, a_re, a_im = (t[direction] for t in tables)
        y = pl_s5_scan(u_tb, w_drive, w_read, a_re, a_im, y, n_lat_steps=n_lat_steps, reverse=direction == 1)
    return y


def rope_tables(L, Lc):
    rows = L // GRID_W
    row = jnp.repeat(jnp.arange(rows, dtype=F32), GRID_W)
    col = jnp.tile(jnp.arange(GRID_W, dtype=F32), rows)
    inv = ROPE_BASE ** (-jnp.arange(ROPE_PAIRS_AXIS, dtype=F32) / ROPE_PAIRS_AXIS)
    ang = jnp.concatenate([row[:, None] * inv, col[:, None] * inv], axis=-1)
    cos, sin = jnp.cos(ang), jnp.sin(ang)
    n_maps = ATT_W // ATT_HEAD_DIM
    cosf = jnp.tile(jnp.concatenate([cos, cos], axis=-1), (1, n_maps))
    sinf = jnp.tile(jnp.concatenate([-sin, sin], axis=-1), (1, n_maps))
    return (jnp.concatenate([cosf, jnp.ones((Lc, ATT_W), F32)], axis=0),
            jnp.concatenate([sinf, jnp.zeros((Lc, ATT_W), F32)], axis=0))


def moe_dispatch(route, counts_row):
    T = route.shape[0]
    expert = route[:, 0:MOE_TOP_K].astype(jnp.int32).reshape(-1)
    gate = route[:, MOE_TOP_K:2 * MOE_TOP_K]
    rank = route[:, 2 * MOE_TOP_K:3 * MOE_TOP_K].astype(jnp.int32).reshape(-1)
    counts = counts_row[MOE_GROUPS:MOE_GROUPS + N_EXPERTS].astype(jnp.int32)
    tok = jnp.repeat(jnp.arange(T, dtype=jnp.int32), MOE_TOP_K)
    n_assign = T * MOE_TOP_K
    n_blocks = -(-n_assign // MOE_BLOCK) + N_EXPERTS
    n_pad = n_blocks * MOE_BLOCK
    order = jnp.argsort(expert).astype(jnp.int32)
    start = jnp.cumsum(counts) - counts
    padded = (counts + MOE_BLOCK - 1) // MOE_BLOCK * MOE_BLOCK
    pad_end = jnp.cumsum(padded)
    pad_start = pad_end - padded
    slot_of_assign = (pad_start[expert] + rank).astype(jnp.int32)
    block_first = jnp.arange(n_blocks, dtype=jnp.int32) * MOE_BLOCK
    block_e = jnp.minimum(jnp.sum((pad_end[None, :] <= block_first[:, None]).astype(jnp.int32), axis=1),
                          N_EXPERTS - 1).astype(jnp.int32)
    slot_e = jnp.repeat(block_e, MOE_BLOCK)
    slot_r = jnp.arange(n_pad, dtype=jnp.int32) - pad_start[slot_e]
    slot_valid = (slot_r < counts[slot_e]) & (jnp.arange(n_pad) < pad_end[-1])
    slot_assign = order[jnp.clip(start[slot_e] + slot_r, 0, n_assign - 1)]
    slot_tok = jnp.where(slot_valid, tok[slot_assign], jnp.arange(n_pad, dtype=jnp.int32) % T)
    slot_gate = jnp.where(slot_valid, gate.reshape(-1)[slot_assign], 0.0)
    n_used = (pad_end[-1:] // MOE_BLOCK).astype(jnp.int32)
    return slot_tok, slot_gate, slot_of_assign, block_e, n_used


def hier_moe(h2, route, counts_row, w1, w3, w2, layer):
    T, D = h2.shape
    slot_tok, slot_gate, slot_of_assign, block_e, n_used = moe_dispatch(route, counts_row)
    yb = pl_moe_ffn(h2[slot_tok], slot_gate, block_e, n_used, w1, w3, w2, layer)
    slots = slot_of_assign.reshape(T, MOE_TOP_K)
    return yb[slots[:, 0]], yb[slots[:, 1]]


def kernel(x, c, ctx, c_ctx, w_mod, b_mod, norm1_g, norm2_g, final_g, w_in, w_out, hy_conv_w, hy_conv_b, hy_ffn_w1, hy_ffn_b1, hy_ffn_w2, hy_ffn_b2, hy_ffn_w3, hy_freq, hy_skip, hy_norm_g, s5_a_re, s5_a_im, s5_log_dt, s5_b_re, s5_b_im, s5_c_re, s5_c_im, s5_d, s5_glu_w, s5_norm_g, att_lq1, att_lk1, att_lq2, att_lk2, att_subln_g, moe_wg, moe_bg, moe_we, moe_be, moe_w1, moe_w3, moe_w2):
    B, L, D = x.shape
    Lc = ctx.shape[1]
    Lt = L + Lc
    assert B == SUBLANES and Lc == TOKEN_TILE and L % ATT_TQ == 0
    n_lat_tiles = L // TOKEN_TILE
    cosf, sinf = rope_tables(L, Lc)
    hp = lax.Precision.HIGHEST
    q_scale = ATT_HEAD_DIM ** -0.5 * math.log2(math.e)

    mods = jnp.einsum('bd,ldk->lbk', jax.nn.silu(c), w_mod, precision=hp) + b_mod[:, None, :]
    cmods = jnp.einsum('d,ldk->lk', jax.nn.silu(c_ctx), w_mod, precision=hp) + b_mod
    modvs = jnp.stack([mods, jnp.broadcast_to(cmods[:, None, :], mods.shape)], axis=2)
    modvs = modvs.reshape(DEPTH, 2 * B, 1, N_MOD * D)
    filt_args = (hy_ffn_w1, hy_ffn_b1, hy_ffn_w2, hy_ffn_b2, hy_ffn_w3, hy_freq)
    h_lat = jax.vmap(lambda *p: hyena_spectrum(hyena_filter(L, *p)))(*filt_args)
    ff_ctx = jax.vmap(lambda *p: jnp.fft.rfft(hyena_filter(Lc, *p), n=2 * Lc, axis=0))(*filt_args)
    dft_tables = _fft_tables(L)
    s5_tabs = jax.vmap(jax.vmap(s5_tables))(s5_a_re, s5_a_im, s5_log_dt, s5_b_re, s5_b_im, s5_c_re, s5_c_im)
    lam_inits = [0.8 - 0.6 * math.exp(-0.3 * l) for l in range(DEPTH)]
    lams = (jnp.exp(jnp.sum(att_lq1 * att_lk1, axis=-1)) - jnp.exp(jnp.sum(att_lq2 * att_lk2, axis=-1))
            + jnp.asarray(lam_inits, F32))
    w_router = jnp.zeros((DEPTH, D, LANES), F32).at[:, :, :MOE_GROUPS].set(moe_wg)
    w_router = w_router.at[:, :, MOE_GROUPS:MOE_GROUPS + N_EXPERTS].set(moe_we)
    rwh = w_router.astype(BF16)
    rwl = (w_router - rwh.astype(F32)).astype(BF16)
    router_bias = jnp.zeros((DEPTH, 1, LANES), F32).at[:, 0, :MOE_GROUPS].set(moe_bg)
    router_bias = router_bias.at[:, 0, MOE_GROUPS:MOE_GROUPS + N_EXPERTS].set(moe_be)
    w_in_b, w_out_b, glu_w_b = w_in.astype(BF16), w_out.astype(BF16), s5_glu_w.astype(BF16)

    xs = jnp.concatenate([x, ctx], axis=1)
    for l in range(DEPTH):
        modv = modvs[l]
        u0, z, s5_u, q_p, q_r, k_r, v = pl_in_proj(xs, modv, norm1_g[l], w_in_b[l], hy_conv_w[l], hy_conv_b[l],
                                                   cosf, sinf, n_lat_tiles, q_scale)

        y_lat, y_ctx = hyena_conv(z, h_lat[l], ff_ctx[l], dft_tables, L)

        s5_y = s5_scan(s5_u.reshape(Lt * B, S5_CH), tuple(t[l] for t in s5_tabs), s5_d[l], L)

        g_scaled = (att_subln_g[l] * (1.0 - lam_inits[l])).reshape(1, ATT_V_DIM)
        att_lat, att_ctx = pl_diff_attention(q_p, q_r, k_r, v, g_scaled, lams[l], L)

        xs, h2, route, counts = pl_out_proj(xs, u0, z, y_lat, y_ctx, s5_y.reshape(Lt, B * S5_CH), att_lat, att_ctx,
                                            modv, hy_skip[l], hy_norm_g[l], glu_w_b[l], s5_norm_g[l], w_out_b[l],
                                            norm2_g[l], rwh[l], rwl[l], router_bias[l], n_lat_tiles)

        y0, y1 = hier_moe(h2.reshape(B * Lt, D), route.reshape(B * Lt, LANES), counts[0], moe_w1, moe_w3, moe_w2, l)
        xs = pl_moe_combine(xs, y0, y1, modv, final_g, n_lat_tiles, final=l == DEPTH - 1)
    return xs
```

```python
import functools
import math

import jax
import jax.numpy as jnp
import numpy as np
from jax import lax
from jax.experimental import pallas as pl
from jax.experimental.pallas import tpu as pltpu

D_MODEL = 1024
DEPTH = 4
GRID_W = 64
N_MOD = 6
EPS = 1e-6
HY_CH = D_MODEL // 4
S5_CH = D_MODEL // 4
ATT_W = D_MODEL // 2
HY_BANDS = 16
HY_DECAY_MIN = -math.log(1e-2) / 1.5
HY_DECAY_MAX = -math.log(1e-2) / 0.3
S5_GROUP = 16
S5_GROUPS = S5_CH // S5_GROUP
S5_STATE = 64
ATT_HEAD_DIM = 64
ATT_HEADS = ATT_W // (2 * ATT_HEAD_DIM)
ATT_V_DIM = 2 * ATT_HEAD_DIM
ROPE_HALF = ATT_HEAD_DIM // 2
ROPE_PAIRS_AXIS = ROPE_HALF // 2
ROPE_BASE = 10000.0
MOE_GROUPS = 4
MOE_EPG = 8
N_EXPERTS = MOE_GROUPS * MOE_EPG
MOE_TOP_K = 2
MOE_BLOCK = 512
IN_COLS = 3 * HY_CH + S5_CH + 3 * ATT_W
COL_S5 = 3 * HY_CH
COL_Q = COL_S5 + S5_CH
COL_K = COL_Q + ATT_W
COL_V = COL_K + ATT_W

LANES = 128
SUBLANES = 8
VMEM_LIMIT = 48 * 1024 * 1024
TOKEN_TILE = 256

F32 = jnp.float32
BF16 = jnp.bfloat16


def _params(n_axes, vmem=VMEM_LIMIT):
    return pltpu.CompilerParams(dimension_semantics=("arbitrary",) * n_axes, vmem_limit_bytes=vmem)


def _rms(x):
    return x * lax.rsqrt(jnp.mean(x * x, axis=-1, keepdims=True) + EPS)


def _in_kernel(x_ref, xp_ref, xn_ref, mod_ref, g_ref, w_ref, cw_ref, cb_ref, cos_ref, sin_ref,
               u0_ref, z_ref, s5_ref, qp_ref, qr_ref, k_ref, v_ref, *, n_lat_tiles, q_scale):
    i = pl.program_id(1)
    tm = x_ref.shape[1]
    d = x_ref.shape[2]
    g = g_ref[...]
    shift = mod_ref[0, :, 0:d]
    scale = mod_ref[0, :, d:2 * d]

    def norm_mod(xt):
        return (_rms(xt) * g) * (1.0 + scale) + shift

    h = jnp.concatenate([norm_mod(x_ref[0]), norm_mod(xp_ref[0]), norm_mod(xn_ref[0])], axis=0).astype(BF16)
    p = jnp.dot(h, w_ref[...], preferred_element_type=F32)

    hy = p[:tm, :COL_S5]
    is_ctx = i == n_lat_tiles
    has_prev = jnp.logical_and(i != 0, jnp.logical_not(is_ctx))
    has_next = jnp.logical_and(i != n_lat_tiles - 1, jnp.logical_not(is_ctx))
    prev_row = jnp.where(has_prev, p[tm + SUBLANES - 1:tm + SUBLANES, :COL_S5], 0.0)
    next_row = jnp.where(has_next, p[tm + SUBLANES:tm + SUBLANES + 1, :COL_S5], 0.0)
    rows = lax.broadcasted_iota(jnp.int32, (tm, 1), 0)
    up = jnp.where(rows == 0, prev_row, pltpu.roll(hy, 1, axis=0))
    dn = jnp.where(rows == tm - 1, next_row, pltpu.roll(hy, tm - 1, axis=0))
    u = up * cw_ref[0:1, :] + hy * cw_ref[1:2, :] + dn * cw_ref[2:3, :] + cb_ref[...]
    u0_ref[0] = u[:, :HY_CH].astype(u0_ref.dtype)
    z_ref[0] = u[:, HY_CH:2 * HY_CH] * u[:, 2 * HY_CH:]

    s5_ref[...] = p[:tm, COL_S5:COL_Q]

    lane = lax.broadcasted_iota(jnp.int32, (1, ATT_W), 1)
    first_half = jnp.bitwise_and(lane, ATT_HEAD_DIM - 1) < ROPE_HALF
    cos = cos_ref[...]
    sin = sin_ref[...]

    def rope(t):
        partner = jnp.where(first_half, pltpu.roll(t, ATT_W - ROPE_HALF, axis=1), pltpu.roll(t, ROPE_HALF, axis=1))
        return t * cos + partner * sin

    q = p[:tm, COL_Q:COL_K] * q_scale
    qp_ref[0] = q.astype(BF16)
    qr_ref[0] = rope(q).astype(BF16)
    k_ref[0] = rope(p[:tm, COL_K:COL_V]).astype(BF16)
    v_ref[0] = p[:tm, COL_V:].astype(BF16)


def pl_in_proj(xs, modv, norm_g, w_in, conv_w, conv_b, cosf, sinf, n_lat_tiles, q_scale):
    B, Lt, D = xs.shape
    tm = TOKEN_TILE
    n_tiles = Lt // tm
    halo_per_tile = tm // SUBLANES
    n_halo_blocks = Lt // SUBLANES
    tok = lambda w: pl.BlockSpec((1, tm, w), lambda b, i: (b, i, 0))
    const = lambda shape: pl.BlockSpec(shape, lambda b, i: (0,) * len(shape))
    out_shapes = (jax.ShapeDtypeStruct((B, Lt, HY_CH), BF16), jax.ShapeDtypeStruct((B, Lt, HY_CH), F32),
                  jax.ShapeDtypeStruct((Lt, B * S5_CH), F32),
                  jax.ShapeDtypeStruct((B, Lt, ATT_W), BF16), jax.ShapeDtypeStruct((B, Lt, ATT_W), BF16),
                  jax.ShapeDtypeStruct((B, Lt, ATT_W), BF16), jax.ShapeDtypeStruct((B, Lt, ATT_W), BF16))
    return pl.pallas_call(
        functools.partial(_in_kernel, n_lat_tiles=n_lat_tiles, q_scale=q_scale),
        grid=(B, n_tiles),
        in_specs=[tok(D),
                  pl.BlockSpec((1, SUBLANES, D), lambda b, i: (b, jnp.maximum(i * halo_per_tile - 1, 0), 0)),
                  pl.BlockSpec((1, SUBLANES, D),
                               lambda b, i: (b, jnp.minimum((i + 1) * halo_per_tile, n_halo_blocks - 1), 0)),
                  pl.BlockSpec((1, 1, N_MOD * D), lambda b, i: (2 * b + jnp.where(i >= n_lat_tiles, 1, 0), 0, 0)),
                  const((1, D)), const((D, IN_COLS)), const((3, COL_S5)), const((1, COL_S5)),
                  pl.BlockSpec((tm, ATT_W), lambda b, i: (i, 0)), pl.BlockSpec((tm, ATT_W), lambda b, i: (i, 0))],
        out_specs=(tok(HY_CH), tok(HY_CH), pl.BlockSpec((tm, S5_CH), lambda b, i: (i, b)),
                   tok(ATT_W), tok(ATT_W), tok(ATT_W), tok(ATT_W)),
        out_shape=out_shapes,
        compiler_params=_params(2),
        name="in_proj",
    )(xs, xs, xs, modv, norm_g.reshape(1, D), w_in, conv_w, conv_b.reshape(1, COL_S5), cosf, sinf)


def _out_kernel(x_ref, u0_ref, z_ref, yl_ref, yc_ref, s5_ref, al_ref, ac_ref, mod_ref, skip_ref, hg_ref,
                gw_ref, sg_ref, wo_ref, n2g_ref, rwh_ref, rwl_ref, rb_ref, tri_ref, xo_ref, h2_ref, rt_ref, cnt_ref,
                cnt_scr, *, n_lat_tiles):
    d = x_ref.shape[2]
    is_ctx = pl.program_id(1) == n_lat_tiles
    y = jnp.where(is_ctx, yc_ref[0], yl_ref[0])
    att = jnp.where(is_ctx, ac_ref[0], al_ref[0])
    hy = _rms(u0_ref[0].astype(F32) * (y + skip_ref[...] * z_ref[0])) * hg_ref[...]
    gl = jax.nn.gelu(s5_ref[...].astype(F32))
    gate = jax.nn.sigmoid(jnp.dot(gl.astype(BF16), gw_ref[...], preferred_element_type=F32))
    s5 = _rms(gl * gate) * sg_ref[...]
    mix = jnp.concatenate([hy.astype(BF16), s5.astype(BF16), att], axis=1)
    proj = jnp.dot(mix, wo_ref[...], preferred_element_type=F32)
    xn = x_ref[0] + mod_ref[0, :, 2 * d:3 * d] * proj
    xo_ref[0] = xn
    h2 = (_rms(xn) * n2g_ref[...]) * (1.0 + mod_ref[0, :, 4 * d:5 * d]) + mod_ref[0, :, 3 * d:4 * d]
    hh = h2.astype(BF16)
    hl = (h2 - hh.astype(F32)).astype(BF16)
    h2_ref[0] = hh
    lg = jnp.dot(hh, rwh_ref[...], preferred_element_type=F32)
    lg += jnp.dot(hl, rwh_ref[...], preferred_element_type=F32)
    lg += jnp.dot(hh, rwl_ref[...], preferred_element_type=F32)
    lg = lg + rb_ref[...]

    tm = lg.shape[0]
    lane = lax.broadcasted_iota(jnp.int32, (1, LANES), 1)
    neg_inf = jnp.float32(-jnp.inf)

    def first_max(v):
        m = jnp.max(v, axis=1, keepdims=True)
        return m, jnp.min(jnp.where(v == m, lane, LANES), axis=1, keepdims=True)

    g_logit = jnp.where(lane < MOE_GROUPS, lg, neg_inf)
    g_max, g_idx = first_max(g_logit)
    p_group = 1.0 / jnp.sum(jnp.exp(g_logit - g_max), axis=1, keepdims=True)
    e_lane = lane - MOE_GROUPS
    in_group = jnp.logical_and(jnp.logical_and(e_lane >= 0, e_lane < N_EXPERTS),
                               jnp.right_shift(e_lane, MOE_EPG.bit_length() - 1) == g_idx)
    e_logit = jnp.where(in_group, lg, neg_inf)
    e_exp = jnp.exp(e_logit - jnp.max(e_logit, axis=1, keepdims=True))
    probs = jnp.where(in_group, e_exp / jnp.sum(e_exp, axis=1, keepdims=True), -1.0)
    p1, l1 = first_max(probs)
    p2, l2 = first_max(jnp.where(lane == l1, -1.0, probs))
    hit1 = lane == l1
    hit2 = lane == l2
    onehot = jnp.where(hit1, 1.0, 0.0) + jnp.where(hit2, 1.0, 0.0)

    @pl.when(jnp.logical_and(pl.program_id(0) == 0, pl.program_id(1) == 0))
    def _():
        cnt_scr[...] = jnp.zeros_like(cnt_scr)

    earlier = jnp.dot(tri_ref[...], onehot.astype(BF16), preferred_element_type=F32)
    before = cnt_scr[0:1, :] + earlier
    r1 = jnp.sum(jnp.where(hit1, before, 0.0), axis=1, keepdims=True)
    r2 = jnp.sum(jnp.where(hit2, before, 0.0), axis=1, keepdims=True)
    cnt_scr[...] = jnp.broadcast_to(cnt_scr[0:1, :] + jnp.sum(onehot, axis=0, keepdims=True), cnt_scr.shape)
    cnt_ref[...] = cnt_scr[...]
    cols = [(l1 - MOE_GROUPS).astype(F32), (l2 - MOE_GROUPS).astype(F32), p_group * p1 / (p1 + p2),
            p_group * p2 / (p1 + p2), r1, r2]
    route = jnp.zeros((tm, LANES), F32)
    for j, col in enumerate(cols):
        route = jnp.where(lane == j, col, route)
    rt_ref[0] = route


def pl_out_proj(xs, u0, z, y_lat, y_ctx, s5_y, att_lat, att_ctx, modv, hy_skip, hy_norm_g, glu_w, s5_norm_g,
                w_out, norm2_g, rwh, rwl, router_bias, n_lat_tiles):
    B, Lt, D = xs.shape
    tm = TOKEN_TILE
    tri = jnp.asarray(np.tril(np.ones((tm, tm), np.float32), -1), BF16)
    tok = lambda w: pl.BlockSpec((1, tm, w), lambda b, i: (b, i, 0))
    lat_tok = lambda w: pl.BlockSpec((1, tm, w), lambda b, i: (b, jnp.minimum(i, n_lat_tiles - 1), 0))
    ctx_tok = lambda w: pl.BlockSpec((1, tm, w), lambda b, i: (b, 0, 0))
    tb = pl.BlockSpec((tm, S5_CH), lambda b, i: (i, b))
    const = lambda shape: pl.BlockSpec(shape, lambda b, i: (0,) * len(shape))
    return pl.pallas_call(
        functools.partial(_out_kernel, n_lat_tiles=n_lat_tiles),
        grid=(B, Lt // tm),
        in_specs=[tok(D), tok(HY_CH), tok(HY_CH), lat_tok(HY_CH), ctx_tok(HY_CH), tb, lat_tok(ATT_W),
                  ctx_tok(ATT_W),
                  pl.BlockSpec((1, 1, N_MOD * D), lambda b, i: (2 * b + jnp.where(i >= n_lat_tiles, 1, 0), 0, 0)),
                  const((1, HY_CH)), const((1, HY_CH)), const((S5_CH, S5_CH)), const((1, S5_CH)),
                  const((D, D)), const((1, D)), const((D, LANES)), const((D, LANES)), const((1, LANES)),
                  const((tm, tm))],
        out_specs=(tok(D), tok(D), tok(LANES), const((SUBLANES, LANES))),
        out_shape=(jax.ShapeDtypeStruct((B, Lt, D), F32), jax.ShapeDtypeStruct((B, Lt, D), BF16),
                   jax.ShapeDtypeStruct((B, Lt, LANES), F32), jax.ShapeDtypeStruct((SUBLANES, LANES), F32)),
        scratch_shapes=[pltpu.VMEM((SUBLANES, LANES), F32)],
        compiler_params=_params(2),
        name="out_proj",
    )(xs, u0, z, y_lat, y_ctx, s5_y, att_lat, att_ctx, modv, hy_skip.reshape(1, HY_CH),
      hy_norm_g.reshape(1, HY_CH), glu_w, s5_norm_g.reshape(1, S5_CH), w_out, norm2_g.reshape(1, D), rwh, rwl,
      router_bias, tri)


def _dot_nt(a, b):
    return lax.dot_general(a, b, (((1,), (1,)), ((), ())), preferred_element_type=F32)


ATT_TQ = 512
ATT_SUB = 256


def _attn_kernel(qp_ref, qr_ref, k_ref, v_ref, g_ref, lam_ref, o_ref, *, n_lat):
    tq = qp_ref.shape[1]
    first_map = lax.broadcasted_iota(jnp.int32, (1, LANES), 1) < ATT_HEAD_DIM
    zero = jnp.zeros((), BF16)
    sub = min(ATT_SUB, tq)
    for r0 in range(0, tq, sub):
        qp = qp_ref[0, r0:r0 + sub, :]
        qr = qr_ref[0, r0:r0 + sub, :]
        probs = []
        for m in range(2):
            in_map = first_map if m == 0 else jnp.logical_not(first_map)
            s_c = _dot_nt(jnp.where(in_map, qp, zero), k_ref[0, n_lat:, :])
            mx = jnp.max(s_c, axis=-1, keepdims=True)
            if n_lat:
                s_l = _dot_nt(jnp.where(in_map, qr, zero), k_ref[0, :n_lat, :])
                mx = jnp.maximum(mx, jnp.max(s_l, axis=-1, keepdims=True))
                p_l = jnp.exp2(s_l - mx)
            p_c = jnp.exp2(s_c - mx)
            den = jnp.sum(p_c, axis=-1, keepdims=True)
            if n_lat:
                den = den + jnp.sum(p_l, axis=-1, keepdims=True)
            probs.append((p_c.astype(BF16), p_l.astype(BF16) if n_lat else None, 1.0 / den))
        w0 = probs[0][2].astype(BF16)
        w1 = (lam_ref[0:1, 0:1] * probs[1][2]).astype(BF16)
        a_c = probs[0][0] * w0 - probs[1][0] * w1
        o = jnp.dot(a_c, v_ref[0, n_lat:, :], preferred_element_type=F32)
        if n_lat:
            a_l = probs[0][1] * w0 - probs[1][1] * w1
            o = o + jnp.dot(a_l, v_ref[0, :n_lat, :], preferred_element_type=F32)
        o_ref[0, r0:r0 + sub, :] = (_rms(o) * g_ref[...]).astype(o_ref.dtype)


def pl_diff_attention(qp, qr, k, v, g_scaled, lam, n_lat):
    B, Lt, _ = qp.shape
    n_ctx = Lt - n_lat
    lam_arr = jnp.full((SUBLANES, LANES), lam, F32)
    small = [pl.BlockSpec((1, LANES), lambda b, h, i: (0, 0)), pl.BlockSpec((SUBLANES, LANES), lambda b, h, i: (0, 0))]
    tq = ATT_TQ
    qspec = pl.BlockSpec((1, tq, LANES), lambda b, h, i: (b, i, h))
    kspec = pl.BlockSpec((1, Lt, LANES), lambda b, h, i: (b, 0, h))
    out_lat = pl.pallas_call(
        functools.partial(_attn_kernel, n_lat=n_lat),
        grid=(B, ATT_HEADS, n_lat // tq),
        in_specs=[qspec, qspec, kspec, kspec] + small,
        out_specs=qspec,
        out_shape=jax.ShapeDtypeStruct((B, n_lat, ATT_W), BF16),
        compiler_params=_params(3),
        name="diff_attention",
    )(qp, qr, k, v, g_scaled, lam_arr)
    ctx_blk = n_lat // n_ctx
    cspec = pl.BlockSpec((1, n_ctx, LANES), lambda b, h, i: (b, ctx_blk, h))
    out_ctx = pl.pallas_call(
        functools.partial(_attn_kernel, n_lat=0),
        grid=(B, ATT_HEADS, 1),
        in_specs=[cspec, cspec, cspec, cspec] + small,
        out_specs=pl.BlockSpec((1, n_ctx, LANES), lambda b, h, i: (b, 0, h)),
        out_shape=jax.ShapeDtypeStruct((B, n_ctx, ATT_W), BF16),
        compiler_params=_params(3),
        name="diff_attention_ctx",
    )(qp, qr, k, v, g_scaled, lam_arr)
    return out_lat, out_ctx


def _moe_kernel(be_ref, nb_ref, x_ref, gate_ref, w1_ref, w3_ref, w2_ref, o_ref, w1_scr, w3_scr, w2_scr):
    i = pl.program_id(0)
    new_expert = jnp.logical_or(i == 0, be_ref[i] != be_ref[jnp.maximum(i - 1, 0)])

    @pl.when(jnp.logical_and(i < nb_ref[0], new_expert))
    def _():
        w1_scr[...] = w1_ref[0, 0].astype(BF16)
        w3_scr[...] = w3_ref[0, 0].astype(BF16)
        w2_scr[...] = w2_ref[0, 0].astype(BF16)

    @pl.when(i < nb_ref[0])
    def _():
        x = x_ref[...]
        a = jnp.dot(x, w1_scr[...], preferred_element_type=F32)
        b = jnp.dot(x, w3_scr[...], preferred_element_type=F32)
        h = (a * jax.nn.sigmoid(a)) * b
        y = jnp.dot(h.astype(BF16), w2_scr[...], preferred_element_type=F32)
        rows = lax.broadcasted_iota(jnp.int32, (MOE_BLOCK, 1), 0)
        lane = lax.broadcasted_iota(jnp.int32, (1, LANES), 1)
        g_rows = gate_ref[0, 0:1, :]
        for r in range(1, MOE_BLOCK // LANES):
            g_rows = jnp.where(rows >= r * LANES, gate_ref[0, r:r + 1, :], g_rows)
        g_col = jnp.sum(jnp.where(lane == jnp.bitwise_and(rows, LANES - 1), g_rows, 0.0), axis=1, keepdims=True)
        o_ref[...] = (y * g_col).astype(o_ref.dtype)

    @pl.when(i >= nb_ref[0])
    def _():
        o_ref[...] = jnp.zeros_like(o_ref)


def pl_moe_ffn(xb, slot_gate, block_e, n_used, w1, w3, w2, layer):
    n_pad, D = xb.shape
    n_blocks = n_pad // MOE_BLOCK
    F = w1.shape[-1]
    grid_spec = pltpu.PrefetchScalarGridSpec(
        num_scalar_prefetch=2,
        grid=(n_blocks,),
        in_specs=[pl.BlockSpec((MOE_BLOCK, D), lambda i, be, nb: (i, 0)),
                  pl.BlockSpec((1, MOE_BLOCK // LANES, LANES), lambda i, be, nb: (i, 0, 0)),
                  pl.BlockSpec((1, 1, D, F), lambda i, be, nb: (layer, be[i], 0, 0)),
                  pl.BlockSpec((1, 1, D, F), lambda i, be, nb: (layer, be[i], 0, 0)),
                  pl.BlockSpec((1, 1, F, D), lambda i, be, nb: (layer, be[i], 0, 0))],
        out_specs=pl.BlockSpec((MOE_BLOCK, D), lambda i, be, nb: (i, 0)),
        scratch_shapes=[pltpu.VMEM((D, F), BF16), pltpu.VMEM((D, F), BF16), pltpu.VMEM((F, D), BF16)],
    )
    return pl.pallas_call(
        _moe_kernel,
        grid_spec=grid_spec,
        out_shape=jax.ShapeDtypeStruct((n_pad, D), BF16),
        compiler_params=_params(1),
        name="moe_ffn",
    )(block_e, n_used, xb, slot_gate.reshape(n_blocks, MOE_BLOCK // LANES, LANES), w1, w3, w2)


def _combine_kernel(x_ref, y0_ref, y1_ref, mod_ref, g_ref, o_ref, *, final):
    d = x_ref.shape[2]
    xn = x_ref[0] + mod_ref[0, :, 5 * d:6 * d] * (y0_ref[0].astype(F32) + y1_ref[0].astype(F32))
    o_ref[0] = _rms(xn) * g_ref[...] if final else xn


def pl_moe_combine(xs, y0, y1, modv, final_g, n_lat_tiles, final):
    B, Lt, D = xs.shape
    tm = TOKEN_TILE
    n_tiles = n_lat_tiles if final else Lt // tm
    tok = pl.BlockSpec((1, tm, D), lambda b, i: (b, i, 0))
    return pl.pallas_call(
        functools.partial(_combine_kernel, final=final),
        grid=(B, n_tiles),
        in_specs=[tok, tok, tok,
                  pl.BlockSpec((1, 1, N_MOD * D), lambda b, i: (2 * b + jnp.where(i >= n_lat_tiles, 1, 0), 0, 0)),
                  pl.BlockSpec((1, D), lambda b, i: (0, 0))],
        out_specs=tok,
        out_shape=jax.ShapeDtypeStruct((B, n_tiles * tm, D), F32),
        compiler_params=_params(2),
        name="moe_combine",
    )(xs, y0.reshape(B, Lt, D), y1.reshape(B, Lt, D), modv, final_g.reshape(1, D))


S5_STATES = S5_GROUPS * S5_STATE
S5_CHUNK = 64


def _s5_kernel(u_ref, wd_ref, wr_ref, ar_ref, ai_ref, d_ref, y_ref, x_scr, h_scr, hr_scr, hi_scr, *, reverse):
    ns = S5_STATES

    @pl.when(pl.program_id(0) == 0)
    def _():
        hr_scr[...] = jnp.zeros_like(hr_scr)
        hi_scr[...] = jnp.zeros_like(hi_scr)

    u = u_ref[...]
    x_scr[...] = jnp.dot(u.astype(BF16), wd_ref[...], preferred_element_type=F32)
    ar = ar_ref[...]
    ai = ai_ref[...]

    def step(hr, hi, t):
        r = pl.multiple_of(t * SUBLANES, SUBLANES)
        xr = x_scr[pl.ds(r, SUBLANES), :ns]
        xi = x_scr[pl.ds(r, SUBLANES), ns:]
        return ar * hr - ai * hi + xr, ar * hi + ai * hr + xi

    def body(j, carry):
        hr, hi = carry
        t0 = (S5_CHUNK - 1 - 2 * j) if reverse else 2 * j
        t1 = t0 - 1 if reverse else t0 + 1
        hr0, hi0 = step(hr, hi, t0)
        hr1, hi1 = step(hr0, hi0, t1)
        lo = t1 if reverse else t0
        first_r, second_r = (hr1, hr0) if reverse else (hr0, hr1)
        first_i, second_i = (hi1, hi0) if reverse else (hi0, hi1)
        r = pl.multiple_of(lo * SUBLANES, 2 * SUBLANES)
        h_scr[pl.ds(r, 2 * SUBLANES), :ns] = jnp.concatenate([first_r, second_r], axis=0).astype(BF16)
        h_scr[pl.ds(r, 2 * SUBLANES), ns:] = jnp.concatenate([first_i, second_i], axis=0).astype(BF16)
        return hr1, hi1

    hr, hi = lax.fori_loop(0, S5_CHUNK // 2, body, (hr_scr[...], hi_scr[...]))
    hr_scr[...] = hr
    hi_scr[...] = hi
    y = jnp.dot(h_scr[...], wr_ref[...], preferred_element_type=F32)
    y = y + (d_ref[...].astype(F32) if reverse else u * d_ref[...])
    y_ref[...] = y.astype(y_ref.dtype)


def pl_s5_scan(u_tb, w_drive, w_read, a_re, a_im, addend, *, n_lat_steps, reverse):
    rows, ch = u_tb.shape
    rc = S5_CHUNK * SUBLANES
    n_chunks = rows // rc
    n_lat = n_lat_steps // S5_CHUNK
    n_ctx = n_chunks - n_lat
    assert rows % rc == 0 and n_lat_steps % S5_CHUNK == 0
    if reverse:
        def idx(i):
            return (n_chunks - 1 - i, 0)
    else:
        def idx(i):
            return (jnp.where(i < n_ctx, n_lat + i, i - n_ctx), 0)
    const = lambda i: (0, 0)
    ns2 = 2 * S5_STATES
    return pl.pallas_call(
        functools.partial(_s5_kernel, reverse=reverse),
        grid=(n_chunks,),
        in_specs=[pl.BlockSpec((rc, ch), idx),
                  pl.BlockSpec((ch, ns2), const),
                  pl.BlockSpec((ns2, ch), const),
                  pl.BlockSpec((SUBLANES, S5_STATES), const),
                  pl.BlockSpec((SUBLANES, S5_STATES), const),
                  pl.BlockSpec((rc, ch), idx) if reverse else pl.BlockSpec((1, ch), const)],
        out_specs=pl.BlockSpec((rc, ch), idx),
        out_shape=jax.ShapeDtypeStruct((rows, ch), BF16),
        scratch_shapes=[pltpu.VMEM((rc, ns2), F32), pltpu.VMEM((rc, ns2), BF16),
                        pltpu.VMEM((SUBLANES, S5_STATES), F32), pltpu.VMEM((SUBLANES, S5_STATES), F32)],
        compiler_params=_params(1),
        name="s5_scan_rev" if reverse else "s5_scan_fwd",
    )(u_tb, w_drive, w_read,
      jnp.broadcast_to(a_re[None, :], (SUBLANES, S5_STATES)),
      jnp.broadcast_to(a_im[None, :], (SUBLANES, S5_STATES)),
      addend if reverse else addend.reshape(1, ch))


FFT_N2 = 128


def _fft_tables(L):
    N = 2 * L
    N1 = N // FFT_N2
    k1 = np.arange(N1)[:, None]
    n1 = np.arange(N1 // 2)[None, :]
    n2 = np.arange(FFT_N2)[:, None, None]
    ang = -2.0 * np.pi * (k1[None] * (n2 + FFT_N2 * n1[None])) / N
    mr, mi = np.cos(ang), np.sin(ang)
    ma = np.concatenate([np.concatenate([mr, -mi], axis=2), np.concatenate([mi, mr], axis=2)], axis=1)
    gr, gi = np.transpose(mr, (0, 2, 1)), -np.transpose(mi, (0, 2, 1))
    mainv = np.concatenate([np.concatenate([gr, -gi], axis=2), np.concatenate([gi, gr], axis=2)], axis=1)
    kk = np.arange(FFT_N2)
    a2 = -2.0 * np.pi * np.outer(kk, kk) / FFT_N2
    fr, fi = np.cos(a2), np.sin(a2)
    f_fwd = np.block([[fr, -fi], [fi, fr]])
    f_inv = np.block([[fr, fi], [-fi, fr]])
    return (jnp.asarray(ma, BF16), jnp.asarray(mainv, BF16), jnp.asarray(f_fwd, BF16), jnp.asarray(f_inv, BF16))


HYENA_VMEM_LIMIT = 56 * 1024 * 1024
FFT_UNROLL = 8


def _hyena_fft_kernel(z_ref, h_ref, ma_ref, mainv_ref, ff_ref, fi_ref, o_ref, a_scr, b_scr, *, n1_count):
    half = n1_count // 2
    n2c = FFT_N2

    def stage_a(n2, c):
        xr = z_ref[0, pl.ds(n2, half, stride=n2c), :]
        xi = z_ref[1, pl.ds(n2, half, stride=n2c), :]
        x = jnp.concatenate([xr, xi], axis=0).astype(BF16)
        r = jnp.dot(ma_ref[n2], x, preferred_element_type=F32)
        a_scr[pl.ds(pl.multiple_of(n2 * 2 * n1_count, 2 * n1_count), 2 * n1_count), :] = r
        return c

    lax.fori_loop(0, n2c, stage_a, 0, unroll=FFT_UNROLL)

    def stage_c(k1, c):
        ar = a_scr[pl.ds(k1, n2c, stride=2 * n1_count), :]
        ai = a_scr[pl.ds(n1_count + k1, n2c, stride=2 * n1_count), :]
        x = jnp.concatenate([ar, ai], axis=0).astype(BF16)
        y = jnp.dot(ff_ref[...], x, preferred_element_type=F32)
        yr, yi = y[:n2c], y[n2c:]
        hr = h_ref[0, k1].astype(F32)
        hi = h_ref[1, k1].astype(F32)
        x2 = jnp.concatenate([yr * hr - yi * hi, yr * hi + yi * hr], axis=0).astype(BF16)
        b = jnp.dot(fi_ref[...], x2, preferred_element_type=F32)
        b_scr[pl.ds(pl.multiple_of(k1 * 2 * n2c, 2 * n2c), 2 * n2c), :] = b
        return c

    lax.fori_loop(0, n1_count, stage_c, 0, unroll=FFT_UNROLL)

    def stage_a_inv(n2, c):
        br = b_scr[pl.ds(n2, n1_count, stride=2 * n2c), :]
        bi = b_scr[pl.ds(n2c + n2, n1_count, stride=2 * n2c), :]
        x = jnp.concatenate([br, bi], axis=0).astype(BF16)
        r = jnp.dot(mainv_ref[n2], x, preferred_element_type=F32)
        o_ref[0, pl.ds(n2, half, stride=n2c), :] = r[:half]
        o_ref[1, pl.ds(n2, half, stride=n2c), :] = r[half:]
        return c

    lax.fori_loop(0, n2c, stage_a_inv, 0, unroll=FFT_UNROLL)


def hyena_spectrum(filt):
    N, C = filt.shape
    N1 = N // FFT_N2
    h2 = (jnp.fft.fft(filt, axis=0) / N).reshape(FFT_N2, N1, C).transpose(1, 0, 2)
    return jnp.stack([h2.real, h2.imag]).astype(BF16)


def pl_hyena_conv(z, h, dft_tables, L):
    B, _, C = z.shape
    N1 = 2 * L // FFT_N2
    ma, mainv, f_fwd, f_inv = dft_tables
    cw = LANES
    full = lambda arr: pl.BlockSpec(arr.shape, lambda ct, bp: (0,) * arr.ndim)
    return pl.pallas_call(
        functools.partial(_hyena_fft_kernel, n1_count=N1),
        grid=(C // cw, B // 2),
        in_specs=[pl.BlockSpec((2, L, cw), lambda ct, bp: (bp, 0, ct)),
                  pl.BlockSpec((2, N1, FFT_N2, cw), lambda ct, bp: (0, 0, 0, ct)),
                  full(ma), full(mainv), full(f_fwd), full(f_inv)],
        out_specs=pl.BlockSpec((2, L, cw), lambda ct, bp: (bp, 0, ct)),
        out_shape=jax.ShapeDtypeStruct((B, L, C), F32),
        scratch_shapes=[pltpu.VMEM((FFT_N2 * 2 * N1, cw), F32), pltpu.VMEM((N1 * 2 * FFT_N2, cw), F32)],
        compiler_params=_params(2, HYENA_VMEM_LIMIT),
        name="hyena_fft_conv",
    )(z, h, ma, mainv, f_fwd, f_inv)


def hyena_filter(L, w1, b1, w2, b2, w3, freq):
    t = jnp.arange(L, dtype=F32) / L
    ang = (2.0 * math.pi) * t[:, None] * jnp.arange(1, HY_BANDS + 1, dtype=F32)
    feat = jnp.concatenate([t[:, None], jnp.cos(ang), jnp.sin(ang)], axis=-1)
    hp = lax.Precision.HIGHEST
    h = jnp.sin(freq * (jnp.dot(feat, w1, precision=hp) + b1))
    h = jnp.sin(freq * (jnp.dot(h, w2, precision=hp) + b2))
    h = jnp.dot(h, w3, precision=hp).reshape(L, 2, HY_CH)
    window = jnp.exp(-t[:, None] * jnp.linspace(HY_DECAY_MIN, HY_DECAY_MAX, HY_CH, dtype=F32))
    h = h * window[:, None, :]
    filt = jnp.concatenate([h[:, 0], jnp.zeros((1, HY_CH), F32), h[:0:-1, 1]], axis=0)
    return filt / (jnp.sum(jnp.abs(filt), axis=0, keepdims=True) + EPS)


def hyena_conv(z, h_lat, ff_ctx, dft_tables, L):
    Lc = z.shape[1] - L
    y_lat = pl_hyena_conv(z, h_lat, dft_tables, L)
    zf = jnp.fft.rfft(z[:, L:], n=2 * Lc, axis=1)
    y_ctx = jnp.fft.irfft(zf * ff_ctx[None], n=2 * Lc, axis=1)[:, :Lc]
    return y_lat, y_ctx


def _block_diag(blocks):
    G, r, c = blocks.shape
    eye = jnp.eye(G, dtype=blocks.dtype)
    return (eye[:, None, :, None] * blocks[:, :, None, :]).reshape(G * r, G * c)


def s5_tables(a_re, a_im, log_dt, b_re, b_im, c_re, c_im):
    A = lax.complex(a_re, a_im)
    dtA = jnp.exp(log_dt)[:, None] * A
    a_bar = jnp.exp(dtA)
    b_bar = ((a_bar - 1.0) / A)[:, :, None] * lax.complex(b_re, b_im)
    bt_re = jnp.transpose(b_bar.real, (0, 2, 1))
    bt_im = jnp.transpose(b_bar.imag, (0, 2, 1))
    w_drive = jnp.concatenate([_block_diag(bt_re), _block_diag(bt_im)], axis=1)
    ct_re = jnp.transpose(c_re, (0, 2, 1))
    ct_im = jnp.transpose(c_im, (0, 2, 1))
    w_read = jnp.concatenate([_block_diag(ct_re), -_block_diag(ct_im)], axis=0)
    return w_drive.astype(BF16), w_read.astype(BF16), a_bar.real.reshape(-1), a_bar.imag.reshape(-1)


def s5_scan(u_tb, tables, d_skip, n_lat_steps):
    y = d_skip
    for direction in range(2):
        w_drive, w_read, a_re, a_im = (t[direction] for t in tables)
        y = pl_s5_scan(u_tb, w_drive, w_read, a_re, a_im, y, n_lat_steps=n_lat_steps, reverse=direction == 1)
    return y


def rope_tables(L, Lc):
    rows = L // GRID_W
    row = jnp.repeat(jnp.arange(rows, dtype=F32), GRID_W)
    col = jnp.tile(jnp.arange(GRID_W, dtype=F32), rows)
    inv = ROPE_BASE ** (-jnp.arange(ROPE_PAIRS_AXIS, dtype=F32) / ROPE_PAIRS_AXIS)
    ang = jnp.concatenate([row[:, None] * inv, col[:, None] * inv], axis=-1)
    cos, sin = jnp.cos(ang), jnp.sin(ang)
    n_maps = ATT_W // ATT_HEAD_DIM
    cosf = jnp.tile(jnp.concatenate([cos, cos], axis=-1), (1, n_maps))
    sinf = jnp.tile(jnp.concatenate([-sin, sin], axis=-1), (1, n_maps))
    return (jnp.concatenate([cosf, jnp.ones((Lc, ATT_W), F32)], axis=0),
            jnp.concatenate([sinf, jnp.zeros((Lc, ATT_W), F32)], axis=0))


def moe_dispatch(route, counts_row):
    T = route.shape[0]
    experts = [route[:, j].astype(jnp.int32) for j in range(MOE_TOP_K)]
    ranks = [route[:, 2 * MOE_TOP_K + j].astype(jnp.int32) for j in range(MOE_TOP_K)]
    gate_cat = jnp.concatenate([route[:, MOE_TOP_K + j] for j in range(MOE_TOP_K)])
    counts = counts_row[MOE_GROUPS:MOE_GROUPS + N_EXPERTS].astype(jnp.int32)
    n_assign = T * MOE_TOP_K
    n_blocks = -(-n_assign // MOE_BLOCK) + N_EXPERTS
    n_pad = n_blocks * MOE_BLOCK
    a_bits = (n_assign - 1).bit_length()
    t_ids = jnp.arange(T, dtype=jnp.int32)
    keys = jnp.concatenate([(experts[j] << a_bits) + (MOE_TOP_K * t_ids + j) for j in range(MOE_TOP_K)])
    order = jnp.sort(keys) & ((1 << a_bits) - 1)
    start = jnp.cumsum(counts) - counts
    padded = (counts + MOE_BLOCK - 1) // MOE_BLOCK * MOE_BLOCK
    pad_end = jnp.cumsum(padded)
    pad_start = pad_end - padded
    slots_of_tok = [(pad_start[experts[j]] + ranks[j]).astype(jnp.int32) for j in range(MOE_TOP_K)]
    block_first = jnp.arange(n_blocks, dtype=jnp.int32) * MOE_BLOCK
    block_e = jnp.minimum(jnp.sum((pad_end[None, :] <= block_first[:, None]).astype(jnp.int32), axis=1),
                          N_EXPERTS - 1).astype(jnp.int32)
    slot_e = jnp.repeat(block_e, MOE_BLOCK)
    slot_r = jnp.arange(n_pad, dtype=jnp.int32) - pad_start[slot_e]
    slot_valid = (slot_r < counts[slot_e]) & (jnp.arange(n_pad) < pad_end[-1])
    slot_assign = order[jnp.clip(start[slot_e] + slot_r, 0, n_assign - 1)]
    slot_t = slot_assign // MOE_TOP_K
    slot_j = slot_assign % MOE_TOP_K
    slot_tok = jnp.where(slot_valid, slot_t, jnp.arange(n_pad, dtype=jnp.int32) % T)
    slot_gate = jnp.where(slot_valid, gate_cat[slot_j * T + slot_t], 0.0)
    n_used = (pad_end[-1:] // MOE_BLOCK).astype(jnp.int32)
    return slot_tok, slot_gate, slots_of_tok, block_e, n_used


def hier_moe(h2, route, counts_row, w1, w3, w2, layer):
    slot_tok, slot_gate, slots_of_tok, block_e, n_used = moe_dispatch(route, counts_row)
    yb = pl_moe_ffn(h2[slot_tok], slot_gate, block_e, n_used, w1, w3, w2, layer)
    return tuple(yb[s] for s in slots_of_tok)


def kernel(x, c, ctx, c_ctx, w_mod, b_mod, norm1_g, norm2_g, final_g, w_in, w_out, hy_conv_w, hy_conv_b, hy_ffn_w1, hy_ffn_b1, hy_ffn_w2, hy_ffn_b2, hy_ffn_w3, hy_freq, hy_skip, hy_norm_g, s5_a_re, s5_a_im, s5_log_dt, s5_b_re, s5_b_im, s5_c_re, s5_c_im, s5_d, s5_glu_w, s5_norm_g, att_lq1, att_lk1, att_lq2, att_lk2, att_subln_g, moe_wg, moe_bg, moe_we, moe_be, moe_w1, moe_w3, moe_w2):
    B, L, D = x.shape
    Lc = ctx.shape[1]
    Lt = L + Lc
    assert B == SUBLANES and Lc == TOKEN_TILE and L % ATT_TQ == 0
    n_lat_tiles = L // TOKEN_TILE
    cosf, sinf = rope_tables(L, Lc)
    hp = lax.Precision.HIGHEST
    q_scale = ATT_HEAD_DIM ** -0.5 * math.log2(math.e)

    mods = jnp.einsum('bd,ldk->lbk', jax.nn.silu(c), w_mod, precision=hp) + b_mod[:, None, :]
    cmods = jnp.einsum('d,ldk->lk', jax.nn.silu(c_ctx), w_mod, precision=hp) + b_mod
    modvs = jnp.stack([mods, jnp.broadcast_to(cmods[:, None, :], mods.shape)], axis=2)
    modvs = modvs.reshape(DEPTH, 2 * B, 1, N_MOD * D)
    filt_args = (hy_ffn_w1, hy_ffn_b1, hy_ffn_w2, hy_ffn_b2, hy_ffn_w3, hy_freq)
    h_lat = jax.vmap(lambda *p: hyena_spectrum(hyena_filter(L, *p)))(*filt_args)
    ff_ctx = jax.vmap(lambda *p: jnp.fft.rfft(hyena_filter(Lc, *p), n=2 * Lc, axis=0))(*filt_args)
    dft_tables = _fft_tables(L)
    s5_tabs = jax.vmap(jax.vmap(s5_tables))(s5_a_re, s5_a_im, s5_log_dt, s5_b_re, s5_b_im, s5_c_re, s5_c_im)
    lam_inits = [0.8 - 0.6 * math.exp(-0.3 * l) for l in range(DEPTH)]
    lams = (jnp.exp(jnp.sum(att_lq1 * att_lk1, axis=-1)) - jnp.exp(jnp.sum(att_lq2 * att_lk2, axis=-1))
            + jnp.asarray(lam_inits, F32))
    w_router = jnp.zeros((DEPTH, D, LANES), F32).at[:, :, :MOE_GROUPS].set(moe_wg)
    w_router = w_router.at[:, :, MOE_GROUPS:MOE_GROUPS + N_EXPERTS].set(moe_we)
    rwh = w_router.astype(BF16)
    rwl = (w_router - rwh.astype(F32)).astype(BF16)
    router_bias = jnp.zeros((DEPTH, 1, LANES), F32).at[:, 0, :MOE_GROUPS].set(moe_bg)
    router_bias = router_bias.at[:, 0, MOE_GROUPS:MOE_GROUPS + N_EXPERTS].set(moe_be)
    w_in_b, w_out_b, glu_w_b = w_in.astype(BF16), w_out.astype(BF16), s5_glu_w.astype(BF16)

    xs = jnp.concatenate([x, ctx], axis=1)
    for l in range(DEPTH):
        modv = modvs[l]
        u0, z, s5_u, q_p, q_r, k_r, v = pl_in_proj(xs, modv, norm1_g[l], w_in_b[l], hy_conv_w[l], hy_conv_b[l],
                                                   cosf, sinf, n_lat_tiles, q_scale)

        y_lat, y_ctx = hyena_conv(z, h_lat[l], ff_ctx[l], dft_tables, L)

        s5_y = s5_scan(s5_u.reshape(Lt * B, S5_CH), tuple(t[l] for t in s5_tabs), s5_d[l], L)

        g_scaled = (att_subln_g[l] * (1.0 - lam_inits[l])).reshape(1, ATT_V_DIM)
        att_lat, att_ctx = pl_diff_attention(q_p, q_r, k_r, v, g_scaled, lams[l], L)

        xs, h2, route, counts = pl_out_proj(xs, u0, z, y_lat, y_ctx, s5_y.reshape(Lt, B * S5_CH), att_lat, att_ctx,
                                            modv, hy_skip[l], hy_norm_g[l], glu_w_b[l], s5_norm_g[l], w_out_b[l],
                                            norm2_g[l], rwh[l], rwl[l], router_bias[l], n_lat_tiles)

        y0, y1 = hier_moe(h2.reshape(B * Lt, D), route.reshape(B * Lt, LANES), counts[0], moe_w1, moe_w3, moe_w2, l)
        xs = pl_moe_combine(xs, y0, y1, modv, final_g, n_lat_tiles, final=l == DEPTH - 1)
    return xs
```

```python
import functools
import math

import jax
import jax.numpy as jnp
import numpy as np
from jax import lax
from jax.experimental import pallas as pl
from jax.experimental.pallas import tpu as pltpu

D_MODEL = 1024
DEPTH = 4
GRID_W = 64
N_MOD = 6
EPS = 1e-6
HY_CH = D_MODEL // 4
S5_CH = D_MODEL // 4
ATT_W = D_MODEL // 2
HY_BANDS = 16
HY_DECAY_MIN = -math.log(1e-2) / 1.5
HY_DECAY_MAX = -math.log(1e-2) / 0.3
S5_GROUP = 16
S5_GROUPS = S5_CH // S5_GROUP
S5_STATE = 64
ATT_HEAD_DIM = 64
ATT_HEADS = ATT_W // (2 * ATT_HEAD_DIM)
ATT_V_DIM = 2 * ATT_HEAD_DIM
ROPE_HALF = ATT_HEAD_DIM // 2
ROPE_PAIRS_AXIS = ROPE_HALF // 2
ROPE_BASE = 10000.0
MOE_GROUPS = 4
MOE_EPG = 8
N_EXPERTS = MOE_GROUPS * MOE_EPG
MOE_TOP_K = 2
MOE_BLOCK = 512
IN_COLS = 3 * HY_CH + S5_CH + 3 * ATT_W
COL_S5 = 3 * HY_CH
COL_Q = COL_S5 + S5_CH
COL_K = COL_Q + ATT_W
COL_V = COL_K + ATT_W

LANES = 128
SUBLANES = 8
VMEM_LIMIT = 48 * 1024 * 1024
TOKEN_TILE = 256

F32 = jnp.float32
BF16 = jnp.bfloat16


def _params(n_axes, vmem=VMEM_LIMIT):
    return pltpu.CompilerParams(dimension_semantics=("arbitrary",) * n_axes, vmem_limit_bytes=vmem)


def _rms(x):
    return x * lax.rsqrt(jnp.mean(x * x, axis=-1, keepdims=True) + EPS)


def _in_kernel(x_ref, xp_ref, xn_ref, mod_ref, g_ref, w_ref, cw_ref, cb_ref, cos_ref, sin_ref,
               u0_ref, z_ref, s5_ref, qp_ref, qr_ref, k_ref, v_ref, *, n_lat_tiles, q_scale):
    i = pl.program_id(1)
    tm = x_ref.shape[1]
    d = x_ref.shape[2]
    g = g_ref[...]
    shift = mod_ref[0, :, 0:d]
    scale = mod_ref[0, :, d:2 * d]

    def norm_mod(xt):
        return (_rms(xt) * g) * (1.0 + scale) + shift

    h = jnp.concatenate([norm_mod(x_ref[0]), norm_mod(xp_ref[0]), norm_mod(xn_ref[0])], axis=0).astype(BF16)
    p = jnp.dot(h, w_ref[...], preferred_element_type=F32)

    hy = p[:tm, :COL_S5]
    is_ctx = i == n_lat_tiles
    has_prev = jnp.logical_and(i != 0, jnp.logical_not(is_ctx))
    has_next = jnp.logical_and(i != n_lat_tiles - 1, jnp.logical_not(is_ctx))
    prev_row = jnp.where(has_prev, p[tm + SUBLANES - 1:tm + SUBLANES, :COL_S5], 0.0)
    next_row = jnp.where(has_next, p[tm + SUBLANES:tm + SUBLANES + 1, :COL_S5], 0.0)
    rows = lax.broadcasted_iota(jnp.int32, (tm, 1), 0)
    up = jnp.where(rows == 0, prev_row, pltpu.roll(hy, 1, axis=0))
    dn = jnp.where(rows == tm - 1, next_row, pltpu.roll(hy, tm - 1, axis=0))
    u = up * cw_ref[0:1, :] + hy * cw_ref[1:2, :] + dn * cw_ref[2:3, :] + cb_ref[...]
    u0_ref[0] = u[:, :HY_CH].astype(u0_ref.dtype)
    z_ref[0] = u[:, HY_CH:2 * HY_CH] * u[:, 2 * HY_CH:]

    s5_ref[...] = p[:tm, COL_S5:COL_Q]

    lane = lax.broadcasted_iota(jnp.int32, (1, ATT_W), 1)
    first_half = jnp.bitwise_and(lane, ATT_HEAD_DIM - 1) < ROPE_HALF
    cos = cos_ref[...]
    sin = sin_ref[...]

    def rope(t):
        partner = jnp.where(first_half, pltpu.roll(t, ATT_W - ROPE_HALF, axis=1), pltpu.roll(t, ROPE_HALF, axis=1))
        return t * cos + partner * sin

    q = p[:tm, COL_Q:COL_K] * q_scale
    qp_ref[0] = q.astype(BF16)
    qr_ref[0] = rope(q).astype(BF16)
    k_ref[0] = rope(p[:tm, COL_K:COL_V]).astype(BF16)
    v_ref[0] = p[:tm, COL_V:].astype(BF16)


def pl_in_proj(xs, modv, norm_g, w_in, conv_w, conv_b, cosf, sinf, n_lat_tiles, q_scale):
    B, Lt, D = xs.shape
    tm = TOKEN_TILE
    n_tiles = Lt // tm
    halo_per_tile = tm // SUBLANES
    n_halo_blocks = Lt // SUBLANES
    tok = lambda w: pl.BlockSpec((1, tm, w), lambda b, i: (b, i, 0))
    const = lambda shape: pl.BlockSpec(shape, lambda b, i: (0,) * len(shape))
    out_shapes = (jax.ShapeDtypeStruct((B, Lt, HY_CH), BF16), jax.ShapeDtypeStruct((B, Lt, HY_CH), F32),
                  jax.ShapeDtypeStruct((Lt, B * S5_CH), F32),
                  jax.ShapeDtypeStruct((B, Lt, ATT_W), BF16), jax.ShapeDtypeStruct((B, Lt, ATT_W), BF16),
                  jax.ShapeDtypeStruct((B, Lt, ATT_W), BF16), jax.ShapeDtypeStruct((B, Lt, ATT_W), BF16))
    return pl.pallas_call(
        functools.partial(_in_kernel, n_lat_tiles=n_lat_tiles, q_scale=q_scale),
        grid=(B, n_tiles),
        in_specs=[tok(D),
                  pl.BlockSpec((1, SUBLANES, D), lambda b, i: (b, jnp.maximum(i * halo_per_tile - 1, 0), 0)),
                  pl.BlockSpec((1, SUBLANES, D),
                               lambda b, i: (b, jnp.minimum((i + 1) * halo_per_tile, n_halo_blocks - 1), 0)),
                  pl.BlockSpec((1, 1, N_MOD * D), lambda b, i: (2 * b + jnp.where(i >= n_lat_tiles, 1, 0), 0, 0)),
                  const((1, D)), const((D, IN_COLS)), const((3, COL_S5)), const((1, COL_S5)),
                  pl.BlockSpec((tm, ATT_W), lambda b, i: (i, 0)), pl.BlockSpec((tm, ATT_W), lambda b, i: (i, 0))],
        out_specs=(tok(HY_CH), tok(HY_CH), pl.BlockSpec((tm, S5_CH), lambda b, i: (i, b)),
                   tok(ATT_W), tok(ATT_W), tok(ATT_W), tok(ATT_W)),
        out_shape=out_shapes,
        compiler_params=_params(2),
        name="in_proj",
    )(xs, xs, xs, modv, norm_g.reshape(1, D), w_in, conv_w, conv_b.reshape(1, COL_S5), cosf, sinf)


def _out_kernel(x_ref, u0_ref, z_ref, yl_ref, yc_ref, s5_ref, al_ref, ac_ref, mod_ref, skip_ref, hg_ref,
                gw_ref, sg_ref, wo_ref, n2g_ref, rwh_ref, rwl_ref, rb_ref, tri_ref, xo_ref, h2_ref, rt_ref, cnt_ref,
                cnt_scr, *, n_lat_tiles):
    d = x_ref.shape[2]
    is_ctx = pl.program_id(1) == n_lat_tiles
    y = jnp.where(is_ctx, yc_ref[0], yl_ref[0])
    att = jnp.where(is_ctx, ac_ref[0], al_ref[0])
    hy = _rms(u0_ref[0].astype(F32) * (y + skip_ref[...] * z_ref[0])) * hg_ref[...]
    gl = jax.nn.gelu(s5_ref[...].astype(F32))
    gate = jax.nn.sigmoid(jnp.dot(gl.astype(BF16), gw_ref[...], preferred_element_type=F32))
    s5 = _rms(gl * gate) * sg_ref[...]
    mix = jnp.concatenate([hy.astype(BF16), s5.astype(BF16), att], axis=1)
    proj = jnp.dot(mix, wo_ref[...], preferred_element_type=F32)
    xn = x_ref[0] + mod_ref[0, :, 2 * d:3 * d] * proj
    xo_ref[0] = xn
    h2 = (_rms(xn) * n2g_ref[...]) * (1.0 + mod_ref[0, :, 4 * d:5 * d]) + mod_ref[0, :, 3 * d:4 * d]
    hh = h2.astype(BF16)
    hl = (h2 - hh.astype(F32)).astype(BF16)
    h2_ref[0] = hh
    lg = jnp.dot(hh, rwh_ref[...], preferred_element_type=F32)
    lg += jnp.dot(hl, rwh_ref[...], preferred_element_type=F32)
    lg += jnp.dot(hh, rwl_ref[...], preferred_element_type=F32)
    lg = lg + rb_ref[...]

    tm = lg.shape[0]
    lane = lax.broadcasted_iota(jnp.int32, (1, LANES), 1)
    neg_inf = jnp.float32(-jnp.inf)

    def first_max(v):
        m = jnp.max(v, axis=1, keepdims=True)
        return m, jnp.min(jnp.where(v == m, lane, LANES), axis=1, keepdims=True)

    g_logit = jnp.where(lane < MOE_GROUPS, lg, neg_inf)
    g_max, g_idx = first_max(g_logit)
    p_group = 1.0 / jnp.sum(jnp.exp(g_logit - g_max), axis=1, keepdims=True)
    e_lane = lane - MOE_GROUPS
    in_group = jnp.logical_and(jnp.logical_and(e_lane >= 0, e_lane < N_EXPERTS),
                               jnp.right_shift(e_lane, MOE_EPG.bit_length() - 1) == g_idx)
    e_logit = jnp.where(in_group, lg, neg_inf)
    e_exp = jnp.exp(e_logit - jnp.max(e_logit, axis=1, keepdims=True))
    probs = jnp.where(in_group, e_exp / jnp.sum(e_exp, axis=1, keepdims=True), -1.0)
    p1, l1 = first_max(probs)
    p2, l2 = first_max(jnp.where(lane == l1, -1.0, probs))
    hit1 = lane == l1
    hit2 = lane == l2
    onehot = jnp.where(hit1, 1.0, 0.0) + jnp.where(hit2, 1.0, 0.0)

    @pl.when(jnp.logical_and(pl.program_id(0) == 0, pl.program_id(1) == 0))
    def _():
        cnt_scr[...] = jnp.zeros_like(cnt_scr)

    earlier = jnp.dot(tri_ref[...], onehot.astype(BF16), preferred_element_type=F32)
    before = cnt_scr[0:1, :] + earlier
    r1 = jnp.sum(jnp.where(hit1, before, 0.0), axis=1, keepdims=True)
    r2 = jnp.sum(jnp.where(hit2, before, 0.0), axis=1, keepdims=True)
    cnt_scr[...] = jnp.broadcast_to(cnt_scr[0:1, :] + jnp.sum(onehot, axis=0, keepdims=True), cnt_scr.shape)
    cnt_ref[...] = cnt_scr[...]
    cols = [(l1 - MOE_GROUPS).astype(F32), (l2 - MOE_GROUPS).astype(F32), p_group * p1 / (p1 + p2),
            p_group * p2 / (p1 + p2), r1, r2]
    route = jnp.zeros((tm, LANES), F32)
    for j, col in enumerate(cols):
        route = jnp.where(lane == j, col, route)
    rt_ref[...] = jnp.transpose(route)[:SUBLANES, :]


def pl_out_proj(xs, u0, z, y_lat, y_ctx, s5_y, att_lat, att_ctx, modv, hy_skip, hy_norm_g, glu_w, s5_norm_g,
                w_out, norm2_g, rwh, rwl, router_bias, n_lat_tiles):
    B, Lt, D = xs.shape
    tm = TOKEN_TILE
    tri = jnp.asarray(np.tril(np.ones((tm, tm), np.float32), -1), BF16)
    tok = lambda w: pl.BlockSpec((1, tm, w), lambda b, i: (b, i, 0))
    lat_tok = lambda w: pl.BlockSpec((1, tm, w), lambda b, i: (b, jnp.minimum(i, n_lat_tiles - 1), 0))
    ctx_tok = lambda w: pl.BlockSpec((1, tm, w), lambda b, i: (b, 0, 0))
    tb = pl.BlockSpec((tm, S5_CH), lambda b, i: (i, b))
    const = lambda shape: pl.BlockSpec(shape, lambda b, i: (0,) * len(shape))
    return pl.pallas_call(
        functools.partial(_out_kernel, n_lat_tiles=n_lat_tiles),
        grid=(B, Lt // tm),
        in_specs=[tok(D), tok(HY_CH), tok(HY_CH), lat_tok(HY_CH), ctx_tok(HY_CH), tb, lat_tok(ATT_W),
                  ctx_tok(ATT_W),
                  pl.BlockSpec((1, 1, N_MOD * D), lambda b, i: (2 * b + jnp.where(i >= n_lat_tiles, 1, 0), 0, 0)),
                  const((1, HY_CH)), const((1, HY_CH)), const((S5_CH, S5_CH)), const((1, S5_CH)),
                  const((D, D)), const((1, D)), const((D, LANES)), const((D, LANES)), const((1, LANES)),
                  const((tm, tm))],
        out_specs=(tok(D), tok(D), pl.BlockSpec((SUBLANES, tm), lambda b, i: (0, b * (Lt // tm) + i)),
                   const((SUBLANES, LANES))),
        out_shape=(jax.ShapeDtypeStruct((B, Lt, D), F32), jax.ShapeDtypeStruct((B, Lt, D), BF16),
                   jax.ShapeDtypeStruct((SUBLANES, B * Lt), F32), jax.ShapeDtypeStruct((SUBLANES, LANES), F32)),
        scratch_shapes=[pltpu.VMEM((SUBLANES, LANES), F32)],
        compiler_params=_params(2),
        name="out_proj",
    )(xs, u0, z, y_lat, y_ctx, s5_y, att_lat, att_ctx, modv, hy_skip.reshape(1, HY_CH),
      hy_norm_g.reshape(1, HY_CH), glu_w, s5_norm_g.reshape(1, S5_CH), w_out, norm2_g.reshape(1, D), rwh, rwl,
      router_bias, tri)


def _dot_nt(a, b):
    return lax.dot_general(a, b, (((1,), (1,)), ((), ())), preferred_element_type=F32)


ATT_TQ = 512
ATT_SUB = 256


def _attn_kernel(qp_ref, qr_ref, k_ref, v_ref, g_ref, lam_ref, o_ref, *, n_lat):
    tq = qp_ref.shape[1]
    first_map = lax.broadcasted_iota(jnp.int32, (1, LANES), 1) < ATT_HEAD_DIM
    zero = jnp.zeros((), BF16)
    sub = min(ATT_SUB, tq)
    for r0 in range(0, tq, sub):
        qp = qp_ref[0, r0:r0 + sub, :]
        qr = qr_ref[0, r0:r0 + sub, :]
        probs = []
        for m in range(2):
            in_map = first_map if m == 0 else jnp.logical_not(first_map)
            s_c = _dot_nt(jnp.where(in_map, qp, zero), k_ref[0, n_lat:, :])
            mx = jnp.max(s_c, axis=-1, keepdims=True)
            if n_lat:
                s_l = _dot_nt(jnp.where(in_map, qr, zero), k_ref[0, :n_lat, :])
                mx = jnp.maximum(mx, jnp.max(s_l, axis=-1, keepdims=True))
                p_l = jnp.exp2(s_l - mx)
            p_c = jnp.exp2(s_c - mx)
            den = jnp.sum(p_c, axis=-1, keepdims=True)
            if n_lat:
                den = den + jnp.sum(p_l, axis=-1, keepdims=True)
            probs.append((p_c.astype(BF16), p_l.astype(BF16) if n_lat else None, 1.0 / den))
        w0 = probs[0][2].astype(BF16)
        w1 = (lam_ref[0:1, 0:1] * probs[1][2]).astype(BF16)
        a_c = probs[0][0] * w0 - probs[1][0] * w1
        o = jnp.dot(a_c, v_ref[0, n_lat:, :], preferred_element_type=F32)
        if n_lat:
            a_l = probs[0][1] * w0 - probs[1][1] * w1
            o = o + jnp.dot(a_l, v_ref[0, :n_lat, :], preferred_element_type=F32)
        o_ref[0, r0:r0 + sub, :] = (_rms(o) * g_ref[...]).astype(o_ref.dtype)


def pl_diff_attention(qp, qr, k, v, g_scaled, lam, n_lat):
    B, Lt, _ = qp.shape
    n_ctx = Lt - n_lat
    lam_arr = jnp.full((SUBLANES, LANES), lam, F32)
    small = [pl.BlockSpec((1, LANES), lambda b, h, i: (0, 0)), pl.BlockSpec((SUBLANES, LANES), lambda b, h, i: (0, 0))]
    tq = ATT_TQ
    qspec = pl.BlockSpec((1, tq, LANES), lambda b, h, i: (b, i, h))
    kspec = pl.BlockSpec((1, Lt, LANES), lambda b, h, i: (b, 0, h))
    out_lat = pl.pallas_call(
        functools.partial(_attn_kernel, n_lat=n_lat),
        grid=(B, ATT_HEADS, n_lat // tq),
        in_specs=[qspec, qspec, kspec, kspec] + small,
        out_specs=qspec,
        out_shape=jax.ShapeDtypeStruct((B, n_lat, ATT_W), BF16),
        compiler_params=_params(3),
        name="diff_attention",
    )(qp, qr, k, v, g_scaled, lam_arr)
    ctx_blk = n_lat // n_ctx
    cspec = pl.BlockSpec((1, n_ctx, LANES), lambda b, h, i: (b, ctx_blk, h))
    out_ctx = pl.pallas_call(
        functools.partial(_attn_kernel, n_lat=0),
        grid=(B, ATT_HEADS, 1),
        in_specs=[cspec, cspec, cspec, cspec] + small,
        out_specs=pl.BlockSpec((1, n_ctx, LANES), lambda b, h, i: (b, 0, h)),
        out_shape=jax.ShapeDtypeStruct((B, n_ctx, ATT_W), BF16),
        compiler_params=_params(3),
        name="diff_attention_ctx",
    )(qp, qr, k, v, g_scaled, lam_arr)
    return out_lat, out_ctx


def _moe_kernel(be_ref, nb_ref, x_ref, gate_ref, w1_ref, w3_ref, w2_ref, o_ref, w1_scr, w3_scr, w2_scr):
    i = pl.program_id(0)
    new_expert = jnp.logical_or(i == 0, be_ref[i] != be_ref[jnp.maximum(i - 1, 0)])

    @pl.when(jnp.logical_and(i < nb_ref[0], new_expert))
    def _():
        w1_scr[...] = w1_ref[0, 0].astype(BF16)
        w3_scr[...] = w3_ref[0, 0].astype(BF16)
        w2_scr[...] = w2_ref[0, 0].astype(BF16)

    @pl.when(i < nb_ref[0])
    def _():
        x = x_ref[...]
        a = jnp.dot(x, w1_scr[...], preferred_element_type=F32)
        b = jnp.dot(x, w3_scr[...], preferred_element_type=F32)
        h = (a * jax.nn.sigmoid(a)) * b
        y = jnp.dot(h.astype(BF16), w2_scr[...], preferred_element_type=F32)
        rows = lax.broadcasted_iota(jnp.int32, (MOE_BLOCK, 1), 0)
        lane = lax.broadcasted_iota(jnp.int32, (1, LANES), 1)
        g_rows = gate_ref[0, 0:1, :]
        for r in range(1, MOE_BLOCK // LANES):
            g_rows = jnp.where(rows >= r * LANES, gate_ref[0, r:r + 1, :], g_rows)
        g_col = jnp.sum(jnp.where(lane == jnp.bitwise_and(rows, LANES - 1), g_rows, 0.0), axis=1, keepdims=True)
        o_ref[...] = (y * g_col).astype(o_ref.dtype)

    @pl.when(i >= nb_ref[0])
    def _():
        o_ref[...] = jnp.zeros_like(o_ref)


def pl_moe_ffn(xb, slot_gate, block_e, n_used, w1, w3, w2, layer):
    n_pad, D = xb.shape
    n_blocks = n_pad // MOE_BLOCK
    F = w1.shape[-1]
    grid_spec = pltpu.PrefetchScalarGridSpec(
        num_scalar_prefetch=2,
        grid=(n_blocks,),
        in_specs=[pl.BlockSpec((MOE_BLOCK, D), lambda i, be, nb: (i, 0)),
                  pl.BlockSpec((1, MOE_BLOCK // LANES, LANES), lambda i, be, nb: (i, 0, 0)),
                  pl.BlockSpec((1, 1, D, F), lambda i, be, nb: (layer, be[i], 0, 0)),
                  pl.BlockSpec((1, 1, D, F), lambda i, be, nb: (layer, be[i], 0, 0)),
                  pl.BlockSpec((1, 1, F, D), lambda i, be, nb: (layer, be[i], 0, 0))],
        out_specs=pl.BlockSpec((MOE_BLOCK, D), lambda i, be, nb: (i, 0)),
        scratch_shapes=[pltpu.VMEM((D, F), BF16), pltpu.VMEM((D, F), BF16), pltpu.VMEM((F, D), BF16)],
    )
    return pl.pallas_call(
        _moe_kernel,
        grid_spec=grid_spec,
        out_shape=jax.ShapeDtypeStruct((n_pad, D), BF16),
        compiler_params=_params(1),
        name="moe_ffn",
    )(block_e, n_used, xb, slot_gate.reshape(n_blocks, MOE_BLOCK // LANES, LANES), w1, w3, w2)


def _combine_kernel(x_ref, y0_ref, y1_ref, mod_ref, g_ref, o_ref, *, final):
    d = x_ref.shape[2]
    xn = x_ref[0] + mod_ref[0, :, 5 * d:6 * d] * (y0_ref[0].astype(F32) + y1_ref[0].astype(F32))
    o_ref[0] = _rms(xn) * g_ref[...] if final else xn


def pl_moe_combine(xs, y0, y1, modv, final_g, n_lat_tiles, final):
    B, Lt, D = xs.shape
    tm = TOKEN_TILE
    n_tiles = n_lat_tiles if final else Lt // tm
    tok = pl.BlockSpec((1, tm, D), lambda b, i: (b, i, 0))
    return pl.pallas_call(
        functools.partial(_combine_kernel, final=final),
        grid=(B, n_tiles),
        in_specs=[tok, tok, tok,
                  pl.BlockSpec((1, 1, N_MOD * D), lambda b, i: (2 * b + jnp.where(i >= n_lat_tiles, 1, 0), 0, 0)),
                  pl.BlockSpec((1, D), lambda b, i: (0, 0))],
        out_specs=tok,
        out_shape=jax.ShapeDtypeStruct((B, n_tiles * tm, D), F32),
        compiler_params=_params(2),
        name="moe_combine",
    )(xs, y0.reshape(B, Lt, D), y1.reshape(B, Lt, D), modv, final_g.reshape(1, D))


S5_STATES = S5_GROUPS * S5_STATE
S5_CHUNK = 64


def _s5_kernel(u_ref, wd_ref, wr_ref, ar_ref, ai_ref, d_ref, y_ref, x_scr, h_scr, hr_scr, hi_scr, *, reverse):
    ns = S5_STATES

    @pl.when(pl.program_id(0) == 0)
    def _():
        hr_scr[...] = jnp.zeros_like(hr_scr)
        hi_scr[...] = jnp.zeros_like(hi_scr)

    u = u_ref[...]
    x_scr[...] = jnp.dot(u.astype(BF16), wd_ref[...], preferred_element_type=F32)
    ar = ar_ref[...]
    ai = ai_ref[...]

    def step(hr, hi, t):
        r = pl.multiple_of(t * SUBLANES, SUBLANES)
        xr = x_scr[pl.ds(r, SUBLANES), :ns]
        xi = x_scr[pl.ds(r, SUBLANES), ns:]
        return ar * hr - ai * hi + xr, ar * hi + ai * hr + xi

    def body(j, carry):
        hr, hi = carry
        t0 = (S5_CHUNK - 1 - 2 * j) if reverse else 2 * j
        t1 = t0 - 1 if reverse else t0 + 1
        hr0, hi0 = step(hr, hi, t0)
        hr1, hi1 = step(hr0, hi0, t1)
        lo = t1 if reverse else t0
        first_r, second_r = (hr1, hr0) if reverse else (hr0, hr1)
        first_i, second_i = (hi1, hi0) if reverse else (hi0, hi1)
        r = pl.multiple_of(lo * SUBLANES, 2 * SUBLANES)
        h_scr[pl.ds(r, 2 * SUBLANES), :ns] = jnp.concatenate([first_r, second_r], axis=0).astype(BF16)
        h_scr[pl.ds(r, 2 * SUBLANES), ns:] = jnp.concatenate([first_i, second_i], axis=0).astype(BF16)
        return hr1, hi1

    hr, hi = lax.fori_loop(0, S5_CHUNK // 2, body, (hr_scr[...], hi_scr[...]))
    hr_scr[...] = hr
    hi_scr[...] = hi
    y = jnp.dot(h_scr[...], wr_ref[...], preferred_element_type=F32)
    y = y + (d_ref[...].astype(F32) if reverse else u * d_ref[...])
    y_ref[...] = y.astype(y_ref.dtype)


def pl_s5_scan(u_tb, w_drive, w_read, a_re, a_im, addend, *, n_lat_steps, reverse):
    rows, ch = u_tb.shape
    rc = S5_CHUNK * SUBLANES
    n_chunks = rows // rc
    n_lat = n_lat_steps // S5_CHUNK
    n_ctx = n_chunks - n_lat
    assert rows % rc == 0 and n_lat_steps % S5_CHUNK == 0
    if reverse:
        def idx(i):
            return (n_chunks - 1 - i, 0)
    else:
        def idx(i):
            return (jnp.where(i < n_ctx, n_lat + i, i - n_ctx), 0)
    const = lambda i: (0, 0)
    ns2 = 2 * S5_STATES
    return pl.pallas_call(
        functools.partial(_s5_kernel, reverse=reverse),
        grid=(n_chunks,),
        in_specs=[pl.BlockSpec((rc, ch), idx),
                  pl.BlockSpec((ch, ns2), const),
                  pl.BlockSpec((ns2, ch), const),
                  pl.BlockSpec((SUBLANES, S5_STATES), const),
                  pl.BlockSpec((SUBLANES, S5_STATES), const),
                  pl.BlockSpec((rc, ch), idx) if reverse else pl.BlockSpec((1, ch), const)],
        out_specs=pl.BlockSpec((rc, ch), idx),
        out_shape=jax.ShapeDtypeStruct((rows, ch), BF16),
        scratch_shapes=[pltpu.VMEM((rc, ns2), F32), pltpu.VMEM((rc, ns2), BF16),
                        pltpu.VMEM((SUBLANES, S5_STATES), F32), pltpu.VMEM((SUBLANES, S5_STATES), F32)],
        compiler_params=_params(1),
        name="s5_scan_rev" if reverse else "s5_scan_fwd",
    )(u_tb, w_drive, w_read,
      jnp.broadcast_to(a_re[None, :], (SUBLANES, S5_STATES)),
      jnp.broadcast_to(a_im[None, :], (SUBLANES, S5_STATES)),
      addend if reverse else addend.reshape(1, ch))


FFT_N2 = 128


def _fft_tables(L):
    N = 2 * L
    N1 = N // FFT_N2
    k1 = np.arange(N1)[:, None]
    n1 = np.arange(N1 // 2)[None, :]
    n2 = np.arange(FFT_N2)[:, None, None]
    ang = -2.0 * np.pi * (k1[None] * (n2 + FFT_N2 * n1[None])) / N
    mr, mi = np.cos(ang), np.sin(ang)
    ma = np.concatenate([np.concatenate([mr, -mi], axis=2), np.concatenate([mi, mr], axis=2)], axis=1)
    gr, gi = np.transpose(mr, (0, 2, 1)), -np.transpose(mi, (0, 2, 1))
    mainv = np.concatenate([np.concatenate([gr, -gi], axis=2), np.concatenate([gi, gr], axis=2)], axis=1)
    kk = np.arange(FFT_N2)
    a2 = -2.0 * np.pi * np.outer(kk, kk) / FFT_N2
    fr, fi = np.cos(a2), np.sin(a2)
    f_fwd = np.block([[fr, -fi], [fi, fr]])
    f_inv = np.block([[fr, fi], [-fi, fr]])
    return (jnp.asarray(ma, BF16), jnp.asarray(mainv, BF16), jnp.asarray(f_fwd, BF16), jnp.asarray(f_inv, BF16))


HYENA_VMEM_LIMIT = 56 * 1024 * 1024
FFT_UNROLL = 8


def _hyena_fft_kernel(z_ref, h_ref, ma_ref, mainv_ref, ff_ref, fi_ref, o_ref, a_scr, b_scr, *, n1_count):
    half = n1_count // 2
    n2c = FFT_N2

    def stage_a(n2, c):
        xr = z_ref[0, pl.ds(n2, half, stride=n2c), :]
        xi = z_ref[1, pl.ds(n2, half, stride=n2c), :]
        x = jnp.concatenate([xr, xi], axis=0).astype(BF16)
        r = jnp.dot(ma_ref[n2], x, preferred_element_type=F32)
        a_scr[pl.ds(pl.multiple_of(n2 * 2 * n1_count, 2 * n1_count), 2 * n1_count), :] = r
        return c

    lax.fori_loop(0, n2c, stage_a, 0, unroll=FFT_UNROLL)

    def stage_c(k1, c):
        ar = a_scr[pl.ds(k1, n2c, stride=2 * n1_count), :]
        ai = a_scr[pl.ds(n1_count + k1, n2c, stride=2 * n1_count), :]
        x = jnp.concatenate([ar, ai], axis=0).astype(BF16)
        y = jnp.dot(ff_ref[...], x, preferred_element_type=F32)
        yr, yi = y[:n2c], y[n2c:]
        hr = h_ref[0, k1].astype(F32)
        hi = h_ref[1, k1].astype(F32)
        x2 = jnp.concatenate([yr * hr - yi * hi, yr * hi + yi * hr], axis=0).astype(BF16)
        b = jnp.dot(fi_ref[...], x2, preferred_element_type=F32)
        b_scr[pl.ds(pl.multiple_of(k1 * 2 * n2c, 2 * n2c), 2 * n2c), :] = b
        return c

    lax.fori_loop(0, n1_count, stage_c, 0, unroll=FFT_UNROLL)

    def stage_a_inv(n2, c):
        br = b_scr[pl.ds(n2, n1_count, stride=2 * n2c), :]
        bi = b_scr[pl.ds(n2c + n2, n1_count, stride=2 * n2c), :]
        x = jnp.concatenate([br, bi], axis=0).astype(BF16)
        r = jnp.dot(mainv_ref[n2], x, preferred_element_type=F32)
        o_ref[0, pl.ds(n2, half, stride=n2c), :] = r[:half]
        o_ref[1, pl.ds(n2, half, stride=n2c), :] = r[half:]
        return c

    lax.fori_loop(0, n2c, stage_a_inv, 0, unroll=FFT_UNROLL)


def hyena_spectrum(filt):
    N, C = filt.shape
    N1 = N // FFT_N2
    h2 = (jnp.fft.fft(filt, axis=0) / N).reshape(FFT_N2, N1, C).transpose(1, 0, 2)
    return jnp.stack([h2.real, h2.imag]).astype(BF16)


def pl_hyena_conv(z, h, dft_tables, L):
    B, _, C = z.shape
    N1 = 2 * L // FFT_N2
    ma, mainv, f_fwd, f_inv = dft_tables
    cw = LANES
    full = lambda arr: pl.BlockSpec(arr.shape, lambda ct, bp: (0,) * arr.ndim)
    return pl.pallas_call(
        functools.partial(_hyena_fft_kernel, n1_count=N1),
        grid=(C // cw, B // 2),
        in_specs=[pl.BlockSpec((2, L, cw), lambda ct, bp: (bp, 0, ct)),
                  pl.BlockSpec((2, N1, FFT_N2, cw), lambda ct, bp: (0, 0, 0, ct)),
                  full(ma), full(mainv), full(f_fwd), full(f_inv)],
        out_specs=pl.BlockSpec((2, L, cw), lambda ct, bp: (bp, 0, ct)),
        out_shape=jax.ShapeDtypeStruct((B, L, C), F32),
        scratch_shapes=[pltpu.VMEM((FFT_N2 * 2 * N1, cw), F32), pltpu.VMEM((N1 * 2 * FFT_N2, cw), F32)],
        compiler_params=_params(2, HYENA_VMEM_LIMIT),
        name="hyena_fft_conv",
    )(z, h, ma, mainv, f_fwd, f_inv)


def hyena_filter(L, w1, b1, w2, b2, w3, freq):
    t = jnp.arange(L, dtype=F32) / L
    ang = (2.0 * math.pi) * t[:, None] * jnp.arange(1, HY_BANDS + 1, dtype=F32)
    feat = jnp.concatenate([t[:, None], jnp.cos(ang), jnp.sin(ang)], axis=-1)
    hp = lax.Precision.HIGHEST
    h = jnp.sin(freq * (jnp.dot(feat, w1, precision=hp) + b1))
    h = jnp.sin(freq * (jnp.dot(h, w2, precision=hp) + b2))
    h = jnp.dot(h, w3, precision=hp).reshape(L, 2, HY_CH)
    window = jnp.exp(-t[:, None] * jnp.linspace(HY_DECAY_MIN, HY_DECAY_MAX, HY_CH, dtype=F32))
    h = h * window[:, None, :]
    filt = jnp.concatenate([h[:, 0], jnp.zeros((1, HY_CH), F32), h[:0:-1, 1]], axis=0)
    return filt / (jnp.sum(jnp.abs(filt), axis=0, keepdims=True) + EPS)


def hyena_conv(z, h_lat, ff_ctx, dft_tables, L):
    Lc = z.shape[1] - L
    y_lat = pl_hyena_conv(z, h_lat, dft_tables, L)
    zf = jnp.fft.rfft(z[:, L:], n=2 * Lc, axis=1)
    y_ctx = jnp.fft.irfft(zf * ff_ctx[None], n=2 * Lc, axis=1)[:, :Lc]
    return y_lat, y_ctx


def _block_diag(blocks):
    G, r, c = blocks.shape
    eye = jnp.eye(G, dtype=blocks.dtype)
    return (eye[:, None, :, None] * blocks[:, :, None, :]).reshape(G * r, G * c)


def s5_tables(a_re, a_im, log_dt, b_re, b_im, c_re, c_im):
    A = lax.complex(a_re, a_im)
    dtA = jnp.exp(log_dt)[:, None] * A
    a_bar = jnp.exp(dtA)
    b_bar = ((a_bar - 1.0) / A)[:, :, None] * lax.complex(b_re, b_im)
    bt_re = jnp.transpose(b_bar.real, (0, 2, 1))
    bt_im = jnp.transpose(b_bar.imag, (0, 2, 1))
    w_drive = jnp.concatenate([_block_diag(bt_re), _block_diag(bt_im)], axis=1)
    ct_re = jnp.transpose(c_re, (0, 2, 1))
    ct_im = jnp.transpose(c_im, (0, 2, 1))
    w_read = jnp.concatenate([_block_diag(ct_re), -_block_diag(ct_im)], axis=0)
    return w_drive.astype(BF16), w_read.astype(BF16), a_bar.real.reshape(-1), a_bar.imag.reshape(-1)


def s5_scan(u_tb, tables, d_skip, n_lat_steps):
    y = d_skip
    for direction in range(2):
        w_drive, w_read, a_re, a_im = (t[direction] for t in tables)
        y = pl_s5_scan(u_tb, w_drive, w_read, a_re, a_im, y, n_lat_steps=n_lat_steps, reverse=direction == 1)
    return y


def rope_tables(L, Lc):
    rows = L // GRID_W
    row = jnp.repeat(jnp.arange(rows, dtype=F32), GRID_W)
    col = jnp.tile(jnp.arange(GRID_W, dtype=F32), rows)
    inv = ROPE_BASE ** (-jnp.arange(ROPE_PAIRS_AXIS, dtype=F32) / ROPE_PAIRS_AXIS)
    ang = jnp.concatenate([row[:, None] * inv, col[:, None] * inv], axis=-1)
    cos, sin = jnp.cos(ang), jnp.sin(ang)
    n_maps = ATT_W // ATT_HEAD_DIM
    cosf = jnp.tile(jnp.concatenate([cos, cos], axis=-1), (1, n_maps))
    sinf = jnp.tile(jnp.concatenate([-sin, sin], axis=-1), (1, n_maps))
    return (jnp.concatenate([cosf, jnp.ones((Lc, ATT_W), F32)], axis=0),
            jnp.concatenate([sinf, jnp.zeros((Lc, ATT_W), F32)], axis=0))


def moe_dispatch(route, counts_row):
    T = route.shape[1]
    experts = [route[j].astype(jnp.int32) for j in range(MOE_TOP_K)]
    ranks = [route[2 * MOE_TOP_K + j].astype(jnp.int32) for j in range(MOE_TOP_K)]
    gate_cat = route[MOE_TOP_K:2 * MOE_TOP_K].reshape(-1)
    counts = counts_row[MOE_GROUPS:MOE_GROUPS + N_EXPERTS].astype(jnp.int32)
    n_assign = T * MOE_TOP_K
    n_blocks = -(-n_assign // MOE_BLOCK) + N_EXPERTS
    n_pad = n_blocks * MOE_BLOCK
    a_bits = (n_assign - 1).bit_length()
    t_ids = jnp.arange(T, dtype=jnp.int32)
    keys = jnp.concatenate([(experts[j] << a_bits) + (MOE_TOP_K * t_ids + j) for j in range(MOE_TOP_K)])
    order = jnp.sort(keys) & ((1 << a_bits) - 1)
    start = jnp.cumsum(counts) - counts
    padded = (counts + MOE_BLOCK - 1) // MOE_BLOCK * MOE_BLOCK
    pad_end = jnp.cumsum(padded)
    pad_start = pad_end - padded
    slots_of_tok = [(pad_start[experts[j]] + ranks[j]).astype(jnp.int32) for j in range(MOE_TOP_K)]
    block_first = jnp.arange(n_blocks, dtype=jnp.int32) * MOE_BLOCK
    block_e = jnp.minimum(jnp.sum((pad_end[None, :] <= block_first[:, None]).astype(jnp.int32), axis=1),
                          N_EXPERTS - 1).astype(jnp.int32)
    slot_e = jnp.repeat(block_e, MOE_BLOCK)
    slot_r = jnp.arange(n_pad, dtype=jnp.int32) - pad_start[slot_e]
    slot_valid = (slot_r < counts[slot_e]) & (jnp.arange(n_pad) < pad_end[-1])
    slot_assign = order[jnp.clip(start[slot_e] + slot_r, 0, n_assign - 1)]
    slot_t = slot_assign // MOE_TOP_K
    slot_j = slot_assign % MOE_TOP_K
    slot_tok = jnp.where(slot_valid, slot_t, jnp.arange(n_pad, dtype=jnp.int32) % T)
    slot_gate = jnp.where(slot_valid, gate_cat[slot_j * T + slot_t], 0.0)
    n_used = (pad_end[-1:] // MOE_BLOCK).astype(jnp.int32)
    return slot_tok, slot_gate, slots_of_tok, block_e, n_used


def hier_moe(h2, route, counts_row, w1, w3, w2, layer):
    slot_tok, slot_gate, slots_of_tok, block_e, n_used = moe_dispatch(route, counts_row)
    yb = pl_moe_ffn(h2[slot_tok], slot_gate, block_e, n_used, w1, w3, w2, layer)
    return tuple(yb[s] for s in slots_of_tok)


def kernel(x, c, ctx, c_ctx, w_mod, b_mod, norm1_g, norm2_g, final_g, w_in, w_out, hy_conv_w, hy_conv_b, hy_ffn_w1, hy_ffn_b1, hy_ffn_w2, hy_ffn_b2, hy_ffn_w3, hy_freq, hy_skip, hy_norm_g, s5_a_re, s5_a_im, s5_log_dt, s5_b_re, s5_b_im, s5_c_re, s5_c_im, s5_d, s5_glu_w, s5_norm_g, att_lq1, att_lk1, att_lq2, att_lk2, att_subln_g, moe_wg, moe_bg, moe_we, moe_be, moe_w1, moe_w3, moe_w2):
    B, L, D = x.shape
    Lc = ctx.shape[1]
    Lt = L + Lc
    assert B == SUBLANES and Lc == TOKEN_TILE and L % ATT_TQ == 0
    n_lat_tiles = L // TOKEN_TILE
    cosf, sinf = rope_tables(L, Lc)
    hp = lax.Precision.HIGHEST
    q_scale = ATT_HEAD_DIM ** -0.5 * math.log2(math.e)

    mods = jnp.einsum('bd,ldk->lbk', jax.nn.silu(c), w_mod, precision=hp) + b_mod[:, None, :]
    cmods = jnp.einsum('d,ldk->lk', jax.nn.silu(c_ctx), w_mod, precision=hp) + b_mod
    modvs = jnp.stack([mods, jnp.broadcast_to(cmods[:, None, :], mods.shape)], axis=2)
    modvs = modvs.reshape(DEPTH, 2 * B, 1, N_MOD * D)
    filt_args = (hy_ffn_w1, hy_ffn_b1, hy_ffn_w2, hy_ffn_b2, hy_ffn_w3, hy_freq)
    h_lat = jax.vmap(lambda *p: hyena_spectrum(hyena_filter(L, *p)))(*filt_args)
    ff_ctx = jax.vmap(lambda *p: jnp.fft.rfft(hyena_filter(Lc, *p), n=2 * Lc, axis=0))(*filt_args)
    dft_tables = _fft_tables(L)
    s5_tabs = jax.vmap(jax.vmap(s5_tables))(s5_a_re, s5_a_im, s5_log_dt, s5_b_re, s5_b_im, s5_c_re, s5_c_im)
    lam_inits = [0.8 - 0.6 * math.exp(-0.3 * l) for l in range(DEPTH)]
    lams = (jnp.exp(jnp.sum(att_lq1 * att_lk1, axis=-1)) - jnp.exp(jnp.sum(att_lq2 * att_lk2, axis=-1))
            + jnp.asarray(lam_inits, F32))
    w_router = jnp.zeros((DEPTH, D, LANES), F32).at[:, :, :MOE_GROUPS].set(moe_wg)
    w_router = w_router.at[:, :, MOE_GROUPS:MOE_GROUPS + N_EXPERTS].set(moe_we)
    rwh = w_router.astype(BF16)
    rwl = (w_router - rwh.astype(F32)).astype(BF16)
    router_bias = jnp.zeros((DEPTH, 1, LANES), F32).at[:, 0, :MOE_GROUPS].set(moe_bg)
    router_bias = router_bias.at[:, 0, MOE_GROUPS:MOE_GROUPS + N_EXPERTS].set(moe_be)
    w_in_b, w_out_b, glu_w_b = w_in.astype(BF16), w_out.astype(BF16), s5_glu_w.astype(BF16)

    xs = jnp.concatenate([x, ctx], axis=1)
    for l in range(DEPTH):
        modv = modvs[l]
        u0, z, s5_u, q_p, q_r, k_r, v = pl_in_proj(xs, modv, norm1_g[l], w_in_b[l], hy_conv_w[l], hy_conv_b[l],
                                                   cosf, sinf, n_lat_tiles, q_scale)

        y_lat, y_ctx = hyena_conv(z, h_lat[l], ff_ctx[l], dft_tables, L)

        s5_y = s5_scan(s5_u.reshape(Lt * B, S5_CH), tuple(t[l] for t in s5_tabs), s5_d[l], L)

        g_scaled = (att_subln_g[l] * (1.0 - lam_inits[l])).reshape(1, ATT_V_DIM)
        att_lat, att_ctx = pl_diff_attention(q_p, q_r, k_r, v, g_scaled, lams[l], L)

        xs, h2, route, counts = pl_out_proj(xs, u0, z, y_lat, y_ctx, s5_y.reshape(Lt, B * S5_CH), att_lat, att_ctx,
                                            modv, hy_skip[l], hy_norm_g[l], glu_w_b[l], s5_norm_g[l], w_out_b[l],
                                            norm2_g[l], rwh[l], rwl[l], router_bias[l], n_lat_tiles)

        y0, y1 = hier_moe(h2.reshape(B * Lt, D), route, counts[0], moe_w1, moe_w3, moe_w2, l)
        xs = pl_moe_combine(xs, y0, y1, modv, final_g, n_lat_tiles, final=l == DEPTH - 1)
    return xs
```

```python
import functools
import math

import jax
import jax.numpy as jnp
import numpy as np
from jax import lax
from jax.experimental import pallas as pl
from jax.experimental.pallas import tpu as pltpu

D_MODEL = 1024
DEPTH = 4
GRID_W = 64
N_MOD = 6
EPS = 1e-6
HY_CH = D_MODEL // 4
S5_CH = D_MODEL // 4
ATT_W = D_MODEL // 2
HY_BANDS = 16
HY_DECAY_MIN = -math.log(1e-2) / 1.5
HY_DECAY_MAX = -math.log(1e-2) / 0.3
S5_GROUP = 16
S5_GROUPS = S5_CH // S5_GROUP
S5_STATE = 64
ATT_HEAD_DIM = 64
ATT_HEADS = ATT_W // (2 * ATT_HEAD_DIM)
ATT_V_DIM = 2 * ATT_HEAD_DIM
ROPE_HALF = ATT_HEAD_DIM // 2
ROPE_PAIRS_AXIS = ROPE_HALF // 2
ROPE_BASE = 10000.0
MOE_GROUPS = 4
MOE_EPG = 8
N_EXPERTS = MOE_GROUPS * MOE_EPG
MOE_TOP_K = 2
MOE_BLOCK = 512
IN_COLS = 3 * HY_CH + S5_CH + 3 * ATT_W
COL_S5 = 3 * HY_CH
COL_Q = COL_S5 + S5_CH
COL_K = COL_Q + ATT_W
COL_V = COL_K + ATT_W

LANES = 128
SUBLANES = 8
VMEM_LIMIT = 48 * 1024 * 1024
TOKEN_TILE = 256

F32 = jnp.float32
BF16 = jnp.bfloat16


def _params(n_axes, vmem=VMEM_LIMIT):
    return pltpu.CompilerParams(dimension_semantics=("arbitrary",) * n_axes, vmem_limit_bytes=vmem)


def _rms(x):
    return x * lax.rsqrt(jnp.mean(x * x, axis=-1, keepdims=True) + EPS)


def _in_kernel(x_ref, xp_ref, xn_ref, mod_ref, g_ref, w_ref, cw_ref, cb_ref, cos_ref, sin_ref,
               u0_ref, z_ref, s5_ref, qp_ref, qr_ref, k_ref, v_ref, *, n_lat_tiles, q_scale):
    i = pl.program_id(1)
    tm = x_ref.shape[1]
    d = x_ref.shape[2]
    g = g_ref[...]
    shift = mod_ref[0, :, 0:d]
    scale = mod_ref[0, :, d:2 * d]

    def norm_mod(xt):
        return (_rms(xt) * g) * (1.0 + scale) + shift

    h = jnp.concatenate([norm_mod(x_ref[0]), norm_mod(xp_ref[0]), norm_mod(xn_ref[0])], axis=0).astype(BF16)
    p = jnp.dot(h, w_ref[...], preferred_element_type=F32)

    hy = p[:tm, :COL_S5]
    is_ctx = i == n_lat_tiles
    has_prev = jnp.logical_and(i != 0, jnp.logical_not(is_ctx))
    has_next = jnp.logical_and(i != n_lat_tiles - 1, jnp.logical_not(is_ctx))
    prev_row = jnp.where(has_prev, p[tm + SUBLANES - 1:tm + SUBLANES, :COL_S5], 0.0)
    next_row = jnp.where(has_next, p[tm + SUBLANES:tm + SUBLANES + 1, :COL_S5], 0.0)
    rows = lax.broadcasted_iota(jnp.int32, (tm, 1), 0)
    up = jnp.where(rows == 0, prev_row, pltpu.roll(hy, 1, axis=0))
    dn = jnp.where(rows == tm - 1, next_row, pltpu.roll(hy, tm - 1, axis=0))
    u = up * cw_ref[0:1, :] + hy * cw_ref[1:2, :] + dn * cw_ref[2:3, :] + cb_ref[...]
    u0_ref[0] = u[:, :HY_CH].astype(u0_ref.dtype)
    z_ref[0] = u[:, HY_CH:2 * HY_CH] * u[:, 2 * HY_CH:]

    s5_ref[...] = p[:tm, COL_S5:COL_Q]

    lane = lax.broadcasted_iota(jnp.int32, (1, ATT_W), 1)
    first_half = jnp.bitwise_and(lane, ATT_HEAD_DIM - 1) < ROPE_HALF
    cos = cos_ref[...]
    sin = sin_ref[...]

    def rope(t):
        partner = jnp.where(first_half, pltpu.roll(t, ATT_W - ROPE_HALF, axis=1), pltpu.roll(t, ROPE_HALF, axis=1))
        return t * cos + partner * sin

    q = p[:tm, COL_Q:COL_K] * q_scale
    qp_ref[0] = q.astype(BF16)
    qr_ref[0] = rope(q).astype(BF16)
    k_ref[0] = rope(p[:tm, COL_K:COL_V]).astype(BF16)
    v_ref[0] = p[:tm, COL_V:].astype(BF16)


def pl_in_proj(xs, modv, norm_g, w_in, conv_w, conv_b, cosf, sinf, n_lat_tiles, q_scale):
    B, Lt, D = xs.shape
    tm = TOKEN_TILE
    n_tiles = Lt // tm
    halo_per_tile = tm // SUBLANES
    n_halo_blocks = Lt // SUBLANES
    tok = lambda w: pl.BlockSpec((1, tm, w), lambda b, i: (b, i, 0))
    const = lambda shape: pl.BlockSpec(shape, lambda b, i: (0,) * len(shape))
    out_shapes = (jax.ShapeDtypeStruct((B, Lt, HY_CH), BF16), jax.ShapeDtypeStruct((B, Lt, HY_CH), F32),
                  jax.ShapeDtypeStruct((Lt, B * S5_CH), F32),
                  jax.ShapeDtypeStruct((B, Lt, ATT_W), BF16), jax.ShapeDtypeStruct((B, Lt, ATT_W), BF16),
                  jax.ShapeDtypeStruct((B, Lt, ATT_W), BF16), jax.ShapeDtypeStruct((B, Lt, ATT_W), BF16))
    return pl.pallas_call(
        functools.partial(_in_kernel, n_lat_tiles=n_lat_tiles, q_scale=q_scale),
        grid=(B, n_tiles),
        in_specs=[tok(D),
                  pl.BlockSpec((1, SUBLANES, D), lambda b, i: (b, jnp.maximum(i * halo_per_tile - 1, 0), 0)),
                  pl.BlockSpec((1, SUBLANES, D),
                               lambda b, i: (b, jnp.minimum((i + 1) * halo_per_tile, n_halo_blocks - 1), 0)),
                  pl.BlockSpec((1, 1, N_MOD * D), lambda b, i: (2 * b + jnp.where(i >= n_lat_tiles, 1, 0), 0, 0)),
                  const((1, D)), const((D, IN_COLS)), const((3, COL_S5)), const((1, COL_S5)),
                  pl.BlockSpec((tm, ATT_W), lambda b, i: (i, 0)), pl.BlockSpec((tm, ATT_W), lambda b, i: (i, 0))],
        out_specs=(tok(HY_CH), tok(HY_CH), pl.BlockSpec((tm, S5_CH), lambda b, i: (i, b)),
                   tok(ATT_W), tok(ATT_W), tok(ATT_W), tok(ATT_W)),
        out_shape=out_shapes,
        compiler_params=_params(2),
        name="in_proj",
    )(xs, xs, xs, modv, norm_g.reshape(1, D), w_in, conv_w, conv_b.reshape(1, COL_S5), cosf, sinf)


def _out_kernel(x_ref, u0_ref, z_ref, yl_ref, yc_ref, s5_ref, al_ref, ac_ref, mod_ref, skip_ref, hg_ref,
                gw_ref, sg_ref, wo_ref, n2g_ref, rwh_ref, rwl_ref, rb_ref, tri_ref, xo_ref, h2_ref, rt_ref, cnt_ref,
                cnt_scr, *, n_lat_tiles):
    d = x_ref.shape[2]
    is_ctx = pl.program_id(1) == n_lat_tiles
    y = jnp.where(is_ctx, yc_ref[0], yl_ref[0])
    att = jnp.where(is_ctx, ac_ref[0], al_ref[0])
    hy = _rms(u0_ref[0].astype(F32) * (y + skip_ref[...] * z_ref[0])) * hg_ref[...]
    gl = jax.nn.gelu(s5_ref[...].astype(F32))
    gate = jax.nn.sigmoid(jnp.dot(gl.astype(BF16), gw_ref[...], preferred_element_type=F32))
    s5 = _rms(gl * gate) * sg_ref[...]
    mix = jnp.concatenate([hy.astype(BF16), s5.astype(BF16), att], axis=1)
    proj = jnp.dot(mix, wo_ref[...], preferred_element_type=F32)
    xn = x_ref[0] + mod_ref[0, :, 2 * d:3 * d] * proj
    xo_ref[0] = xn
    h2 = (_rms(xn) * n2g_ref[...]) * (1.0 + mod_ref[0, :, 4 * d:5 * d]) + mod_ref[0, :, 3 * d:4 * d]
    hh = h2.astype(BF16)
    hl = (h2 - hh.astype(F32)).astype(BF16)
    h2_ref[0] = hh
    lg = jnp.dot(hh, rwh_ref[...], preferred_element_type=F32)
    lg += jnp.dot(hl, rwh_ref[...], preferred_element_type=F32)
    lg += jnp.dot(hh, rwl_ref[...], preferred_element_type=F32)
    lg = lg + rb_ref[...]

    n_rows = cnt_scr.shape[0]
    lt = jnp.transpose(lg)[:n_rows, :]
    row = lax.broadcasted_iota(jnp.int32, (n_rows, 1), 0)
    neg_inf = jnp.float32(-jnp.inf)

    def first_max(v):
        m = jnp.max(v, axis=0, keepdims=True)
        return m, jnp.min(jnp.where(v == m, row, n_rows), axis=0, keepdims=True)

    g_logit = jnp.where(row < MOE_GROUPS, lt, neg_inf)
    g_max, g_idx = first_max(g_logit)
    p_group = 1.0 / jnp.sum(jnp.exp(g_logit - g_max), axis=0, keepdims=True)
    e_row = row - MOE_GROUPS
    in_group = jnp.logical_and(jnp.logical_and(e_row >= 0, e_row < N_EXPERTS),
                               jnp.right_shift(e_row, MOE_EPG.bit_length() - 1) == g_idx)
    e_logit = jnp.where(in_group, lt, neg_inf)
    e_exp = jnp.exp(e_logit - jnp.max(e_logit, axis=0, keepdims=True))
    probs = jnp.where(in_group, e_exp / jnp.sum(e_exp, axis=0, keepdims=True), -1.0)
    p1, l1 = first_max(probs)
    p2, l2 = first_max(jnp.where(row == l1, -1.0, probs))
    hit1 = row == l1
    hit2 = row == l2
    onehot = jnp.where(hit1, 1.0, 0.0) + jnp.where(hit2, 1.0, 0.0)

    @pl.when(jnp.logical_and(pl.program_id(0) == 0, pl.program_id(1) == 0))
    def _():
        cnt_scr[...] = jnp.zeros_like(cnt_scr)

    earlier = jnp.dot(onehot.astype(BF16), tri_ref[...], preferred_element_type=F32)
    before = cnt_scr[:, 0:1] + earlier
    r1 = jnp.sum(jnp.where(hit1, before, 0.0), axis=0, keepdims=True)
    r2 = jnp.sum(jnp.where(hit2, before, 0.0), axis=0, keepdims=True)
    cnt_scr[...] = cnt_scr[...] + jnp.sum(onehot, axis=1, keepdims=True)
    cnt_ref[...] = cnt_scr[...]
    fields = [(l1 - MOE_GROUPS).astype(F32), (l2 - MOE_GROUPS).astype(F32), p_group * p1 / (p1 + p2),
              p_group * p2 / (p1 + p2), r1, r2]
    field_row = lax.broadcasted_iota(jnp.int32, (SUBLANES, 1), 0)
    route = jnp.zeros((SUBLANES, lt.shape[1]), F32)
    for j, f in enumerate(fields):
        route = jnp.where(field_row == j, f, route)
    rt_ref[...] = route


def pl_out_proj(xs, u0, z, y_lat, y_ctx, s5_y, att_lat, att_ctx, modv, hy_skip, hy_norm_g, glu_w, s5_norm_g,
                w_out, norm2_g, rwh, rwl, router_bias, n_lat_tiles):
    B, Lt, D = xs.shape
    tm = TOKEN_TILE
    tri = jnp.asarray(np.triu(np.ones((tm, tm), np.float32), 1), BF16)
    n_rows = -(-(MOE_GROUPS + N_EXPERTS) // 16) * 16
    tok = lambda w: pl.BlockSpec((1, tm, w), lambda b, i: (b, i, 0))
    lat_tok = lambda w: pl.BlockSpec((1, tm, w), lambda b, i: (b, jnp.minimum(i, n_lat_tiles - 1), 0))
    ctx_tok = lambda w: pl.BlockSpec((1, tm, w), lambda b, i: (b, 0, 0))
    tb = pl.BlockSpec((tm, S5_CH), lambda b, i: (i, b))
    const = lambda shape: pl.BlockSpec(shape, lambda b, i: (0,) * len(shape))
    return pl.pallas_call(
        functools.partial(_out_kernel, n_lat_tiles=n_lat_tiles),
        grid=(B, Lt // tm),
        in_specs=[tok(D), tok(HY_CH), tok(HY_CH), lat_tok(HY_CH), ctx_tok(HY_CH), tb, lat_tok(ATT_W),
                  ctx_tok(ATT_W),
                  pl.BlockSpec((1, 1, N_MOD * D), lambda b, i: (2 * b + jnp.where(i >= n_lat_tiles, 1, 0), 0, 0)),
                  const((1, HY_CH)), const((1, HY_CH)), const((S5_CH, S5_CH)), const((1, S5_CH)),
                  const((D, D)), const((1, D)), const((D, LANES)), const((D, LANES)), const((1, LANES)),
                  const((tm, tm))],
        out_specs=(tok(D), tok(D), pl.BlockSpec((SUBLANES, tm), lambda b, i: (0, b * (Lt // tm) + i)),
                   const((n_rows, LANES))),
        out_shape=(jax.ShapeDtypeStruct((B, Lt, D), F32), jax.ShapeDtypeStruct((B, Lt, D), BF16),
                   jax.ShapeDtypeStruct((SUBLANES, B * Lt), F32), jax.ShapeDtypeStruct((n_rows, LANES), F32)),
        scratch_shapes=[pltpu.VMEM((n_rows, LANES), F32)],
        compiler_params=_params(2),
        name="out_proj",
    )(xs, u0, z, y_lat, y_ctx, s5_y, att_lat, att_ctx, modv, hy_skip.reshape(1, HY_CH),
      hy_norm_g.reshape(1, HY_CH), glu_w, s5_norm_g.reshape(1, S5_CH), w_out, norm2_g.reshape(1, D), rwh, rwl,
      router_bias, tri)


def _dot_nt(a, b):
    return lax.dot_general(a, b, (((1,), (1,)), ((), ())), preferred_element_type=F32)


ATT_TQ = 512
ATT_SUB = 256


def _attn_kernel(qp_ref, qr_ref, k_ref, v_ref, g_ref, lam_ref, o_ref, *, n_lat):
    tq = qp_ref.shape[1]
    first_map = lax.broadcasted_iota(jnp.int32, (1, LANES), 1) < ATT_HEAD_DIM
    zero = jnp.zeros((), BF16)
    sub = min(ATT_SUB, tq)
    for r0 in range(0, tq, sub):
        qp = qp_ref[0, r0:r0 + sub, :]
        qr = qr_ref[0, r0:r0 + sub, :]
        probs = []
        for m in range(2):
            in_map = first_map if m == 0 else jnp.logical_not(first_map)
            s_c = _dot_nt(jnp.where(in_map, qp, zero), k_ref[0, n_lat:, :])
            mx = jnp.max(s_c, axis=-1, keepdims=True)
            if n_lat:
                s_l = _dot_nt(jnp.where(in_map, qr, zero), k_ref[0, :n_lat, :])
                mx = jnp.maximum(mx, jnp.max(s_l, axis=-1, keepdims=True))
                p_l = jnp.exp2(s_l - mx)
            p_c = jnp.exp2(s_c - mx)
            den = jnp.sum(p_c, axis=-1, keepdims=True)
            if n_lat:
                den = den + jnp.sum(p_l, axis=-1, keepdims=True)
            probs.append((p_c.astype(BF16), p_l.astype(BF16) if n_lat else None, 1.0 / den))
        w0 = probs[0][2].astype(BF16)
        w1 = (lam_ref[0:1, 0:1] * probs[1][2]).astype(BF16)
        a_c = probs[0][0] * w0 - probs[1][0] * w1
        o = jnp.dot(a_c, v_ref[0, n_lat:, :], preferred_element_type=F32)
        if n_lat:
            a_l = probs[0][1] * w0 - probs[1][1] * w1
            o = o + jnp.dot(a_l, v_ref[0, :n_lat, :], preferred_element_type=F32)
        o_ref[0, r0:r0 + sub, :] = (_rms(o) * g_ref[...]).astype(o_ref.dtype)


def pl_diff_attention(qp, qr, k, v, g_scaled, lam, n_lat):
    B, Lt, _ = qp.shape
    n_ctx = Lt - n_lat
    lam_arr = jnp.full((SUBLANES, LANES), lam, F32)
    small = [pl.BlockSpec((1, LANES), lambda b, h, i: (0, 0)), pl.BlockSpec((SUBLANES, LANES), lambda b, h, i: (0, 0))]
    tq = ATT_TQ
    qspec = pl.BlockSpec((1, tq, LANES), lambda b, h, i: (b, i, h))
    kspec = pl.BlockSpec((1, Lt, LANES), lambda b, h, i: (b, 0, h))
    out_lat = pl.pallas_call(
        functools.partial(_attn_kernel, n_lat=n_lat),
        grid=(B, ATT_HEADS, n_lat // tq),
        in_specs=[qspec, qspec, kspec, kspec] + small,
        out_specs=qspec,
        out_shape=jax.ShapeDtypeStruct((B, n_lat, ATT_W), BF16),
        compiler_params=_params(3),
        name="diff_attention",
    )(qp, qr, k, v, g_scaled, lam_arr)
    ctx_blk = n_lat // n_ctx
    cspec = pl.BlockSpec((1, n_ctx, LANES), lambda b, h, i: (b, ctx_blk, h))
    out_ctx = pl.pallas_call(
        functools.partial(_attn_kernel, n_lat=0),
        grid=(B, ATT_HEADS, 1),
        in_specs=[cspec, cspec, cspec, cspec] + small,
        out_specs=pl.BlockSpec((1, n_ctx, LANES), lambda b, h, i: (b, 0, h)),
        out_shape=jax.ShapeDtypeStruct((B, n_ctx, ATT_W), BF16),
        compiler_params=_params(3),
        name="diff_attention_ctx",
    )(qp, qr, k, v, g_scaled, lam_arr)
    return out_lat, out_ctx


def _moe_kernel(be_ref, nb_ref, x_ref, gate_ref, w1_ref, w3_ref, w2_ref, o_ref, w1_scr, w3_scr, w2_scr):
    i = pl.program_id(0)
    new_expert = jnp.logical_or(i == 0, be_ref[i] != be_ref[jnp.maximum(i - 1, 0)])

    @pl.when(jnp.logical_and(i < nb_ref[0], new_expert))
    def _():
        w1_scr[...] = w1_ref[0, 0].astype(BF16)
        w3_scr[...] = w3_ref[0, 0].astype(BF16)
        w2_scr[...] = w2_ref[0, 0].astype(BF16)

    @pl.when(i < nb_ref[0])
    def _():
        x = x_ref[...]
        a = jnp.dot(x, w1_scr[...], preferred_element_type=F32)
        b = jnp.dot(x, w3_scr[...], preferred_element_type=F32)
        h = (a * jax.nn.sigmoid(a)) * b
        y = jnp.dot(h.astype(BF16), w2_scr[...], preferred_element_type=F32)
        rows = lax.broadcasted_iota(jnp.int32, (MOE_BLOCK, 1), 0)
        lane = lax.broadcasted_iota(jnp.int32, (1, LANES), 1)
        g_rows = gate_ref[0, 0:1, :]
        for r in range(1, MOE_BLOCK // LANES):
            g_rows = jnp.where(rows >= r * LANES, gate_ref[0, r:r + 1, :], g_rows)
        g_col = jnp.sum(jnp.where(lane == jnp.bitwise_and(rows, LANES - 1), g_rows, 0.0), axis=1, keepdims=True)
        o_ref[...] = (y * g_col).astype(o_ref.dtype)

    @pl.when(i >= nb_ref[0])
    def _():
        o_ref[...] = jnp.zeros_like(o_ref)


def pl_moe_ffn(xb, slot_gate, block_e, n_used, w1, w3, w2, layer):
    n_pad, D = xb.shape
    n_blocks = n_pad // MOE_BLOCK
    F = w1.shape[-1]
    grid_spec = pltpu.PrefetchScalarGridSpec(
        num_scalar_prefetch=2,
        grid=(n_blocks,),
        in_specs=[pl.BlockSpec((MOE_BLOCK, D), lambda i, be, nb: (i, 0)),
                  pl.BlockSpec((1, MOE_BLOCK // LANES, LANES), lambda i, be, nb: (i, 0, 0)),
                  pl.BlockSpec((1, 1, D, F), lambda i, be, nb: (layer, be[i], 0, 0)),
                  pl.BlockSpec((1, 1, D, F), lambda i, be, nb: (layer, be[i], 0, 0)),
                  pl.BlockSpec((1, 1, F, D), lambda i, be, nb: (layer, be[i], 0, 0))],
        out_specs=pl.BlockSpec((MOE_BLOCK, D), lambda i, be, nb: (i, 0)),
        scratch_shapes=[pltpu.VMEM((D, F), BF16), pltpu.VMEM((D, F), BF16), pltpu.VMEM((F, D), BF16)],
    )
    return pl.pallas_call(
        _moe_kernel,
        grid_spec=grid_spec,
        out_shape=jax.ShapeDtypeStruct((n_pad, D), BF16),
        compiler_params=_params(1),
        name="moe_ffn",
    )(block_e, n_used, xb, slot_gate.reshape(n_blocks, MOE_BLOCK // LANES, LANES), w1, w3, w2)


def _combine_kernel(x_ref, y0_ref, y1_ref, mod_ref, g_ref, o_ref, *, final):
    d = x_ref.shape[2]
    xn = x_ref[0] + mod_ref[0, :, 5 * d:6 * d] * (y0_ref[0].astype(F32) + y1_ref[0].astype(F32))
    o_ref[0] = _rms(xn) * g_ref[...] if final else xn


def pl_moe_combine(xs, y0, y1, modv, final_g, n_lat_tiles, final):
    B, Lt, D = xs.shape
    tm = TOKEN_TILE
    n_tiles = n_lat_tiles if final else Lt // tm
    tok = pl.BlockSpec((1, tm, D), lambda b, i: (b, i, 0))
    return pl.pallas_call(
        functools.partial(_combine_kernel, final=final),
        grid=(B, n_tiles),
        in_specs=[tok, tok, tok,
                  pl.BlockSpec((1, 1, N_MOD * D), lambda b, i: (2 * b + jnp.where(i >= n_lat_tiles, 1, 0), 0, 0)),
                  pl.BlockSpec((1, D), lambda b, i: (0, 0))],
        out_specs=tok,
        out_shape=jax.ShapeDtypeStruct((B, n_tiles * tm, D), F32),
        compiler_params=_params(2),
        name="moe_combine",
    )(xs, y0.reshape(B, Lt, D), y1.reshape(B, Lt, D), modv, final_g.reshape(1, D))


S5_STATES = S5_GROUPS * S5_STATE
S5_CHUNK = 64


def _s5_kernel(u_ref, wd_ref, wr_ref, ar_ref, ai_ref, d_ref, y_ref, x_scr, h_scr, hr_scr, hi_scr, *, reverse):
    ns = S5_STATES

    @pl.when(pl.program_id(0) == 0)
    def _():
        hr_scr[...] = jnp.zeros_like(hr_scr)
        hi_scr[...] = jnp.zeros_like(hi_scr)

    u = u_ref[...]
    x_scr[...] = jnp.dot(u.astype(BF16), wd_ref[...], preferred_element_type=F32)
    ar = ar_ref[...]
    ai = ai_ref[...]

    def step(hr, hi, t):
        r = pl.multiple_of(t * SUBLANES, SUBLANES)
        xr = x_scr[pl.ds(r, SUBLANES), :ns]
        xi = x_scr[pl.ds(r, SUBLANES), ns:]
        return ar * hr - ai * hi + xr, ar * hi + ai * hr + xi

    def body(j, carry):
        hr, hi = carry
        t0 = (S5_CHUNK - 1 - 2 * j) if reverse else 2 * j
        t1 = t0 - 1 if reverse else t0 + 1
        hr0, hi0 = step(hr, hi, t0)
        hr1, hi1 = step(hr0, hi0, t1)
        lo = t1 if reverse else t0
        first_r, second_r = (hr1, hr0) if reverse else (hr0, hr1)
        first_i, second_i = (hi1, hi0) if reverse else (hi0, hi1)
        r = pl.multiple_of(lo * SUBLANES, 2 * SUBLANES)
        h_scr[pl.ds(r, 2 * SUBLANES), :ns] = jnp.concatenate([first_r, second_r], axis=0).astype(BF16)
        h_scr[pl.ds(r, 2 * SUBLANES), ns:] = jnp.concatenate([first_i, second_i], axis=0).astype(BF16)
        return hr1, hi1

    hr, hi = lax.fori_loop(0, S5_CHUNK // 2, body, (hr_scr[...], hi_scr[...]))
    hr_scr[...] = hr
    hi_scr[...] = hi
    y = jnp.dot(h_scr[...], wr_ref[...], preferred_element_type=F32)
    y = y + (d_ref[...].astype(F32) if reverse else u * d_ref[...])
    y_ref[...] = y.astype(y_ref.dtype)


def pl_s5_scan(u_tb, w_drive, w_read, a_re, a_im, addend, *, n_lat_steps, reverse):
    rows, ch = u_tb.shape
    rc = S5_CHUNK * SUBLANES
    n_chunks = rows // rc
    n_lat = n_lat_steps // S5_CHUNK
    n_ctx = n_chunks - n_lat
    assert rows % rc == 0 and n_lat_steps % S5_CHUNK == 0
    if reverse:
        def idx(i):
            return (n_chunks - 1 - i, 0)
    else:
        def idx(i):
            return (jnp.where(i < n_ctx, n_lat + i, i - n_ctx), 0)
    const = lambda i: (0, 0)
    ns2 = 2 * S5_STATES
    return pl.pallas_call(
        functools.partial(_s5_kernel, reverse=reverse),
        grid=(n_chunks,),
        in_specs=[pl.BlockSpec((rc, ch), idx),
                  pl.BlockSpec((ch, ns2), const),
                  pl.BlockSpec((ns2, ch), const),
                  pl.BlockSpec((SUBLANES, S5_STATES), const),
                  pl.BlockSpec((SUBLANES, S5_STATES), const),
                  pl.BlockSpec((rc, ch), idx) if reverse else pl.BlockSpec((1, ch), const)],
        out_specs=pl.BlockSpec((rc, ch), idx),
        out_shape=jax.ShapeDtypeStruct((rows, ch), BF16),
        scratch_shapes=[pltpu.VMEM((rc, ns2), F32), pltpu.VMEM((rc, ns2), BF16),
                        pltpu.VMEM((SUBLANES, S5_STATES), F32), pltpu.VMEM((SUBLANES, S5_STATES), F32)],
        compiler_params=_params(1),
        name="s5_scan_rev" if reverse else "s5_scan_fwd",
    )(u_tb, w_drive, w_read,
      jnp.broadcast_to(a_re[None, :], (SUBLANES, S5_STATES)),
      jnp.broadcast_to(a_im[None, :], (SUBLANES, S5_STATES)),
      addend if reverse else addend.reshape(1, ch))


FFT_N2 = 128


def _fft_tables(L):
    N = 2 * L
    N1 = N // FFT_N2
    k1 = np.arange(N1)[:, None]
    n1 = np.arange(N1 // 2)[None, :]
    n2 = np.arange(FFT_N2)[:, None, None]
    ang = -2.0 * np.pi * (k1[None] * (n2 + FFT_N2 * n1[None])) / N
    mr, mi = np.cos(ang), np.sin(ang)
    ma = np.concatenate([np.concatenate([mr, -mi], axis=2), np.concatenate([mi, mr], axis=2)], axis=1)
    gr, gi = np.transpose(mr, (0, 2, 1)), -np.transpose(mi, (0, 2, 1))
    mainv = np.concatenate([np.concatenate([gr, -gi], axis=2), np.concatenate([gi, gr], axis=2)], axis=1)
    kk = np.arange(FFT_N2)
    a2 = -2.0 * np.pi * np.outer(kk, kk) / FFT_N2
    fr, fi = np.cos(a2), np.sin(a2)
    f_fwd = np.block([[fr, -fi], [fi, fr]])
    f_inv = np.block([[fr, fi], [-fi, fr]])
    return (jnp.asarray(ma, BF16), jnp.asarray(mainv, BF16), jnp.asarray(f_fwd, BF16), jnp.asarray(f_inv, BF16))


HYENA_VMEM_LIMIT = 56 * 1024 * 1024
FFT_UNROLL = 8


def _hyena_fft_kernel(z_ref, h_ref, ma_ref, mainv_ref, ff_ref, fi_ref, o_ref, a_scr, b_scr, *, n1_count):
    half = n1_count // 2
    n2c = FFT_N2

    def stage_a(n2, c):
        xr = z_ref[0, pl.ds(n2, half, stride=n2c), :]
        xi = z_ref[1, pl.ds(n2, half, stride=n2c), :]
        x = jnp.concatenate([xr, xi], axis=0).astype(BF16)
        r = jnp.dot(ma_ref[n2], x, preferred_element_type=F32)
        a_scr[pl.ds(pl.multiple_of(n2 * 2 * n1_count, 2 * n1_count), 2 * n1_count), :] = r
        return c

    lax.fori_loop(0, n2c, stage_a, 0, unroll=FFT_UNROLL)

    def stage_c(k1, c):
        ar = a_scr[pl.ds(k1, n2c, stride=2 * n1_count), :]
        ai = a_scr[pl.ds(n1_count + k1, n2c, stride=2 * n1_count), :]
        x = jnp.concatenate([ar, ai], axis=0).astype(BF16)
        y = jnp.dot(ff_ref[...], x, preferred_element_type=F32)
        yr, yi = y[:n2c], y[n2c:]
        hr = h_ref[0, k1].astype(F32)
        hi = h_ref[1, k1].astype(F32)
        x2 = jnp.concatenate([yr * hr - yi * hi, yr * hi + yi * hr], axis=0).astype(BF16)
        b = jnp.dot(fi_ref[...], x2, preferred_element_type=F32)
        b_scr[pl.ds(pl.multiple_of(k1 * 2 * n2c, 2 * n2c), 2 * n2c), :] = b
        return c

    lax.fori_loop(0, n1_count, stage_c, 0, unroll=FFT_UNROLL)

    def stage_a_inv(n2, c):
        br = b_scr[pl.ds(n2, n1_count, stride=2 * n2c), :]
        bi = b_scr[pl.ds(n2c + n2, n1_count, stride=2 * n2c), :]
        x = jnp.concatenate([br, bi], axis=0).astype(BF16)
        r = jnp.dot(mainv_ref[n2], x, preferred_element_type=F32)
        o_ref[0, pl.ds(n2, half, stride=n2c), :] = r[:half]
        o_ref[1, pl.ds(n2, half, stride=n2c), :] = r[half:]
        return c

    lax.fori_loop(0, n2c, stage_a_inv, 0, unroll=FFT_UNROLL)


def hyena_spectrum(filt):
    N, C = filt.shape
    N1 = N // FFT_N2
    h2 = (jnp.fft.fft(filt, axis=0) / N).reshape(FFT_N2, N1, C).transpose(1, 0, 2)
    return jnp.stack([h2.real, h2.imag]).astype(BF16)


def pl_hyena_conv(z, h, dft_tables, L):
    B, _, C = z.shape
    N1 = 2 * L // FFT_N2
    ma, mainv, f_fwd, f_inv = dft_tables
    cw = LANES
    full = lambda arr: pl.BlockSpec(arr.shape, lambda ct, bp: (0,) * arr.ndim)
    return pl.pallas_call(
        functools.partial(_hyena_fft_kernel, n1_count=N1),
        grid=(C // cw, B // 2),
        in_specs=[pl.BlockSpec((2, L, cw), lambda ct, bp: (bp, 0, ct)),
                  pl.BlockSpec((2, N1, FFT_N2, cw), lambda ct, bp: (0, 0, 0, ct)),
                  full(ma), full(mainv), full(f_fwd), full(f_inv)],
        out_specs=pl.BlockSpec((2, L, cw), lambda ct, bp: (bp, 0, ct)),
        out_shape=jax.ShapeDtypeStruct((B, L, C), F32),
        scratch_shapes=[pltpu.VMEM((FFT_N2 * 2 * N1, cw), F32), pltpu.VMEM((N1 * 2 * FFT_N2, cw), F32)],
        compiler_params=_params(2, HYENA_VMEM_LIMIT),
        name="hyena_fft_conv",
    )(z, h, ma, mainv, f_fwd, f_inv)


def hyena_filter(L, w1, b1, w2, b2, w3, freq):
    t = jnp.arange(L, dtype=F32) / L
    ang = (2.0 * math.pi) * t[:, None] * jnp.arange(1, HY_BANDS + 1, dtype=F32)
    feat = jnp.concatenate([t[:, None], jnp.cos(ang), jnp.sin(ang)], axis=-1)
    hp = lax.Precision.HIGHEST
    h = jnp.sin(freq * (jnp.dot(feat, w1, precision=hp) + b1))
    h = jnp.sin(freq * (jnp.dot(h, w2, precision=hp) + b2))
    h = jnp.dot(h, w3, precision=hp).reshape(L, 2, HY_CH)
    window = jnp.exp(-t[:, None] * jnp.linspace(HY_DECAY_MIN, HY_DECAY_MAX, HY_CH, dtype=F32))
    h = h * window[:, None, :]
    filt = jnp.concatenate([h[:, 0], jnp.zeros((1, HY_CH), F32), h[:0:-1, 1]], axis=0)
    return filt / (jnp.sum(jnp.abs(filt), axis=0, keepdims=True) + EPS)


def hyena_conv(z, h_lat, ff_ctx, dft_tables, L):
    Lc = z.shape[1] - L
    y_lat = pl_hyena_conv(z, h_lat, dft_tables, L)
    zf = jnp.fft.rfft(z[:, L:], n=2 * Lc, axis=1)
    y_ctx = jnp.fft.irfft(zf * ff_ctx[None], n=2 * Lc, axis=1)[:, :Lc]
    return y_lat, y_ctx


def _block_diag(blocks):
    G, r, c = blocks.shape
    eye = jnp.eye(G, dtype=blocks.dtype)
    return (eye[:, None, :, None] * blocks[:, :, None, :]).reshape(G * r, G * c)


def s5_tables(a_re, a_im, log_dt, b_re, b_im, c_re, c_im):
    A = lax.complex(a_re, a_im)
    dtA = jnp.exp(log_dt)[:, None] * A
    a_bar = jnp.exp(dtA)
    b_bar = ((a_bar - 1.0) / A)[:, :, None] * lax.complex(b_re, b_im)
    bt_re = jnp.transpose(b_bar.real, (0, 2, 1))
    bt_im = jnp.transpose(b_bar.imag, (0, 2, 1))
    w_drive = jnp.concatenate([_block_diag(bt_re), _block_diag(bt_im)], axis=1)
    ct_re = jnp.transpose(c_re, (0, 2, 1))
    ct_im = jnp.transpose(c_im, (0, 2, 1))
    w_read = jnp.concatenate([_block_diag(ct_re), -_block_diag(ct_im)], axis=0)
    return w_drive.astype(BF16), w_read.astype(BF16), a_bar.real.reshape(-1), a_bar.imag.reshape(-1)


def s5_scan(u_tb, tables, d_skip, n_lat_steps):
    y = d_skip
    for direction in range(2):
        w_drive, w_read, a_re, a_im = (t[direction] for t in tables)
        y = pl_s5_scan(u_tb, w_drive, w_read, a_re, a_im, y, n_lat_steps=n_lat_steps, reverse=direction == 1)
    return y


def rope_tables(L, Lc):
    rows = L // GRID_W
    row = jnp.repeat(jnp.arange(rows, dtype=F32), GRID_W)
    col = jnp.tile(jnp.arange(GRID_W, dtype=F32), rows)
    inv = ROPE_BASE ** (-jnp.arange(ROPE_PAIRS_AXIS, dtype=F32) / ROPE_PAIRS_AXIS)
    ang = jnp.concatenate([row[:, None] * inv, col[:, None] * inv], axis=-1)
    cos, sin = jnp.cos(ang), jnp.sin(ang)
    n_maps = ATT_W // ATT_HEAD_DIM
    cosf = jnp.tile(jnp.concatenate([cos, cos], axis=-1), (1, n_maps))
    sinf = jnp.tile(jnp.concatenate([-sin, sin], axis=-1), (1, n_maps))
    return (jnp.concatenate([cosf, jnp.ones((Lc, ATT_W), F32)], axis=0),
            jnp.concatenate([sinf, jnp.zeros((Lc, ATT_W), F32)], axis=0))


def moe_dispatch(route, counts_row):
    T = route.shape[1]
    experts = [route[j].astype(jnp.int32) for j in range(MOE_TOP_K)]
    ranks = [route[2 * MOE_TOP_K + j].astype(jnp.int32) for j in range(MOE_TOP_K)]
    gate_cat = route[MOE_TOP_K:2 * MOE_TOP_K].reshape(-1)
    counts = counts_row[MOE_GROUPS:MOE_GROUPS + N_EXPERTS].astype(jnp.int32)
    n_assign = T * MOE_TOP_K
    n_blocks = -(-n_assign // MOE_BLOCK) + N_EXPERTS
    n_pad = n_blocks * MOE_BLOCK
    a_bits = (n_assign - 1).bit_length()
    t_ids = jnp.arange(T, dtype=jnp.int32)
    keys = jnp.concatenate([(experts[j] << a_bits) + (MOE_TOP_K * t_ids + j) for j in range(MOE_TOP_K)])
    order = jnp.sort(keys) & ((1 << a_bits) - 1)
    start = jnp.cumsum(counts) - counts
    padded = (counts + MOE_BLOCK - 1) // MOE_BLOCK * MOE_BLOCK
    pad_end = jnp.cumsum(padded)
    pad_start = pad_end - padded
    slots_of_tok = [(pad_start[experts[j]] + ranks[j]).astype(jnp.int32) for j in range(MOE_TOP_K)]
    block_first = jnp.arange(n_blocks, dtype=jnp.int32) * MOE_BLOCK
    block_e = jnp.minimum(jnp.sum((pad_end[None, :] <= block_first[:, None]).astype(jnp.int32), axis=1),
                          N_EXPERTS - 1).astype(jnp.int32)
    slot_e = jnp.repeat(block_e, MOE_BLOCK)
    slot_r = jnp.arange(n_pad, dtype=jnp.int32) - pad_start[slot_e]
    slot_valid = (slot_r < counts[slot_e]) & (jnp.arange(n_pad) < pad_end[-1])
    slot_assign = order[jnp.clip(start[slot_e] + slot_r, 0, n_assign - 1)]
    slot_t = slot_assign // MOE_TOP_K
    slot_j = slot_assign % MOE_TOP_K
    slot_tok = jnp.where(slot_valid, slot_t, jnp.arange(n_pad, dtype=jnp.int32) % T)
    slot_gate = jnp.where(slot_valid, gate_cat[slot_j * T + slot_t], 0.0)
    n_used = (pad_end[-1:] // MOE_BLOCK).astype(jnp.int32)
    return slot_tok, slot_gate, slots_of_tok, block_e, n_used


def hier_moe(h2, route, counts_row, w1, w3, w2, layer):
    slot_tok, slot_gate, slots_of_tok, block_e, n_used = moe_dispatch(route, counts_row)
    yb = pl_moe_ffn(h2[slot_tok], slot_gate, block_e, n_used, w1, w3, w2, layer)
    return tuple(yb[s] for s in slots_of_tok)


def kernel(x, c, ctx, c_ctx, w_mod, b_mod, norm1_g, norm2_g, final_g, w_in, w_out, hy_conv_w, hy_conv_b, hy_ffn_w1, hy_ffn_b1, hy_ffn_w2, hy_ffn_b2, hy_ffn_w3, hy_freq, hy_skip, hy_norm_g, s5_a_re, s5_a_im, s5_log_dt, s5_b_re, s5_b_im, s5_c_re, s5_c_im, s5_d, s5_glu_w, s5_norm_g, att_lq1, att_lk1, att_lq2, att_lk2, att_subln_g, moe_wg, moe_bg, moe_we, moe_be, moe_w1, moe_w3, moe_w2):
    B, L, D = x.shape
    Lc = ctx.shape[1]
    Lt = L + Lc
    assert B == SUBLANES and Lc == TOKEN_TILE and L % ATT_TQ == 0
    n_lat_tiles = L // TOKEN_TILE
    cosf, sinf = rope_tables(L, Lc)
    hp = lax.Precision.HIGHEST
    q_scale = ATT_HEAD_DIM ** -0.5 * math.log2(math.e)

    mods = jnp.einsum('bd,ldk->lbk', jax.nn.silu(c), w_mod, precision=hp) + b_mod[:, None, :]
    cmods = jnp.einsum('d,ldk->lk', jax.nn.silu(c_ctx), w_mod, precision=hp) + b_mod
    modvs = jnp.stack([mods, jnp.broadcast_to(cmods[:, None, :], mods.shape)], axis=2)
    modvs = modvs.reshape(DEPTH, 2 * B, 1, N_MOD * D)
    filt_args = (hy_ffn_w1, hy_ffn_b1, hy_ffn_w2, hy_ffn_b2, hy_ffn_w3, hy_freq)
    h_lat = jax.vmap(lambda *p: hyena_spectrum(hyena_filter(L, *p)))(*filt_args)
    ff_ctx = jax.vmap(lambda *p: jnp.fft.rfft(hyena_filter(Lc, *p), n=2 * Lc, axis=0))(*filt_args)
    dft_tables = _fft_tables(L)
    s5_tabs = jax.vmap(jax.vmap(s5_tables))(s5_a_re, s5_a_im, s5_log_dt, s5_b_re, s5_b_im, s5_c_re, s5_c_im)
    lam_inits = [0.8 - 0.6 * math.exp(-0.3 * l) for l in range(DEPTH)]
    lams = (jnp.exp(jnp.sum(att_lq1 * att_lk1, axis=-1)) - jnp.exp(jnp.sum(att_lq2 * att_lk2, axis=-1))
            + jnp.asarray(lam_inits, F32))
    w_router = jnp.zeros((DEPTH, D, LANES), F32).at[:, :, :MOE_GROUPS].set(moe_wg)
    w_router = w_router.at[:, :, MOE_GROUPS:MOE_GROUPS + N_EXPERTS].set(moe_we)
    rwh = w_router.astype(BF16)
    rwl = (w_router - rwh.astype(F32)).astype(BF16)
    router_bias = jnp.zeros((DEPTH, 1, LANES), F32).at[:, 0, :MOE_GROUPS].set(moe_bg)
    router_bias = router_bias.at[:, 0, MOE_GROUPS:MOE_GROUPS + N_EXPERTS].set(moe_be)
    w_in_b, w_out_b, glu_w_b = w_in.astype(BF16), w_out.astype(BF16), s5_glu_w.astype(BF16)

    xs = jnp.concatenate([x, ctx], axis=1)
    for l in range(DEPTH):
        modv = modvs[l]
        u0, z, s5_u, q_p, q_r, k_r, v = pl_in_proj(xs, modv, norm1_g[l], w_in_b[l], hy_conv_w[l], hy_conv_b[l],
                                                   cosf, sinf, n_lat_tiles, q_scale)

        y_lat, y_ctx = hyena_conv(z, h_lat[l], ff_ctx[l], dft_tables, L)

        s5_y = s5_scan(s5_u.reshape(Lt * B, S5_CH), tuple(t[l] for t in s5_tabs), s5_d[l], L)

        g_scaled = (att_subln_g[l] * (1.0 - lam_inits[l])).reshape(1, ATT_V_DIM)
        att_lat, att_ctx = pl_diff_attention(q_p, q_r, k_r, v, g_scaled, lams[l], L)

        xs, h2, route, counts = pl_out_proj(xs, u0, z, y_lat, y_ctx, s5_y.reshape(Lt, B * S5_CH), att_lat, att_ctx,
                                            modv, hy_skip[l], hy_norm_g[l], glu_w_b[l], s5_norm_g[l], w_out_b[l],
                                            norm2_g[l], rwh[l], rwl[l], router_bias[l], n_lat_tiles)

        y0, y1 = hier_moe(h2.reshape(B * Lt, D), route, counts[:, 0], moe_w1, moe_w3, moe_w2, l)
        xs = pl_moe_combine(xs, y0, y1, modv, final_g, n_lat_tiles, final=l == DEPTH - 1)
    return xs
```

```python
import functools
import math

import jax
import jax.numpy as jnp
import numpy as np
from jax import lax
from jax.experimental import pallas as pl
from jax.experimental.pallas import tpu as pltpu

D_MODEL = 1024
DEPTH = 4
GRID_W = 64
N_MOD = 6
EPS = 1e-6
HY_CH = D_MODEL // 4
S5_CH = D_MODEL // 4
ATT_W = D_MODEL // 2
HY_BANDS = 16
HY_DECAY_MIN = -math.log(1e-2) / 1.5
HY_DECAY_MAX = -math.log(1e-2) / 0.3
S5_GROUP = 16
S5_GROUPS = S5_CH // S5_GROUP
S5_STATE = 64
ATT_HEAD_DIM = 64
ATT_HEADS = ATT_W // (2 * ATT_HEAD_DIM)
ATT_V_DIM = 2 * ATT_HEAD_DIM
ROPE_HALF = ATT_HEAD_DIM // 2
ROPE_PAIRS_AXIS = ROPE_HALF // 2
ROPE_BASE = 10000.0
MOE_GROUPS = 4
MOE_EPG = 8
N_EXPERTS = MOE_GROUPS * MOE_EPG
MOE_TOP_K = 2
MOE_BLOCK = 512
IN_COLS = 3 * HY_CH + S5_CH + 3 * ATT_W
COL_S5 = 3 * HY_CH
COL_Q = COL_S5 + S5_CH
COL_K = COL_Q + ATT_W
COL_V = COL_K + ATT_W

LANES = 128
SUBLANES = 8
VMEM_LIMIT = 48 * 1024 * 1024
TOKEN_TILE = 256

F32 = jnp.float32
BF16 = jnp.bfloat16


def _params(n_axes, vmem=VMEM_LIMIT):
    return pltpu.CompilerParams(dimension_semantics=("arbitrary",) * n_axes, vmem_limit_bytes=vmem)


def _rms(x):
    return x * lax.rsqrt(jnp.mean(x * x, axis=-1, keepdims=True) + EPS)


def _in_kernel(x_ref, xp_ref, xn_ref, mod_ref, g_ref, w_ref, cw_ref, cb_ref, cos_ref, sin_ref,
               u0_ref, z_ref, s5_ref, qp_ref, qr_ref, k_ref, v_ref, *, n_lat_tiles, q_scale):
    i = pl.program_id(1)
    tm = x_ref.shape[1]
    d = x_ref.shape[2]
    g = g_ref[...]
    shift = mod_ref[0, :, 0:d]
    scale = mod_ref[0, :, d:2 * d]

    def norm_mod(xt):
        return (_rms(xt) * g) * (1.0 + scale) + shift

    h = jnp.concatenate([norm_mod(x_ref[0]), norm_mod(xp_ref[0]), norm_mod(xn_ref[0])], axis=0).astype(BF16)
    p = jnp.dot(h, w_ref[...], preferred_element_type=F32)

    hy = p[:tm, :COL_S5]
    is_ctx = i == n_lat_tiles
    has_prev = jnp.logical_and(i != 0, jnp.logical_not(is_ctx))
    has_next = jnp.logical_and(i != n_lat_tiles - 1, jnp.logical_not(is_ctx))
    prev_row = jnp.where(has_prev, p[tm + SUBLANES - 1:tm + SUBLANES, :COL_S5], 0.0)
    next_row = jnp.where(has_next, p[tm + SUBLANES:tm + SUBLANES + 1, :COL_S5], 0.0)
    rows = lax.broadcasted_iota(jnp.int32, (tm, 1), 0)
    up = jnp.where(rows == 0, prev_row, pltpu.roll(hy, 1, axis=0))
    dn = jnp.where(rows == tm - 1, next_row, pltpu.roll(hy, tm - 1, axis=0))
    u = up * cw_ref[0:1, :] + hy * cw_ref[1:2, :] + dn * cw_ref[2:3, :] + cb_ref[...]
    u0_ref[0] = u[:, :HY_CH].astype(u0_ref.dtype)
    z_ref[0] = u[:, HY_CH:2 * HY_CH] * u[:, 2 * HY_CH:]

    s5_ref[...] = p[:tm, COL_S5:COL_Q]

    lane = lax.broadcasted_iota(jnp.int32, (1, ATT_W), 1)
    first_half = jnp.bitwise_and(lane, ATT_HEAD_DIM - 1) < ROPE_HALF
    cos = cos_ref[...]
    sin = sin_ref[...]

    def rope(t):
        partner = jnp.where(first_half, pltpu.roll(t, ATT_W - ROPE_HALF, axis=1), pltpu.roll(t, ROPE_HALF, axis=1))
        return t * cos + partner * sin

    q = p[:tm, COL_Q:COL_K] * q_scale
    qp_ref[0] = q.astype(BF16)
    qr_ref[0] = rope(q).astype(BF16)
    k_ref[0] = rope(p[:tm, COL_K:COL_V]).astype(BF16)
    v_ref[0] = p[:tm, COL_V:].astype(BF16)


def pl_in_proj(xs, modv, norm_g, w_in, conv_w, conv_b, cosf, sinf, n_lat_tiles, q_scale):
    B, Lt, D = xs.shape
    tm = TOKEN_TILE
    n_tiles = Lt // tm
    halo_per_tile = tm // SUBLANES
    n_halo_blocks = Lt // SUBLANES
    tok = lambda w: pl.BlockSpec((1, tm, w), lambda b, i: (b, i, 0))
    const = lambda shape: pl.BlockSpec(shape, lambda b, i: (0,) * len(shape))
    out_shapes = (jax.ShapeDtypeStruct((B, Lt, HY_CH), BF16), jax.ShapeDtypeStruct((B, Lt, HY_CH), F32),
                  jax.ShapeDtypeStruct((Lt, B * S5_CH), F32),
                  jax.ShapeDtypeStruct((B, Lt, ATT_W), BF16), jax.ShapeDtypeStruct((B, Lt, ATT_W), BF16),
                  jax.ShapeDtypeStruct((B, Lt, ATT_W), BF16), jax.ShapeDtypeStruct((B, Lt, ATT_W), BF16))
    return pl.pallas_call(
        functools.partial(_in_kernel, n_lat_tiles=n_lat_tiles, q_scale=q_scale),
        grid=(B, n_tiles),
        in_specs=[tok(D),
                  pl.BlockSpec((1, SUBLANES, D), lambda b, i: (b, jnp.maximum(i * halo_per_tile - 1, 0), 0)),
                  pl.BlockSpec((1, SUBLANES, D),
                               lambda b, i: (b, jnp.minimum((i + 1) * halo_per_tile, n_halo_blocks - 1), 0)),
                  pl.BlockSpec((1, 1, N_MOD * D), lambda b, i: (2 * b + jnp.where(i >= n_lat_tiles, 1, 0), 0, 0)),
                  const((1, D)), const((D, IN_COLS)), const((3, COL_S5)), const((1, COL_S5)),
                  pl.BlockSpec((tm, ATT_W), lambda b, i: (i, 0)), pl.BlockSpec((tm, ATT_W), lambda b, i: (i, 0))],
        out_specs=(tok(HY_CH), tok(HY_CH), pl.BlockSpec((tm, S5_CH), lambda b, i: (i, b)),
                   tok(ATT_W), tok(ATT_W), tok(ATT_W), tok(ATT_W)),
        out_shape=out_shapes,
        compiler_params=_params(2),
        name="in_proj",
    )(xs, xs, xs, modv, norm_g.reshape(1, D), w_in, conv_w, conv_b.reshape(1, COL_S5), cosf, sinf)


def _out_kernel(x_ref, u0_ref, z_ref, yl_ref, yc_ref, s5_ref, al_ref, ac_ref, mod_ref, skip_ref, hg_ref,
                gw_ref, sg_ref, wo_ref, n2g_ref, rwh_ref, rwl_ref, rb_ref, tri_ref, xo_ref, h2_ref, rt_ref, cnt_ref,
                cnt_scr, *, n_lat_tiles):
    d = x_ref.shape[2]
    is_ctx = pl.program_id(1) == n_lat_tiles
    y = jnp.where(is_ctx, yc_ref[0], yl_ref[0])
    att = jnp.where(is_ctx, ac_ref[0], al_ref[0])
    hy = _rms(u0_ref[0].astype(F32) * (y + skip_ref[...] * z_ref[0])) * hg_ref[...]
    gl = jax.nn.gelu(s5_ref[...].astype(F32))
    gate = jax.nn.sigmoid(jnp.dot(gl.astype(BF16), gw_ref[...], preferred_element_type=F32))
    s5 = _rms(gl * gate) * sg_ref[...]
    mix = jnp.concatenate([hy.astype(BF16), s5.astype(BF16), att], axis=1)
    proj = jnp.dot(mix, wo_ref[...], preferred_element_type=F32)
    xn = x_ref[0] + mod_ref[0, :, 2 * d:3 * d] * proj
    xo_ref[0] = xn
    h2 = (_rms(xn) * n2g_ref[...]) * (1.0 + mod_ref[0, :, 4 * d:5 * d]) + mod_ref[0, :, 3 * d:4 * d]
    hh = h2.astype(BF16)
    hl = (h2 - hh.astype(F32)).astype(BF16)
    h2_ref[0] = hh
    lg = jnp.dot(hh, rwh_ref[...], preferred_element_type=F32)
    lg += jnp.dot(hl, rwh_ref[...], preferred_element_type=F32)
    lg += jnp.dot(hh, rwl_ref[...], preferred_element_type=F32)
    lg = lg + rb_ref[...]

    n_rows = cnt_scr.shape[0]
    lt = jnp.transpose(lg)[:n_rows, :]
    row = lax.broadcasted_iota(jnp.int32, (n_rows, 1), 0)
    neg_inf = jnp.float32(-jnp.inf)

    def first_max(v):
        m = jnp.max(v, axis=0, keepdims=True)
        return m, jnp.min(jnp.where(v == m, row, n_rows), axis=0, keepdims=True)

    g_logit = jnp.where(row < MOE_GROUPS, lt, neg_inf)
    g_max, g_idx = first_max(g_logit)
    p_group = 1.0 / jnp.sum(jnp.exp(g_logit - g_max), axis=0, keepdims=True)
    e_row = row - MOE_GROUPS
    in_group = jnp.logical_and(jnp.logical_and(e_row >= 0, e_row < N_EXPERTS),
                               jnp.right_shift(e_row, MOE_EPG.bit_length() - 1) == g_idx)
    e_logit = jnp.where(in_group, lt, neg_inf)
    e_exp = jnp.exp(e_logit - jnp.max(e_logit, axis=0, keepdims=True))
    probs = jnp.where(in_group, e_exp / jnp.sum(e_exp, axis=0, keepdims=True), -1.0)
    p1, l1 = first_max(probs)
    p2, l2 = first_max(jnp.where(row == l1, -1.0, probs))
    hit1 = row == l1
    hit2 = row == l2
    onehot = jnp.where(hit1, 1.0, 0.0) + jnp.where(hit2, 1.0, 0.0)

    @pl.when(jnp.logical_and(pl.program_id(0) == 0, pl.program_id(1) == 0))
    def _():
        cnt_scr[...] = jnp.zeros_like(cnt_scr)

    earlier = jnp.dot(onehot.astype(BF16), tri_ref[...], preferred_element_type=F32)
    before = cnt_scr[:, 0:1] + earlier
    r1 = jnp.sum(jnp.where(hit1, before, 0.0), axis=0, keepdims=True)
    r2 = jnp.sum(jnp.where(hit2, before, 0.0), axis=0, keepdims=True)
    cnt_scr[...] = cnt_scr[...] + jnp.sum(onehot, axis=1, keepdims=True)
    cnt_ref[...] = cnt_scr[...]
    fields = [(l1 - MOE_GROUPS).astype(F32), (l2 - MOE_GROUPS).astype(F32), p_group * p1 / (p1 + p2),
              p_group * p2 / (p1 + p2), r1, r2]
    field_row = lax.broadcasted_iota(jnp.int32, (SUBLANES, 1), 0)
    route = jnp.zeros((SUBLANES, lt.shape[1]), F32)
    for j, f in enumerate(fields):
        route = jnp.where(field_row == j, f, route)
    rt_ref[...] = route


def pl_out_proj(xs, u0, z, y_lat, y_ctx, s5_y, att_lat, att_ctx, modv, hy_skip, hy_norm_g, glu_w, s5_norm_g,
                w_out, norm2_g, rwh, rwl, router_bias, n_lat_tiles):
    B, Lt, D = xs.shape
    tm = TOKEN_TILE
    tri = jnp.asarray(np.triu(np.ones((tm, tm), np.float32), 1), BF16)
    n_rows = -(-(MOE_GROUPS + N_EXPERTS) // 16) * 16
    tok = lambda w: pl.BlockSpec((1, tm, w), lambda b, i: (b, i, 0))
    lat_tok = lambda w: pl.BlockSpec((1, tm, w), lambda b, i: (b, jnp.minimum(i, n_lat_tiles - 1), 0))
    ctx_tok = lambda w: pl.BlockSpec((1, tm, w), lambda b, i: (b, 0, 0))
    tb = pl.BlockSpec((tm, S5_CH), lambda b, i: (i, b))
    const = lambda shape: pl.BlockSpec(shape, lambda b, i: (0,) * len(shape))
    return pl.pallas_call(
        functools.partial(_out_kernel, n_lat_tiles=n_lat_tiles),
        grid=(B, Lt // tm),
        in_specs=[tok(D), tok(HY_CH), tok(HY_CH), lat_tok(HY_CH), ctx_tok(HY_CH), tb, lat_tok(ATT_W),
                  ctx_tok(ATT_W),
                  pl.BlockSpec((1, 1, N_MOD * D), lambda b, i: (2 * b + jnp.where(i >= n_lat_tiles, 1, 0), 0, 0)),
                  const((1, HY_CH)), const((1, HY_CH)), const((S5_CH, S5_CH)), const((1, S5_CH)),
                  const((D, D)), const((1, D)), const((D, LANES)), const((D, LANES)), const((1, LANES)),
                  const((tm, tm))],
        out_specs=(tok(D), tok(D), pl.BlockSpec((SUBLANES, tm), lambda b, i: (0, b * (Lt // tm) + i)),
                   const((n_rows, LANES))),
        out_shape=(jax.ShapeDtypeStruct((B, Lt, D), F32), jax.ShapeDtypeStruct((B, Lt, D), BF16),
                   jax.ShapeDtypeStruct((SUBLANES, B * Lt), F32), jax.ShapeDtypeStruct((n_rows, LANES), F32)),
        scratch_shapes=[pltpu.VMEM((n_rows, LANES), F32)],
        compiler_params=_params(2),
        name="out_proj",
    )(xs, u0, z, y_lat, y_ctx, s5_y, att_lat, att_ctx, modv, hy_skip.reshape(1, HY_CH),
      hy_norm_g.reshape(1, HY_CH), glu_w, s5_norm_g.reshape(1, S5_CH), w_out, norm2_g.reshape(1, D), rwh, rwl,
      router_bias, tri)


def _dot_nt(a, b):
    return lax.dot_general(a, b, (((1,), (1,)), ((), ())), preferred_element_type=F32)


ATT_TQ = 512
ATT_SUB = 256


def _attn_kernel(qp_ref, qr_ref, k_ref, v_ref, g_ref, lam_ref, o_ref, *, n_lat):
    tq = qp_ref.shape[1]
    first_map = lax.broadcasted_iota(jnp.int32, (1, LANES), 1) < ATT_HEAD_DIM
    zero = jnp.zeros((), BF16)
    sub = min(ATT_SUB, tq)
    for r0 in range(0, tq, sub):
        qp = qp_ref[0, r0:r0 + sub, :]
        qr = qr_ref[0, r0:r0 + sub, :]
        probs = []
        for m in range(2):
            in_map = first_map if m == 0 else jnp.logical_not(first_map)
            s_c = _dot_nt(jnp.where(in_map, qp, zero), k_ref[0, n_lat:, :])
            mx = jnp.max(s_c, axis=-1, keepdims=True)
            if n_lat:
                s_l = _dot_nt(jnp.where(in_map, qr, zero), k_ref[0, :n_lat, :])
                mx = jnp.maximum(mx, jnp.max(s_l, axis=-1, keepdims=True))
                p_l = jnp.exp2(s_l - mx)
            p_c = jnp.exp2(s_c - mx)
            den = jnp.sum(p_c, axis=-1, keepdims=True)
            if n_lat:
                den = den + jnp.sum(p_l, axis=-1, keepdims=True)
            probs.append((p_c.astype(BF16), p_l.astype(BF16) if n_lat else None, 1.0 / den))
        w0 = probs[0][2].astype(BF16)
        w1 = (lam_ref[0:1, 0:1] * probs[1][2]).astype(BF16)
        a_c = probs[0][0] * w0 - probs[1][0] * w1
        o = jnp.dot(a_c, v_ref[0, n_lat:, :], preferred_element_type=F32)
        if n_lat:
            a_l = probs[0][1] * w0 - probs[1][1] * w1
            o = o + jnp.dot(a_l, v_ref[0, :n_lat, :], preferred_element_type=F32)
        o_ref[0, r0:r0 + sub, :] = (_rms(o) * g_ref[...]).astype(o_ref.dtype)


def pl_diff_attention(qp, qr, k, v, g_scaled, lam, n_lat):
    B, Lt, _ = qp.shape
    n_ctx = Lt - n_lat
    lam_arr = jnp.full((SUBLANES, LANES), lam, F32)
    small = [pl.BlockSpec((1, LANES), lambda b, h, i: (0, 0)), pl.BlockSpec((SUBLANES, LANES), lambda b, h, i: (0, 0))]
    tq = ATT_TQ
    qspec = pl.BlockSpec((1, tq, LANES), lambda b, h, i: (b, i, h))
    kspec = pl.BlockSpec((1, Lt, LANES), lambda b, h, i: (b, 0, h))
    out_lat = pl.pallas_call(
        functools.partial(_attn_kernel, n_lat=n_lat),
        grid=(B, ATT_HEADS, n_lat // tq),
        in_specs=[qspec, qspec, kspec, kspec] + small,
        out_specs=qspec,
        out_shape=jax.ShapeDtypeStruct((B, n_lat, ATT_W), BF16),
        compiler_params=_params(3),
        name="diff_attention",
    )(qp, qr, k, v, g_scaled, lam_arr)
    ctx_blk = n_lat // n_ctx
    cspec = pl.BlockSpec((1, n_ctx, LANES), lambda b, h, i: (b, ctx_blk, h))
    out_ctx = pl.pallas_call(
        functools.partial(_attn_kernel, n_lat=0),
        grid=(B, ATT_HEADS, 1),
        in_specs=[cspec, cspec, cspec, cspec] + small,
        out_specs=pl.BlockSpec((1, n_ctx, LANES), lambda b, h, i: (b, 0, h)),
        out_shape=jax.ShapeDtypeStruct((B, n_ctx, ATT_W), BF16),
        compiler_params=_params(3),
        name="diff_attention_ctx",
    )(qp, qr, k, v, g_scaled, lam_arr)
    return out_lat, out_ctx


def _moe_kernel(be_ref, nb_ref, x_ref, gate_ref, w1_ref, w3_ref, w2_ref, o_ref, w1_scr, w3_scr, w2_scr):
    i = pl.program_id(0)
    new_expert = jnp.logical_or(i == 0, be_ref[i] != be_ref[jnp.maximum(i - 1, 0)])

    @pl.when(jnp.logical_and(i < nb_ref[0], new_expert))
    def _():
        w1_scr[...] = w1_ref[0, 0].astype(BF16)
        w3_scr[...] = w3_ref[0, 0].astype(BF16)
        w2_scr[...] = w2_ref[0, 0].astype(BF16)

    @pl.when(i < nb_ref[0])
    def _():
        x = x_ref[...]
        a = jnp.dot(x, w1_scr[...], preferred_element_type=F32)
        b = jnp.dot(x, w3_scr[...], preferred_element_type=F32)
        h = (a * jax.nn.sigmoid(a)) * b
        y = jnp.dot(h.astype(BF16), w2_scr[...], preferred_element_type=F32)
        rows = lax.broadcasted_iota(jnp.int32, (MOE_BLOCK, 1), 0)
        lane = lax.broadcasted_iota(jnp.int32, (1, LANES), 1)
        g_rows = gate_ref[0, 0:1, :]
        for r in range(1, MOE_BLOCK // LANES):
            g_rows = jnp.where(rows >= r * LANES, gate_ref[0, r:r + 1, :], g_rows)
        g_col = jnp.sum(jnp.where(lane == jnp.bitwise_and(rows, LANES - 1), g_rows, 0.0), axis=1, keepdims=True)
        o_ref[...] = (y * g_col).astype(o_ref.dtype)

    @pl.when(i >= nb_ref[0])
    def _():
        o_ref[...] = jnp.zeros_like(o_ref)


def pl_moe_ffn(xb, slot_gate, block_e, n_used, w1, w3, w2, layer):
    n_pad, D = xb.shape
    n_blocks = n_pad // MOE_BLOCK
    F = w1.shape[-1]
    grid_spec = pltpu.PrefetchScalarGridSpec(
        num_scalar_prefetch=2,
        grid=(n_blocks,),
        in_specs=[pl.BlockSpec((MOE_BLOCK, D), lambda i, be, nb: (i, 0)),
                  pl.BlockSpec((1, MOE_BLOCK // LANES, LANES), lambda i, be, nb: (i, 0, 0)),
                  pl.BlockSpec((1, 1, D, F), lambda i, be, nb: (layer, be[i], 0, 0)),
                  pl.BlockSpec((1, 1, D, F), lambda i, be, nb: (layer, be[i], 0, 0)),
                  pl.BlockSpec((1, 1, F, D), lambda i, be, nb: (layer, be[i], 0, 0))],
        out_specs=pl.BlockSpec((MOE_BLOCK, D), lambda i, be, nb: (i, 0)),
        scratch_shapes=[pltpu.VMEM((D, F), BF16), pltpu.VMEM((D, F), BF16), pltpu.VMEM((F, D), BF16)],
    )
    return pl.pallas_call(
        _moe_kernel,
        grid_spec=grid_spec,
        out_shape=jax.ShapeDtypeStruct((n_pad, D), BF16),
        compiler_params=_params(1),
        name="moe_ffn",
    )(block_e, n_used, xb, slot_gate.reshape(n_blocks, MOE_BLOCK // LANES, LANES), w1, w3, w2)


def _combine_kernel(x_ref, y0_ref, y1_ref, mod_ref, g_ref, o_ref, *, final):
    d = x_ref.shape[2]
    xn = x_ref[0] + mod_ref[0, :, 5 * d:6 * d] * (y0_ref[0].astype(F32) + y1_ref[0].astype(F32))
    o_ref[0] = _rms(xn) * g_ref[...] if final else xn


def pl_moe_combine(xs, y0, y1, modv, final_g, n_lat_tiles, final):
    B, Lt, D = xs.shape
    tm = TOKEN_TILE
    n_tiles = n_lat_tiles if final else Lt // tm
    tok = pl.BlockSpec((1, tm, D), lambda b, i: (b, i, 0))
    return pl.pallas_call(
        functools.partial(_combine_kernel, final=final),
        grid=(B, n_tiles),
        in_specs=[tok, tok, tok,
                  pl.BlockSpec((1, 1, N_MOD * D), lambda b, i: (2 * b + jnp.where(i >= n_lat_tiles, 1, 0), 0, 0)),
                  pl.BlockSpec((1, D), lambda b, i: (0, 0))],
        out_specs=tok,
        out_shape=jax.ShapeDtypeStruct((B, n_tiles * tm, D), F32),
        compiler_params=_params(2),
        name="moe_combine",
    )(xs, y0.reshape(B, Lt, D), y1.reshape(B, Lt, D), modv, final_g.reshape(1, D))


S5_STATES = S5_GROUPS * S5_STATE
S5_CHUNK = 64


def _s5_kernel(u_ref, wd_ref, wr_ref, ar_ref, ai_ref, d_ref, y_ref, x_scr, h_scr, hr_scr, hi_scr, *, reverse):
    ns = S5_STATES

    @pl.when(pl.program_id(0) == 0)
    def _():
        hr_scr[...] = jnp.zeros_like(hr_scr)
        hi_scr[...] = jnp.zeros_like(hi_scr)

    u = u_ref[...]
    x_scr[...] = jnp.dot(u.astype(BF16), wd_ref[...], preferred_element_type=F32)
    ar = ar_ref[...]
    ai = ai_ref[...]

    def step(hr, hi, t):
        r = pl.multiple_of(t * SUBLANES, SUBLANES)
        xr = x_scr[pl.ds(r, SUBLANES), :ns]
        xi = x_scr[pl.ds(r, SUBLANES), ns:]
        return ar * hr - ai * hi + xr, ar * hi + ai * hr + xi

    def body(j, carry):
        hr, hi = carry
        t0 = (S5_CHUNK - 1 - 2 * j) if reverse else 2 * j
        t1 = t0 - 1 if reverse else t0 + 1
        hr0, hi0 = step(hr, hi, t0)
        hr1, hi1 = step(hr0, hi0, t1)
        lo = t1 if reverse else t0
        first_r, second_r = (hr1, hr0) if reverse else (hr0, hr1)
        first_i, second_i = (hi1, hi0) if reverse else (hi0, hi1)
        r = pl.multiple_of(lo * SUBLANES, 2 * SUBLANES)
        h_scr[pl.ds(r, 2 * SUBLANES), :ns] = jnp.concatenate([first_r, second_r], axis=0).astype(BF16)
        h_scr[pl.ds(r, 2 * SUBLANES), ns:] = jnp.concatenate([first_i, second_i], axis=0).astype(BF16)
        return hr1, hi1

    hr, hi = lax.fori_loop(0, S5_CHUNK // 2, body, (hr_scr[...], hi_scr[...]))
    hr_scr[...] = hr
    hi_scr[...] = hi
    y = jnp.dot(h_scr[...], wr_ref[...], preferred_element_type=F32)
    y = y + (d_ref[...].astype(F32) if reverse else u * d_ref[...])
    y_ref[...] = y.astype(y_ref.dtype)


def pl_s5_scan(u_tb, w_drive, w_read, a_re, a_im, addend, *, n_lat_steps, reverse):
    rows, ch = u_tb.shape
    rc = S5_CHUNK * SUBLANES
    n_chunks = rows // rc
    n_lat = n_lat_steps // S5_CHUNK
    n_ctx = n_chunks - n_lat
    assert rows % rc == 0 and n_lat_steps % S5_CHUNK == 0
    if reverse:
        def idx(i):
            return (n_chunks - 1 - i, 0)
    else:
        def idx(i):
            return (jnp.where(i < n_ctx, n_lat + i, i - n_ctx), 0)
    const = lambda i: (0, 0)
    ns2 = 2 * S5_STATES
    return pl.pallas_call(
        functools.partial(_s5_kernel, reverse=reverse),
        grid=(n_chunks,),
        in_specs=[pl.BlockSpec((rc, ch), idx),
                  pl.BlockSpec((ch, ns2), const),
                  pl.BlockSpec((ns2, ch), const),
                  pl.BlockSpec((SUBLANES, S5_STATES), const),
                  pl.BlockSpec((SUBLANES, S5_STATES), const),
                  pl.BlockSpec((rc, ch), idx) if reverse else pl.BlockSpec((1, ch), const)],
        out_specs=pl.BlockSpec((rc, ch), idx),
        out_shape=jax.ShapeDtypeStruct((rows, ch), BF16),
        scratch_shapes=[pltpu.VMEM((rc, ns2), F32), pltpu.VMEM((rc, ns2), BF16),
                        pltpu.VMEM((SUBLANES, S5_STATES), F32), pltpu.VMEM((SUBLANES, S5_STATES), F32)],
        compiler_params=_params(1),
        name="s5_scan_rev" if reverse else "s5_scan_fwd",
    )(u_tb, w_drive, w_read,
      jnp.broadcast_to(a_re[None, :], (SUBLANES, S5_STATES)),
      jnp.broadcast_to(a_im[None, :], (SUBLANES, S5_STATES)),
      addend if reverse else addend.reshape(1, ch))


FFT_N2 = 128


def _fft_tables(L):
    N = 2 * L
    N1 = N // FFT_N2
    k1 = np.arange(N1)[:, None]
    n1 = np.arange(N1 // 2)[None, :]
    n2 = np.arange(FFT_N2)[:, None, None]
    ang = -2.0 * np.pi * (k1[None] * (n2 + FFT_N2 * n1[None])) / N
    mr, mi = np.cos(ang), np.sin(ang)
    ma = np.concatenate([np.concatenate([mr, -mi], axis=2), np.concatenate([mi, mr], axis=2)], axis=1)
    gr, gi = np.transpose(mr, (0, 2, 1)), -np.transpose(mi, (0, 2, 1))
    mainv = np.concatenate([np.concatenate([gr, -gi], axis=2), np.concatenate([gi, gr], axis=2)], axis=1)
    kk = np.arange(FFT_N2)
    a2 = -2.0 * np.pi * np.outer(kk, kk) / FFT_N2
    fr, fi = np.cos(a2), np.sin(a2)
    f_fwd = np.block([[fr, -fi], [fi, fr]])
    f_inv = np.block([[fr, fi], [-fi, fr]])
    return (jnp.asarray(ma, BF16), jnp.asarray(mainv, BF16), jnp.asarray(f_fwd, BF16), jnp.asarray(f_inv, BF16))


HYENA_VMEM_LIMIT = 56 * 1024 * 1024
FFT_UNROLL = 8


def _hyena_fft_kernel(z_ref, h_ref, ma_ref, mainv_ref, ff_ref, fi_ref, o_ref, a_scr, b_scr, *, n1_count):
    half = n1_count // 2
    n2c = FFT_N2

    def stage_a(n2, c):
        xr = z_ref[0, pl.ds(n2, half, stride=n2c), :]
        xi = z_ref[1, pl.ds(n2, half, stride=n2c), :]
        x = jnp.concatenate([xr, xi], axis=0).astype(BF16)
        r = jnp.dot(ma_ref[n2], x, preferred_element_type=F32)
        a_scr[pl.ds(pl.multiple_of(n2 * 2 * n1_count, 2 * n1_count), 2 * n1_count), :] = r
        return c

    lax.fori_loop(0, n2c, stage_a, 0, unroll=FFT_UNROLL)

    def stage_c(k1, c):
        ar = a_scr[pl.ds(k1, n2c, stride=2 * n1_count), :]
        ai = a_scr[pl.ds(n1_count + k1, n2c, stride=2 * n1_count), :]
        x = jnp.concatenate([ar, ai], axis=0).astype(BF16)
        y = jnp.dot(ff_ref[...], x, preferred_element_type=F32)
        yr, yi = y[:n2c], y[n2c:]
        hr = h_ref[0, k1].astype(F32)
        hi = h_ref[1, k1].astype(F32)
        x2 = jnp.concatenate([yr * hr - yi * hi, yr * hi + yi * hr], axis=0).astype(BF16)
        b = jnp.dot(fi_ref[...], x2, preferred_element_type=F32)
        b_scr[pl.ds(pl.multiple_of(k1 * 2 * n2c, 2 * n2c), 2 * n2c), :] = b
        return c

    lax.fori_loop(0, n1_count, stage_c, 0, unroll=FFT_UNROLL)

    def stage_a_inv(n2, c):
        br = b_scr[pl.ds(n2, n1_count, stride=2 * n2c), :]
        bi = b_scr[pl.ds(n2c + n2, n1_count, stride=2 * n2c), :]
        x = jnp.concatenate([br, bi], axis=0).astype(BF16)
        r = jnp.dot(mainv_ref[n2], x, preferred_element_type=F32)
        o_ref[0, pl.ds(n2, half, stride=n2c), :] = r[:half]
        o_ref[1, pl.ds(n2, half, stride=n2c), :] = r[half:]
        return c

    lax.fori_loop(0, n2c, stage_a_inv, 0, unroll=FFT_UNROLL)


def hyena_spectrum(filt):
    N, C = filt.shape
    N1 = N // FFT_N2
    h2 = (jnp.fft.fft(filt, axis=0) / N).reshape(FFT_N2, N1, C).transpose(1, 0, 2)
    return jnp.stack([h2.real, h2.imag]).astype(BF16)


def pl_hyena_conv(z, h, dft_tables, L):
    B, _, C = z.shape
    N1 = 2 * L // FFT_N2
    ma, mainv, f_fwd, f_inv = dft_tables
    cw = LANES
    full = lambda arr: pl.BlockSpec(arr.shape, lambda ct, bp: (0,) * arr.ndim)
    return pl.pallas_call(
        functools.partial(_hyena_fft_kernel, n1_count=N1),
        grid=(C // cw, B // 2),
        in_specs=[pl.BlockSpec((2, L, cw), lambda ct, bp: (bp, 0, ct)),
                  pl.BlockSpec((2, N1, FFT_N2, cw), lambda ct, bp: (0, 0, 0, ct)),
                  full(ma), full(mainv), full(f_fwd), full(f_inv)],
        out_specs=pl.BlockSpec((2, L, cw), lambda ct, bp: (bp, 0, ct)),
        out_shape=jax.ShapeDtypeStruct((B, L, C), F32),
        scratch_shapes=[pltpu.VMEM((FFT_N2 * 2 * N1, cw), F32), pltpu.VMEM((N1 * 2 * FFT_N2, cw), F32)],
        compiler_params=_params(2, HYENA_VMEM_LIMIT),
        name="hyena_fft_conv",
    )(z, h, ma, mainv, f_fwd, f_inv)


def hyena_filter(L, w1, b1, w2, b2, w3, freq):
    t = jnp.arange(L, dtype=F32) / L
    ang = (2.0 * math.pi) * t[:, None] * jnp.arange(1, HY_BANDS + 1, dtype=F32)
    feat = jnp.concatenate([t[:, None], jnp.cos(ang), jnp.sin(ang)], axis=-1)
    hp = lax.Precision.HIGHEST
    h = jnp.sin(freq * (jnp.dot(feat, w1, precision=hp) + b1))
    h = jnp.sin(freq * (jnp.dot(h, w2, precision=hp) + b2))
    h = jnp.dot(h, w3, precision=hp).reshape(L, 2, HY_CH)
    window = jnp.exp(-t[:, None] * jnp.linspace(HY_DECAY_MIN, HY_DECAY_MAX, HY_CH, dtype=F32))
    h = h * window[:, None, :]
    filt = jnp.concatenate([h[:, 0], jnp.zeros((1, HY_CH), F32), h[:0:-1, 1]], axis=0)
    return filt / (jnp.sum(jnp.abs(filt), axis=0, keepdims=True) + EPS)


def hyena_conv(z, h_lat, ff_ctx, dft_tables, L):
    Lc = z.shape[1] - L
    y_lat = pl_hyena_conv(z, h_lat, dft_tables, L)
    zf = jnp.fft.rfft(z[:, L:], n=2 * Lc, axis=1)
    y_ctx = jnp.fft.irfft(zf * ff_ctx[None], n=2 * Lc, axis=1)[:, :Lc]
    return y_lat, y_ctx


def _block_diag(blocks):
    G, r, c = blocks.shape
    eye = jnp.eye(G, dtype=blocks.dtype)
    return (eye[:, None, :, None] * blocks[:, :, None, :]).reshape(G * r, G * c)


def s5_tables(a_re, a_im, log_dt, b_re, b_im, c_re, c_im):
    A = lax.complex(a_re, a_im)
    dtA = jnp.exp(log_dt)[:, None] * A
    a_bar = jnp.exp(dtA)
    b_bar = ((a_bar - 1.0) / A)[:, :, None] * lax.complex(b_re, b_im)
    bt_re = jnp.transpose(b_bar.real, (0, 2, 1))
    bt_im = jnp.transpose(b_bar.imag, (0, 2, 1))
    w_drive = jnp.concatenate([_block_diag(bt_re), _block_diag(bt_im)], axis=1)
    ct_re = jnp.transpose(c_re, (0, 2, 1))
    ct_im = jnp.transpose(c_im, (0, 2, 1))
    w_read = jnp.concatenate([_block_diag(ct_re), -_block_diag(ct_im)], axis=0)
    return w_drive.astype(BF16), w_read.astype(BF16), a_bar.real.reshape(-1), a_bar.imag.reshape(-1)


def s5_scan(u_tb, tables, d_skip, n_lat_steps):
    y = d_skip
    for direction in range(2):
        w_drive, w_read, a_re, a_im = (t[direction] for t in tables)
        y = pl_s5_scan(u_tb, w_drive, w_read, a_re, a_im, y, n_lat_steps=n_lat_steps, reverse=direction == 1)
    return y


def rope_tables(L, Lc):
    rows = L // GRID_W
    row = jnp.repeat(jnp.arange(rows, dtype=F32), GRID_W)
    col = jnp.tile(jnp.arange(GRID_W, dtype=F32), rows)
    inv = ROPE_BASE ** (-jnp.arange(ROPE_PAIRS_AXIS, dtype=F32) / ROPE_PAIRS_AXIS)
    ang = jnp.concatenate([row[:, None] * inv, col[:, None] * inv], axis=-1)
    cos, sin = jnp.cos(ang), jnp.sin(ang)
    n_maps = ATT_W // ATT_HEAD_DIM
    cosf = jnp.tile(jnp.concatenate([cos, cos], axis=-1), (1, n_maps))
    sinf = jnp.tile(jnp.concatenate([-sin, sin], axis=-1), (1, n_maps))
    return (jnp.concatenate([cosf, jnp.ones((Lc, ATT_W), F32)], axis=0),
            jnp.concatenate([sinf, jnp.zeros((Lc, ATT_W), F32)], axis=0))


def moe_dispatch(route, counts_row):
    T = route.shape[1]
    experts = [route[j].astype(jnp.int32) for j in range(MOE_TOP_K)]
    ranks = [route[2 * MOE_TOP_K + j].astype(jnp.int32) for j in range(MOE_TOP_K)]
    gate_cat = route[MOE_TOP_K:2 * MOE_TOP_K].reshape(-1)
    counts = counts_row[MOE_GROUPS:MOE_GROUPS + N_EXPERTS].astype(jnp.int32)
    n_assign = T * MOE_TOP_K
    n_blocks = -(-n_assign // MOE_BLOCK) + N_EXPERTS
    n_pad = n_blocks * MOE_BLOCK
    a_bits = (n_assign - 1).bit_length()
    t_ids = jnp.arange(T, dtype=jnp.int32)
    keys = jnp.concatenate([(experts[j] << a_bits) + (MOE_TOP_K * t_ids + j) for j in range(MOE_TOP_K)])
    order = jnp.sort(keys) & ((1 << a_bits) - 1)
    start = jnp.cumsum(counts) - counts
    padded = (counts + MOE_BLOCK - 1) // MOE_BLOCK * MOE_BLOCK
    pad_end = jnp.cumsum(padded)
    pad_start = pad_end - padded
    slots_of_tok = [(pad_start[experts[j]] + ranks[j]).astype(jnp.int32) for j in range(MOE_TOP_K)]
    block_first = jnp.arange(n_blocks, dtype=jnp.int32) * MOE_BLOCK
    block_e = jnp.minimum(jnp.sum((pad_end[None, :] <= block_first[:, None]).astype(jnp.int32), axis=1),
                          N_EXPERTS - 1).astype(jnp.int32)
    within = jnp.arange(MOE_BLOCK, dtype=jnp.int32)[None, :]
    slot_r = (block_first - pad_start[block_e])[:, None] + within
    slot_valid = ((slot_r < counts[block_e][:, None]) & (block_first < pad_end[-1])[:, None]).reshape(-1)
    sorted_pos = jnp.clip(start[block_e][:, None] + slot_r, 0, n_assign - 1).reshape(-1)
    slot_assign = order[sorted_pos]
    slot_t = slot_assign // MOE_TOP_K
    slot_j = slot_assign % MOE_TOP_K
    slot_tok = jnp.where(slot_valid, slot_t, jnp.arange(n_pad, dtype=jnp.int32) % T)
    slot_gate = jnp.where(slot_valid, gate_cat[slot_j * T + slot_t], 0.0)
    n_used = (pad_end[-1:] // MOE_BLOCK).astype(jnp.int32)
    return slot_tok, slot_gate, slots_of_tok, block_e, n_used


def hier_moe(h2, route, counts_row, w1, w3, w2, layer):
    slot_tok, slot_gate, slots_of_tok, block_e, n_used = moe_dispatch(route, counts_row)
    yb = pl_moe_ffn(h2[slot_tok], slot_gate, block_e, n_used, w1, w3, w2, layer)
    return tuple(yb[s] for s in slots_of_tok)


def kernel(x, c, ctx, c_ctx, w_mod, b_mod, norm1_g, norm2_g, final_g, w_in, w_out, hy_conv_w, hy_conv_b, hy_ffn_w1, hy_ffn_b1, hy_ffn_w2, hy_ffn_b2, hy_ffn_w3, hy_freq, hy_skip, hy_norm_g, s5_a_re, s5_a_im, s5_log_dt, s5_b_re, s5_b_im, s5_c_re, s5_c_im, s5_d, s5_glu_w, s5_norm_g, att_lq1, att_lk1, att_lq2, att_lk2, att_subln_g, moe_wg, moe_bg, moe_we, moe_be, moe_w1, moe_w3, moe_w2):
    B, L, D = x.shape
    Lc = ctx.shape[1]
    Lt = L + Lc
    assert B == SUBLANES and Lc == TOKEN_TILE and L % ATT_TQ == 0
    n_lat_tiles = L // TOKEN_TILE
    cosf, sinf = rope_tables(L, Lc)
    hp = lax.Precision.HIGHEST
    q_scale = ATT_HEAD_DIM ** -0.5 * math.log2(math.e)

    mods = jnp.einsum('bd,ldk->lbk', jax.nn.silu(c), w_mod, precision=hp) + b_mod[:, None, :]
    cmods = jnp.einsum('d,ldk->lk', jax.nn.silu(c_ctx), w_mod, precision=hp) + b_mod
    modvs = jnp.stack([mods, jnp.broadcast_to(cmods[:, None, :], mods.shape)], axis=2)
    modvs = modvs.reshape(DEPTH, 2 * B, 1, N_MOD * D)
    filt_args = (hy_ffn_w1, hy_ffn_b1, hy_ffn_w2, hy_ffn_b2, hy_ffn_w3, hy_freq)
    h_lat = jax.vmap(lambda *p: hyena_spectrum(hyena_filter(L, *p)))(*filt_args)
    ff_ctx = jax.vmap(lambda *p: jnp.fft.rfft(hyena_filter(Lc, *p), n=2 * Lc, axis=0))(*filt_args)
    dft_tables = _fft_tables(L)
    s5_tabs = jax.vmap(jax.vmap(s5_tables))(s5_a_re, s5_a_im, s5_log_dt, s5_b_re, s5_b_im, s5_c_re, s5_c_im)
    lam_inits = [0.8 - 0.6 * math.exp(-0.3 * l) for l in range(DEPTH)]
    lams = (jnp.exp(jnp.sum(att_lq1 * att_lk1, axis=-1)) - jnp.exp(jnp.sum(att_lq2 * att_lk2, axis=-1))
            + jnp.asarray(lam_inits, F32))
    w_router = jnp.zeros((DEPTH, D, LANES), F32).at[:, :, :MOE_GROUPS].set(moe_wg)
    w_router = w_router.at[:, :, MOE_GROUPS:MOE_GROUPS + N_EXPERTS].set(moe_we)
    rwh = w_router.astype(BF16)
    rwl = (w_router - rwh.astype(F32)).astype(BF16)
    router_bias = jnp.zeros((DEPTH, 1, LANES), F32).at[:, 0, :MOE_GROUPS].set(moe_bg)
    router_bias = router_bias.at[:, 0, MOE_GROUPS:MOE_GROUPS + N_EXPERTS].set(moe_be)
    w_in_b, w_out_b, glu_w_b = w_in.astype(BF16), w_out.astype(BF16), s5_glu_w.astype(BF16)

    xs = jnp.concatenate([x, ctx], axis=1)
    for l in range(DEPTH):
        modv = modvs[l]
        u0, z, s5_u, q_p, q_r, k_r, v = pl_in_proj(xs, modv, norm1_g[l], w_in_b[l], hy_conv_w[l], hy_conv_b[l],
                                                   cosf, sinf, n_lat_tiles, q_scale)

        y_lat, y_ctx = hyena_conv(z, h_lat[l], ff_ctx[l], dft_tables, L)

        s5_y = s5_scan(s5_u.reshape(Lt * B, S5_CH), tuple(t[l] for t in s5_tabs), s5_d[l], L)

        g_scaled = (att_subln_g[l] * (1.0 - lam_inits[l])).reshape(1, ATT_V_DIM)
        att_lat, att_ctx = pl_diff_attention(q_p, q_r, k_r, v, g_scaled, lams[l], L)

        xs, h2, route, counts = pl_out_proj(xs, u0, z, y_lat, y_ctx, s5_y.reshape(Lt, B * S5_CH), att_lat, att_ctx,
                                            modv, hy_skip[l], hy_norm_g[l], glu_w_b[l], s5_norm_g[l], w_out_b[l],
                                            norm2_g[l], rwh[l], rwl[l], router_bias[l], n_lat_tiles)

        y0, y1 = hier_moe(h2.reshape(B * Lt, D), route, counts[:, 0], moe_w1, moe_w3, moe_w2, l)
        xs = pl_moe_combine(xs, y0, y1, modv, final_g, n_lat_tiles, final=l == DEPTH - 1)
    return xs
```

```python
import functools
import math

import jax
import jax.numpy as jnp
import numpy as np
from jax import lax
from jax.experimental import pallas as pl
from jax.experimental.pallas import tpu as pltpu

D_MODEL = 1024
DEPTH = 4
GRID_W = 64
N_MOD = 6
EPS = 1e-6
HY_CH = D_MODEL // 4
S5_CH = D_MODEL // 4
ATT_W = D_MODEL // 2
HY_BANDS = 16
HY_DECAY_MIN = -math.log(1e-2) / 1.5
HY_DECAY_MAX = -math.log(1e-2) / 0.3
S5_GROUP = 16
S5_GROUPS = S5_CH // S5_GROUP
S5_STATE = 64
ATT_HEAD_DIM = 64
ATT_HEADS = ATT_W // (2 * ATT_HEAD_DIM)
ATT_V_DIM = 2 * ATT_HEAD_DIM
ROPE_HALF = ATT_HEAD_DIM // 2
ROPE_PAIRS_AXIS = ROPE_HALF // 2
ROPE_BASE = 10000.0
MOE_GROUPS = 4
MOE_EPG = 8
N_EXPERTS = MOE_GROUPS * MOE_EPG
MOE_TOP_K = 2
MOE_BLOCK = 512
IN_COLS = 3 * HY_CH + S5_CH + 3 * ATT_W
COL_S5 = 3 * HY_CH
COL_Q = COL_S5 + S5_CH
COL_K = COL_Q + ATT_W
COL_V = COL_K + ATT_W

LANES = 128
SUBLANES = 8
VMEM_LIMIT = 48 * 1024 * 1024
TOKEN_TILE = 256

F32 = jnp.float32
BF16 = jnp.bfloat16


def _params(n_axes, vmem=VMEM_LIMIT):
    return pltpu.CompilerParams(dimension_semantics=("arbitrary",) * n_axes, vmem_limit_bytes=vmem)


def _rms(x):
    return x * lax.rsqrt(jnp.mean(x * x, axis=-1, keepdims=True) + EPS)


def _in_kernel(x_ref, xp_ref, xn_ref, mod_ref, g_ref, w_ref, cw_ref, cb_ref, cos_ref, sin_ref,
               u0_ref, z_ref, s5_ref, qp_ref, qr_ref, k_ref, v_ref, *, n_lat_tiles, q_scale):
    i = pl.program_id(1)
    tm = x_ref.shape[1]
    d = x_ref.shape[2]
    g = g_ref[...]
    shift = mod_ref[0, :, 0:d]
    scale = mod_ref[0, :, d:2 * d]

    def norm_mod(xt):
        return (_rms(xt) * g) * (1.0 + scale) + shift

    h = jnp.concatenate([norm_mod(x_ref[0]), norm_mod(xp_ref[0]), norm_mod(xn_ref[0])], axis=0).astype(BF16)
    p = jnp.dot(h, w_ref[...], preferred_element_type=F32)

    hy = p[:tm, :COL_S5]
    is_ctx = i == n_lat_tiles
    has_prev = jnp.logical_and(i != 0, jnp.logical_not(is_ctx))
    has_next = jnp.logical_and(i != n_lat_tiles - 1, jnp.logical_not(is_ctx))
    prev_row = jnp.where(has_prev, p[tm + SUBLANES - 1:tm + SUBLANES, :COL_S5], 0.0)
    next_row = jnp.where(has_next, p[tm + SUBLANES:tm + SUBLANES + 1, :COL_S5], 0.0)
    rows = lax.broadcasted_iota(jnp.int32, (tm, 1), 0)
    up = jnp.where(rows == 0, prev_row, pltpu.roll(hy, 1, axis=0))
    dn = jnp.where(rows == tm - 1, next_row, pltpu.roll(hy, tm - 1, axis=0))
    u = up * cw_ref[0:1, :] + hy * cw_ref[1:2, :] + dn * cw_ref[2:3, :] + cb_ref[...]
    u0_ref[0] = u[:, :HY_CH].astype(u0_ref.dtype)
    z_ref[0] = u[:, HY_CH:2 * HY_CH] * u[:, 2 * HY_CH:]

    s5_ref[...] = p[:tm, COL_S5:COL_Q]

    lane = lax.broadcasted_iota(jnp.int32, (1, ATT_W), 1)
    first_half = jnp.bitwise_and(lane, ATT_HEAD_DIM - 1) < ROPE_HALF
    cos = cos_ref[...]
    sin = sin_ref[...]

    def rope(t):
        partner = jnp.where(first_half, pltpu.roll(t, ATT_W - ROPE_HALF, axis=1), pltpu.roll(t, ROPE_HALF, axis=1))
        return t * cos + partner * sin

    q = p[:tm, COL_Q:COL_K] * q_scale
    qp_ref[0] = q.astype(BF16)
    qr_ref[0] = rope(q).astype(BF16)
    k_ref[0] = rope(p[:tm, COL_K:COL_V]).astype(BF16)
    v_ref[0] = p[:tm, COL_V:].astype(BF16)


def pl_in_proj(xs, modv, norm_g, w_in, conv_w, conv_b, cosf, sinf, n_lat_tiles, q_scale):
    B, Lt, D = xs.shape
    tm = TOKEN_TILE
    n_tiles = Lt // tm
    halo_per_tile = tm // SUBLANES
    n_halo_blocks = Lt // SUBLANES
    tok = lambda w: pl.BlockSpec((1, tm, w), lambda b, i: (b, i, 0))
    const = lambda shape: pl.BlockSpec(shape, lambda b, i: (0,) * len(shape))
    out_shapes = (jax.ShapeDtypeStruct((B, Lt, HY_CH), BF16), jax.ShapeDtypeStruct((B, Lt, HY_CH), F32),
                  jax.ShapeDtypeStruct((Lt, B * S5_CH), F32),
                  jax.ShapeDtypeStruct((B, Lt, ATT_W), BF16), jax.ShapeDtypeStruct((B, Lt, ATT_W), BF16),
                  jax.ShapeDtypeStruct((B, Lt, ATT_W), BF16), jax.ShapeDtypeStruct((B, Lt, ATT_W), BF16))
    return pl.pallas_call(
        functools.partial(_in_kernel, n_lat_tiles=n_lat_tiles, q_scale=q_scale),
        grid=(B, n_tiles),
        in_specs=[tok(D),
                  pl.BlockSpec((1, SUBLANES, D), lambda b, i: (b, jnp.maximum(i * halo_per_tile - 1, 0), 0)),
                  pl.BlockSpec((1, SUBLANES, D),
                               lambda b, i: (b, jnp.minimum((i + 1) * halo_per_tile, n_halo_blocks - 1), 0)),
                  pl.BlockSpec((1, 1, N_MOD * D), lambda b, i: (2 * b + jnp.where(i >= n_lat_tiles, 1, 0), 0, 0)),
                  const((1, D)), const((D, IN_COLS)), const((3, COL_S5)), const((1, COL_S5)),
                  pl.BlockSpec((tm, ATT_W), lambda b, i: (i, 0)), pl.BlockSpec((tm, ATT_W), lambda b, i: (i, 0))],
        out_specs=(tok(HY_CH), tok(HY_CH), pl.BlockSpec((tm, S5_CH), lambda b, i: (i, b)),
                   tok(ATT_W), tok(ATT_W), tok(ATT_W), tok(ATT_W)),
        out_shape=out_shapes,
        compiler_params=_params(2),
        name="in_proj",
    )(xs, xs, xs, modv, norm_g.reshape(1, D), w_in, conv_w, conv_b.reshape(1, COL_S5), cosf, sinf)


def _out_kernel(x_ref, u0_ref, z_ref, yl_ref, yc_ref, s5_ref, al_ref, ac_ref, mod_ref, skip_ref, hg_ref,
                gw_ref, sg_ref, wo_ref, n2g_ref, rwh_ref, rwl_ref, rb_ref, tri_ref, xo_ref, h2_ref, rt_ref, cnt_ref,
                cnt_scr, *, n_lat_tiles):
    d = x_ref.shape[2]
    is_ctx = pl.program_id(1) == n_lat_tiles
    y = jnp.where(is_ctx, yc_ref[0], yl_ref[0])
    att = jnp.where(is_ctx, ac_ref[0], al_ref[0])
    hy = _rms(u0_ref[0].astype(F32) * (y + skip_ref[...] * z_ref[0])) * hg_ref[...]
    gl = jax.nn.gelu(s5_ref[...].astype(F32))
    gate = jax.nn.sigmoid(jnp.dot(gl.astype(BF16), gw_ref[...], preferred_element_type=F32))
    s5 = _rms(gl * gate) * sg_ref[...]
    mix = jnp.concatenate([hy.astype(BF16), s5.astype(BF16), att], axis=1)
    proj = jnp.dot(mix, wo_ref[...], preferred_element_type=F32)
    xn = x_ref[0] + mod_ref[0, :, 2 * d:3 * d] * proj
    xo_ref[0] = xn
    h2 = (_rms(xn) * n2g_ref[...]) * (1.0 + mod_ref[0, :, 4 * d:5 * d]) + mod_ref[0, :, 3 * d:4 * d]
    hh = h2.astype(BF16)
    hl = (h2 - hh.astype(F32)).astype(BF16)
    h2_ref[0] = hh
    lg = jnp.dot(hh, rwh_ref[...], preferred_element_type=F32)
    lg += jnp.dot(hl, rwh_ref[...], preferred_element_type=F32)
    lg += jnp.dot(hh, rwl_ref[...], preferred_element_type=F32)
    lg = lg + rb_ref[...]

    n_rows = cnt_scr.shape[0]
    lt = jnp.transpose(lg)[:n_rows, :]
    row = lax.broadcasted_iota(jnp.int32, (n_rows, 1), 0)
    neg_inf = jnp.float32(-jnp.inf)

    def first_max(v):
        m = jnp.max(v, axis=0, keepdims=True)
        return m, jnp.min(jnp.where(v == m, row, n_rows), axis=0, keepdims=True)

    g_logit = jnp.where(row < MOE_GROUPS, lt, neg_inf)
    g_max, g_idx = first_max(g_logit)
    p_group = 1.0 / jnp.sum(jnp.exp(g_logit - g_max), axis=0, keepdims=True)
    e_row = row - MOE_GROUPS
    in_group = jnp.logical_and(jnp.logical_and(e_row >= 0, e_row < N_EXPERTS),
                               jnp.right_shift(e_row, MOE_EPG.bit_length() - 1) == g_idx)
    e_logit = jnp.where(in_group, lt, neg_inf)
    e_exp = jnp.exp(e_logit - jnp.max(e_logit, axis=0, keepdims=True))
    probs = jnp.where(in_group, e_exp / jnp.sum(e_exp, axis=0, keepdims=True), -1.0)
    p1, l1 = first_max(probs)
    p2, l2 = first_max(jnp.where(row == l1, -1.0, probs))
    hit1 = row == l1
    hit2 = row == l2
    onehot = jnp.where(hit1, 1.0, 0.0) + jnp.where(hit2, 1.0, 0.0)

    @pl.when(jnp.logical_and(pl.program_id(0) == 0, pl.program_id(1) == 0))
    def _():
        cnt_scr[...] = jnp.zeros_like(cnt_scr)

    earlier = jnp.dot(onehot.astype(BF16), tri_ref[...], preferred_element_type=F32)
    before = cnt_scr[:, 0:1] + earlier
    r1 = jnp.sum(jnp.where(hit1, before, 0.0), axis=0, keepdims=True)
    r2 = jnp.sum(jnp.where(hit2, before, 0.0), axis=0, keepdims=True)
    cnt_scr[...] = cnt_scr[...] + jnp.sum(onehot, axis=1, keepdims=True)
    cnt_ref[...] = cnt_scr[...]
    fields = [(l1 - MOE_GROUPS).astype(F32), (l2 - MOE_GROUPS).astype(F32), p_group * p1 / (p1 + p2),
              p_group * p2 / (p1 + p2), r1, r2]
    field_row = lax.broadcasted_iota(jnp.int32, (SUBLANES, 1), 0)
    route = jnp.zeros((SUBLANES, lt.shape[1]), F32)
    for j, f in enumerate(fields):
        route = jnp.where(field_row == j, f, route)
    rt_ref[...] = route


def pl_out_proj(xs, u0, z, y_lat, y_ctx, s5_y, att_lat, att_ctx, modv, hy_skip, hy_norm_g, glu_w, s5_norm_g,
                w_out, norm2_g, rwh, rwl, router_bias, n_lat_tiles):
    B, Lt, D = xs.shape
    tm = TOKEN_TILE
    tri = jnp.asarray(np.triu(np.ones((tm, tm), np.float32), 1), BF16)
    n_rows = -(-(MOE_GROUPS + N_EXPERTS) // 16) * 16
    tok = lambda w: pl.BlockSpec((1, tm, w), lambda b, i: (b, i, 0))
    lat_tok = lambda w: pl.BlockSpec((1, tm, w), lambda b, i: (b, jnp.minimum(i, n_lat_tiles - 1), 0))
    ctx_tok = lambda w: pl.BlockSpec((1, tm, w), lambda b, i: (b, 0, 0))
    tb = pl.BlockSpec((tm, S5_CH), lambda b, i: (i, b))
    const = lambda shape: pl.BlockSpec(shape, lambda b, i: (0,) * len(shape))
    return pl.pallas_call(
        functools.partial(_out_kernel, n_lat_tiles=n_lat_tiles),
        grid=(B, Lt // tm),
        in_specs=[tok(D), tok(HY_CH), tok(HY_CH), lat_tok(HY_CH), ctx_tok(HY_CH), tb, lat_tok(ATT_W),
                  ctx_tok(ATT_W),
                  pl.BlockSpec((1, 1, N_MOD * D), lambda b, i: (2 * b + jnp.where(i >= n_lat_tiles, 1, 0), 0, 0)),
                  const((1, HY_CH)), const((1, HY_CH)), const((S5_CH, S5_CH)), const((1, S5_CH)),
                  const((D, D)), const((1, D)), const((D, LANES)), const((D, LANES)), const((1, LANES)),
                  const((tm, tm))],
        out_specs=(tok(D), tok(D), pl.BlockSpec((SUBLANES, tm), lambda b, i: (0, b * (Lt // tm) + i)),
                   const((n_rows, LANES))),
        out_shape=(jax.ShapeDtypeStruct((B, Lt, D), F32), jax.ShapeDtypeStruct((B, Lt, D), BF16),
                   jax.ShapeDtypeStruct((SUBLANES, B * Lt), F32), jax.ShapeDtypeStruct((n_rows, LANES), F32)),
        scratch_shapes=[pltpu.VMEM((n_rows, LANES), F32)],
        compiler_params=_params(2),
        name="out_proj",
    )(xs, u0, z, y_lat, y_ctx, s5_y, att_lat, att_ctx, modv, hy_skip.reshape(1, HY_CH),
      hy_norm_g.reshape(1, HY_CH), glu_w, s5_norm_g.reshape(1, S5_CH), w_out, norm2_g.reshape(1, D), rwh, rwl,
      router_bias, tri)


def _dot_nt(a, b):
    return lax.dot_general(a, b, (((1,), (1,)), ((), ())), preferred_element_type=F32)


ATT_TQ = 512
ATT_SUB = 256


def _attn_kernel(qp_ref, qr_ref, k_ref, v_ref, g_ref, lam_ref, o_ref, *, n_lat):
    tq = qp_ref.shape[1]
    first_map = lax.broadcasted_iota(jnp.int32, (1, LANES), 1) < ATT_HEAD_DIM
    zero = jnp.zeros((), BF16)
    sub = min(ATT_SUB, tq)
    for r0 in range(0, tq, sub):
        qp = qp_ref[0, r0:r0 + sub, :]
        qr = qr_ref[0, r0:r0 + sub, :]
        probs = []
        for m in range(2):
            in_map = first_map if m == 0 else jnp.logical_not(first_map)
            s_c = _dot_nt(jnp.where(in_map, qp, zero), k_ref[0, n_lat:, :])
            mx = jnp.max(s_c, axis=-1, keepdims=True)
            if n_lat:
                s_l = _dot_nt(jnp.where(in_map, qr, zero), k_ref[0, :n_lat, :])
                mx = jnp.maximum(mx, jnp.max(s_l, axis=-1, keepdims=True))
                p_l = jnp.exp2(s_l - mx)
            p_c = jnp.exp2(s_c - mx)
            den = jnp.sum(p_c, axis=-1, keepdims=True)
            if n_lat:
                den = den + jnp.sum(p_l, axis=-1, keepdims=True)
            probs.append((p_c.astype(BF16), p_l.astype(BF16) if n_lat else None, 1.0 / den))
        w0 = probs[0][2].astype(BF16)
        w1 = (lam_ref[0:1, 0:1] * probs[1][2]).astype(BF16)
        a_c = probs[0][0] * w0 - probs[1][0] * w1
        o = jnp.dot(a_c, v_ref[0, n_lat:, :], preferred_element_type=F32)
        if n_lat:
            a_l = probs[0][1] * w0 - probs[1][1] * w1
            o = o + jnp.dot(a_l, v_ref[0, :n_lat, :], preferred_element_type=F32)
        o_ref[0, r0:r0 + sub, :] = (_rms(o) * g_ref[...]).astype(o_ref.dtype)


def pl_diff_attention(qp, qr, k, v, g_scaled, lam, n_lat):
    B, Lt, _ = qp.shape
    n_ctx = Lt - n_lat
    lam_arr = jnp.full((SUBLANES, LANES), lam, F32)
    small = [pl.BlockSpec((1, LANES), lambda b, h, i: (0, 0)), pl.BlockSpec((SUBLANES, LANES), lambda b, h, i: (0, 0))]
    tq = ATT_TQ
    qspec = pl.BlockSpec((1, tq, LANES), lambda b, h, i: (b, i, h))
    kspec = pl.BlockSpec((1, Lt, LANES), lambda b, h, i: (b, 0, h))
    out_lat = pl.pallas_call(
        functools.partial(_attn_kernel, n_lat=n_lat),
        grid=(B, ATT_HEADS, n_lat // tq),
        in_specs=[qspec, qspec, kspec, kspec] + small,
        out_specs=qspec,
        out_shape=jax.ShapeDtypeStruct((B, n_lat, ATT_W), BF16),
        compiler_params=_params(3),
        name="diff_attention",
    )(qp, qr, k, v, g_scaled, lam_arr)
    ctx_blk = n_lat // n_ctx
    cspec = pl.BlockSpec((1, n_ctx, LANES), lambda b, h, i: (b, ctx_blk, h))
    out_ctx = pl.pallas_call(
        functools.partial(_attn_kernel, n_lat=0),
        grid=(B, ATT_HEADS, 1),
        in_specs=[cspec, cspec, cspec, cspec] + small,
        out_specs=pl.BlockSpec((1, n_ctx, LANES), lambda b, h, i: (b, 0, h)),
        out_shape=jax.ShapeDtypeStruct((B, n_ctx, ATT_W), BF16),
        compiler_params=_params(3),
        name="diff_attention_ctx",
    )(qp, qr, k, v, g_scaled, lam_arr)
    return out_lat, out_ctx


def _moe_kernel(be_ref, nb_ref, x_ref, w1_ref, w3_ref, w2_ref, o_ref, w1_scr, w3_scr, w2_scr):
    i = pl.program_id(0)
    new_expert = jnp.logical_or(i == 0, be_ref[i] != be_ref[jnp.maximum(i - 1, 0)])

    @pl.when(jnp.logical_and(i < nb_ref[0], new_expert))
    def _():
        w1_scr[...] = w1_ref[0, 0].astype(BF16)
        w3_scr[...] = w3_ref[0, 0].astype(BF16)
        w2_scr[...] = w2_ref[0, 0].astype(BF16)

    @pl.when(i < nb_ref[0])
    def _():
        x = x_ref[...]
        a = jnp.dot(x, w1_scr[...], preferred_element_type=F32)
        b = jnp.dot(x, w3_scr[...], preferred_element_type=F32)
        h = (a * jax.nn.sigmoid(a)) * b
        o_ref[...] = jnp.dot(h.astype(BF16), w2_scr[...], preferred_element_type=F32).astype(o_ref.dtype)

    @pl.when(i >= nb_ref[0])
    def _():
        o_ref[...] = jnp.zeros_like(o_ref)


def pl_moe_ffn(xb, block_e, n_used, w1, w3, w2, layer):
    n_pad, D = xb.shape
    n_blocks = n_pad // MOE_BLOCK
    F = w1.shape[-1]
    grid_spec = pltpu.PrefetchScalarGridSpec(
        num_scalar_prefetch=2,
        grid=(n_blocks,),
        in_specs=[pl.BlockSpec((MOE_BLOCK, D), lambda i, be, nb: (i, 0)),
                  pl.BlockSpec((1, 1, D, F), lambda i, be, nb: (layer, be[i], 0, 0)),
                  pl.BlockSpec((1, 1, D, F), lambda i, be, nb: (layer, be[i], 0, 0)),
                  pl.BlockSpec((1, 1, F, D), lambda i, be, nb: (layer, be[i], 0, 0))],
        out_specs=pl.BlockSpec((MOE_BLOCK, D), lambda i, be, nb: (i, 0)),
        scratch_shapes=[pltpu.VMEM((D, F), BF16), pltpu.VMEM((D, F), BF16), pltpu.VMEM((F, D), BF16)],
    )
    return pl.pallas_call(
        _moe_kernel,
        grid_spec=grid_spec,
        out_shape=jax.ShapeDtypeStruct((n_pad, D), BF16),
        compiler_params=_params(1),
        name="moe_ffn",
    )(block_e, n_used, xb, w1, w3, w2)


def _combine_kernel(x_ref, y0_ref, y1_ref, rt_ref, mod_ref, g_ref, o_ref, *, final):
    d = x_ref.shape[2]
    tm = x_ref.shape[1]
    rows = lax.broadcasted_iota(jnp.int32, (tm, 1), 0)
    lane = lax.broadcasted_iota(jnp.int32, (1, LANES), 1)
    on_diag = lane == jnp.bitwise_and(rows, LANES - 1)

    def gate_col(field):
        g_rows = rt_ref[field:field + 1, 0:LANES]
        for r in range(1, tm // LANES):
            g_rows = jnp.where(rows >= r * LANES, rt_ref[field:field + 1, r * LANES:(r + 1) * LANES], g_rows)
        return jnp.sum(jnp.where(on_diag, g_rows, 0.0), axis=1, keepdims=True)

    y = y0_ref[0].astype(F32) * gate_col(MOE_TOP_K) + y1_ref[0].astype(F32) * gate_col(MOE_TOP_K + 1)
    xn = x_ref[0] + mod_ref[0, :, 5 * d:6 * d] * y
    o_ref[0] = _rms(xn) * g_ref[...] if final else xn


def pl_moe_combine(xs, y0, y1, route, modv, final_g, n_lat_tiles, final):
    B, Lt, D = xs.shape
    tm = TOKEN_TILE
    tiles_per_batch = Lt // tm
    n_tiles = n_lat_tiles if final else tiles_per_batch
    tok = pl.BlockSpec((1, tm, D), lambda b, i: (b, i, 0))
    return pl.pallas_call(
        functools.partial(_combine_kernel, final=final),
        grid=(B, n_tiles),
        in_specs=[tok, tok, tok,
                  pl.BlockSpec((SUBLANES, tm), lambda b, i: (0, b * tiles_per_batch + i)),
                  pl.BlockSpec((1, 1, N_MOD * D), lambda b, i: (2 * b + jnp.where(i >= n_lat_tiles, 1, 0), 0, 0)),
                  pl.BlockSpec((1, D), lambda b, i: (0, 0))],
        out_specs=tok,
        out_shape=jax.ShapeDtypeStruct((B, n_tiles * tm, D), F32),
        compiler_params=_params(2),
        name="moe_combine",
    )(xs, y0.reshape(B, Lt, D), y1.reshape(B, Lt, D), route, modv, final_g.reshape(1, D))


S5_STATES = S5_GROUPS * S5_STATE
S5_CHUNK = 64


def _s5_kernel(u_ref, wd_ref, wr_ref, ar_ref, ai_ref, d_ref, y_ref, x_scr, h_scr, hr_scr, hi_scr, *, reverse):
    ns = S5_STATES

    @pl.when(pl.program_id(0) == 0)
    def _():
        hr_scr[...] = jnp.zeros_like(hr_scr)
        hi_scr[...] = jnp.zeros_like(hi_scr)

    u = u_ref[...]
    x_scr[...] = jnp.dot(u.astype(BF16), wd_ref[...], preferred_element_type=F32)
    ar = ar_ref[...]
    ai = ai_ref[...]

    def step(hr, hi, t):
        r = pl.multiple_of(t * SUBLANES, SUBLANES)
        xr = x_scr[pl.ds(r, SUBLANES), :ns]
        xi = x_scr[pl.ds(r, SUBLANES), ns:]
        return ar * hr - ai * hi + xr, ar * hi + ai * hr + xi

    def body(j, carry):
        hr, hi = carry
        t0 = (S5_CHUNK - 1 - 2 * j) if reverse else 2 * j
        t1 = t0 - 1 if reverse else t0 + 1
        hr0, hi0 = step(hr, hi, t0)
        hr1, hi1 = step(hr0, hi0, t1)
        lo = t1 if reverse else t0
        first_r, second_r = (hr1, hr0) if reverse else (hr0, hr1)
        first_i, second_i = (hi1, hi0) if reverse else (hi0, hi1)
        r = pl.multiple_of(lo * SUBLANES, 2 * SUBLANES)
        h_scr[pl.ds(r, 2 * SUBLANES), :ns] = jnp.concatenate([first_r, second_r], axis=0).astype(BF16)
        h_scr[pl.ds(r, 2 * SUBLANES), ns:] = jnp.concatenate([first_i, second_i], axis=0).astype(BF16)
        return hr1, hi1

    hr, hi = lax.fori_loop(0, S5_CHUNK // 2, body, (hr_scr[...], hi_scr[...]))
    hr_scr[...] = hr
    hi_scr[...] = hi
    y = jnp.dot(h_scr[...], wr_ref[...], preferred_element_type=F32)
    y = y + (d_ref[...].astype(F32) if reverse else u * d_ref[...])
    y_ref[...] = y.astype(y_ref.dtype)


def pl_s5_scan(u_tb, w_drive, w_read, a_re, a_im, addend, *, n_lat_steps, reverse):
    rows, ch = u_tb.shape
    rc = S5_CHUNK * SUBLANES
    n_chunks = rows // rc
    n_lat = n_lat_steps // S5_CHUNK
    n_ctx = n_chunks - n_lat
    assert rows % rc == 0 and n_lat_steps % S5_CHUNK == 0
    if reverse:
        def idx(i):
            return (n_chunks - 1 - i, 0)
    else:
        def idx(i):
            return (jnp.where(i < n_ctx, n_lat + i, i - n_ctx), 0)
    const = lambda i: (0, 0)
    ns2 = 2 * S5_STATES
    return pl.pallas_call(
        functools.partial(_s5_kernel, reverse=reverse),
        grid=(n_chunks,),
        in_specs=[pl.BlockSpec((rc, ch), idx),
                  pl.BlockSpec((ch, ns2), const),
                  pl.BlockSpec((ns2, ch), const),
                  pl.BlockSpec((SUBLANES, S5_STATES), const),
                  pl.BlockSpec((SUBLANES, S5_STATES), const),
                  pl.BlockSpec((rc, ch), idx) if reverse else pl.BlockSpec((1, ch), const)],
        out_specs=pl.BlockSpec((rc, ch), idx),
        out_shape=jax.ShapeDtypeStruct((rows, ch), BF16),
        scratch_shapes=[pltpu.VMEM((rc, ns2), F32), pltpu.VMEM((rc, ns2), BF16),
                        pltpu.VMEM((SUBLANES, S5_STATES), F32), pltpu.VMEM((SUBLANES, S5_STATES), F32)],
        compiler_params=_params(1),
        name="s5_scan_rev" if reverse else "s5_scan_fwd",
    )(u_tb, w_drive, w_read,
      jnp.broadcast_to(a_re[None, :], (SUBLANES, S5_STATES)),
      jnp.broadcast_to(a_im[None, :], (SUBLANES, S5_STATES)),
      addend if reverse else addend.reshape(1, ch))


FFT_N2 = 128


def _fft_tables(L):
    N = 2 * L
    N1 = N // FFT_N2
    k1 = np.arange(N1)[:, None]
    n1 = np.arange(N1 // 2)[None, :]
    n2 = np.arange(FFT_N2)[:, None, None]
    ang = -2.0 * np.pi * (k1[None] * (n2 + FFT_N2 * n1[None])) / N
    mr, mi = np.cos(ang), np.sin(ang)
    ma = np.concatenate([np.concatenate([mr, -mi], axis=2), np.concatenate([mi, mr], axis=2)], axis=1)
    gr, gi = np.transpose(mr, (0, 2, 1)), -np.transpose(mi, (0, 2, 1))
    mainv = np.concatenate([np.concatenate([gr, -gi], axis=2), np.concatenate([gi, gr], axis=2)], axis=1)
    kk = np.arange(FFT_N2)
    a2 = -2.0 * np.pi * np.outer(kk, kk) / FFT_N2
    fr, fi = np.cos(a2), np.sin(a2)
    f_fwd = np.block([[fr, -fi], [fi, fr]])
    f_inv = np.block([[fr, fi], [-fi, fr]])
    return (jnp.asarray(ma, BF16), jnp.asarray(mainv, BF16), jnp.asarray(f_fwd, BF16), jnp.asarray(f_inv, BF16))


HYENA_VMEM_LIMIT = 56 * 1024 * 1024
FFT_UNROLL = 8


def _hyena_fft_kernel(z_ref, h_ref, ma_ref, mainv_ref, ff_ref, fi_ref, o_ref, a_scr, b_scr, *, n1_count):
    half = n1_count // 2
    n2c = FFT_N2

    def stage_a(n2, c):
        xr = z_ref[0, pl.ds(n2, half, stride=n2c), :]
        xi = z_ref[1, pl.ds(n2, half, stride=n2c), :]
        x = jnp.concatenate([xr, xi], axis=0).astype(BF16)
        r = jnp.dot(ma_ref[n2], x, preferred_element_type=F32)
        a_scr[pl.ds(pl.multiple_of(n2 * 2 * n1_count, 2 * n1_count), 2 * n1_count), :] = r
        return c

    lax.fori_loop(0, n2c, stage_a, 0, unroll=FFT_UNROLL)

    def stage_c(k1, c):
        ar = a_scr[pl.ds(k1, n2c, stride=2 * n1_count), :]
        ai = a_scr[pl.ds(n1_count + k1, n2c, stride=2 * n1_count), :]
        x = jnp.concatenate([ar, ai], axis=0).astype(BF16)
        y = jnp.dot(ff_ref[...], x, preferred_element_type=F32)
        yr, yi = y[:n2c], y[n2c:]
        hr = h_ref[0, k1].astype(F32)
        hi = h_ref[1, k1].astype(F32)
        x2 = jnp.concatenate([yr * hr - yi * hi, yr * hi + yi * hr], axis=0).astype(BF16)
        b = jnp.dot(fi_ref[...], x2, preferred_element_type=F32)
        b_scr[pl.ds(pl.multiple_of(k1 * 2 * n2c, 2 * n2c), 2 * n2c), :] = b
        return c

    lax.fori_loop(0, n1_count, stage_c, 0, unroll=FFT_UNROLL)

    def stage_a_inv(n2, c):
        br = b_scr[pl.ds(n2, n1_count, stride=2 * n2c), :]
        bi = b_scr[pl.ds(n2c + n2, n1_count, stride=2 * n2c), :]
        x = jnp.concatenate([br, bi], axis=0).astype(BF16)
        r = jnp.dot(mainv_ref[n2], x, preferred_element_type=F32)
        o_ref[0, pl.ds(n2, half, stride=n2c), :] = r[:half]
        o_ref[1, pl.ds(n2, half, stride=n2c), :] = r[half:]
        return c

    lax.fori_loop(0, n2c, stage_a_inv, 0, unroll=FFT_UNROLL)


def hyena_spectrum(filt):
    N, C = filt.shape
    N1 = N // FFT_N2
    h2 = (jnp.fft.fft(filt, axis=0) / N).reshape(FFT_N2, N1, C).transpose(1, 0, 2)
    return jnp.stack([h2.real, h2.imag]).astype(BF16)


def pl_hyena_conv(z, h, dft_tables, L):
    B, _, C = z.shape
    N1 = 2 * L // FFT_N2
    ma, mainv, f_fwd, f_inv = dft_tables
    cw = LANES
    full = lambda arr: pl.BlockSpec(arr.shape, lambda ct, bp: (0,) * arr.ndim)
    return pl.pallas_call(
        functools.partial(_hyena_fft_kernel, n1_count=N1),
        grid=(C // cw, B // 2),
        in_specs=[pl.BlockSpec((2, L, cw), lambda ct, bp: (bp, 0, ct)),
                  pl.BlockSpec((2, N1, FFT_N2, cw), lambda ct, bp: (0, 0, 0, ct)),
                  full(ma), full(mainv), full(f_fwd), full(f_inv)],
        out_specs=pl.BlockSpec((2, L, cw), lambda ct, bp: (bp, 0, ct)),
        out_shape=jax.ShapeDtypeStruct((B, L, C), F32),
        scratch_shapes=[pltpu.VMEM((FFT_N2 * 2 * N1, cw), F32), pltpu.VMEM((N1 * 2 * FFT_N2, cw), F32)],
        compiler_params=_params(2, HYENA_VMEM_LIMIT),
        name="hyena_fft_conv",
    )(z, h, ma, mainv, f_fwd, f_inv)


def hyena_filter(L, w1, b1, w2, b2, w3, freq):
    t = jnp.arange(L, dtype=F32) / L
    ang = (2.0 * math.pi) * t[:, None] * jnp.arange(1, HY_BANDS + 1, dtype=F32)
    feat = jnp.concatenate([t[:, None], jnp.cos(ang), jnp.sin(ang)], axis=-1)
    hp = lax.Precision.HIGHEST
    h = jnp.sin(freq * (jnp.dot(feat, w1, precision=hp) + b1))
    h = jnp.sin(freq * (jnp.dot(h, w2, precision=hp) + b2))
    h = jnp.dot(h, w3, precision=hp).reshape(L, 2, HY_CH)
    window = jnp.exp(-t[:, None] * jnp.linspace(HY_DECAY_MIN, HY_DECAY_MAX, HY_CH, dtype=F32))
    h = h * window[:, None, :]
    filt = jnp.concatenate([h[:, 0], jnp.zeros((1, HY_CH), F32), h[:0:-1, 1]], axis=0)
    return filt / (jnp.sum(jnp.abs(filt), axis=0, keepdims=True) + EPS)


def hyena_conv(z, h_lat, ff_ctx, dft_tables, L):
    Lc = z.shape[1] - L
    y_lat = pl_hyena_conv(z, h_lat, dft_tables, L)
    zf = jnp.fft.rfft(z[:, L:], n=2 * Lc, axis=1)
    y_ctx = jnp.fft.irfft(zf * ff_ctx[None], n=2 * Lc, axis=1)[:, :Lc]
    return y_lat, y_ctx


def _block_diag(blocks):
    G, r, c = blocks.shape
    eye = jnp.eye(G, dtype=blocks.dtype)
    return (eye[:, None, :, None] * blocks[:, :, None, :]).reshape(G * r, G * c)


def s5_tables(a_re, a_im, log_dt, b_re, b_im, c_re, c_im):
    A = lax.complex(a_re, a_im)
    dtA = jnp.exp(log_dt)[:, None] * A
    a_bar = jnp.exp(dtA)
    b_bar = ((a_bar - 1.0) / A)[:, :, None] * lax.complex(b_re, b_im)
    bt_re = jnp.transpose(b_bar.real, (0, 2, 1))
    bt_im = jnp.transpose(b_bar.imag, (0, 2, 1))
    w_drive = jnp.concatenate([_block_diag(bt_re), _block_diag(bt_im)], axis=1)
    ct_re = jnp.transpose(c_re, (0, 2, 1))
    ct_im = jnp.transpose(c_im, (0, 2, 1))
    w_read = jnp.concatenate([_block_diag(ct_re), -_block_diag(ct_im)], axis=0)
    return w_drive.astype(BF16), w_read.astype(BF16), a_bar.real.reshape(-1), a_bar.imag.reshape(-1)


def s5_scan(u_tb, tables, d_skip, n_lat_steps):
    y = d_skip
    for direction in range(2):
        w_drive, w_read, a_re, a_im = (t[direction] for t in tables)
        y = pl_s5_scan(u_tb, w_drive, w_read, a_re, a_im, y, n_lat_steps=n_lat_steps, reverse=direction == 1)
    return y


def rope_tables(L, Lc):
    rows = L // GRID_W
    row = jnp.repeat(jnp.arange(rows, dtype=F32), GRID_W)
    col = jnp.tile(jnp.arange(GRID_W, dtype=F32), rows)
    inv = ROPE_BASE ** (-jnp.arange(ROPE_PAIRS_AXIS, dtype=F32) / ROPE_PAIRS_AXIS)
    ang = jnp.concatenate([row[:, None] * inv, col[:, None] * inv], axis=-1)
    cos, sin = jnp.cos(ang), jnp.sin(ang)
    n_maps = ATT_W // ATT_HEAD_DIM
    cosf = jnp.tile(jnp.concatenate([cos, cos], axis=-1), (1, n_maps))
    sinf = jnp.tile(jnp.concatenate([-sin, sin], axis=-1), (1, n_maps))
    return (jnp.concatenate([cosf, jnp.ones((Lc, ATT_W), F32)], axis=0),
            jnp.concatenate([sinf, jnp.zeros((Lc, ATT_W), F32)], axis=0))


def moe_dispatch(route, counts_row):
    T = route.shape[1]
    experts = [route[j].astype(jnp.int32) for j in range(MOE_TOP_K)]
    ranks = [route[2 * MOE_TOP_K + j].astype(jnp.int32) for j in range(MOE_TOP_K)]
    counts = counts_row[MOE_GROUPS:MOE_GROUPS + N_EXPERTS].astype(jnp.int32)
    n_assign = T * MOE_TOP_K
    n_blocks = -(-n_assign // MOE_BLOCK) + N_EXPERTS
    n_pad = n_blocks * MOE_BLOCK
    a_bits = (n_assign - 1).bit_length()
    t_ids = jnp.arange(T, dtype=jnp.int32)
    keys = jnp.concatenate([(experts[j] << a_bits) + (MOE_TOP_K * t_ids + j) for j in range(MOE_TOP_K)])
    order = jnp.sort(keys) & ((1 << a_bits) - 1)
    start = jnp.cumsum(counts) - counts
    padded = (counts + MOE_BLOCK - 1) // MOE_BLOCK * MOE_BLOCK
    pad_end = jnp.cumsum(padded)
    pad_start = pad_end - padded
    slots_of_tok = [(pad_start[experts[j]] + ranks[j]).astype(jnp.int32) for j in range(MOE_TOP_K)]
    block_first = jnp.arange(n_blocks, dtype=jnp.int32) * MOE_BLOCK
    block_e = jnp.minimum(jnp.sum((pad_end[None, :] <= block_first[:, None]).astype(jnp.int32), axis=1),
                          N_EXPERTS - 1).astype(jnp.int32)
    within = jnp.arange(MOE_BLOCK, dtype=jnp.int32)[None, :]
    slot_r = (block_first - pad_start[block_e])[:, None] + within
    slot_valid = ((slot_r < counts[block_e][:, None]) & (block_first < pad_end[-1])[:, None]).reshape(-1)
    sorted_pos = jnp.clip(start[block_e][:, None] + slot_r, 0, n_assign - 1).reshape(-1)
    slot_t = order[sorted_pos] // MOE_TOP_K
    slot_tok = jnp.where(slot_valid, slot_t, jnp.arange(n_pad, dtype=jnp.int32) % T)
    n_used = (pad_end[-1:] // MOE_BLOCK).astype(jnp.int32)
    return slot_tok, slots_of_tok, block_e, n_used


def hier_moe(h2, route, counts_row, w1, w3, w2, layer):
    slot_tok, slots_of_tok, block_e, n_used = moe_dispatch(route, counts_row)
    yb = pl_moe_ffn(h2[slot_tok], block_e, n_used, w1, w3, w2, layer)
    return tuple(yb[s] for s in slots_of_tok)


def kernel(x, c, ctx, c_ctx, w_mod, b_mod, norm1_g, norm2_g, final_g, w_in, w_out, hy_conv_w, hy_conv_b, hy_ffn_w1, hy_ffn_b1, hy_ffn_w2, hy_ffn_b2, hy_ffn_w3, hy_freq, hy_skip, hy_norm_g, s5_a_re, s5_a_im, s5_log_dt, s5_b_re, s5_b_im, s5_c_re, s5_c_im, s5_d, s5_glu_w, s5_norm_g, att_lq1, att_lk1, att_lq2, att_lk2, att_subln_g, moe_wg, moe_bg, moe_we, moe_be, moe_w1, moe_w3, moe_w2):
    B, L, D = x.shape
    Lc = ctx.shape[1]
    Lt = L + Lc
    assert B == SUBLANES and Lc == TOKEN_TILE and L % ATT_TQ == 0
    n_lat_tiles = L // TOKEN_TILE
    cosf, sinf = rope_tables(L, Lc)
    hp = lax.Precision.HIGHEST
    q_scale = ATT_HEAD_DIM ** -0.5 * math.log2(math.e)

    mods = jnp.einsum('bd,ldk->lbk', jax.nn.silu(c), w_mod, precision=hp) + b_mod[:, None, :]
    cmods = jnp.einsum('d,ldk->lk', jax.nn.silu(c_ctx), w_mod, precision=hp) + b_mod
    modvs = jnp.stack([mods, jnp.broadcast_to(cmods[:, None, :], mods.shape)], axis=2)
    modvs = modvs.reshape(DEPTH, 2 * B, 1, N_MOD * D)
    filt_args = (hy_ffn_w1, hy_ffn_b1, hy_ffn_w2, hy_ffn_b2, hy_ffn_w3, hy_freq)
    h_lat = jax.vmap(lambda *p: hyena_spectrum(hyena_filter(L, *p)))(*filt_args)
    ff_ctx = jax.vmap(lambda *p: jnp.fft.rfft(hyena_filter(Lc, *p), n=2 * Lc, axis=0))(*filt_args)
    dft_tables = _fft_tables(L)
    s5_tabs = jax.vmap(jax.vmap(s5_tables))(s5_a_re, s5_a_im, s5_log_dt, s5_b_re, s5_b_im, s5_c_re, s5_c_im)
    lam_inits = [0.8 - 0.6 * math.exp(-0.3 * l) for l in range(DEPTH)]
    lams = (jnp.exp(jnp.sum(att_lq1 * att_lk1, axis=-1)) - jnp.exp(jnp.sum(att_lq2 * att_lk2, axis=-1))
            + jnp.asarray(lam_inits, F32))
    w_router = jnp.zeros((DEPTH, D, LANES), F32).at[:, :, :MOE_GROUPS].set(moe_wg)
    w_router = w_router.at[:, :, MOE_GROUPS:MOE_GROUPS + N_EXPERTS].set(moe_we)
    rwh = w_router.astype(BF16)
    rwl = (w_router - rwh.astype(F32)).astype(BF16)
    router_bias = jnp.zeros((DEPTH, 1, LANES), F32).at[:, 0, :MOE_GROUPS].set(moe_bg)
    router_bias = router_bias.at[:, 0, MOE_GROUPS:MOE_GROUPS + N_EXPERTS].set(moe_be)
    w_in_b, w_out_b, glu_w_b = w_in.astype(BF16), w_out.astype(BF16), s5_glu_w.astype(BF16)

    xs = jnp.concatenate([x, ctx], axis=1)
    for l in range(DEPTH):
        modv = modvs[l]
        u0, z, s5_u, q_p, q_r, k_r, v = pl_in_proj(xs, modv, norm1_g[l], w_in_b[l], hy_conv_w[l], hy_conv_b[l],
                                                   cosf, sinf, n_lat_tiles, q_scale)

        y_lat, y_ctx = hyena_conv(z, h_lat[l], ff_ctx[l], dft_tables, L)

        s5_y = s5_scan(s5_u.reshape(Lt * B, S5_CH), tuple(t[l] for t in s5_tabs), s5_d[l], L)

        g_scaled = (att_subln_g[l] * (1.0 - lam_inits[l])).reshape(1, ATT_V_DIM)
        att_lat, att_ctx = pl_diff_attention(q_p, q_r, k_r, v, g_scaled, lams[l], L)

        xs, h2, route, counts = pl_out_proj(xs, u0, z, y_lat, y_ctx, s5_y.reshape(Lt, B * S5_CH), att_lat, att_ctx,
                                            modv, hy_skip[l], hy_norm_g[l], glu_w_b[l], s5_norm_g[l], w_out_b[l],
                                            norm2_g[l], rwh[l], rwl[l], router_bias[l], n_lat_tiles)

        y0, y1 = hier_moe(h2.reshape(B * Lt, D), route, counts[:, 0], moe_w1, moe_w3, moe_w2, l)
        xs = pl_moe_combine(xs, y0, y1, route, modv, final_g, n_lat_tiles, final=l == DEPTH - 1)
    return xs
```

```python
import functools
import math

import jax
import jax.numpy as jnp
import numpy as np
from jax import lax
from jax.experimental import pallas as pl
from jax.experimental.pallas import tpu as pltpu

D_MODEL = 1024
DEPTH = 4
GRID_W = 64
N_MOD = 6
EPS = 1e-6
HY_CH = D_MODEL // 4
S5_CH = D_MODEL // 4
ATT_W = D_MODEL // 2
HY_BANDS = 16
HY_DECAY_MIN = -math.log(1e-2) / 1.5
HY_DECAY_MAX = -math.log(1e-2) / 0.3
S5_GROUP = 16
S5_GROUPS = S5_CH // S5_GROUP
S5_STATE = 64
ATT_HEAD_DIM = 64
ATT_HEADS = ATT_W // (2 * ATT_HEAD_DIM)
ATT_V_DIM = 2 * ATT_HEAD_DIM
ROPE_HALF = ATT_HEAD_DIM // 2
ROPE_PAIRS_AXIS = ROPE_HALF // 2
ROPE_BASE = 10000.0
MOE_GROUPS = 4
MOE_EPG = 8
N_EXPERTS = MOE_GROUPS * MOE_EPG
MOE_TOP_K = 2
MOE_BLOCK = 512
IN_COLS = 3 * HY_CH + S5_CH + 3 * ATT_W
COL_S5 = 3 * HY_CH
COL_Q = COL_S5 + S5_CH
COL_K = COL_Q + ATT_W
COL_V = COL_K + ATT_W

LANES = 128
SUBLANES = 8
VMEM_LIMIT = 48 * 1024 * 1024
TOKEN_TILE = 256

F32 = jnp.float32
BF16 = jnp.bfloat16


def _params(n_axes, vmem=VMEM_LIMIT):
    return pltpu.CompilerParams(dimension_semantics=("arbitrary",) * n_axes, vmem_limit_bytes=vmem)


def _rms(x):
    return x * lax.rsqrt(jnp.mean(x * x, axis=-1, keepdims=True) + EPS)


def _in_kernel(x_ref, xp_ref, xn_ref, mod_ref, g_ref, w_ref, cw_ref, cb_ref, cos_ref, sin_ref,
               u0_ref, z_ref, s5_ref, qp_ref, qr_ref, k_ref, v_ref, *, n_lat_tiles, q_scale):
    i = pl.program_id(1)
    tm = x_ref.shape[1]
    d = x_ref.shape[2]
    g = g_ref[...]
    shift = mod_ref[0, :, 0:d]
    scale = mod_ref[0, :, d:2 * d]

    def norm_mod(xt):
        return (_rms(xt) * g) * (1.0 + scale) + shift

    h = jnp.concatenate([norm_mod(x_ref[0]), norm_mod(xp_ref[0]), norm_mod(xn_ref[0])], axis=0).astype(BF16)
    p = jnp.dot(h, w_ref[...], preferred_element_type=F32)

    hy = p[:tm, :COL_S5]
    is_ctx = i == n_lat_tiles
    has_prev = jnp.logical_and(i != 0, jnp.logical_not(is_ctx))
    has_next = jnp.logical_and(i != n_lat_tiles - 1, jnp.logical_not(is_ctx))
    prev_row = jnp.where(has_prev, p[tm + SUBLANES - 1:tm + SUBLANES, :COL_S5], 0.0)
    next_row = jnp.where(has_next, p[tm + SUBLANES:tm + SUBLANES + 1, :COL_S5], 0.0)
    rows = lax.broadcasted_iota(jnp.int32, (tm, 1), 0)
    up = jnp.where(rows == 0, prev_row, pltpu.roll(hy, 1, axis=0))
    dn = jnp.where(rows == tm - 1, next_row, pltpu.roll(hy, tm - 1, axis=0))
    u = up * cw_ref[0:1, :] + hy * cw_ref[1:2, :] + dn * cw_ref[2:3, :] + cb_ref[...]
    u0_ref[0] = u[:, :HY_CH].astype(u0_ref.dtype)
    z_ref[0] = u[:, HY_CH:2 * HY_CH] * u[:, 2 * HY_CH:]

    s5_ref[...] = p[:tm, COL_S5:COL_Q]

    lane = lax.broadcasted_iota(jnp.int32, (1, ATT_W), 1)
    first_half = jnp.bitwise_and(lane, ATT_HEAD_DIM - 1) < ROPE_HALF
    cos = cos_ref[...]
    sin = sin_ref[...]

    def rope(t):
        partner = jnp.where(first_half, pltpu.roll(t, ATT_W - ROPE_HALF, axis=1), pltpu.roll(t, ROPE_HALF, axis=1))
        return t * cos + partner * sin

    q = p[:tm, COL_Q:COL_K] * q_scale
    qp_ref[0] = q.astype(BF16)
    qr_ref[0] = rope(q).astype(BF16)
    k_ref[0] = rope(p[:tm, COL_K:COL_V]).astype(BF16)
    v_ref[0] = p[:tm, COL_V:].astype(BF16)


def pl_in_proj(xs, modv, norm_g, w_in, conv_w, conv_b, cosf, sinf, n_lat_tiles, q_scale):
    B, Lt, D = xs.shape
    tm = TOKEN_TILE
    n_tiles = Lt // tm
    halo_per_tile = tm // SUBLANES
    n_halo_blocks = Lt // SUBLANES
    tok = lambda w: pl.BlockSpec((1, tm, w), lambda b, i: (b, i, 0))
    const = lambda shape: pl.BlockSpec(shape, lambda b, i: (0,) * len(shape))
    out_shapes = (jax.ShapeDtypeStruct((B, Lt, HY_CH), BF16), jax.ShapeDtypeStruct((B, Lt, HY_CH), F32),
                  jax.ShapeDtypeStruct((Lt, B * S5_CH), F32),
                  jax.ShapeDtypeStruct((B, Lt, ATT_W), BF16), jax.ShapeDtypeStruct((B, Lt, ATT_W), BF16),
                  jax.ShapeDtypeStruct((B, Lt, ATT_W), BF16), jax.ShapeDtypeStruct((B, Lt, ATT_W), BF16))
    return pl.pallas_call(
        functools.partial(_in_kernel, n_lat_tiles=n_lat_tiles, q_scale=q_scale),
        grid=(B, n_tiles),
        in_specs=[tok(D),
                  pl.BlockSpec((1, SUBLANES, D), lambda b, i: (b, jnp.maximum(i * halo_per_tile - 1, 0), 0)),
                  pl.BlockSpec((1, SUBLANES, D),
                               lambda b, i: (b, jnp.minimum((i + 1) * halo_per_tile, n_halo_blocks - 1), 0)),
                  pl.BlockSpec((1, 1, N_MOD * D), lambda b, i: (2 * b + jnp.where(i >= n_lat_tiles, 1, 0), 0, 0)),
                  const((1, D)), const((D, IN_COLS)), const((3, COL_S5)), const((1, COL_S5)),
                  pl.BlockSpec((tm, ATT_W), lambda b, i: (i, 0)), pl.BlockSpec((tm, ATT_W), lambda b, i: (i, 0))],
        out_specs=(tok(HY_CH), tok(HY_CH), pl.BlockSpec((tm, S5_CH), lambda b, i: (i, b)),
                   tok(ATT_W), tok(ATT_W), tok(ATT_W), tok(ATT_W)),
        out_shape=out_shapes,
        compiler_params=_params(2),
        name="in_proj",
    )(xs, xs, xs, modv, norm_g.reshape(1, D), w_in, conv_w, conv_b.reshape(1, COL_S5), cosf, sinf)


def _out_kernel(x_ref, u0_ref, z_ref, yl_ref, yc_ref, s5_ref, al_ref, ac_ref, mod_ref, skip_ref, hg_ref,
                gw_ref, sg_ref, wo_ref, n2g_ref, rwh_ref, rwl_ref, rb_ref, tri_ref, xo_ref, h2_ref, rt_ref, cnt_ref,
                cnt_scr, *, n_lat_tiles):
    d = x_ref.shape[2]
    is_ctx = pl.program_id(1) == n_lat_tiles
    y = jnp.where(is_ctx, yc_ref[0], yl_ref[0])
    att = jnp.where(is_ctx, ac_ref[0], al_ref[0])
    hy = _rms(u0_ref[0].astype(F32) * (y + skip_ref[...] * z_ref[0])) * hg_ref[...]
    gl = jax.nn.gelu(s5_ref[...].astype(F32))
    gate = jax.nn.sigmoid(jnp.dot(gl.astype(BF16), gw_ref[...], preferred_element_type=F32))
    s5 = _rms(gl * gate) * sg_ref[...]
    mix = jnp.concatenate([hy.astype(BF16), s5.astype(BF16), att], axis=1)
    proj = jnp.dot(mix, wo_ref[...], preferred_element_type=F32)
    xn = x_ref[0] + mod_ref[0, :, 2 * d:3 * d] * proj
    xo_ref[0] = xn
    h2 = (_rms(xn) * n2g_ref[...]) * (1.0 + mod_ref[0, :, 4 * d:5 * d]) + mod_ref[0, :, 3 * d:4 * d]
    hh = h2.astype(BF16)
    hl = (h2 - hh.astype(F32)).astype(BF16)
    h2_ref[0] = hh
    lg = jnp.dot(hh, rwh_ref[...], preferred_element_type=F32)
    lg += jnp.dot(hl, rwh_ref[...], preferred_element_type=F32)
    lg += jnp.dot(hh, rwl_ref[...], preferred_element_type=F32)
    lg = lg + rb_ref[...]

    n_rows = cnt_scr.shape[0]
    lt = jnp.transpose(lg)[:n_rows, :]
    row = lax.broadcasted_iota(jnp.int32, (n_rows, 1), 0)
    neg_inf = jnp.float32(-jnp.inf)

    def first_max(v):
        m = jnp.max(v, axis=0, keepdims=True)
        return m, jnp.min(jnp.where(v == m, row, n_rows), axis=0, keepdims=True)

    g_logit = jnp.where(row < MOE_GROUPS, lt, neg_inf)
    g_max, g_idx = first_max(g_logit)
    p_group = 1.0 / jnp.sum(jnp.exp(g_logit - g_max), axis=0, keepdims=True)
    e_row = row - MOE_GROUPS
    in_group = jnp.logical_and(jnp.logical_and(e_row >= 0, e_row < N_EXPERTS),
                               jnp.right_shift(e_row, MOE_EPG.bit_length() - 1) == g_idx)
    e_logit = jnp.where(in_group, lt, neg_inf)
    e_exp = jnp.exp(e_logit - jnp.max(e_logit, axis=0, keepdims=True))
    probs = jnp.where(in_group, e_exp / jnp.sum(e_exp, axis=0, keepdims=True), -1.0)
    p1, l1 = first_max(probs)
    p2, l2 = first_max(jnp.where(row == l1, -1.0, probs))
    hit1 = row == l1
    hit2 = row == l2
    onehot = jnp.where(hit1, 1.0, 0.0) + jnp.where(hit2, 1.0, 0.0)

    @pl.when(jnp.logical_and(pl.program_id(0) == 0, pl.program_id(1) == 0))
    def _():
        cnt_scr[...] = jnp.zeros_like(cnt_scr)

    earlier = jnp.dot(onehot.astype(BF16), tri_ref[...], preferred_element_type=F32)
    before = cnt_scr[:, 0:1] + earlier
    r1 = jnp.sum(jnp.where(hit1, before, 0.0), axis=0, keepdims=True)
    r2 = jnp.sum(jnp.where(hit2, before, 0.0), axis=0, keepdims=True)
    cnt_scr[...] = cnt_scr[...] + jnp.sum(onehot, axis=1, keepdims=True)
    cnt_ref[...] = cnt_scr[...]
    fields = [(l1 - MOE_GROUPS).astype(F32), (l2 - MOE_GROUPS).astype(F32), p_group * p1 / (p1 + p2),
              p_group * p2 / (p1 + p2), r1, r2]
    field_row = lax.broadcasted_iota(jnp.int32, (SUBLANES, 1), 0)
    route = jnp.zeros((SUBLANES, lt.shape[1]), F32)
    for j, f in enumerate(fields):
        route = jnp.where(field_row == j, f, route)
    rt_ref[...] = route


def pl_out_proj(xs, u0, z, y_lat, y_ctx, s5_y, att_lat, att_ctx, modv, hy_skip, hy_norm_g, glu_w, s5_norm_g,
                w_out, norm2_g, rwh, rwl, router_bias, n_lat_tiles):
    B, Lt, D = xs.shape
    tm = TOKEN_TILE
    tri = jnp.asarray(np.triu(np.ones((tm, tm), np.float32), 1), BF16)
    n_rows = -(-(MOE_GROUPS + N_EXPERTS) // 16) * 16
    tok = lambda w: pl.BlockSpec((1, tm, w), lambda b, i: (b, i, 0))
    lat_tok = lambda w: pl.BlockSpec((1, tm, w), lambda b, i: (b, jnp.minimum(i, n_lat_tiles - 1), 0))
    ctx_tok = lambda w: pl.BlockSpec((1, tm, w), lambda b, i: (b, 0, 0))
    tb = pl.BlockSpec((tm, S5_CH), lambda b, i: (i, b))
    const = lambda shape: pl.BlockSpec(shape, lambda b, i: (0,) * len(shape))
    return pl.pallas_call(
        functools.partial(_out_kernel, n_lat_tiles=n_lat_tiles),
        grid=(B, Lt // tm),
        in_specs=[tok(D), tok(HY_CH), tok(HY_CH), lat_tok(HY_CH), ctx_tok(HY_CH), tb, lat_tok(ATT_W),
                  ctx_tok(ATT_W),
                  pl.BlockSpec((1, 1, N_MOD * D), lambda b, i: (2 * b + jnp.where(i >= n_lat_tiles, 1, 0), 0, 0)),
                  const((1, HY_CH)), const((1, HY_CH)), const((S5_CH, S5_CH)), const((1, S5_CH)),
                  const((D, D)), const((1, D)), const((D, LANES)), const((D, LANES)), const((1, LANES)),
                  const((tm, tm))],
        out_specs=(tok(D), tok(D), pl.BlockSpec((SUBLANES, tm), lambda b, i: (0, b * (Lt // tm) + i)),
                   const((n_rows, LANES))),
        out_shape=(jax.ShapeDtypeStruct((B, Lt, D), F32), jax.ShapeDtypeStruct((B, Lt, D), BF16),
                   jax.ShapeDtypeStruct((SUBLANES, B * Lt), F32), jax.ShapeDtypeStruct((n_rows, LANES), F32)),
        scratch_shapes=[pltpu.VMEM((n_rows, LANES), F32)],
        compiler_params=_params(2),
        name="out_proj",
    )(xs, u0, z, y_lat, y_ctx, s5_y, att_lat, att_ctx, modv, hy_skip.reshape(1, HY_CH),
      hy_norm_g.reshape(1, HY_CH), glu_w, s5_norm_g.reshape(1, S5_CH), w_out, norm2_g.reshape(1, D), rwh, rwl,
      router_bias, tri)


def _dot_nt(a, b):
    return lax.dot_general(a, b, (((1,), (1,)), ((), ())), preferred_element_type=F32)


ATT_TQ = 1024
ATT_SUB = 256


def _attn_kernel(qp_ref, qr_ref, k_ref, v_ref, g_ref, lam_ref, o_ref, *, n_lat):
    tq = qp_ref.shape[1]
    first_map = lax.broadcasted_iota(jnp.int32, (1, LANES), 1) < ATT_HEAD_DIM
    zero = jnp.zeros((), BF16)
    sub = min(ATT_SUB, tq)
    for r0 in range(0, tq, sub):
        qp = qp_ref[0, r0:r0 + sub, :]
        qr = qr_ref[0, r0:r0 + sub, :]
        probs = []
        for m in range(2):
            in_map = first_map if m == 0 else jnp.logical_not(first_map)
            s_c = _dot_nt(jnp.where(in_map, qp, zero), k_ref[0, n_lat:, :])
            mx = jnp.max(s_c, axis=-1, keepdims=True)
            if n_lat:
                s_l = _dot_nt(jnp.where(in_map, qr, zero), k_ref[0, :n_lat, :])
                mx = jnp.maximum(mx, jnp.max(s_l, axis=-1, keepdims=True))
                p_l = jnp.exp2(s_l - mx)
            p_c = jnp.exp2(s_c - mx)
            den = jnp.sum(p_c, axis=-1, keepdims=True)
            if n_lat:
                den = den + jnp.sum(p_l, axis=-1, keepdims=True)
            probs.append((p_c.astype(BF16), p_l.astype(BF16) if n_lat else None, 1.0 / den))
        w0 = probs[0][2].astype(BF16)
        w1 = (lam_ref[0:1, 0:1] * probs[1][2]).astype(BF16)
        a_c = probs[0][0] * w0 - probs[1][0] * w1
        o = jnp.dot(a_c, v_ref[0, n_lat:, :], preferred_element_type=F32)
        if n_lat:
            a_l = probs[0][1] * w0 - probs[1][1] * w1
            o = o + jnp.dot(a_l, v_ref[0, :n_lat, :], preferred_element_type=F32)
        o_ref[0, r0:r0 + sub, :] = (_rms(o) * g_ref[...]).astype(o_ref.dtype)


def pl_diff_attention(qp, qr, k, v, g_scaled, lam, n_lat):
    B, Lt, _ = qp.shape
    n_ctx = Lt - n_lat
    lam_arr = jnp.full((SUBLANES, LANES), lam, F32)
    small = [pl.BlockSpec((1, LANES), lambda b, h, i: (0, 0)), pl.BlockSpec((SUBLANES, LANES), lambda b, h, i: (0, 0))]
    tq = ATT_TQ
    qspec = pl.BlockSpec((1, tq, LANES), lambda b, h, i: (b, i, h))
    kspec = pl.BlockSpec((1, Lt, LANES), lambda b, h, i: (b, 0, h))
    out_lat = pl.pallas_call(
        functools.partial(_attn_kernel, n_lat=n_lat),
        grid=(B, ATT_HEADS, n_lat // tq),
        in_specs=[qspec, qspec, kspec, kspec] + small,
        out_specs=qspec,
        out_shape=jax.ShapeDtypeStruct((B, n_lat, ATT_W), BF16),
        compiler_params=_params(3),
        name="diff_attention",
    )(qp, qr, k, v, g_scaled, lam_arr)
    ctx_blk = n_lat // n_ctx
    cspec = pl.BlockSpec((1, n_ctx, LANES), lambda b, h, i: (b, ctx_blk, h))
    out_ctx = pl.pallas_call(
        functools.partial(_attn_kernel, n_lat=0),
        grid=(B, ATT_HEADS, 1),
        in_specs=[cspec, cspec, cspec, cspec] + small,
        out_specs=pl.BlockSpec((1, n_ctx, LANES), lambda b, h, i: (b, 0, h)),
        out_shape=jax.ShapeDtypeStruct((B, n_ctx, ATT_W), BF16),
        compiler_params=_params(3),
        name="diff_attention_ctx",
    )(qp, qr, k, v, g_scaled, lam_arr)
    return out_lat, out_ctx


def _moe_kernel(be_ref, nb_ref, x_ref, w1_ref, w3_ref, w2_ref, o_ref, w1_scr, w3_scr, w2_scr):
    i = pl.program_id(0)
    new_expert = jnp.logical_or(i == 0, be_ref[i] != be_ref[jnp.maximum(i - 1, 0)])

    @pl.when(jnp.logical_and(i < nb_ref[0], new_expert))
    def _():
        w1_scr[...] = w1_ref[0, 0].astype(BF16)
        w3_scr[...] = w3_ref[0, 0].astype(BF16)
        w2_scr[...] = w2_ref[0, 0].astype(BF16)

    @pl.when(i < nb_ref[0])
    def _():
        x = x_ref[...]
        a = jnp.dot(x, w1_scr[...], preferred_element_type=F32)
        b = jnp.dot(x, w3_scr[...], preferred_element_type=F32)
        h = (a * jax.nn.sigmoid(a)) * b
        o_ref[...] = jnp.dot(h.astype(BF16), w2_scr[...], preferred_element_type=F32).astype(o_ref.dtype)

    @pl.when(i >= nb_ref[0])
    def _():
        o_ref[...] = jnp.zeros_like(o_ref)


def pl_moe_ffn(xb, block_e, n_used, w1, w3, w2, layer):
    n_pad, D = xb.shape
    n_blocks = n_pad // MOE_BLOCK
    F = w1.shape[-1]
    grid_spec = pltpu.PrefetchScalarGridSpec(
        num_scalar_prefetch=2,
        grid=(n_blocks,),
        in_specs=[pl.BlockSpec((MOE_BLOCK, D), lambda i, be, nb: (i, 0)),
                  pl.BlockSpec((1, 1, D, F), lambda i, be, nb: (layer, be[i], 0, 0)),
                  pl.BlockSpec((1, 1, D, F), lambda i, be, nb: (layer, be[i], 0, 0)),
                  pl.BlockSpec((1, 1, F, D), lambda i, be, nb: (layer, be[i], 0, 0))],
        out_specs=pl.BlockSpec((MOE_BLOCK, D), lambda i, be, nb: (i, 0)),
        scratch_shapes=[pltpu.VMEM((D, F), BF16), pltpu.VMEM((D, F), BF16), pltpu.VMEM((F, D), BF16)],
    )
    return pl.pallas_call(
        _moe_kernel,
        grid_spec=grid_spec,
        out_shape=jax.ShapeDtypeStruct((n_pad, D), BF16),
        compiler_params=_params(1),
        name="moe_ffn",
    )(block_e, n_used, xb, w1, w3, w2)


def _combine_kernel(x_ref, y0_ref, y1_ref, rt_ref, mod_ref, g_ref, o_ref, *, final):
    d = x_ref.shape[2]
    tm = x_ref.shape[1]
    rows = lax.broadcasted_iota(jnp.int32, (tm, 1), 0)
    lane = lax.broadcasted_iota(jnp.int32, (1, LANES), 1)
    on_diag = lane == jnp.bitwise_and(rows, LANES - 1)

    def gate_col(field):
        g_rows = rt_ref[field:field + 1, 0:LANES]
        for r in range(1, tm // LANES):
            g_rows = jnp.where(rows >= r * LANES, rt_ref[field:field + 1, r * LANES:(r + 1) * LANES], g_rows)
        return jnp.sum(jnp.where(on_diag, g_rows, 0.0), axis=1, keepdims=True)

    y = y0_ref[0].astype(F32) * gate_col(MOE_TOP_K) + y1_ref[0].astype(F32) * gate_col(MOE_TOP_K + 1)
    xn = x_ref[0] + mod_ref[0, :, 5 * d:6 * d] * y
    o_ref[0] = _rms(xn) * g_ref[...] if final else xn


def pl_moe_combine(xs, y0, y1, route, modv, final_g, n_lat_tiles, final):
    B, Lt, D = xs.shape
    tm = TOKEN_TILE
    tiles_per_batch = Lt // tm
    n_tiles = n_lat_tiles if final else tiles_per_batch
    tok = pl.BlockSpec((1, tm, D), lambda b, i: (b, i, 0))
    return pl.pallas_call(
        functools.partial(_combine_kernel, final=final),
        grid=(B, n_tiles),
        in_specs=[tok, tok, tok,
                  pl.BlockSpec((SUBLANES, tm), lambda b, i: (0, b * tiles_per_batch + i)),
                  pl.BlockSpec((1, 1, N_MOD * D), lambda b, i: (2 * b + jnp.where(i >= n_lat_tiles, 1, 0), 0, 0)),
                  pl.BlockSpec((1, D), lambda b, i: (0, 0))],
        out_specs=tok,
        out_shape=jax.ShapeDtypeStruct((B, n_tiles * tm, D), F32),
        compiler_params=_params(2),
        name="moe_combine",
    )(xs, y0.reshape(B, Lt, D), y1.reshape(B, Lt, D), route, modv, final_g.reshape(1, D))


S5_STATES = S5_GROUPS * S5_STATE
S5_CHUNK = 64


def _s5_kernel(u_ref, wd_ref, wr_ref, ar_ref, ai_ref, d_ref, y_ref, x_scr, h_scr, hr_scr, hi_scr, *, reverse):
    ns = S5_STATES

    @pl.when(pl.program_id(0) == 0)
    def _():
        hr_scr[...] = jnp.zeros_like(hr_scr)
        hi_scr[...] = jnp.zeros_like(hi_scr)

    u = u_ref[...]
    x_scr[...] = jnp.dot(u.astype(BF16), wd_ref[...], preferred_element_type=F32)
    ar = ar_ref[...]
    ai = ai_ref[...]

    def step(hr, hi, t):
        r = pl.multiple_of(t * SUBLANES, SUBLANES)
        xr = x_scr[pl.ds(r, SUBLANES), :ns]
        xi = x_scr[pl.ds(r, SUBLANES), ns:]
        return ar * hr - ai * hi + xr, ar * hi + ai * hr + xi

    def body(j, carry):
        hr, hi = carry
        t0 = (S5_CHUNK - 1 - 2 * j) if reverse else 2 * j
        t1 = t0 - 1 if reverse else t0 + 1
        hr0, hi0 = step(hr, hi, t0)
        hr1, hi1 = step(hr0, hi0, t1)
        lo = t1 if reverse else t0
        first_r, second_r = (hr1, hr0) if reverse else (hr0, hr1)
        first_i, second_i = (hi1, hi0) if reverse else (hi0, hi1)
        r = pl.multiple_of(lo * SUBLANES, 2 * SUBLANES)
        h_scr[pl.ds(r, 2 * SUBLANES), :ns] = jnp.concatenate([first_r, second_r], axis=0).astype(BF16)
        h_scr[pl.ds(r, 2 * SUBLANES), ns:] = jnp.concatenate([first_i, second_i], axis=0).astype(BF16)
        return hr1, hi1

    hr, hi = lax.fori_loop(0, S5_CHUNK // 2, body, (hr_scr[...], hi_scr[...]))
    hr_scr[...] = hr
    hi_scr[...] = hi
    y = jnp.dot(h_scr[...], wr_ref[...], preferred_element_type=F32)
    y = y + (d_ref[...].astype(F32) if reverse else u * d_ref[...])
    y_ref[...] = y.astype(y_ref.dtype)


def pl_s5_scan(u_tb, w_drive, w_read, a_re, a_im, addend, *, n_lat_steps, reverse):
    rows, ch = u_tb.shape
    rc = S5_CHUNK * SUBLANES
    n_chunks = rows // rc
    n_lat = n_lat_steps // S5_CHUNK
    n_ctx = n_chunks - n_lat
    assert rows % rc == 0 and n_lat_steps % S5_CHUNK == 0
    if reverse:
        def idx(i):
            return (n_chunks - 1 - i, 0)
    else:
        def idx(i):
            return (jnp.where(i < n_ctx, n_lat + i, i - n_ctx), 0)
    const = lambda i: (0, 0)
    ns2 = 2 * S5_STATES
    return pl.pallas_call(
        functools.partial(_s5_kernel, reverse=reverse),
        grid=(n_chunks,),
        in_specs=[pl.BlockSpec((rc, ch), idx),
                  pl.BlockSpec((ch, ns2), const),
                  pl.BlockSpec((ns2, ch), const),
                  pl.BlockSpec((SUBLANES, S5_STATES), const),
                  pl.BlockSpec((SUBLANES, S5_STATES), const),
                  pl.BlockSpec((rc, ch), idx) if reverse else pl.BlockSpec((1, ch), const)],
        out_specs=pl.BlockSpec((rc, ch), idx),
        out_shape=jax.ShapeDtypeStruct((rows, ch), BF16),
        scratch_shapes=[pltpu.VMEM((rc, ns2), F32), pltpu.VMEM((rc, ns2), BF16),
                        pltpu.VMEM((SUBLANES, S5_STATES), F32), pltpu.VMEM((SUBLANES, S5_STATES), F32)],
        compiler_params=_params(1),
        name="s5_scan_rev" if reverse else "s5_scan_fwd",
    )(u_tb, w_drive, w_read,
      jnp.broadcast_to(a_re[None, :], (SUBLANES, S5_STATES)),
      jnp.broadcast_to(a_im[None, :], (SUBLANES, S5_STATES)),
      addend if reverse else addend.reshape(1, ch))


FFT_N2 = 128


def _fft_tables(L):
    N = 2 * L
    N1 = N // FFT_N2
    k1 = np.arange(N1)[:, None]
    n1 = np.arange(N1 // 2)[None, :]
    n2 = np.arange(FFT_N2)[:, None, None]
    ang = -2.0 * np.pi * (k1[None] * (n2 + FFT_N2 * n1[None])) / N
    mr, mi = np.cos(ang), np.sin(ang)
    ma = np.concatenate([np.concatenate([mr, -mi], axis=2), np.concatenate([mi, mr], axis=2)], axis=1)
    gr, gi = np.transpose(mr, (0, 2, 1)), -np.transpose(mi, (0, 2, 1))
    mainv = np.concatenate([np.concatenate([gr, -gi], axis=2), np.concatenate([gi, gr], axis=2)], axis=1)
    kk = np.arange(FFT_N2)
    a2 = -2.0 * np.pi * np.outer(kk, kk) / FFT_N2
    fr, fi = np.cos(a2), np.sin(a2)
    f_fwd = np.block([[fr, -fi], [fi, fr]])
    f_inv = np.block([[fr, fi], [-fi, fr]])
    return (jnp.asarray(ma, BF16), jnp.asarray(mainv, BF16), jnp.asarray(f_fwd, BF16), jnp.asarray(f_inv, BF16))


HYENA_VMEM_LIMIT = 56 * 1024 * 1024
FFT_UNROLL = 8


def _hyena_fft_kernel(z_ref, h_ref, ma_ref, mainv_ref, ff_ref, fi_ref, o_ref, a_scr, b_scr, *, n1_count):
    half = n1_count // 2
    n2c = FFT_N2

    def stage_a(n2, c):
        xr = z_ref[0, pl.ds(n2, half, stride=n2c), :]
        xi = z_ref[1, pl.ds(n2, half, stride=n2c), :]
        x = jnp.concatenate([xr, xi], axis=0).astype(BF16)
        r = jnp.dot(ma_ref[n2], x, preferred_element_type=F32)
        a_scr[pl.ds(pl.multiple_of(n2 * 2 * n1_count, 2 * n1_count), 2 * n1_count), :] = r
        return c

    lax.fori_loop(0, n2c, stage_a, 0, unroll=FFT_UNROLL)

    def stage_c(k1, c):
        ar = a_scr[pl.ds(k1, n2c, stride=2 * n1_count), :]
        ai = a_scr[pl.ds(n1_count + k1, n2c, stride=2 * n1_count), :]
        x = jnp.concatenate([ar, ai], axis=0).astype(BF16)
        y = jnp.dot(ff_ref[...], x, preferred_element_type=F32)
        yr, yi = y[:n2c], y[n2c:]
        hr = h_ref[0, k1].astype(F32)
        hi = h_ref[1, k1].astype(F32)
        x2 = jnp.concatenate([yr * hr - yi * hi, yr * hi + yi * hr], axis=0).astype(BF16)
        b = jnp.dot(fi_ref[...], x2, preferred_element_type=F32)
        b_scr[pl.ds(pl.multiple_of(k1 * 2 * n2c, 2 * n2c), 2 * n2c), :] = b
        return c

    lax.fori_loop(0, n1_count, stage_c, 0, unroll=FFT_UNROLL)

    def stage_a_inv(n2, c):
        br = b_scr[pl.ds(n2, n1_count, stride=2 * n2c), :]
        bi = b_scr[pl.ds(n2c + n2, n1_count, stride=2 * n2c), :]
        x = jnp.concatenate([br, bi], axis=0).astype(BF16)
        r = jnp.dot(mainv_ref[n2], x, preferred_element_type=F32)
        o_ref[0, pl.ds(n2, half, stride=n2c), :] = r[:half]
        o_ref[1, pl.ds(n2, half, stride=n2c), :] = r[half:]
        return c

    lax.fori_loop(0, n2c, stage_a_inv, 0, unroll=FFT_UNROLL)


def hyena_spectrum(filt):
    N, C = filt.shape
    N1 = N // FFT_N2
    h2 = (jnp.fft.fft(filt, axis=0) / N).reshape(FFT_N2, N1, C).transpose(1, 0, 2)
    return jnp.stack([h2.real, h2.imag]).astype(BF16)


def pl_hyena_conv(z, h, dft_tables, L):
    B, _, C = z.shape
    N1 = 2 * L // FFT_N2
    ma, mainv, f_fwd, f_inv = dft_tables
    cw = LANES
    full = lambda arr: pl.BlockSpec(arr.shape, lambda ct, bp: (0,) * arr.ndim)
    return pl.pallas_call(
        functools.partial(_hyena_fft_kernel, n1_count=N1),
        grid=(C // cw, B // 2),
        in_specs=[pl.BlockSpec((2, L, cw), lambda ct, bp: (bp, 0, ct)),
                  pl.BlockSpec((2, N1, FFT_N2, cw), lambda ct, bp: (0, 0, 0, ct)),
                  full(ma), full(mainv), full(f_fwd), full(f_inv)],
        out_specs=pl.BlockSpec((2, L, cw), lambda ct, bp: (bp, 0, ct)),
        out_shape=jax.ShapeDtypeStruct((B, L, C), F32),
        scratch_shapes=[pltpu.VMEM((FFT_N2 * 2 * N1, cw), F32), pltpu.VMEM((N1 * 2 * FFT_N2, cw), F32)],
        compiler_params=_params(2, HYENA_VMEM_LIMIT),
        name="hyena_fft_conv",
    )(z, h, ma, mainv, f_fwd, f_inv)


def hyena_filter(L, w1, b1, w2, b2, w3, freq):
    t = jnp.arange(L, dtype=F32) / L
    ang = (2.0 * math.pi) * t[:, None] * jnp.arange(1, HY_BANDS + 1, dtype=F32)
    feat = jnp.concatenate([t[:, None], jnp.cos(ang), jnp.sin(ang)], axis=-1)
    hp = lax.Precision.HIGHEST
    h = jnp.sin(freq * (jnp.dot(feat, w1, precision=hp) + b1))
    h = jnp.sin(freq * (jnp.dot(h, w2, precision=hp) + b2))
    h = jnp.dot(h, w3, precision=hp).reshape(L, 2, HY_CH)
    window = jnp.exp(-t[:, None] * jnp.linspace(HY_DECAY_MIN, HY_DECAY_MAX, HY_CH, dtype=F32))
    h = h * window[:, None, :]
    filt = jnp.concatenate([h[:, 0], jnp.zeros((1, HY_CH), F32), h[:0:-1, 1]], axis=0)
    return filt / (jnp.sum(jnp.abs(filt), axis=0, keepdims=True) + EPS)


def hyena_conv(z, h_lat, ff_ctx, dft_tables, L):
    Lc = z.shape[1] - L
    y_lat = pl_hyena_conv(z, h_lat, dft_tables, L)
    zf = jnp.fft.rfft(z[:, L:], n=2 * Lc, axis=1)
    y_ctx = jnp.fft.irfft(zf * ff_ctx[None], n=2 * Lc, axis=1)[:, :Lc]
    return y_lat, y_ctx


def _block_diag(blocks):
    G, r, c = blocks.shape
    eye = jnp.eye(G, dtype=blocks.dtype)
    return (eye[:, None, :, None] * blocks[:, :, None, :]).reshape(G * r, G * c)


def s5_tables(a_re, a_im, log_dt, b_re, b_im, c_re, c_im):
    A = lax.complex(a_re, a_im)
    dtA = jnp.exp(log_dt)[:, None] * A
    a_bar = jnp.exp(dtA)
    b_bar = ((a_bar - 1.0) / A)[:, :, None] * lax.complex(b_re, b_im)
    bt_re = jnp.transpose(b_bar.real, (0, 2, 1))
    bt_im = jnp.transpose(b_bar.imag, (0, 2, 1))
    w_drive = jnp.concatenate([_block_diag(bt_re), _block_diag(bt_im)], axis=1)
    ct_re = jnp.transpose(c_re, (0, 2, 1))
    ct_im = jnp.transpose(c_im, (0, 2, 1))
    w_read = jnp.concatenate([_block_diag(ct_re), -_block_diag(ct_im)], axis=0)
    return w_drive.astype(BF16), w_read.astype(BF16), a_bar.real.reshape(-1), a_bar.imag.reshape(-1)


def s5_scan(u_tb, tables, d_skip, n_lat_steps):
    y = d_skip
    for direction in range(2):
        w_drive, w_read, a_re, a_im = (t[direction] for t in tables)
        y = pl_s5_scan(u_tb, w_drive, w_read, a_re, a_im, y, n_lat_steps=n_lat_steps, reverse=direction == 1)
    return y


def rope_tables(L, Lc):
    rows = L // GRID_W
    row = jnp.repeat(jnp.arange(rows, dtype=F32), GRID_W)
    col = jnp.tile(jnp.arange(GRID_W, dtype=F32), rows)
    inv = ROPE_BASE ** (-jnp.arange(ROPE_PAIRS_AXIS, dtype=F32) / ROPE_PAIRS_AXIS)
    ang = jnp.concatenate([row[:, None] * inv, col[:, None] * inv], axis=-1)
    cos, sin = jnp.cos(ang), jnp.sin(ang)
    n_maps = ATT_W // ATT_HEAD_DIM
    cosf = jnp.tile(jnp.concatenate([cos, cos], axis=-1), (1, n_maps))
    sinf = jnp.tile(jnp.concatenate([-sin, sin], axis=-1), (1, n_maps))
    return (jnp.concatenate([cosf, jnp.ones((Lc, ATT_W), F32)], axis=0),
            jnp.concatenate([sinf, jnp.zeros((Lc, ATT_W), F32)], axis=0))


def moe_dispatch(route, counts_row):
    T = route.shape[1]
    experts = [route[j].astype(jnp.int32) for j in range(MOE_TOP_K)]
    ranks = [route[2 * MOE_TOP_K + j].astype(jnp.int32) for j in range(MOE_TOP_K)]
    counts = counts_row[MOE_GROUPS:MOE_GROUPS + N_EXPERTS].astype(jnp.int32)
    n_assign = T * MOE_TOP_K
    n_blocks = -(-n_assign // MOE_BLOCK) + N_EXPERTS
    n_pad = n_blocks * MOE_BLOCK
    a_bits = (n_assign - 1).bit_length()
    t_ids = jnp.arange(T, dtype=jnp.int32)
    keys = jnp.concatenate([(experts[j] << a_bits) + (MOE_TOP_K * t_ids + j) for j in range(MOE_TOP_K)])
    order = jnp.sort(keys) & ((1 << a_bits) - 1)
    start = jnp.cumsum(counts) - counts
    padded = (counts + MOE_BLOCK - 1) // MOE_BLOCK * MOE_BLOCK
    pad_end = jnp.cumsum(padded)
    pad_start = pad_end - padded
    slots_of_tok = [(pad_start[experts[j]] + ranks[j]).astype(jnp.int32) for j in range(MOE_TOP_K)]
    block_first = jnp.arange(n_blocks, dtype=jnp.int32) * MOE_BLOCK
    block_e = jnp.minimum(jnp.sum((pad_end[None, :] <= block_first[:, None]).astype(jnp.int32), axis=1),
                          N_EXPERTS - 1).astype(jnp.int32)
    within = jnp.arange(MOE_BLOCK, dtype=jnp.int32)[None, :]
    slot_r = (block_first - pad_start[block_e])[:, None] + within
    slot_valid = ((slot_r < counts[block_e][:, None]) & (block_first < pad_end[-1])[:, None]).reshape(-1)
    sorted_pos = jnp.clip(start[block_e][:, None] + slot_r, 0, n_assign - 1).reshape(-1)
    slot_t = order[sorted_pos] // MOE_TOP_K
    slot_tok = jnp.where(slot_valid, slot_t, jnp.arange(n_pad, dtype=jnp.int32) % T)
    n_used = (pad_end[-1:] // MOE_BLOCK).astype(jnp.int32)
    return slot_tok, slots_of_tok, block_e, n_used


def hier_moe(h2, route, counts_row, w1, w3, w2, layer):
    slot_tok, slots_of_tok, block_e, n_used = moe_dispatch(route, counts_row)
    yb = pl_moe_ffn(h2[slot_tok], block_e, n_used, w1, w3, w2, layer)
    return tuple(yb[s] for s in slots_of_tok)


def kernel(x, c, ctx, c_ctx, w_mod, b_mod, norm1_g, norm2_g, final_g, w_in, w_out, hy_conv_w, hy_conv_b, hy_ffn_w1, hy_ffn_b1, hy_ffn_w2, hy_ffn_b2, hy_ffn_w3, hy_freq, hy_skip, hy_norm_g, s5_a_re, s5_a_im, s5_log_dt, s5_b_re, s5_b_im, s5_c_re, s5_c_im, s5_d, s5_glu_w, s5_norm_g, att_lq1, att_lk1, att_lq2, att_lk2, att_subln_g, moe_wg, moe_bg, moe_we, moe_be, moe_w1, moe_w3, moe_w2):
    B, L, D = x.shape
    Lc = ctx.shape[1]
    Lt = L + Lc
    assert B == SUBLANES and Lc == TOKEN_TILE and L % ATT_TQ == 0
    n_lat_tiles = L // TOKEN_TILE
    cosf, sinf = rope_tables(L, Lc)
    hp = lax.Precision.HIGHEST
    q_scale = ATT_HEAD_DIM ** -0.5 * math.log2(math.e)

    mods = jnp.einsum('bd,ldk->lbk', jax.nn.silu(c), w_mod, precision=hp) + b_mod[:, None, :]
    cmods = jnp.einsum('d,ldk->lk', jax.nn.silu(c_ctx), w_mod, precision=hp) + b_mod
    modvs = jnp.stack([mods, jnp.broadcast_to(cmods[:, None, :], mods.shape)], axis=2)
    modvs = modvs.reshape(DEPTH, 2 * B, 1, N_MOD * D)
    filt_args = (hy_ffn_w1, hy_ffn_b1, hy_ffn_w2, hy_ffn_b2, hy_ffn_w3, hy_freq)
    h_lat = jax.vmap(lambda *p: hyena_spectrum(hyena_filter(L, *p)))(*filt_args)
    ff_ctx = jax.vmap(lambda *p: jnp.fft.rfft(hyena_filter(Lc, *p), n=2 * Lc, axis=0))(*filt_args)
    dft_tables = _fft_tables(L)
    s5_tabs = jax.vmap(jax.vmap(s5_tables))(s5_a_re, s5_a_im, s5_log_dt, s5_b_re, s5_b_im, s5_c_re, s5_c_im)
    lam_inits = [0.8 - 0.6 * math.exp(-0.3 * l) for l in range(DEPTH)]
    lams = (jnp.exp(jnp.sum(att_lq1 * att_lk1, axis=-1)) - jnp.exp(jnp.sum(att_lq2 * att_lk2, axis=-1))
            + jnp.asarray(lam_inits, F32))
    w_router = jnp.zeros((DEPTH, D, LANES), F32).at[:, :, :MOE_GROUPS].set(moe_wg)
    w_router = w_router.at[:, :, MOE_GROUPS:MOE_GROUPS + N_EXPERTS].set(moe_we)
    rwh = w_router.astype(BF16)
    rwl = (w_router - rwh.astype(F32)).astype(BF16)
    router_bias = jnp.zeros((DEPTH, 1, LANES), F32).at[:, 0, :MOE_GROUPS].set(moe_bg)
    router_bias = router_bias.at[:, 0, MOE_GROUPS:MOE_GROUPS + N_EXPERTS].set(moe_be)
    w_in_b, w_out_b, glu_w_b = w_in.astype(BF16), w_out.astype(BF16), s5_glu_w.astype(BF16)

    xs = jnp.concatenate([x, ctx], axis=1)
    for l in range(DEPTH):
        modv = modvs[l]
        u0, z, s5_u, q_p, q_r, k_r, v = pl_in_proj(xs, modv, norm1_g[l], w_in_b[l], hy_conv_w[l], hy_conv_b[l],
                                                   cosf, sinf, n_lat_tiles, q_scale)

        y_lat, y_ctx = hyena_conv(z, h_lat[l], ff_ctx[l], dft_tables, L)

        s5_y = s5_scan(s5_u.reshape(Lt * B, S5_CH), tuple(t[l] for t in s5_tabs), s5_d[l], L)

        g_scaled = (att_subln_g[l] * (1.0 - lam_inits[l])).reshape(1, ATT_V_DIM)
        att_lat, att_ctx = pl_diff_attention(q_p, q_r, k_r, v, g_scaled, lams[l], L)

        xs, h2, route, counts = pl_out_proj(xs, u0, z, y_lat, y_ctx, s5_y.reshape(Lt, B * S5_CH), att_lat, att_ctx,
                                            modv, hy_skip[l], hy_norm_g[l], glu_w_b[l], s5_norm_g[l], w_out_b[l],
                                            norm2_g[l], rwh[l], rwl[l], router_bias[l], n_lat_tiles)

        y0, y1 = hier_moe(h2.reshape(B * Lt, D), route, counts[:, 0], moe_w1, moe_w3, moe_w2, l)
        xs = pl_moe_combine(xs, y0, y1, route, modv, final_g, n_lat_tiles, final=l == DEPTH - 1)
    return xs
```
